```python
import math
import jax
import jax.numpy as jnp
from jax import lax
import numpy as np

D_MODEL = 2048
BATCH = 4
SEQ = 2048
DEPTH = 1
DEC_BATCH = 8
DEC_SEQ = 8
PAST_LEN = 16384
PAGE_SIZE = 128

HG_HEADS = 8
HG_DK = 128
HG_DV = 128
HG_WIDTH = HG_HEADS * HG_DV
HG_CHUNK = 64
DA_HEADS = 8
DA_DH = 64
DA_DV = 2 * DA_DH
DA_WIDTH = DA_HEADS * DA_DV
N_BUCKETS = 32
REL_MAX_DISTANCE = 128
Q_BLOCK = 128
D_FF = ((8 * D_MODEL // 3 + 255) // 256) * 256
RMS_EPS = 1e-6
NEG_INF = -1e30

_IN_SIZES = (HG_HEADS * HG_DK, HG_HEADS * HG_DK, HG_WIDTH, HG_WIDTH,
             2 * DA_HEADS * DA_DH, 2 * DA_HEADS * DA_DH, DA_WIDTH, D_MODEL, D_MODEL)
IN_SPLIT_POINTS = tuple(int(s) for s in np.cumsum(_IN_SIZES)[:-1])
N_IN = int(sum(_IN_SIZES))

kernel_name = 'hgrn2_diffattn_gated_hybrid_step'


def rms_norm(x, w):
    xf = x.astype(jnp.float32)
    y = xf * lax.rsqrt(jnp.mean(xf * xf, axis=-1, keepdims=True) + RMS_EPS)
    return (y * w.astype(jnp.float32)).astype(x.dtype)


def t5_bucket(dist):
    n = jnp.maximum(dist, 0)
    max_exact = N_BUCKETS // 2
    nf = jnp.maximum(n, 1).astype(jnp.float32)
    large = max_exact + (jnp.log(nf / max_exact) / math.log(REL_MAX_DISTANCE / max_exact)
                         * (N_BUCKETS - max_exact)).astype(jnp.int32)
    large = jnp.minimum(large, N_BUCKETS - 1)
    return jnp.where(n < max_exact, n, large)


def diff_attend(q, k, v, q_pos, k_pos, bias_table, lam):
    b, lq = q.shape[:2]
    lk = k.shape[1]
    s = jnp.einsum('bqhd,bkhd->bhqk', q.astype(jnp.float32), k.astype(jnp.float32)) * (DA_DH ** -0.5)
    bucket = t5_bucket(q_pos[:, None] - k_pos[None, :])
    s = s + jnp.transpose(bias_table.astype(jnp.float32)[bucket], (2, 0, 1))[None]
    s = jnp.where((k_pos[None, :] <= q_pos[:, None])[None, None], s, NEG_INF)
    p = jax.nn.softmax(s, axis=-1).reshape(b, 2, DA_HEADS, lq, lk)
    pd = p[:, 0] - lam * p[:, 1]
    return jnp.einsum('bhqk,bkhv->bqhv', pd, v.astype(jnp.float32))


def attend_prompt(q, k, v, bias_table, lam):
    b, L = q.shape[:2]
    nb = L // Q_BLOCK
    pos = jnp.arange(L, dtype=jnp.int32)
    kf = k.astype(jnp.float32)
    vf = v.astype(jnp.float32)
    qb = jnp.moveaxis(q.reshape(b, nb, Q_BLOCK, 2 * DA_HEADS, DA_DH), 1, 0)
    pb = pos.reshape(nb, Q_BLOCK)
    o = lax.map(lambda a: diff_attend(a[0], kf, vf, a[1], pos, bias_table, lam), (qb, pb))
    return jnp.moveaxis(o, 0, 1).reshape(b, L, DA_HEADS, DA_DV)


def hgrn2_recurrence(q, f_logit, i, lb, s0):
    b, L, H, dk = q.shape
    dv = i.shape[-1]
    f = lb + (1.0 - lb) * jax.nn.sigmoid(f_logit.astype(jnp.float32))
    g = jnp.log(f)
    kin = 1.0 - f
    c = min(HG_CHUNK, L)
    pad = (-L) % c
    n = (L + pad) // c

    def blk(t):
        t = jnp.pad(t, ((0, 0), (0, pad), (0, 0), (0, 0)))
        return t.reshape(b, n, c, H, t.shape[-1]).transpose(0, 3, 1, 2, 4)

    qc = blk(q.astype(jnp.float32))
    kc = blk(kin)
    vc = blk(i.astype(jnp.float32))
    G = jnp.cumsum(blk(g), axis=3)
    qg = qc * jnp.exp(G)
    kg = kc * jnp.exp(-G)
    causal = jnp.tril(jnp.ones((c, c), dtype=bool))
    A = jnp.where(causal, jnp.einsum('bhncd,bhnsd->bhncs', qg, kg), 0.0)
    o_intra = jnp.einsum('bhncs,bhnsv->bhncv', A, vc)
    G_last = G[:, :, :, -1]
    dS = jnp.einsum('bhncd,bhncv->bhndv', kc * jnp.exp(G_last[:, :, :, None, :] - G), vc)
    decay = jnp.exp(G_last)

    def step(S, xs):
        d, ds = xs
        return d[..., None] * S + ds, S

    s_final, s_prev = lax.scan(step, s0.astype(jnp.float32),
                               (jnp.moveaxis(decay, 2, 0), jnp.moveaxis(dS, 2, 0)))
    o_inter = jnp.einsum('bhncd,nbhdv->bhncv', qg, s_prev)
    o = (o_intra + o_inter).transpose(0, 2, 3, 1, 4).reshape(b, n * c, H, dv)[:, :L]
    return o, s_final.astype(s0.dtype)


def decoder_layer(x, l, s0, past_k, past_v, params):
    (norm_mix_pre, norm_mix_post, norm_ffn_pre, norm_ffn_post, w_in, hg_lb_logits, hg_norm_w,
     lq1, lk1, lq2, lk2, da_subln_w, rel_bias_table, w_branch_hg, w_branch_da, w_out,
     w_ffn_up, w_ffn_down) = params
    b, L, _ = x.shape
    h = rms_norm(x, norm_mix_pre[l])
    u = h @ w_in[l]
    hq, hf, hi, hgate, dq, dk, dv, g_hg, g_da = jnp.split(u, IN_SPLIT_POINTS, axis=-1)

    lb = jnp.cumsum(jax.nn.softmax(hg_lb_logits.astype(jnp.float32), axis=0), axis=0)[l]
    lb = lb.reshape(HG_HEADS, HG_DK)
    o_hg, s_new = hgrn2_recurrence(hq.reshape(b, L, HG_HEADS, HG_DK), hf.reshape(b, L, HG_HEADS, HG_DK),
                                   hi.reshape(b, L, HG_HEADS, HG_DV), lb, s0)
    o_hg = rms_norm(o_hg.astype(x.dtype), hg_norm_w[l]) * jax.nn.silu(hgate.reshape(b, L, HG_HEADS, HG_DV))
    y_hg = o_hg.reshape(b, L, HG_WIDTH) @ w_branch_hg[l]

    q = dq.reshape(b, L, 2 * DA_HEADS, DA_DH)
    k = dk.reshape(b, L, 2 * DA_HEADS, DA_DH)
    v = dv.reshape(b, L, DA_HEADS, DA_DV)
    lam_init = 0.8 - 0.6 * math.exp(-0.3 * l)
    lam = (jnp.exp(jnp.sum(lq1[l].astype(jnp.float32) * lk1[l].astype(jnp.float32)))
           - jnp.exp(jnp.sum(lq2[l].astype(jnp.float32) * lk2[l].astype(jnp.float32))) + lam_init)
    if past_k is None:
        o_da = attend_prompt(q, k, v, rel_bias_table, lam)
    else:
        past_len = past_k.shape[1]
        k_all = jnp.concatenate([past_k, k.astype(past_k.dtype)], axis=1)
        v_all = jnp.concatenate([past_v, v.astype(past_v.dtype)], axis=1)
        q_pos = past_len + jnp.arange(L, dtype=jnp.int32)
        k_pos = jnp.arange(past_len + L, dtype=jnp.int32)
        o_da = diff_attend(q, k_all, v_all, q_pos, k_pos, rel_bias_table, lam)
    o_da = rms_norm(o_da.astype(x.dtype), da_subln_w[l]) * (1.0 - lam_init)
    y_da = o_da.reshape(b, L, DA_WIDTH) @ w_branch_da[l]

    mixed = (jax.nn.sigmoid(g_hg) * y_hg + jax.nn.sigmoid(g_da) * y_da) @ w_out[l]
    x = x + rms_norm(mixed, norm_mix_post[l])

    h = rms_norm(x, norm_ffn_pre[l])
    gate, up = jnp.split(h @ w_ffn_up[l], 2, axis=-1)
    ffn = (jax.nn.silu(gate) * up) @ w_ffn_down[l]
    x = x + rms_norm(ffn, norm_ffn_post[l])
    return x, k, v, s_new


def setup_inputs(seed: int = 0) -> dict:
    key = jax.random.key(seed)
    ks = jax.random.split(key, 26)
    n_pages = PAST_LEN // PAGE_SIZE
    n_used = DEC_BATCH * n_pages
    n_pool = n_used + max(1, n_used // 4)
    f32 = jnp.float32

    def nrm(k, shape, scale):
        return jax.random.normal(k, shape, f32) * scale

    def gain(k, shape):
        return 1.0 + 0.02 * jax.random.normal(k, shape, f32)

    page_table = jax.random.permutation(ks[5], n_pool)[:n_used].reshape(DEC_BATCH, n_pages).astype(jnp.int32)
    return {
        'x_prompt': nrm(ks[0], (BATCH, SEQ, D_MODEL), 1.0),
        'x_sample': nrm(ks[1], (DEC_BATCH, DEC_SEQ, D_MODEL), 1.0),
        'cache_k': nrm(ks[2], (DEPTH, n_pool, PAGE_SIZE, 2 * DA_HEADS, DA_DH), 1.0),
        'cache_v': nrm(ks[3], (DEPTH, n_pool, PAGE_SIZE, DA_HEADS, DA_DV), 1.0),
        'state_hgrn': nrm(ks[4], (DEPTH, DEC_BATCH, HG_HEADS, HG_DK, HG_DV), 0.5),
        'page_table': page_table,
        'norm_mix_pre': gain(ks[6], (DEPTH, D_MODEL)),
        'norm_mix_post': gain(ks[7], (DEPTH, D_MODEL)),
        'norm_ffn_pre': gain(ks[8], (DEPTH, D_MODEL)),
        'norm_ffn_post': gain(ks[9], (DEPTH, D_MODEL)),
        'w_in': nrm(ks[10], (DEPTH, D_MODEL, N_IN), D_MODEL ** -0.5),
        'hg_lb_logits': nrm(ks[11], (DEPTH + 1, HG_HEADS * HG_DK), 0.1),
        'hg_norm_w': gain(ks[12], (DEPTH, HG_DV)),
        'da_lambda_q1': nrm(ks[13], (DEPTH, DA_DH), 0.1),
        'da_lambda_k1': nrm(ks[14], (DEPTH, DA_DH), 0.1),
        'da_lambda_q2': nrm(ks[15], (DEPTH, DA_DH), 0.1),
        'da_lambda_k2': nrm(ks[16], (DEPTH, DA_DH), 0.1),
        'da_subln_w': gain(ks[17], (DEPTH, DA_DV)),
        'rel_bias_table': nrm(ks[18], (N_BUCKETS, 2 * DA_HEADS), 0.5),
        'w_branch_hg': nrm(ks[19], (DEPTH, HG_WIDTH, D_MODEL), HG_WIDTH ** -0.5),
        'w_branch_da': nrm(ks[20], (DEPTH, DA_WIDTH, D_MODEL), DA_WIDTH ** -0.5),
        'w_out': nrm(ks[21], (DEPTH, D_MODEL, D_MODEL), D_MODEL ** -0.5),
        'w_ffn_up': nrm(ks[22], (DEPTH, D_MODEL, 2 * D_FF), D_MODEL ** -0.5),
        'w_ffn_down': nrm(ks[23], (DEPTH, D_FF, D_MODEL), D_FF ** -0.5),
    }


def reference(x_prompt, x_sample, cache_k, cache_v, state_hgrn, page_table, norm_mix_pre, norm_mix_post,
              norm_ffn_pre, norm_ffn_post, w_in, hg_lb_logits, hg_norm_w, da_lambda_q1, da_lambda_k1,
              da_lambda_q2, da_lambda_k2, da_subln_w, rel_bias_table, w_branch_hg, w_branch_da, w_out,
              w_ffn_up, w_ffn_down):
    params = (norm_mix_pre, norm_mix_post, norm_ffn_pre, norm_ffn_post, w_in, hg_lb_logits, hg_norm_w,
              da_lambda_q1, da_lambda_k1, da_lambda_q2, da_lambda_k2, da_subln_w, rel_bias_table,
              w_branch_hg, w_branch_da, w_out, w_ffn_up, w_ffn_down)
    n_seq, n_pages = page_table.shape
    past_len = n_pages * cache_k.shape[2]
    y_p = x_prompt
    y_s = x_sample
    kp_l, vp_l, sp_l, ks_l, vs_l, ss_l = [], [], [], [], [], []
    for l in range(DEPTH):
        s0 = jnp.zeros((x_prompt.shape[0], HG_HEADS, HG_DK, HG_DV), state_hgrn.dtype)
        y_p, kp, vp, sp = decoder_layer(y_p, l, s0, None, None, params)
        past_k = cache_k[l][page_table].reshape(n_seq, past_len, 2 * DA_HEADS, DA_DH)
        past_v = cache_v[l][page_table].reshape(n_seq, past_len, DA_HEADS, DA_DV)
        y_s, ks_, vs_, ss_ = decoder_layer(y_s, l, state_hgrn[l], past_k, past_v, params)
        kp_l.append(kp)
        vp_l.append(vp)
        sp_l.append(sp)
        ks_l.append(ks_)
        vs_l.append(vs_)
        ss_l.append(ss_)
    return (y_p, y_s, jnp.stack(kp_l), jnp.stack(vp_l), jnp.stack(sp_l),
            jnp.stack(ks_l), jnp.stack(vs_l), jnp.stack(ss_l))
```

```python
import functools
import math

import jax
import jax.numpy as jnp
import numpy as np
from jax import lax
from jax.experimental import pallas as pl
from jax.experimental.pallas import tpu as pltpu

F32 = jnp.float32
BF16 = jnp.bfloat16

RMS_EPS = 1e-6
NEG_INF = -1e30
HG_CHUNK = 64
REL_MAX_DISTANCE = 128
LANES = 128
VMEM_LIMIT = 56 * 1024 * 1024

_NT = (((1,), (1,)), ((), ()))
_TN = (((0,), (0,)), ((), ()))


def _params(*sem):
    return pltpu.CompilerParams(dimension_semantics=sem, vmem_limit_bytes=VMEM_LIMIT)


def _rms(x, w):
    return x * lax.rsqrt(jnp.mean(x * x, axis=-1, keepdims=True) + RMS_EPS) * w


def _sigmoid(x):
    return 1.0 / (1.0 + jnp.exp(-x))


def _pick(n, prefs):
    for p in prefs:
        if n % p == 0:
            return p
    return n


def _norm_proj_kernel(x_ref, nw_ref, w_ref, o_ref, h_ref):
    @pl.when(pl.program_id(1) == 0)
    def _():
        h_ref[...] = _rms(x_ref[...], nw_ref[...]).astype(BF16)

    o_ref[...] = jnp.dot(h_ref[...], w_ref[...], preferred_element_type=F32)


def _norm_proj(x, nw, w_bf):
    m, d = x.shape
    n = w_bf.shape[1]
    tm = _pick(m, (1024, 512, 256, 128, 64, 8))
    tn = _pick(n, (1024, 512, 256, 128))
    return pl.pallas_call(
        _norm_proj_kernel,
        grid=(m // tm, n // tn),
        in_specs=[pl.BlockSpec((tm, d), lambda i, j: (i, 0)),
                  pl.BlockSpec((1, d), lambda i, j: (0, 0)),
                  pl.BlockSpec((d, tn), lambda i, j: (0, j))],
        out_specs=pl.BlockSpec((tm, tn), lambda i, j: (i, j)),
        out_shape=jax.ShapeDtypeStruct((m, n), F32),
        scratch_shapes=[pltpu.VMEM((tm, d), BF16)],
        compiler_params=_params("parallel", "arbitrary"),
        name="norm_proj",
    )(x, nw, w_bf)


def _hgrn_kernel(*refs, heads, dk, dv, chunk, rows, has_s0):
    if has_s0:
        q_ref, f_ref, i_ref, gate_ref, lb_ref, nw_ref, s0_ref, o_ref, sout_ref, st_ref = refs
    else:
        q_ref, f_ref, i_ref, gate_ref, lb_ref, nw_ref, o_ref, sout_ref, st_ref = refs
    t = pl.program_id(1)
    real = q_ref.shape[0]

    @pl.when(t == 0)
    def _():
        for h in range(heads):
            if has_s0:
                st_ref[h] = s0_ref[h].T
            else:
                st_ref[h] = jnp.zeros((dv, dk), F32)

    lb = lb_ref[...]
    f = lb + (1.0 - lb) * _sigmoid(f_ref[...])
    g = jnp.log(f)
    kin = 1.0 - f
    q = q_ref[...]
    v = i_ref[...]
    if rows > real:
        def pad(a):
            return jnp.concatenate([a, jnp.zeros((rows - real, a.shape[1]), F32)], axis=0)
        g, kin, q, v = pad(g), pad(kin), pad(q), pad(v)

    r = lax.broadcasted_iota(jnp.int32, (rows, rows), 0)
    c = lax.broadcasted_iota(jnp.int32, (rows, rows), 1)
    tri = jnp.where((r // chunk == c // chunk) & (c <= r), 1.0, 0.0).astype(BF16)
    g_hi = g.astype(BF16)
    g_r1 = g - g_hi.astype(F32)
    g_mid = g_r1.astype(BF16)
    g_lo = (g_r1 - g_mid.astype(F32)).astype(BF16)
    G = (jnp.dot(tri, g_hi, preferred_element_type=F32)
         + jnp.dot(tri, g_mid, preferred_element_type=F32)
         + jnp.dot(tri, g_lo, preferred_element_type=F32))

    qg = (q * jnp.exp(G)).astype(BF16)
    kg = (kin * jnp.exp(-G)).astype(BF16)
    vb = v.astype(BF16)
    cr = lax.broadcasted_iota(jnp.int32, (chunk, chunk), 0)
    cc = lax.broadcasted_iota(jnp.int32, (chunk, chunk), 1)
    causal = cc <= cr
    nw = nw_ref[...]

    for ci in range(rows // chunk):
        lo = ci * chunk
        Gc = G[lo:lo + chunk]
        Gl = Gc[chunk - 1:chunk]
        kdec = (kin[lo:lo + chunk] * jnp.exp(Gl - Gc)).astype(BF16)
        decay = jnp.exp(Gl)
        n_out = min(chunk, real - lo)
        for h in range(heads):
            sk = slice(h * dk, (h + 1) * dk)
            sv = slice(h * dv, (h + 1) * dv)
            qg_h = qg[lo:lo + chunk, sk]
            v_h = vb[lo:lo + chunk, sv]
            a = lax.dot_general(qg_h, kg[lo:lo + chunk, sk], _NT, preferred_element_type=F32)
            a = jnp.where(causal, a, 0.0).astype(BF16)
            st = st_ref[h]
            o = (jnp.dot(a, v_h, preferred_element_type=F32)
                 + lax.dot_general(qg_h, st.astype(BF16), _NT, preferred_element_type=F32))
            st_ref[h] = st * decay[:, sk] + lax.dot_general(v_h, kdec[:, sk], _TN,
                                                            preferred_element_type=F32)
            if n_out > 0:
                gt = gate_ref[lo:lo + n_out, sv]
                on = _rms(o[:n_out], nw) * (gt * _sigmoid(gt))
                o_ref[lo:lo + n_out, sv] = on.astype(o_ref.dtype)

    @pl.when(t == pl.num_programs(1) - 1)
    def _():
        for h in range(heads):
            sout_ref[h] = st_ref[h].T


def _hgrn(u, col0, lb, nw, s0, batch, seqlen, heads, dk, dv):
    width = heads * dk
    assert dk == dv and col0 % width == 0
    cb = col0 // width
    chunk = min(HG_CHUNK, seqlen)
    if seqlen >= LANES:
        tb = _pick(seqlen, (256, 128))
        rows = tb
        assert tb % chunk == 0
    else:
        tb = seqlen
        rows = LANES
        chunk = LANES
    nt = seqlen // tb
    has_s0 = s0 is not None

    def col(k):
        return pl.BlockSpec((tb, width), lambda b, t: (b * nt + t, cb + k))

    in_specs = [col(0), col(1), col(2), col(3),
                pl.BlockSpec((1, width), lambda b, t: (0, 0)),
                pl.BlockSpec((1, dv), lambda b, t: (0, 0))]
    args = [u, u, u, u, lb, nw]
    if has_s0:
        in_specs.append(pl.BlockSpec((None, heads, dk, dv), lambda b, t: (b, 0, 0, 0)))
        args.append(s0)
    kern = functools.partial(_hgrn_kernel, heads=heads, dk=dk, dv=dv, chunk=chunk, rows=rows,
                             has_s0=has_s0)
    return pl.pallas_call(
        kern,
        grid=(batch, nt),
        in_specs=in_specs,
        out_specs=[pl.BlockSpec((tb, heads * dv), lambda b, t: (b * nt + t, 0)),
                   pl.BlockSpec((None, heads, dk, dv), lambda b, t: (b, 0, 0, 0))],
        out_shape=[jax.ShapeDtypeStruct((batch * seqlen, heads * dv), BF16),
                   jax.ShapeDtypeStruct((batch, heads, dk, dv), F32)],
        scratch_shapes=[pltpu.VMEM((heads, dv, dk), F32)],
        compiler_params=_params("parallel", "arbitrary"),
        name="hgrn",
    )(*args)


def _bucket(dist, n_buckets):
    n = jnp.maximum(dist, 0)
    max_exact = n_buckets // 2
    nf = jnp.maximum(n, 1).astype(F32)
    large = max_exact + (jnp.log(nf / max_exact) / math.log(REL_MAX_DISTANCE / max_exact)
                         * (n_buckets - max_exact)).astype(jnp.int32)
    large = jnp.minimum(large, n_buckets - 1)
    return jnp.where(n < max_exact, n, large)


def _far_start(n_buckets):
    n = np.arange(1, 8 * REL_MAX_DISTANCE, dtype=np.int64)
    max_exact = n_buckets // 2
    nf = n.astype(np.float32)
    large = max_exact + (np.log(nf / np.float32(max_exact)) / np.float32(math.log(REL_MAX_DISTANCE / max_exact))
                         * np.float32(n_buckets - max_exact)).astype(np.int32)
    b = np.where(n < max_exact, n, np.minimum(large, n_buckets - 1))
    below = n[b < n_buckets - 1]
    return int(below.max()) + 2


def _attn_kernel(tbl_ref, lam_ref, nw_ref, q1_ref, q2_ref, k1_ref, k2_ref, v_ref, o_ref,
                 bnear_ref, bdiag_ref, m_ref, l_ref, acc_ref, *,
                 t, dh, dv, heads, n_buckets, out_scale):
    hp = pl.program_id(0)
    b = pl.program_id(1)
    qb = pl.program_id(2)

    @pl.when((b == 0) & (qb == 0))
    def _():
        row = lax.broadcasted_iota(jnp.int32, (t, t), 0)
        col = lax.broadcasted_iota(jnp.int32, (t, t), 1)
        bk_diag = _bucket(row - col, n_buckets)
        bk_near = _bucket(row - col + t, n_buckets)
        for mp in range(2):
            for e in range(2):
                hcol = mp * heads + 2 * hp + e
                far = tbl_ref[n_buckets - 1, hcol]
                bd = jnp.zeros((t, t), F32)
                bn = jnp.zeros((t, t), F32)
                for k in range(n_buckets - 1):
                    val = tbl_ref[k, hcol] - far
                    bd = jnp.where(bk_diag == k, val, bd)
                    bn = jnp.where(bk_near == k, val, bn)
                bdiag_ref[mp, e * t:(e + 1) * t, :] = jnp.where(col <= row, bd, NEG_INF)
                bnear_ref[mp, e * t:(e + 1) * t, :] = bn

    lane = lax.broadcasted_iota(jnp.int32, (t, 2 * dh), 1)
    first = lane < dh
    scale = dh ** -0.5
    outs = []
    for mp, (q_ref, k_ref) in enumerate(((q1_ref, k1_ref), (q2_ref, k2_ref))):
        q = q_ref[...] * scale
        qs = jnp.concatenate([jnp.where(first, q, 0.0), jnp.where(first, 0.0, q)],
                             axis=0).astype(BF16)
        m_ref[...] = jnp.full(m_ref.shape, NEG_INF, F32)
        l_ref[...] = jnp.zeros(l_ref.shape, F32)
        acc_ref[...] = jnp.zeros(acc_ref.shape, F32)

        def step(j, bias):
            start = pl.multiple_of(j * t, t)
            kb = k_ref[pl.ds(start, t), :].astype(BF16)
            s = lax.dot_general(qs, kb, _NT, preferred_element_type=F32)
            if bias is not None:
                s = s + bias
            m_prev = m_ref[...]
            m_new = jnp.maximum(m_prev, jnp.max(s, axis=1, keepdims=True))
            p = jnp.exp(s - jnp.tile(m_new, (1, t // LANES)))
            alpha = jnp.exp(m_prev - m_new)
            l_ref[...] = alpha * l_ref[...] + jnp.sum(p, axis=1, keepdims=True)
            m_ref[...] = m_new
            pb = p.astype(BF16)
            for e in range(2):
                ve = v_ref[pl.ds(start, t), e * dv:(e + 1) * dv].astype(BF16)
                rs = slice(e * t, (e + 1) * t)
                acc_ref[rs, :] = acc_ref[rs, :] * alpha[rs] + jnp.dot(
                    pb[rs], ve, preferred_element_type=F32)

        def far_body(j, carry):
            step(j, None)
            return carry

        lax.fori_loop(0, qb - 1, far_body, 0)

        @pl.when(qb > 0)
        def _():
            step(qb - 1, bnear_ref[mp])

        step(qb, bdiag_ref[mp])
        outs.append(acc_ref[...] / l_ref[...])

    o = outs[0] - lam_ref[0, 0] * outs[1]
    on = _rms(o, nw_ref[...]) * out_scale
    for e in range(2):
        o_ref[:, e * dv:(e + 1) * dv] = on[e * t:(e + 1) * t].astype(o_ref.dtype)


def _attn_prompt(u, qcol, kcol, vcol, tbl, lam, nw, batch, seqlen, heads, dh, dv, out_scale):
    assert dv == LANES and 2 * dh == LANES and heads % 2 == 0
    n_buckets = tbl.shape[0]
    t = _pick(seqlen, (256, 128))
    assert seqlen % t == 0 and t % LANES == 0 and t + 1 >= _far_start(n_buckets)
    nq = seqlen // t
    nhp = heads // 2
    qc, kc, vc = qcol // LANES, kcol // LANES, vcol // (2 * dv)
    smem = pl.BlockSpec(memory_space=pltpu.SMEM)
    kern = functools.partial(_attn_kernel, t=t, dh=dh, dv=dv, heads=heads, n_buckets=n_buckets,
                             out_scale=out_scale)
    return pl.pallas_call(
        kern,
        grid=(nhp, batch, nq),
        in_specs=[smem, smem,
                  pl.BlockSpec((1, dv), lambda h, b, i: (0, 0)),
                  pl.BlockSpec((t, LANES), lambda h, b, i: (b * nq + i, qc + h)),
                  pl.BlockSpec((t, LANES), lambda h, b, i: (b * nq + i, qc + nhp + h)),
                  pl.BlockSpec((seqlen, LANES), lambda h, b, i: (b, kc + h)),
                  pl.BlockSpec((seqlen, LANES), lambda h, b, i: (b, kc + nhp + h)),
                  pl.BlockSpec((seqlen, 2 * dv), lambda h, b, i: (b, vc + h))],
        out_specs=pl.BlockSpec((t, 2 * dv), lambda h, b, i: (b * nq + i, h)),
        out_shape=jax.ShapeDtypeStruct((batch * seqlen, heads * dv), BF16),
        scratch_shapes=[pltpu.VMEM((2, 2 * t, t), F32), pltpu.VMEM((2, 2 * t, t), F32),
                        pltpu.VMEM((2 * t, LANES), F32), pltpu.VMEM((2 * t, LANES), F32),
                        pltpu.VMEM((2 * t, dv), F32)],
        compiler_params=_params("arbitrary", "arbitrary", "arbitrary"),
        name="attn_prompt",
    )(tbl, lam, nw, u, u, u, u, u)


def _decode_kernel(pt_ref, lam_ref, trow_ref, nw_ref, q_ref, kn_ref, vn_ref, *rest,
                   pages, page, lq, heads, dh, dv, n_buckets, past_len, first_near, out_scale):
    k_refs = rest[:pages]
    v_refs = rest[pages:2 * pages]
    o_ref, m_ref, l_ref, acc_ref, s_ref = rest[2 * pages:]
    g = pl.program_id(1)
    ng = pl.num_programs(1)
    mh = 2 * heads
    rows = mh * lq

    @pl.when(g == 0)
    def _():
        m_ref[...] = jnp.full(m_ref.shape, NEG_INF, F32)
        l_ref[...] = jnp.zeros(l_ref.shape, F32)
        acc_ref[...] = jnp.zeros(acc_ref.shape, F32)

    q = q_ref[...] * (dh ** -0.5)
    qrep = jnp.tile(q, (mh, 1))
    rr = lax.broadcasted_iota(jnp.int32, (rows, mh * dh), 0)
    ll = lax.broadcasted_iota(jnp.int32, (rows, mh * dh), 1)
    qbd = jnp.where(rr // lq == ll // dh, qrep, 0.0).astype(BF16)

    def near_bias(key0, n_valid):
        r = lax.broadcasted_iota(jnp.int32, (rows, page), 0)
        col = lax.broadcasted_iota(jnp.int32, (rows, page), 1)
        dist = past_len + r % lq - (key0 + col)
        bk = _bucket(dist, n_buckets)
        far = trow_ref[:, n_buckets - 1:n_buckets]
        bias = jnp.zeros((rows, page), F32)
        for k in range(n_buckets - 1):
            bias = jnp.where(bk == k, trow_ref[:, k:k + 1] - far, bias)
        return jnp.where((dist >= 0) & (col < n_valid), bias, NEG_INF)

    def update(kb, vb):
        s = s_ref[...]
        m_prev = m_ref[...]
        m_new = jnp.maximum(m_prev, jnp.max(s, axis=1, keepdims=True))
        p = jnp.exp(s - jnp.tile(m_new, (1, page // LANES)))
        alpha = jnp.exp(m_prev - m_new)
        l_ref[...] = alpha * l_ref[...] + jnp.sum(p, axis=1, keepdims=True)
        m_ref[...] = m_new
        acc_ref[...] = (acc_ref[...] * jnp.tile(alpha, (1, heads * dv // LANES))
                        + jnp.dot(p.astype(BF16), vb, preferred_element_type=F32))

    for s_i in range(pages):
        kb = k_refs[s_i][...].astype(BF16)
        vb = v_refs[s_i][...].astype(BF16)
        s_ref[...] = lax.dot_general(qbd, kb, _NT, preferred_element_type=F32)
        last_page = past_len // page - pages + s_i
        if last_page >= first_near:
            @pl.when(g == ng - 1)
            def _(key0=last_page * page):
                s_ref[...] = s_ref[...] + near_bias(key0, page)
        update(kb, vb)

    @pl.when(g == ng - 1)
    def _():
        zk = jnp.zeros((page - lq, mh * dh), F32)
        zv = jnp.zeros((page - lq, heads * dv), F32)
        kb = jnp.concatenate([kn_ref[...], zk], axis=0).astype(BF16)
        vb = jnp.concatenate([vn_ref[...], zv], axis=0).astype(BF16)
        s_ref[...] = (lax.dot_general(qbd, kb, _NT, preferred_element_type=F32)
                      + near_bias(past_len, lq))
        update(kb, vb)
        full = acc_ref[...] / jnp.tile(l_ref[...], (1, heads * dv // LANES))
        lam = lam_ref[0, 0]
        nw = nw_ref[...]
        for h in range(heads):
            sv = slice(h * dv, (h + 1) * dv)
            o0 = full[h * lq:(h + 1) * lq, sv]
            o1 = full[(heads + h) * lq:(heads + h + 1) * lq, sv]
            o_ref[:, sv] = (_rms(o0 - lam * o1, nw) * out_scale).astype(o_ref.dtype)


def _attn_decode(u, qcol, kcol, vcol, cache_k, cache_v, page_table, tbl, lam, nw,
                 batch, lq, heads, dh, dv, out_scale):
    n_pool, page = cache_k.shape[0], cache_k.shape[1]
    n_pages = page_table.shape[1]
    past_len = n_pages * page
    mh = 2 * heads
    kw, vw = mh * dh, heads * dv
    assert page % LANES == 0 and kw % LANES == 0 and vw % LANES == 0 and lq % 8 == 0
    n_buckets = tbl.shape[0]
    far = _far_start(n_buckets)
    first_near = max(0, -(-(past_len - page + 1 - far + 1) // page))
    pages = _pick(n_pages, (8, 4, 2, 1))
    assert n_pages - pages <= first_near, "near pages must fall in the last grid step"
    ck = cache_k.reshape(n_pool, page, kw)
    cv = cache_v.reshape(n_pool, page, vw)
    trow = jnp.repeat(tbl.T, lq, axis=0)
    rows = mh * lq

    def page_spec(width, slot):
        return pl.BlockSpec((None, page, width),
                            lambda b, g, pt: (pt[b * n_pages + g * pages + slot], 0, 0))

    smem = pl.BlockSpec(memory_space=pltpu.SMEM)
    in_specs = ([smem,
                 pl.BlockSpec((rows, n_buckets), lambda b, g, pt: (0, 0)),
                 pl.BlockSpec((1, dv), lambda b, g, pt: (0, 0)),
                 pl.BlockSpec((lq, kw), lambda b, g, pt: (b, qcol // kw)),
                 pl.BlockSpec((lq, kw), lambda b, g, pt: (b, kcol // kw)),
                 pl.BlockSpec((lq, vw), lambda b, g, pt: (b, vcol // vw))]
                + [page_spec(kw, s) for s in range(pages)]
                + [page_spec(vw, s) for s in range(pages)])
    assert qcol % kw == 0 and kcol % kw == 0 and vcol % vw == 0
    kern = functools.partial(_decode_kernel, pages=pages, page=page, lq=lq, heads=heads, dh=dh,
                             dv=dv, n_buckets=n_buckets, past_len=past_len,
                             first_near=first_near, out_scale=out_scale)
    grid_spec = pltpu.PrefetchScalarGridSpec(
        num_scalar_prefetch=1,
        grid=(batch, n_pages // pages),
        in_specs=in_specs,
        out_specs=pl.BlockSpec((lq, vw), lambda b, g, pt: (b, 0)),
        scratch_shapes=[pltpu.VMEM((rows, LANES), F32), pltpu.VMEM((rows, LANES), F32),
                        pltpu.VMEM((rows, vw), F32), pltpu.VMEM((rows, page), F32)])
    return pl.pallas_call(
        kern,
        grid_spec=grid_spec,
        out_shape=jax.ShapeDtypeStruct((batch * lq, vw), BF16),
        compiler_params=_params("parallel", "arbitrary"),
        name="attn_decode",
    )(page_table.reshape(-1), lam, trow, nw, u, u, u, *([ck] * pages), *([cv] * pages))


def _merge_kernel(*refs, n_g):
    ohg_ref, oda_ref = refs[:2]
    g_refs = refs[2:2 + 2 * n_g]
    x_ref, wbh_ref, wbd_ref, wo_ref, nw_ref, o_ref = refs[2 + 2 * n_g:]
    g_hg = jnp.concatenate([r[...] for r in g_refs[:n_g]], axis=1)
    g_da = jnp.concatenate([r[...] for r in g_refs[n_g:]], axis=1)
    y_hg = jnp.dot(ohg_ref[...], wbh_ref[...], preferred_element_type=F32)
    y_da = jnp.dot(oda_ref[...], wbd_ref[...], preferred_element_type=F32)
    mixed = (_sigmoid(g_hg) * y_hg + _sigmoid(g_da) * y_da).astype(BF16)
    z = jnp.dot(mixed, wo_ref[...], preferred_element_type=F32)
    o_ref[...] = x_ref[...] + _rms(z, nw_ref[...])


def _merge(o_hg, o_da, u, gcol, x, wbh, wbd, wo, nw):
    m, d = x.shape
    w = o_hg.shape[1]
    gw = math.gcd(gcol, d)
    n_g = d // gw
    assert gw % LANES == 0
    tm = _pick(m, (256, 128, 64, 8))
    const = lambda i: (0, 0)
    g_specs = [pl.BlockSpec((tm, gw), functools.partial(lambda i, c: (i, c), c=gcol // gw + k))
               for k in range(2 * n_g)]
    return pl.pallas_call(
        functools.partial(_merge_kernel, n_g=n_g),
        grid=(m // tm,),
        in_specs=[pl.BlockSpec((tm, w), lambda i: (i, 0)),
                  pl.BlockSpec((tm, w), lambda i: (i, 0))]
                 + g_specs
                 + [pl.BlockSpec((tm, d), lambda i: (i, 0)),
                    pl.BlockSpec((w, d), const, pipeline_mode=pl.Buffered(1)),
                    pl.BlockSpec((w, d), const, pipeline_mode=pl.Buffered(1)),
                    pl.BlockSpec((d, d), const, pipeline_mode=pl.Buffered(1)),
                    pl.BlockSpec((1, d), const)],
        out_specs=pl.BlockSpec((tm, d), lambda i: (i, 0)),
        out_shape=jax.ShapeDtypeStruct((m, d), F32),
        compiler_params=_params("parallel"),
        name="merge",
    )(o_hg, o_da, *([u] * (2 * n_g)), x, wbh, wbd, wo, nw)


def _ffn_kernel(x_ref, npre_ref, wg_ref, wu_ref, wd_ref, npost_ref, o_ref, h_ref, acc_ref):
    j = pl.program_id(1)

    @pl.when(j == 0)
    def _():
        h_ref[...] = _rms(x_ref[...], npre_ref[...]).astype(BF16)

    h = h_ref[...]
    gate = jnp.dot(h, wg_ref[...], preferred_element_type=F32)
    up = jnp.dot(h, wu_ref[...], preferred_element_type=F32)
    act = (gate * _sigmoid(gate) * up).astype(BF16)
    part = jnp.dot(act, wd_ref[...], preferred_element_type=F32)

    @pl.when(j == 0)
    def _():
        acc_ref[...] = part

    @pl.when(j > 0)
    def _():
        acc_ref[...] = acc_ref[...] + part

    @pl.when(j == pl.num_programs(1) - 1)
    def _():
        o_ref[...] = x_ref[...] + _rms(acc_ref[...], npost_ref[...])


def _ffn(x, npre, w_up, w_down, npost):
    m, d = x.shape
    ff = w_down.shape[0]
    tm = _pick(m, (512, 256, 128, 64, 8))
    tf = _pick(ff, (512, 256, 128))
    nf = ff // tf
    return pl.pallas_call(
        _ffn_kernel,
        grid=(m // tm, nf),
        in_specs=[pl.BlockSpec((tm, d), lambda i, j: (i, 0)),
                  pl.BlockSpec((1, d), lambda i, j: (0, 0)),
                  pl.BlockSpec((d, tf), lambda i, j: (0, j)),
                  pl.BlockSpec((d, tf), lambda i, j: (0, nf + j)),
                  pl.BlockSpec((tf, d), lambda i, j: (j, 0)),
                  pl.BlockSpec((1, d), lambda i, j: (0, 0))],
        out_specs=pl.BlockSpec((tm, d), lambda i, j: (i, 0)),
        out_shape=jax.ShapeDtypeStruct((m, d), F32),
        scratch_shapes=[pltpu.VMEM((tm, d), BF16), pltpu.VMEM((tm, d), F32)],
        compiler_params=_params("parallel", "arbitrary"),
        name="ffn",
    )(x, npre, w_up, w_up, w_down, npost)


def _layer(x, s0, paged, lw, dims):
    batch, seqlen, d = x.shape
    hg_heads, hg_dk, hg_dv, da_heads, da_dh, da_dv = dims
    hg_w, da_qk, da_w = hg_heads * hg_dk, 2 * da_heads * da_dh, da_heads * da_dv
    cols = np.cumsum([0, hg_w, hg_w, hg_heads * hg_dv, hg_heads * hg_dv, da_qk, da_qk, da_w, d, d])
    qcol, kcol, vcol, gcol = int(cols[4]), int(cols[5]), int(cols[6]), int(cols[7])
    x2 = x.reshape(batch * seqlen, d)

    u = _norm_proj(x2, lw["norm_mix_pre"], lw["w_in"])
    o_hg, s_new = _hgrn(u, 0, lw["lb"], lw["hg_norm_w"], s0, batch, seqlen, hg_heads, hg_dk, hg_dv)
    if paged is None:
        o_da = _attn_prompt(u, qcol, kcol, vcol, lw["tbl"], lw["lam"], lw["da_subln_w"],
                            batch, seqlen, da_heads, da_dh, da_dv, lw["out_scale"])
    else:
        o_da = _attn_decode(u, qcol, kcol, vcol, paged[0], paged[1], paged[2], lw["tbl"],
                            lw["lam"], lw["da_subln_w"], batch, seqlen, da_heads, da_dh, da_dv,
                            lw["out_scale"])
    x1 = _merge(o_hg, o_da, u, gcol, x2, lw["w_branch_hg"], lw["w_branch_da"], lw["w_out"],
                lw["norm_mix_post"])
    y = _ffn(x1, lw["norm_ffn_pre"], lw["w_ffn_up"], lw["w_ffn_down"], lw["norm_ffn_post"])
    k = u[:, kcol:kcol + da_qk].reshape(batch, seqlen, 2 * da_heads, da_dh)
    v = u[:, vcol:vcol + da_w].reshape(batch, seqlen, da_heads, da_dv)
    return y.reshape(batch, seqlen, d), k, v, s_new


def kernel(x_prompt, x_sample, cache_k, cache_v, state_hgrn, page_table, norm_mix_pre, norm_mix_post, norm_ffn_pre, norm_ffn_post, w_in, hg_lb_logits, hg_norm_w, da_lambda_q1, da_lambda_k1, da_lambda_q2, da_lambda_k2, da_subln_w, rel_bias_table, w_branch_hg, w_branch_da, w_out, w_ffn_up, w_ffn_down):
    depth = w_in.shape[0]
    _, _, hg_heads, hg_dk, hg_dv = state_hgrn.shape
    da_heads, da_dv = cache_v.shape[3], cache_v.shape[4]
    da_dh = cache_k.shape[4]
    dims = (hg_heads, hg_dk, hg_dv, da_heads, da_dh, da_dv)
    lb_all = jnp.cumsum(jax.nn.softmax(hg_lb_logits.astype(F32), axis=0), axis=0)

    y_p, y_s = x_prompt, x_sample
    outs = [[] for _ in range(6)]
    for l in range(depth):
        lam_init = 0.8 - 0.6 * math.exp(-0.3 * l)
        lam = (jnp.exp(jnp.sum(da_lambda_q1[l] * da_lambda_k1[l]))
               - jnp.exp(jnp.sum(da_lambda_q2[l] * da_lambda_k2[l])) + lam_init)
        lw = {
            "norm_mix_pre": norm_mix_pre[l][None], "norm_mix_post": norm_mix_post[l][None],
            "norm_ffn_pre": norm_ffn_pre[l][None], "norm_ffn_post": norm_ffn_post[l][None],
            "w_in": w_in[l].astype(BF16), "lb": lb_all[l][None], "hg_norm_w": hg_norm_w[l][None],
            "da_subln_w": da_subln_w[l][None], "tbl": rel_bias_table.astype(F32),
            "lam": lam.reshape(1, 1).astype(F32), "out_scale": 1.0 - lam_init,
            "w_branch_hg": w_branch_hg[l].astype(BF16), "w_branch_da": w_branch_da[l].astype(BF16),
            "w_out": w_out[l].astype(BF16), "w_ffn_up": w_ffn_up[l].astype(BF16),
            "w_ffn_down": w_ffn_down[l].astype(BF16),
        }
        y_p, kp, vp, sp = _layer(y_p, None, None, lw, dims)
        y_s, ks, vs, ss = _layer(y_s, state_hgrn[l], (cache_k[l], cache_v[l], page_table), lw, dims)
        for acc, val in zip(outs, (kp, vp, sp, ks, vs, ss)):
            acc.append(val)
    return (y_p, y_s) + tuple(jnp.stack(o) for o in outs)
```

```python
import functools
import math

import jax
import jax.numpy as jnp
import numpy as np
from jax import lax
from jax.experimental import pallas as pl
from jax.experimental.pallas import tpu as pltpu

F32 = jnp.float32
BF16 = jnp.bfloat16

RMS_EPS = 1e-6
NEG_INF = -1e30
HG_CHUNK = 64
REL_MAX_DISTANCE = 128
LANES = 128
VMEM_LIMIT = 56 * 1024 * 1024

_NT = (((1,), (1,)), ((), ()))
_TN = (((0,), (0,)), ((), ()))


def _params(*sem):
    return pltpu.CompilerParams(dimension_semantics=sem, vmem_limit_bytes=VMEM_LIMIT)


def _rms(x, w):
    return x * lax.rsqrt(jnp.mean(x * x, axis=-1, keepdims=True) + RMS_EPS) * w


def _sigmoid(x):
    return 1.0 / (1.0 + jnp.exp(-x))


def _pick(n, prefs):
    for p in prefs:
        if n % p == 0:
            return p
    return n


def _norm_proj_kernel(x_ref, nw_ref, w_ref, o_ref, h_ref):
    @pl.when(pl.program_id(1) == 0)
    def _():
        h_ref[...] = _rms(x_ref[...], nw_ref[...]).astype(BF16)

    o_ref[...] = jnp.dot(h_ref[...], w_ref[...], preferred_element_type=F32)


def _norm_proj(x, nw, w_bf):
    m, d = x.shape
    n = w_bf.shape[1]
    tm = _pick(m, (1024, 512, 256, 128, 64, 8))
    tn = _pick(n, (1024, 512, 256, 128))
    return pl.pallas_call(
        _norm_proj_kernel,
        grid=(m // tm, n // tn),
        in_specs=[pl.BlockSpec((tm, d), lambda i, j: (i, 0)),
                  pl.BlockSpec((1, d), lambda i, j: (0, 0)),
                  pl.BlockSpec((d, tn), lambda i, j: (0, j))],
        out_specs=pl.BlockSpec((tm, tn), lambda i, j: (i, j)),
        out_shape=jax.ShapeDtypeStruct((m, n), F32),
        scratch_shapes=[pltpu.VMEM((tm, d), BF16)],
        compiler_params=_params("parallel", "arbitrary"),
        name="norm_proj",
    )(x, nw, w_bf)


def _hgrn_kernel(*refs, heads, dk, dv, chunk, rows, has_s0):
    if has_s0:
        q_ref, f_ref, i_ref, gate_ref, lb_ref, nw_ref, s0_ref, o_ref, sout_ref, st_ref = refs
    else:
        q_ref, f_ref, i_ref, gate_ref, lb_ref, nw_ref, o_ref, sout_ref, st_ref = refs
    t = pl.program_id(1)
    real = q_ref.shape[0]

    @pl.when(t == 0)
    def _():
        for h in range(heads):
            if has_s0:
                st_ref[h] = s0_ref[h].T
            else:
                st_ref[h] = jnp.zeros((dv, dk), F32)

    lb = lb_ref[...]
    f = lb + (1.0 - lb) * _sigmoid(f_ref[...])
    g = jnp.log(f)
    kin = 1.0 - f
    q = q_ref[...]
    v = i_ref[...]
    if rows > real:
        def pad(a):
            return jnp.concatenate([a, jnp.zeros((rows - real, a.shape[1]), F32)], axis=0)
        g, kin, q, v = pad(g), pad(kin), pad(q), pad(v)

    r = lax.broadcasted_iota(jnp.int32, (rows, rows), 0)
    c = lax.broadcasted_iota(jnp.int32, (rows, rows), 1)
    tri = jnp.where((r // chunk == c // chunk) & (c <= r), 1.0, 0.0).astype(BF16)
    g_hi = g.astype(BF16)
    g_r1 = g - g_hi.astype(F32)
    g_mid = g_r1.astype(BF16)
    g_lo = (g_r1 - g_mid.astype(F32)).astype(BF16)
    G = (jnp.dot(tri, g_hi, preferred_element_type=F32)
         + jnp.dot(tri, g_mid, preferred_element_type=F32)
         + jnp.dot(tri, g_lo, preferred_element_type=F32))

    qg = (q * jnp.exp(G)).astype(BF16)
    kg = (kin * jnp.exp(-G)).astype(BF16)
    vb = v.astype(BF16)
    cr = lax.broadcasted_iota(jnp.int32, (chunk, chunk), 0)
    cc = lax.broadcasted_iota(jnp.int32, (chunk, chunk), 1)
    causal = cc <= cr
    nw = nw_ref[...]

    for ci in range(rows // chunk):
        lo = ci * chunk
        Gc = G[lo:lo + chunk]
        Gl = Gc[chunk - 1:chunk]
        kdec = (kin[lo:lo + chunk] * jnp.exp(Gl - Gc)).astype(BF16)
        decay = jnp.exp(Gl)
        n_out = min(chunk, real - lo)
        for h in range(heads):
            sk = slice(h * dk, (h + 1) * dk)
            sv = slice(h * dv, (h + 1) * dv)
            qg_h = qg[lo:lo + chunk, sk]
            v_h = vb[lo:lo + chunk, sv]
            a = lax.dot_general(qg_h, kg[lo:lo + chunk, sk], _NT, preferred_element_type=F32)
            a = jnp.where(causal, a, 0.0).astype(BF16)
            st = st_ref[h]
            o = (jnp.dot(a, v_h, preferred_element_type=F32)
                 + lax.dot_general(qg_h, st.astype(BF16), _NT, preferred_element_type=F32))
            st_ref[h] = st * decay[:, sk] + lax.dot_general(v_h, kdec[:, sk], _TN,
                                                            preferred_element_type=F32)
            if n_out > 0:
                gt = gate_ref[lo:lo + n_out, sv]
                on = _rms(o[:n_out], nw) * (gt * _sigmoid(gt))
                o_ref[lo:lo + n_out, sv] = on.astype(o_ref.dtype)

    @pl.when(t == pl.num_programs(1) - 1)
    def _():
        for h in range(heads):
            sout_ref[h] = st_ref[h].T


def _hgrn(u, col0, lb, nw, s0, batch, seqlen, heads, dk, dv):
    width = heads * dk
    assert dk == dv and col0 % width == 0
    cb = col0 // width
    chunk = min(HG_CHUNK, seqlen)
    if seqlen >= LANES:
        tb = _pick(seqlen, (256, 128))
        rows = tb
        assert tb % chunk == 0
    else:
        tb = seqlen
        rows = LANES
        chunk = LANES
    nt = seqlen // tb
    has_s0 = s0 is not None

    def col(k):
        return pl.BlockSpec((tb, width), lambda b, t: (b * nt + t, cb + k))

    in_specs = [col(0), col(1), col(2), col(3),
                pl.BlockSpec((1, width), lambda b, t: (0, 0)),
                pl.BlockSpec((1, dv), lambda b, t: (0, 0))]
    args = [u, u, u, u, lb, nw]
    if has_s0:
        in_specs.append(pl.BlockSpec((None, heads, dk, dv), lambda b, t: (b, 0, 0, 0)))
        args.append(s0)
    kern = functools.partial(_hgrn_kernel, heads=heads, dk=dk, dv=dv, chunk=chunk, rows=rows,
                             has_s0=has_s0)
    return pl.pallas_call(
        kern,
        grid=(batch, nt),
        in_specs=in_specs,
        out_specs=[pl.BlockSpec((tb, heads * dv), lambda b, t: (b * nt + t, 0)),
                   pl.BlockSpec((None, heads, dk, dv), lambda b, t: (b, 0, 0, 0))],
        out_shape=[jax.ShapeDtypeStruct((batch * seqlen, heads * dv), BF16),
                   jax.ShapeDtypeStruct((batch, heads, dk, dv), F32)],
        scratch_shapes=[pltpu.VMEM((heads, dv, dk), F32)],
        compiler_params=_params("parallel", "arbitrary"),
        name="hgrn",
    )(*args)


def _bucket(dist, n_buckets):
    n = jnp.maximum(dist, 0)
    max_exact = n_buckets // 2
    nf = jnp.maximum(n, 1).astype(F32)
    large = max_exact + (jnp.log(nf / max_exact) / math.log(REL_MAX_DISTANCE / max_exact)
                         * (n_buckets - max_exact)).astype(jnp.int32)
    large = jnp.minimum(large, n_buckets - 1)
    return jnp.where(n < max_exact, n, large)


def _far_start(n_buckets):
    n = np.arange(1, 8 * REL_MAX_DISTANCE, dtype=np.int64)
    max_exact = n_buckets // 2
    nf = n.astype(np.float32)
    large = max_exact + (np.log(nf / np.float32(max_exact)) / np.float32(math.log(REL_MAX_DISTANCE / max_exact))
                         * np.float32(n_buckets - max_exact)).astype(np.int32)
    b = np.where(n < max_exact, n, np.minimum(large, n_buckets - 1))
    below = n[b < n_buckets - 1]
    return int(below.max()) + 2


def _attn_kernel(tbl_ref, lam_ref, nw_ref, q1_ref, q2_ref, k1_ref, k2_ref, v_ref, o_ref,
                 bnear_ref, bdiag_ref, m_ref, l_ref, acc_ref, *,
                 t, dh, dv, heads, n_buckets, out_scale):
    hp = pl.program_id(0)
    b = pl.program_id(1)
    qb = pl.program_id(2)

    @pl.when((b == 0) & (qb == 0))
    def _():
        row = lax.broadcasted_iota(jnp.int32, (t, t), 0)
        col = lax.broadcasted_iota(jnp.int32, (t, t), 1)
        bk_diag = _bucket(row - col, n_buckets)
        bk_near = _bucket(row - col + t, n_buckets)
        for mp in range(2):
            for e in range(2):
                hcol = mp * heads + 2 * hp + e
                far = tbl_ref[n_buckets - 1, hcol]
                bd = jnp.zeros((t, t), F32)
                bn = jnp.zeros((t, t), F32)
                for k in range(n_buckets - 1):
                    val = tbl_ref[k, hcol] - far
                    bd = jnp.where(bk_diag == k, val, bd)
                    bn = jnp.where(bk_near == k, val, bn)
                bdiag_ref[mp, e * t:(e + 1) * t, :] = jnp.where(col <= row, bd, NEG_INF)
                bnear_ref[mp, e * t:(e + 1) * t, :] = bn

    lane = lax.broadcasted_iota(jnp.int32, (t, 2 * dh), 1)
    first = lane < dh
    scale = dh ** -0.5
    outs = []
    for mp, (q_ref, k_ref) in enumerate(((q1_ref, k1_ref), (q2_ref, k2_ref))):
        q = q_ref[...] * scale
        qs = jnp.concatenate([jnp.where(first, q, 0.0), jnp.where(first, 0.0, q)],
                             axis=0).astype(BF16)
        m_ref[...] = jnp.full(m_ref.shape, NEG_INF, F32)
        l_ref[...] = jnp.zeros(l_ref.shape, F32)
        acc_ref[...] = jnp.zeros(acc_ref.shape, F32)

        def step(j, bias):
            start = pl.multiple_of(j * t, t)
            kb = k_ref[pl.ds(start, t), :].astype(BF16)
            s = lax.dot_general(qs, kb, _NT, preferred_element_type=F32)
            if bias is not None:
                s = s + bias
            m_prev = m_ref[...]
            m_new = jnp.maximum(m_prev, jnp.max(s, axis=1, keepdims=True))
            p = jnp.exp(s - jnp.tile(m_new, (1, t // LANES)))
            alpha = jnp.exp(m_prev - m_new)
            l_ref[...] = alpha * l_ref[...] + jnp.sum(p, axis=1, keepdims=True)
            m_ref[...] = m_new
            pb = p.astype(BF16)
            for e in range(2):
                ve = v_ref[pl.ds(start, t), e * dv:(e + 1) * dv].astype(BF16)
                rs = slice(e * t, (e + 1) * t)
                acc_ref[rs, :] = acc_ref[rs, :] * alpha[rs] + jnp.dot(
                    pb[rs], ve, preferred_element_type=F32)

        def far_body(j, carry):
            step(j, None)
            return carry

        lax.fori_loop(0, qb - 1, far_body, 0)

        @pl.when(qb > 0)
        def _():
            step(qb - 1, bnear_ref[mp])

        step(qb, bdiag_ref[mp])
        outs.append(acc_ref[...] / l_ref[...])

    o = outs[0] - lam_ref[0, 0] * outs[1]
    on = _rms(o, nw_ref[...]) * out_scale
    for e in range(2):
        o_ref[:, e * dv:(e + 1) * dv] = on[e * t:(e + 1) * t].astype(o_ref.dtype)


def _attn_prompt(u, qcol, kcol, vcol, tbl, lam, nw, batch, seqlen, heads, dh, dv, out_scale):
    assert dv == LANES and 2 * dh == LANES and heads % 2 == 0
    n_buckets = tbl.shape[0]
    t = _pick(seqlen, (256, 128))
    assert seqlen % t == 0 and t % LANES == 0 and t + 1 >= _far_start(n_buckets)
    nq = seqlen // t
    nhp = heads // 2
    qc, kc, vc = qcol // LANES, kcol // LANES, vcol // (2 * dv)
    smem = pl.BlockSpec(memory_space=pltpu.SMEM)
    kern = functools.partial(_attn_kernel, t=t, dh=dh, dv=dv, heads=heads, n_buckets=n_buckets,
                             out_scale=out_scale)
    return pl.pallas_call(
        kern,
        grid=(nhp, batch, nq),
        in_specs=[smem, smem,
                  pl.BlockSpec((1, dv), lambda h, b, i: (0, 0)),
                  pl.BlockSpec((t, LANES), lambda h, b, i: (b * nq + i, qc + h)),
                  pl.BlockSpec((t, LANES), lambda h, b, i: (b * nq + i, qc + nhp + h)),
                  pl.BlockSpec((seqlen, LANES), lambda h, b, i: (b, kc + h)),
                  pl.BlockSpec((seqlen, LANES), lambda h, b, i: (b, kc + nhp + h)),
                  pl.BlockSpec((seqlen, 2 * dv), lambda h, b, i: (b, vc + h))],
        out_specs=pl.BlockSpec((t, 2 * dv), lambda h, b, i: (b * nq + i, h)),
        out_shape=jax.ShapeDtypeStruct((batch * seqlen, heads * dv), BF16),
        scratch_shapes=[pltpu.VMEM((2, 2 * t, t), F32), pltpu.VMEM((2, 2 * t, t), F32),
                        pltpu.VMEM((2 * t, LANES), F32), pltpu.VMEM((2 * t, LANES), F32),
                        pltpu.VMEM((2 * t, dv), F32)],
        compiler_params=_params("arbitrary", "arbitrary", "arbitrary"),
        name="attn_prompt",
    )(tbl, lam, nw, u, u, u, u, u)


def _decode_kernel(pt_ref, lam_ref, trow_ref, nw_ref, q_ref, kn_ref, vn_ref, *rest,
                   pages, page, lq, heads, dh, dv, n_buckets, past_len, near_slots, out_scale):
    k_refs = rest[:pages]
    v_refs = rest[pages:2 * pages]
    o_ref, m_ref, l_ref, acc_ref, bias_ref = rest[2 * pages:]
    b = pl.program_id(0)
    g = pl.program_id(1)
    ng = pl.num_programs(1)
    mh = 2 * heads
    rows = mh * lq
    n_near = len(near_slots)
    order = [m * heads + h for h in range(heads) for m in range(2)]

    def near_bias(key0, n_valid):
        r = lax.broadcasted_iota(jnp.int32, (rows, page), 0)
        col = lax.broadcasted_iota(jnp.int32, (rows, page), 1)
        dist = past_len + r % lq - (key0 + col)
        bk = _bucket(dist, n_buckets)
        far = trow_ref[:, n_buckets - 1:n_buckets]
        bias = jnp.zeros((rows, page), F32)
        for k in range(n_buckets - 1):
            bias = jnp.where(bk == k, trow_ref[:, k:k + 1] - far, bias)
        return jnp.where((dist >= 0) & (col < n_valid), bias, NEG_INF)

    @pl.when((b == 0) & (g == 0))
    def _():
        for i, slot in enumerate(near_slots):
            bias_ref[i] = near_bias(past_len - (pages - slot) * page, page)
        bias_ref[n_near] = near_bias(past_len, lq)

    @pl.when(g == 0)
    def _():
        m_ref[...] = jnp.full(m_ref.shape, NEG_INF, F32)
        l_ref[...] = jnp.zeros(l_ref.shape, F32)
        acc_ref[...] = jnp.zeros(acc_ref.shape, F32)

    q = q_ref[...] * (dh ** -0.5)
    q_parts = [q[:, j * dh:(j + 1) * dh].astype(BF16) for j in order]

    def attend(key_rows, value_rows, bias):
        s = jnp.concatenate(
            [lax.dot_general(qj, key_rows(j).astype(BF16), _NT, preferred_element_type=F32)
             for qj, j in zip(q_parts, order)], axis=0)
        if bias is not None:
            s = s + bias
        m_prev = m_ref[...]
        m_new = jnp.maximum(m_prev, jnp.max(s, axis=1, keepdims=True))
        p = jnp.exp(s - jnp.tile(m_new, (1, s.shape[1] // LANES)))
        alpha = jnp.exp(m_prev - m_new)
        l_ref[...] = alpha * l_ref[...] + jnp.sum(p, axis=1, keepdims=True)
        m_ref[...] = m_new
        pv = jnp.concatenate(
            [jnp.dot(p[2 * lq * h:2 * lq * (h + 1)].astype(BF16), value_rows(h).astype(BF16),
                     preferred_element_type=F32) for h in range(heads)], axis=0)
        acc_ref[...] = acc_ref[...] * alpha + pv

    is_last = g == ng - 1
    bias = None
    if near_slots:
        zero = jnp.zeros((rows, page), F32)
        bias = jnp.concatenate(
            [jnp.where(is_last, bias_ref[near_slots.index(s)], 0.0) if s in near_slots else zero
             for s in range(pages)], axis=1)
    attend(lambda j: jnp.concatenate([r[pl.ds(j, page, stride=mh), :] for r in k_refs], axis=0),
           lambda h: jnp.concatenate([r[pl.ds(h, page, stride=heads), :] for r in v_refs], axis=0),
           bias)

    @pl.when(is_last)
    def _():
        kn = kn_ref[...]
        vn = vn_ref[...]
        zk = jnp.zeros((page - lq, dh), F32)
        zv = jnp.zeros((page - lq, dv), F32)
        attend(lambda j: jnp.concatenate([kn[:, j * dh:(j + 1) * dh], zk], axis=0),
               lambda h: jnp.concatenate([vn[:, h * dv:(h + 1) * dv], zv], axis=0),
               bias_ref[n_near])
        full = acc_ref[...] / l_ref[...]
        lam = lam_ref[0, 0]
        nw = nw_ref[...]
        for h in range(heads):
            o0 = full[2 * lq * h:2 * lq * h + lq]
            o1 = full[2 * lq * h + lq:2 * lq * (h + 1)]
            o_ref[:, h * dv:(h + 1) * dv] = (_rms(o0 - lam * o1, nw) * out_scale).astype(o_ref.dtype)


def _attn_decode(u, qcol, kcol, vcol, cache_k, cache_v, page_table, tbl, lam, nw,
                 batch, lq, heads, dh, dv, out_scale):
    n_pool, page = cache_k.shape[0], cache_k.shape[1]
    n_pages = page_table.shape[1]
    past_len = n_pages * page
    mh = 2 * heads
    kw, vw = mh * dh, heads * dv
    assert page % LANES == 0 and kw % LANES == 0 and vw % LANES == 0 and lq % 8 == 0
    n_buckets = tbl.shape[0]
    far = _far_start(n_buckets)
    first_near = max(0, -(-(past_len - page + 1 - far + 1) // page))
    pages = _pick(n_pages, (8, 4, 2, 1))
    assert n_pages - pages <= first_near, "near pages must fall in the last grid step"
    near_slots = tuple(s for s in range(pages) if n_pages - pages + s >= first_near)
    ck = cache_k.reshape(n_pool * page * mh, dh)
    cv = cache_v.reshape(n_pool * page * heads, dv)
    trow = jnp.repeat(tbl.T.reshape(2, heads, -1).transpose(1, 0, 2).reshape(mh, -1), lq, axis=0)
    rows = mh * lq

    def page_spec(n_heads, width, slot):
        return pl.BlockSpec((page * n_heads, width),
                            lambda b, g, pt: (pt[b * n_pages + g * pages + slot], 0))

    smem = pl.BlockSpec(memory_space=pltpu.SMEM)
    in_specs = ([smem,
                 pl.BlockSpec((rows, n_buckets), lambda b, g, pt: (0, 0)),
                 pl.BlockSpec((1, dv), lambda b, g, pt: (0, 0)),
                 pl.BlockSpec((lq, kw), lambda b, g, pt: (b, qcol // kw)),
                 pl.BlockSpec((lq, kw), lambda b, g, pt: (b, kcol // kw)),
                 pl.BlockSpec((lq, vw), lambda b, g, pt: (b, vcol // vw))]
                + [page_spec(mh, dh, s) for s in range(pages)]
                + [page_spec(heads, dv, s) for s in range(pages)])
    assert qcol % kw == 0 and kcol % kw == 0 and vcol % vw == 0
    kern = functools.partial(_decode_kernel, pages=pages, page=page, lq=lq, heads=heads, dh=dh,
                             dv=dv, n_buckets=n_buckets, past_len=past_len,
                             near_slots=near_slots, out_scale=out_scale)
    grid_spec = pltpu.PrefetchScalarGridSpec(
        num_scalar_prefetch=1,
        grid=(batch, n_pages // pages),
        in_specs=in_specs,
        out_specs=pl.BlockSpec((lq, vw), lambda b, g, pt: (b, 0)),
        scratch_shapes=[pltpu.VMEM((rows, LANES), F32), pltpu.VMEM((rows, LANES), F32),
                        pltpu.VMEM((rows, dv), F32),
                        pltpu.VMEM((len(near_slots) + 1, rows, page), F32)])
    return pl.pallas_call(
        kern,
        grid_spec=grid_spec,
        out_shape=jax.ShapeDtypeStruct((batch * lq, vw), BF16),
        compiler_params=_params("arbitrary", "arbitrary"),
        name="attn_decode",
    )(page_table.reshape(-1), lam, trow, nw, u, u, u, *([ck] * pages), *([cv] * pages))


def _merge_kernel(*refs, n_g):
    ohg_ref, oda_ref = refs[:2]
    g_refs = refs[2:2 + 2 * n_g]
    x_ref, wbh_ref, wbd_ref, wo_ref, nw_ref, o_ref = refs[2 + 2 * n_g:]
    g_hg = jnp.concatenate([r[...] for r in g_refs[:n_g]], axis=1)
    g_da = jnp.concatenate([r[...] for r in g_refs[n_g:]], axis=1)
    y_hg = jnp.dot(ohg_ref[...], wbh_ref[...], preferred_element_type=F32)
    y_da = jnp.dot(oda_ref[...], wbd_ref[...], preferred_element_type=F32)
    mixed = (_sigmoid(g_hg) * y_hg + _sigmoid(g_da) * y_da).astype(BF16)
    z = jnp.dot(mixed, wo_ref[...], preferred_element_type=F32)
    o_ref[...] = x_ref[...] + _rms(z, nw_ref[...])


def _merge(o_hg, o_da, u, gcol, x, wbh, wbd, wo, nw):
    m, d = x.shape
    w = o_hg.shape[1]
    gw = math.gcd(gcol, d)
    n_g = d // gw
    assert gw % LANES == 0
    tm = _pick(m, (256, 128, 64, 8))
    const = lambda i: (0, 0)
    g_specs = [pl.BlockSpec((tm, gw), functools.partial(lambda i, c: (i, c), c=gcol // gw + k))
               for k in range(2 * n_g)]
    return pl.pallas_call(
        functools.partial(_merge_kernel, n_g=n_g),
        grid=(m // tm,),
        in_specs=[pl.BlockSpec((tm, w), lambda i: (i, 0)),
                  pl.BlockSpec((tm, w), lambda i: (i, 0))]
                 + g_specs
                 + [pl.BlockSpec((tm, d), lambda i: (i, 0)),
                    pl.BlockSpec((w, d), const, pipeline_mode=pl.Buffered(1)),
                    pl.BlockSpec((w, d), const, pipeline_mode=pl.Buffered(1)),
                    pl.BlockSpec((d, d), const, pipeline_mode=pl.Buffered(1)),
                    pl.BlockSpec((1, d), const)],
        out_specs=pl.BlockSpec((tm, d), lambda i: (i, 0)),
        out_shape=jax.ShapeDtypeStruct((m, d), F32),
        compiler_params=_params("parallel"),
        name="merge",
    )(o_hg, o_da, *([u] * (2 * n_g)), x, wbh, wbd, wo, nw)


def _ffn_kernel(x_ref, npre_ref, wg_ref, wu_ref, wd_ref, npost_ref, o_ref, h_ref, acc_ref):
    j = pl.program_id(1)

    @pl.when(j == 0)
    def _():
        h_ref[...] = _rms(x_ref[...], npre_ref[...]).astype(BF16)

    h = h_ref[...]
    gate = jnp.dot(h, wg_ref[...], preferred_element_type=F32)
    up = jnp.dot(h, wu_ref[...], preferred_element_type=F32)
    act = (gate * _sigmoid(gate) * up).astype(BF16)
    part = jnp.dot(act, wd_ref[...], preferred_element_type=F32)

    @pl.when(j == 0)
    def _():
        acc_ref[...] = part

    @pl.when(j > 0)
    def _():
        acc_ref[...] = acc_ref[...] + part

    @pl.when(j == pl.num_programs(1) - 1)
    def _():
        o_ref[...] = x_ref[...] + _rms(acc_ref[...], npost_ref[...])


def _ffn(x, npre, w_up, w_down, npost):
    m, d = x.shape
    ff = w_down.shape[0]
    tm = _pick(m, (512, 256, 128, 64, 8))
    tf = _pick(ff, (512, 256, 128))
    nf = ff // tf
    return pl.pallas_call(
        _ffn_kernel,
        grid=(m // tm, nf),
        in_specs=[pl.BlockSpec((tm, d), lambda i, j: (i, 0)),
                  pl.BlockSpec((1, d), lambda i, j: (0, 0)),
                  pl.BlockSpec((d, tf), lambda i, j: (0, j)),
                  pl.BlockSpec((d, tf), lambda i, j: (0, nf + j)),
                  pl.BlockSpec((tf, d), lambda i, j: (j, 0)),
                  pl.BlockSpec((1, d), lambda i, j: (0, 0))],
        out_specs=pl.BlockSpec((tm, d), lambda i, j: (i, 0)),
        out_shape=jax.ShapeDtypeStruct((m, d), F32),
        scratch_shapes=[pltpu.VMEM((tm, d), BF16), pltpu.VMEM((tm, d), F32)],
        compiler_params=_params("parallel", "arbitrary"),
        name="ffn",
    )(x, npre, w_up, w_up, w_down, npost)


def _layer(x, s0, paged, lw, dims):
    batch, seqlen, d = x.shape
    hg_heads, hg_dk, hg_dv, da_heads, da_dh, da_dv = dims
    hg_w, da_qk, da_w = hg_heads * hg_dk, 2 * da_heads * da_dh, da_heads * da_dv
    cols = np.cumsum([0, hg_w, hg_w, hg_heads * hg_dv, hg_heads * hg_dv, da_qk, da_qk, da_w, d, d])
    qcol, kcol, vcol, gcol = int(cols[4]), int(cols[5]), int(cols[6]), int(cols[7])
    x2 = x.reshape(batch * seqlen, d)

    u = _norm_proj(x2, lw["norm_mix_pre"], lw["w_in"])
    o_hg, s_new = _hgrn(u, 0, lw["lb"], lw["hg_norm_w"], s0, batch, seqlen, hg_heads, hg_dk, hg_dv)
    if paged is None:
        o_da = _attn_prompt(u, qcol, kcol, vcol, lw["tbl"], lw["lam"], lw["da_subln_w"],
                            batch, seqlen, da_heads, da_dh, da_dv, lw["out_scale"])
    else:
        o_da = _attn_decode(u, qcol, kcol, vcol, paged[0], paged[1], paged[2], lw["tbl"],
                            lw["lam"], lw["da_subln_w"], batch, seqlen, da_heads, da_dh, da_dv,
                            lw["out_scale"])
    x1 = _merge(o_hg, o_da, u, gcol, x2, lw["w_branch_hg"], lw["w_branch_da"], lw["w_out"],
                lw["norm_mix_post"])
    y = _ffn(x1, lw["norm_ffn_pre"], lw["w_ffn_up"], lw["w_ffn_down"], lw["norm_ffn_post"])
    k = u[:, kcol:kcol + da_qk].reshape(batch, seqlen, 2 * da_heads, da_dh)
    v = u[:, vcol:vcol + da_w].reshape(batch, seqlen, da_heads, da_dv)
    return y.reshape(batch, seqlen, d), k, v, s_new


def kernel(x_prompt, x_sample, cache_k, cache_v, state_hgrn, page_table, norm_mix_pre, norm_mix_post, norm_ffn_pre, norm_ffn_post, w_in, hg_lb_logits, hg_norm_w, da_lambda_q1, da_lambda_k1, da_lambda_q2, da_lambda_k2, da_subln_w, rel_bias_table, w_branch_hg, w_branch_da, w_out, w_ffn_up, w_ffn_down):
    depth = w_in.shape[0]
    _, _, hg_heads, hg_dk, hg_dv = state_hgrn.shape
    da_heads, da_dv = cache_v.shape[3], cache_v.shape[4]
    da_dh = cache_k.shape[4]
    dims = (hg_heads, hg_dk, hg_dv, da_heads, da_dh, da_dv)
    lb_all = jnp.cumsum(jax.nn.softmax(hg_lb_logits.astype(F32), axis=0), axis=0)

    y_p, y_s = x_prompt, x_sample
    outs = [[] for _ in range(6)]
    for l in range(depth):
        lam_init = 0.8 - 0.6 * math.exp(-0.3 * l)
        lam = (jnp.exp(jnp.sum(da_lambda_q1[l] * da_lambda_k1[l]))
               - jnp.exp(jnp.sum(da_lambda_q2[l] * da_lambda_k2[l])) + lam_init)
        lw = {
            "norm_mix_pre": norm_mix_pre[l][None], "norm_mix_post": norm_mix_post[l][None],
            "norm_ffn_pre": norm_ffn_pre[l][None], "norm_ffn_post": norm_ffn_post[l][None],
            "w_in": w_in[l].astype(BF16), "lb": lb_all[l][None], "hg_norm_w": hg_norm_w[l][None],
            "da_subln_w": da_subln_w[l][None], "tbl": rel_bias_table.astype(F32),
            "lam": lam.reshape(1, 1).astype(F32), "out_scale": 1.0 - lam_init,
            "w_branch_hg": w_branch_hg[l].astype(BF16), "w_branch_da": w_branch_da[l].astype(BF16),
            "w_out": w_out[l].astype(BF16), "w_ffn_up": w_ffn_up[l].astype(BF16),
            "w_ffn_down": w_ffn_down[l].astype(BF16),
        }
        y_p, kp, vp, sp = _layer(y_p, None, None, lw, dims)
        y_s, ks, vs, ss = _layer(y_s, state_hgrn[l], (cache_k[l], cache_v[l], page_table), lw, dims)
        for acc, val in zip(outs, (kp, vp, sp, ks, vs, ss)):
            acc.append(val)
    return (y_p, y_s) + tuple(jnp.stack(o) for o in outs)
```

```python
import functools
import math

import jax
import jax.numpy as jnp
import numpy as np
from jax import lax
from jax.experimental import pallas as pl
from jax.experimental.pallas import tpu as pltpu

F32 = jnp.float32
BF16 = jnp.bfloat16

RMS_EPS = 1e-6
NEG_INF = -1e30
HG_CHUNK = 64
REL_MAX_DISTANCE = 128
LANES = 128
VMEM_LIMIT = 56 * 1024 * 1024

_NT = (((1,), (1,)), ((), ()))
_TN = (((0,), (0,)), ((), ()))


def _params(*sem):
    return pltpu.CompilerParams(dimension_semantics=sem, vmem_limit_bytes=VMEM_LIMIT)


def _rms(x, w):
    return x * lax.rsqrt(jnp.mean(x * x, axis=-1, keepdims=True) + RMS_EPS) * w


def _sigmoid(x):
    return 1.0 / (1.0 + jnp.exp(-x))


def _pick(n, prefs):
    for p in prefs:
        if n % p == 0:
            return p
    return n


def _norm_proj_kernel(x_ref, nw_ref, w_ref, o_ref, h_ref):
    @pl.when(pl.program_id(1) == 0)
    def _():
        h_ref[...] = _rms(x_ref[...], nw_ref[...]).astype(BF16)

    o_ref[...] = jnp.dot(h_ref[...], w_ref[...], preferred_element_type=F32)


def _norm_proj(x, nw, w_bf):
    m, d = x.shape
    n = w_bf.shape[1]
    tm = _pick(m, (1024, 512, 256, 128, 64, 8))
    tn = _pick(n, (1024, 512, 256, 128))
    return pl.pallas_call(
        _norm_proj_kernel,
        grid=(m // tm, n // tn),
        in_specs=[pl.BlockSpec((tm, d), lambda i, j: (i, 0)),
                  pl.BlockSpec((1, d), lambda i, j: (0, 0)),
                  pl.BlockSpec((d, tn), lambda i, j: (0, j))],
        out_specs=pl.BlockSpec((tm, tn), lambda i, j: (i, j)),
        out_shape=jax.ShapeDtypeStruct((m, n), F32),
        scratch_shapes=[pltpu.VMEM((tm, d), BF16)],
        compiler_params=_params("parallel", "arbitrary"),
        name="norm_proj",
    )(x, nw, w_bf)


def _hgrn_kernel(*refs, heads, dk, dv, chunk, rows, has_s0):
    if has_s0:
        q_ref, f_ref, i_ref, gate_ref, lb_ref, nw_ref, s0_ref, o_ref, sout_ref, st_ref = refs
    else:
        q_ref, f_ref, i_ref, gate_ref, lb_ref, nw_ref, o_ref, sout_ref, st_ref = refs
    t = pl.program_id(1)
    real = q_ref.shape[0]

    @pl.when(t == 0)
    def _():
        for h in range(heads):
            if has_s0:
                st_ref[h] = s0_ref[h].T
            else:
                st_ref[h] = jnp.zeros((dv, dk), F32)

    lb = lb_ref[...]
    f = lb + (1.0 - lb) * _sigmoid(f_ref[...])
    g = jnp.log(f)
    kin = 1.0 - f
    q = q_ref[...]
    v = i_ref[...]
    if rows > real:
        def pad(a):
            return jnp.concatenate([a, jnp.zeros((rows - real, a.shape[1]), F32)], axis=0)
        g, kin, q, v = pad(g), pad(kin), pad(q), pad(v)

    r = lax.broadcasted_iota(jnp.int32, (rows, rows), 0)
    c = lax.broadcasted_iota(jnp.int32, (rows, rows), 1)
    tri = jnp.where((r // chunk == c // chunk) & (c <= r), 1.0, 0.0).astype(BF16)
    g_hi = g.astype(BF16)
    g_r1 = g - g_hi.astype(F32)
    g_mid = g_r1.astype(BF16)
    g_lo = (g_r1 - g_mid.astype(F32)).astype(BF16)
    G = (jnp.dot(tri, g_hi, preferred_element_type=F32)
         + jnp.dot(tri, g_mid, preferred_element_type=F32)
         + jnp.dot(tri, g_lo, preferred_element_type=F32))

    qg = (q * jnp.exp(G)).astype(BF16)
    kg = (kin * jnp.exp(-G)).astype(BF16)
    vb = v.astype(BF16)
    cr = lax.broadcasted_iota(jnp.int32, (chunk, chunk), 0)
    cc = lax.broadcasted_iota(jnp.int32, (chunk, chunk), 1)
    causal = cc <= cr
    nw = nw_ref[...]

    for ci in range(rows // chunk):
        lo = ci * chunk
        Gc = G[lo:lo + chunk]
        Gl = Gc[chunk - 1:chunk]
        kdec = (kin[lo:lo + chunk] * jnp.exp(Gl - Gc)).astype(BF16)
        decay = jnp.exp(Gl)
        n_out = min(chunk, real - lo)
        for h in range(heads):
            sk = slice(h * dk, (h + 1) * dk)
            sv = slice(h * dv, (h + 1) * dv)
            qg_h = qg[lo:lo + chunk, sk]
            v_h = vb[lo:lo + chunk, sv]
            a = lax.dot_general(qg_h, kg[lo:lo + chunk, sk], _NT, preferred_element_type=F32)
            a = jnp.where(causal, a, 0.0).astype(BF16)
            st = st_ref[h]
            o = (jnp.dot(a, v_h, preferred_element_type=F32)
                 + lax.dot_general(qg_h, st.astype(BF16), _NT, preferred_element_type=F32))
            st_ref[h] = st * decay[:, sk] + lax.dot_general(v_h, kdec[:, sk], _TN,
                                                            preferred_element_type=F32)
            if n_out > 0:
                gt = gate_ref[lo:lo + n_out, sv]
                on = _rms(o[:n_out], nw) * (gt * _sigmoid(gt))
                o_ref[lo:lo + n_out, sv] = on.astype(o_ref.dtype)

    @pl.when(t == pl.num_programs(1) - 1)
    def _():
        for h in range(heads):
            sout_ref[h] = st_ref[h].T


def _hgrn(u, col0, lb, nw, s0, batch, seqlen, heads, dk, dv):
    width = heads * dk
    assert dk == dv and col0 % width == 0
    cb = col0 // width
    chunk = min(HG_CHUNK, seqlen)
    if seqlen >= LANES:
        tb = _pick(seqlen, (256, 128))
        rows = tb
        assert tb % chunk == 0
    else:
        tb = seqlen
        rows = LANES
        chunk = LANES
    nt = seqlen // tb
    has_s0 = s0 is not None

    def col(k):
        return pl.BlockSpec((tb, width), lambda b, t: (b * nt + t, cb + k))

    in_specs = [col(0), col(1), col(2), col(3),
                pl.BlockSpec((1, width), lambda b, t: (0, 0)),
                pl.BlockSpec((1, dv), lambda b, t: (0, 0))]
    args = [u, u, u, u, lb, nw]
    if has_s0:
        in_specs.append(pl.BlockSpec((None, heads, dk, dv), lambda b, t: (b, 0, 0, 0)))
        args.append(s0)
    kern = functools.partial(_hgrn_kernel, heads=heads, dk=dk, dv=dv, chunk=chunk, rows=rows,
                             has_s0=has_s0)
    return pl.pallas_call(
        kern,
        grid=(batch, nt),
        in_specs=in_specs,
        out_specs=[pl.BlockSpec((tb, heads * dv), lambda b, t: (b * nt + t, 0)),
                   pl.BlockSpec((None, heads, dk, dv), lambda b, t: (b, 0, 0, 0))],
        out_shape=[jax.ShapeDtypeStruct((batch * seqlen, heads * dv), BF16),
                   jax.ShapeDtypeStruct((batch, heads, dk, dv), F32)],
        scratch_shapes=[pltpu.VMEM((heads, dv, dk), F32)],
        compiler_params=_params("parallel", "arbitrary"),
        name="hgrn",
    )(*args)


def _bucket(dist, n_buckets):
    n = jnp.maximum(dist, 0)
    max_exact = n_buckets // 2
    nf = jnp.maximum(n, 1).astype(F32)
    large = max_exact + (jnp.log(nf / max_exact) / math.log(REL_MAX_DISTANCE / max_exact)
                         * (n_buckets - max_exact)).astype(jnp.int32)
    large = jnp.minimum(large, n_buckets - 1)
    return jnp.where(n < max_exact, n, large)


def _far_start(n_buckets):
    n = np.arange(1, 8 * REL_MAX_DISTANCE, dtype=np.int64)
    max_exact = n_buckets // 2
    nf = n.astype(np.float32)
    large = max_exact + (np.log(nf / np.float32(max_exact)) / np.float32(math.log(REL_MAX_DISTANCE / max_exact))
                         * np.float32(n_buckets - max_exact)).astype(np.int32)
    b = np.where(n < max_exact, n, np.minimum(large, n_buckets - 1))
    below = n[b < n_buckets - 1]
    return int(below.max()) + 2


def _attn_kernel(tbl_ref, lam_ref, nw_ref, q1_ref, q2_ref, k1_ref, k2_ref, v_ref, o_ref,
                 bnear_ref, bdiag_ref, m_ref, l_ref, acc_ref, *,
                 t, dh, dv, heads, n_buckets, out_scale):
    hp = pl.program_id(0)
    b = pl.program_id(1)
    qb = pl.program_id(2)

    @pl.when((b == 0) & (qb == 0))
    def _():
        row = lax.broadcasted_iota(jnp.int32, (t, t), 0)
        col = lax.broadcasted_iota(jnp.int32, (t, t), 1)
        bk_diag = _bucket(row - col, n_buckets)
        bk_near = _bucket(row - col + t, n_buckets)
        for mp in range(2):
            for e in range(2):
                hcol = mp * heads + 2 * hp + e
                far = tbl_ref[n_buckets - 1, hcol]
                bd = jnp.zeros((t, t), F32)
                bn = jnp.zeros((t, t), F32)
                for k in range(n_buckets - 1):
                    val = tbl_ref[k, hcol] - far
                    bd = jnp.where(bk_diag == k, val, bd)
                    bn = jnp.where(bk_near == k, val, bn)
                bdiag_ref[mp, e * t:(e + 1) * t, :] = jnp.where(col <= row, bd, NEG_INF)
                bnear_ref[mp, e * t:(e + 1) * t, :] = bn

    lane = lax.broadcasted_iota(jnp.int32, (t, 2 * dh), 1)
    first = lane < dh
    scale = dh ** -0.5
    outs = []
    for mp, (q_ref, k_ref) in enumerate(((q1_ref, k1_ref), (q2_ref, k2_ref))):
        q = q_ref[...] * scale
        qs = jnp.concatenate([jnp.where(first, q, 0.0), jnp.where(first, 0.0, q)],
                             axis=0).astype(BF16)
        m_ref[...] = jnp.full(m_ref.shape, NEG_INF, F32)
        l_ref[...] = jnp.zeros(l_ref.shape, F32)
        acc_ref[...] = jnp.zeros(acc_ref.shape, F32)

        def step(j, bias):
            start = pl.multiple_of(j * t, t)
            kb = k_ref[pl.ds(start, t), :].astype(BF16)
            s = lax.dot_general(qs, kb, _NT, preferred_element_type=F32)
            if bias is not None:
                s = s + bias
            m_prev = m_ref[...]
            m_new = jnp.maximum(m_prev, jnp.max(s, axis=1, keepdims=True))
            p = jnp.exp(s - jnp.tile(m_new, (1, t // LANES)))
            alpha = jnp.exp(m_prev - m_new)
            l_ref[...] = alpha * l_ref[...] + jnp.sum(p, axis=1, keepdims=True)
            m_ref[...] = m_new
            pb = p.astype(BF16)
            for e in range(2):
                ve = v_ref[pl.ds(start, t), e * dv:(e + 1) * dv].astype(BF16)
                rs = slice(e * t, (e + 1) * t)
                acc_ref[rs, :] = acc_ref[rs, :] * alpha[rs] + jnp.dot(
                    pb[rs], ve, preferred_element_type=F32)

        def far_body(j, carry):
            step(j, None)
            return carry

        lax.fori_loop(0, qb - 1, far_body, 0)

        @pl.when(qb > 0)
        def _():
            step(qb - 1, bnear_ref[mp])

        step(qb, bdiag_ref[mp])
        outs.append(acc_ref[...] / l_ref[...])

    o = outs[0] - lam_ref[0, 0] * outs[1]
    on = _rms(o, nw_ref[...]) * out_scale
    for e in range(2):
        o_ref[:, e * dv:(e + 1) * dv] = on[e * t:(e + 1) * t].astype(o_ref.dtype)


def _attn_prompt(u, qcol, kcol, vcol, tbl, lam, nw, batch, seqlen, heads, dh, dv, out_scale):
    assert dv == LANES and 2 * dh == LANES and heads % 2 == 0
    n_buckets = tbl.shape[0]
    t = _pick(seqlen, (256, 128))
    assert seqlen % t == 0 and t % LANES == 0 and t + 1 >= _far_start(n_buckets)
    nq = seqlen // t
    nhp = heads // 2
    qc, kc, vc = qcol // LANES, kcol // LANES, vcol // (2 * dv)
    smem = pl.BlockSpec(memory_space=pltpu.SMEM)
    kern = functools.partial(_attn_kernel, t=t, dh=dh, dv=dv, heads=heads, n_buckets=n_buckets,
                             out_scale=out_scale)
    return pl.pallas_call(
        kern,
        grid=(nhp, batch, nq),
        in_specs=[smem, smem,
                  pl.BlockSpec((1, dv), lambda h, b, i: (0, 0)),
                  pl.BlockSpec((t, LANES), lambda h, b, i: (b * nq + i, qc + h)),
                  pl.BlockSpec((t, LANES), lambda h, b, i: (b * nq + i, qc + nhp + h)),
                  pl.BlockSpec((seqlen, LANES), lambda h, b, i: (b, kc + h)),
                  pl.BlockSpec((seqlen, LANES), lambda h, b, i: (b, kc + nhp + h)),
                  pl.BlockSpec((seqlen, 2 * dv), lambda h, b, i: (b, vc + h))],
        out_specs=pl.BlockSpec((t, 2 * dv), lambda h, b, i: (b * nq + i, h)),
        out_shape=jax.ShapeDtypeStruct((batch * seqlen, heads * dv), BF16),
        scratch_shapes=[pltpu.VMEM((2, 2 * t, t), F32), pltpu.VMEM((2, 2 * t, t), F32),
                        pltpu.VMEM((2 * t, LANES), F32), pltpu.VMEM((2 * t, LANES), F32),
                        pltpu.VMEM((2 * t, dv), F32)],
        compiler_params=_params("arbitrary", "arbitrary", "arbitrary"),
        name="attn_prompt",
    )(tbl, lam, nw, u, u, u, u, u)


def _decode_kernel(pt_ref, lam_ref, trow_ref, nw_ref, q_ref, kn_ref, vn_ref, *rest,
                   pages, page, lq, heads, dh, dv, n_buckets, past_len, near_slots, out_scale):
    k_refs = rest[:pages]
    v_refs = rest[pages:2 * pages]
    o_ref, m_ref, l_ref, acc_ref, bias_ref = rest[2 * pages:]
    b = pl.program_id(0)
    g = pl.program_id(1)
    ng = pl.num_programs(1)
    mh = 2 * heads
    rows = mh * lq
    n_near = len(near_slots)
    order = [m * heads + h for h in range(heads) for m in range(2)]

    def near_bias(key0, n_valid):
        r = lax.broadcasted_iota(jnp.int32, (rows, page), 0)
        col = lax.broadcasted_iota(jnp.int32, (rows, page), 1)
        dist = past_len + r % lq - (key0 + col)
        bk = _bucket(dist, n_buckets)
        far = trow_ref[:, n_buckets - 1:n_buckets]
        bias = jnp.zeros((rows, page), F32)
        for k in range(n_buckets - 1):
            bias = jnp.where(bk == k, trow_ref[:, k:k + 1] - far, bias)
        return jnp.where((dist >= 0) & (col < n_valid), bias, NEG_INF)

    @pl.when((b == 0) & (g == 0))
    def _():
        for i, slot in enumerate(near_slots):
            bias_ref[i] = near_bias(past_len - (pages - slot) * page, page)
        bias_ref[n_near] = near_bias(past_len, lq)

    @pl.when(g == 0)
    def _():
        m_ref[...] = jnp.full(m_ref.shape, NEG_INF, F32)
        l_ref[...] = jnp.zeros(l_ref.shape, F32)
        acc_ref[...] = jnp.zeros(acc_ref.shape, F32)

    q = q_ref[...] * (dh ** -0.5)
    q_parts = [q[:, j * dh:(j + 1) * dh].astype(BF16) for j in order]

    def attend(score, value_rows, bias):
        s = jnp.concatenate([score(qj, j) for qj, j in zip(q_parts, order)], axis=0)
        if bias is not None:
            s = s + bias
        m_prev = m_ref[...]
        m_new = jnp.maximum(m_prev, jnp.max(s, axis=1, keepdims=True))
        p = jnp.exp(s - jnp.tile(m_new, (1, s.shape[1] // LANES)))
        alpha = jnp.exp(m_prev - m_new)
        l_ref[...] = alpha * l_ref[...] + jnp.sum(p, axis=1, keepdims=True)
        m_ref[...] = m_new
        pv = jnp.concatenate(
            [jnp.dot(p[2 * lq * h:2 * lq * (h + 1)].astype(BF16), value_rows(h).astype(BF16),
                     preferred_element_type=F32) for h in range(heads)], axis=0)
        acc_ref[...] = acc_ref[...] * alpha + pv

    is_last = g == ng - 1
    bias = None
    if near_slots:
        zero = jnp.zeros((rows, page), F32)
        bias = jnp.concatenate(
            [jnp.where(is_last, bias_ref[near_slots.index(s)], 0.0) if s in near_slots else zero
             for s in range(pages)], axis=1)
    def page_score(qj, j):
        kt = jnp.concatenate([r[j * dh:(j + 1) * dh, :] for r in k_refs], axis=1)
        return jnp.dot(qj, kt.astype(BF16), preferred_element_type=F32)

    attend(page_score,
           lambda h: jnp.concatenate([r[pl.ds(h, page, stride=heads), :] for r in v_refs], axis=0),
           bias)

    @pl.when(is_last)
    def _():
        kn = kn_ref[...]
        vn = vn_ref[...]
        zk = jnp.zeros((page - lq, dh), F32)
        zv = jnp.zeros((page - lq, dv), F32)

        def new_score(qj, j):
            kj = jnp.concatenate([kn[:, j * dh:(j + 1) * dh], zk], axis=0).astype(BF16)
            return lax.dot_general(qj, kj, _NT, preferred_element_type=F32)

        attend(new_score,
               lambda h: jnp.concatenate([vn[:, h * dv:(h + 1) * dv], zv], axis=0),
               bias_ref[n_near])
        full = acc_ref[...] / l_ref[...]
        lam = lam_ref[0, 0]
        nw = nw_ref[...]
        for h in range(heads):
            o0 = full[2 * lq * h:2 * lq * h + lq]
            o1 = full[2 * lq * h + lq:2 * lq * (h + 1)]
            o_ref[:, h * dv:(h + 1) * dv] = (_rms(o0 - lam * o1, nw) * out_scale).astype(o_ref.dtype)


def _attn_decode(u, qcol, kcol, vcol, cache_k, cache_v, page_table, tbl, lam, nw,
                 batch, lq, heads, dh, dv, out_scale):
    n_pool, page = cache_k.shape[0], cache_k.shape[1]
    n_pages = page_table.shape[1]
    past_len = n_pages * page
    mh = 2 * heads
    kw, vw = mh * dh, heads * dv
    assert page % LANES == 0 and kw % LANES == 0 and vw % LANES == 0 and lq % 8 == 0
    n_buckets = tbl.shape[0]
    far = _far_start(n_buckets)
    first_near = max(0, -(-(past_len - page + 1 - far + 1) // page))
    pages = _pick(n_pages, (8, 4, 2, 1))
    assert n_pages - pages <= first_near, "near pages must fall in the last grid step"
    near_slots = tuple(s for s in range(pages) if n_pages - pages + s >= first_near)
    ck = jnp.transpose(cache_k, (0, 2, 3, 1)).reshape(n_pool * mh * dh, page)
    cv = cache_v.reshape(n_pool * page * heads, dv)
    trow = jnp.repeat(tbl.T.reshape(2, heads, -1).transpose(1, 0, 2).reshape(mh, -1), lq, axis=0)
    rows = mh * lq

    def page_spec(n_rows, width, slot):
        return pl.BlockSpec((n_rows, width),
                            lambda b, g, pt: (pt[b * n_pages + g * pages + slot], 0))

    smem = pl.BlockSpec(memory_space=pltpu.SMEM)
    in_specs = ([smem,
                 pl.BlockSpec((rows, n_buckets), lambda b, g, pt: (0, 0)),
                 pl.BlockSpec((1, dv), lambda b, g, pt: (0, 0)),
                 pl.BlockSpec((lq, kw), lambda b, g, pt: (b, qcol // kw)),
                 pl.BlockSpec((lq, kw), lambda b, g, pt: (b, kcol // kw)),
                 pl.BlockSpec((lq, vw), lambda b, g, pt: (b, vcol // vw))]
                + [page_spec(mh * dh, page, s) for s in range(pages)]
                + [page_spec(page * heads, dv, s) for s in range(pages)])
    assert qcol % kw == 0 and kcol % kw == 0 and vcol % vw == 0
    kern = functools.partial(_decode_kernel, pages=pages, page=page, lq=lq, heads=heads, dh=dh,
                             dv=dv, n_buckets=n_buckets, past_len=past_len,
                             near_slots=near_slots, out_scale=out_scale)
    grid_spec = pltpu.PrefetchScalarGridSpec(
        num_scalar_prefetch=1,
        grid=(batch, n_pages // pages),
        in_specs=in_specs,
        out_specs=pl.BlockSpec((lq, vw), lambda b, g, pt: (b, 0)),
        scratch_shapes=[pltpu.VMEM((rows, LANES), F32), pltpu.VMEM((rows, LANES), F32),
                        pltpu.VMEM((rows, dv), F32),
                        pltpu.VMEM((len(near_slots) + 1, rows, page), F32)])
    return pl.pallas_call(
        kern,
        grid_spec=grid_spec,
        out_shape=jax.ShapeDtypeStruct((batch * lq, vw), BF16),
        compiler_params=_params("arbitrary", "arbitrary"),
        name="attn_decode",
    )(page_table.reshape(-1), lam, trow, nw, u, u, u, *([ck] * pages), *([cv] * pages))


def _merge_kernel(*refs, n_g):
    ohg_ref, oda_ref = refs[:2]
    g_refs = refs[2:2 + 2 * n_g]
    x_ref, wbh_ref, wbd_ref, wo_ref, nw_ref, o_ref = refs[2 + 2 * n_g:]
    g_hg = jnp.concatenate([r[...] for r in g_refs[:n_g]], axis=1)
    g_da = jnp.concatenate([r[...] for r in g_refs[n_g:]], axis=1)
    y_hg = jnp.dot(ohg_ref[...], wbh_ref[...], preferred_element_type=F32)
    y_da = jnp.dot(oda_ref[...], wbd_ref[...], preferred_element_type=F32)
    mixed = (_sigmoid(g_hg) * y_hg + _sigmoid(g_da) * y_da).astype(BF16)
    z = jnp.dot(mixed, wo_ref[...], preferred_element_type=F32)
    o_ref[...] = x_ref[...] + _rms(z, nw_ref[...])


def _merge(o_hg, o_da, u, gcol, x, wbh, wbd, wo, nw):
    m, d = x.shape
    w = o_hg.shape[1]
    gw = math.gcd(gcol, d)
    n_g = d // gw
    assert gw % LANES == 0
    tm = _pick(m, (256, 128, 64, 8))
    const = lambda i: (0, 0)
    g_specs = [pl.BlockSpec((tm, gw), functools.partial(lambda i, c: (i, c), c=gcol // gw + k))
               for k in range(2 * n_g)]
    return pl.pallas_call(
        functools.partial(_merge_kernel, n_g=n_g),
        grid=(m // tm,),
        in_specs=[pl.BlockSpec((tm, w), lambda i: (i, 0)),
                  pl.BlockSpec((tm, w), lambda i: (i, 0))]
                 + g_specs
                 + [pl.BlockSpec((tm, d), lambda i: (i, 0)),
                    pl.BlockSpec((w, d), const, pipeline_mode=pl.Buffered(1)),
                    pl.BlockSpec((w, d), const, pipeline_mode=pl.Buffered(1)),
                    pl.BlockSpec((d, d), const, pipeline_mode=pl.Buffered(1)),
                    pl.BlockSpec((1, d), const)],
        out_specs=pl.BlockSpec((tm, d), lambda i: (i, 0)),
        out_shape=jax.ShapeDtypeStruct((m, d), F32),
        compiler_params=_params("parallel"),
        name="merge",
    )(o_hg, o_da, *([u] * (2 * n_g)), x, wbh, wbd, wo, nw)


def _ffn_kernel(x_ref, npre_ref, wg_ref, wu_ref, wd_ref, npost_ref, o_ref, h_ref, acc_ref):
    j = pl.program_id(1)

    @pl.when(j == 0)
    def _():
        h_ref[...] = _rms(x_ref[...], npre_ref[...]).astype(BF16)

    h = h_ref[...]
    gate = jnp.dot(h, wg_ref[...], preferred_element_type=F32)
    up = jnp.dot(h, wu_ref[...], preferred_element_type=F32)
    act = (gate * _sigmoid(gate) * up).astype(BF16)
    part = jnp.dot(act, wd_ref[...], preferred_element_type=F32)

    @pl.when(j == 0)
    def _():
        acc_ref[...] = part

    @pl.when(j > 0)
    def _():
        acc_ref[...] = acc_ref[...] + part

    @pl.when(j == pl.num_programs(1) - 1)
    def _():
        o_ref[...] = x_ref[...] + _rms(acc_ref[...], npost_ref[...])


def _ffn(x, npre, w_up, w_down, npost):
    m, d = x.shape
    ff = w_down.shape[0]
    tm = _pick(m, (512, 256, 128, 64, 8))
    tf = _pick(ff, (512, 256, 128))
    nf = ff // tf
    return pl.pallas_call(
        _ffn_kernel,
        grid=(m // tm, nf),
        in_specs=[pl.BlockSpec((tm, d), lambda i, j: (i, 0)),
                  pl.BlockSpec((1, d), lambda i, j: (0, 0)),
                  pl.BlockSpec((d, tf), lambda i, j: (0, j)),
                  pl.BlockSpec((d, tf), lambda i, j: (0, nf + j)),
                  pl.BlockSpec((tf, d), lambda i, j: (j, 0)),
                  pl.BlockSpec((1, d), lambda i, j: (0, 0))],
        out_specs=pl.BlockSpec((tm, d), lambda i, j: (i, 0)),
        out_shape=jax.ShapeDtypeStruct((m, d), F32),
        scratch_shapes=[pltpu.VMEM((tm, d), BF16), pltpu.VMEM((tm, d), F32)],
        compiler_params=_params("parallel", "arbitrary"),
        name="ffn",
    )(x, npre, w_up, w_up, w_down, npost)


def _layer(x, s0, paged, lw, dims):
    batch, seqlen, d = x.shape
    hg_heads, hg_dk, hg_dv, da_heads, da_dh, da_dv = dims
    hg_w, da_qk, da_w = hg_heads * hg_dk, 2 * da_heads * da_dh, da_heads * da_dv
    cols = np.cumsum([0, hg_w, hg_w, hg_heads * hg_dv, hg_heads * hg_dv, da_qk, da_qk, da_w, d, d])
    qcol, kcol, vcol, gcol = int(cols[4]), int(cols[5]), int(cols[6]), int(cols[7])
    x2 = x.reshape(batch * seqlen, d)

    u = _norm_proj(x2, lw["norm_mix_pre"], lw["w_in"])
    o_hg, s_new = _hgrn(u, 0, lw["lb"], lw["hg_norm_w"], s0, batch, seqlen, hg_heads, hg_dk, hg_dv)
    if paged is None:
        o_da = _attn_prompt(u, qcol, kcol, vcol, lw["tbl"], lw["lam"], lw["da_subln_w"],
                            batch, seqlen, da_heads, da_dh, da_dv, lw["out_scale"])
    else:
        o_da = _attn_decode(u, qcol, kcol, vcol, paged[0], paged[1], paged[2], lw["tbl"],
                            lw["lam"], lw["da_subln_w"], batch, seqlen, da_heads, da_dh, da_dv,
                            lw["out_scale"])
    x1 = _merge(o_hg, o_da, u, gcol, x2, lw["w_branch_hg"], lw["w_branch_da"], lw["w_out"],
                lw["norm_mix_post"])
    y = _ffn(x1, lw["norm_ffn_pre"], lw["w_ffn_up"], lw["w_ffn_down"], lw["norm_ffn_post"])
    k = u[:, kcol:kcol + da_qk].reshape(batch, seqlen, 2 * da_heads, da_dh)
    v = u[:, vcol:vcol + da_w].reshape(batch, seqlen, da_heads, da_dv)
    return y.reshape(batch, seqlen, d), k, v, s_new


def kernel(x_prompt, x_sample, cache_k, cache_v, state_hgrn, page_table, norm_mix_pre, norm_mix_post, norm_ffn_pre, norm_ffn_post, w_in, hg_lb_logits, hg_norm_w, da_lambda_q1, da_lambda_k1, da_lambda_q2, da_lambda_k2, da_subln_w, rel_bias_table, w_branch_hg, w_branch_da, w_out, w_ffn_up, w_ffn_down):
    depth = w_in.shape[0]
    _, _, hg_heads, hg_dk, hg_dv = state_hgrn.shape
    da_heads, da_dv = cache_v.shape[3], cache_v.shape[4]
    da_dh = cache_k.shape[4]
    dims = (hg_heads, hg_dk, hg_dv, da_heads, da_dh, da_dv)
    lb_all = jnp.cumsum(jax.nn.softmax(hg_lb_logits.astype(F32), axis=0), axis=0)

    y_p, y_s = x_prompt, x_sample
    outs = [[] for _ in range(6)]
    for l in range(depth):
        lam_init = 0.8 - 0.6 * math.exp(-0.3 * l)
        lam = (jnp.exp(jnp.sum(da_lambda_q1[l] * da_lambda_k1[l]))
               - jnp.exp(jnp.sum(da_lambda_q2[l] * da_lambda_k2[l])) + lam_init)
        lw = {
            "norm_mix_pre": norm_mix_pre[l][None], "norm_mix_post": norm_mix_post[l][None],
            "norm_ffn_pre": norm_ffn_pre[l][None], "norm_ffn_post": norm_ffn_post[l][None],
            "w_in": w_in[l].astype(BF16), "lb": lb_all[l][None], "hg_norm_w": hg_norm_w[l][None],
            "da_subln_w": da_subln_w[l][None], "tbl": rel_bias_table.astype(F32),
            "lam": lam.reshape(1, 1).astype(F32), "out_scale": 1.0 - lam_init,
            "w_branch_hg": w_branch_hg[l].astype(BF16), "w_branch_da": w_branch_da[l].astype(BF16),
            "w_out": w_out[l].astype(BF16), "w_ffn_up": w_ffn_up[l].astype(BF16),
            "w_ffn_down": w_ffn_down[l].astype(BF16),
        }
        y_p, kp, vp, sp = _layer(y_p, None, None, lw, dims)
        y_s, ks, vs, ss = _layer(y_s, state_hgrn[l], (cache_k[l], cache_v[l], page_table), lw, dims)
        for acc, val in zip(outs, (kp, vp, sp, ks, vs, ss)):
            acc.append(val)
    return (y_p, y_s) + tuple(jnp.stack(o) for o in outs)
```

```python
import functools
import math

import jax
import jax.numpy as jnp
import numpy as np
from jax import lax
from jax.experimental import pallas as pl
from jax.experimental.pallas import tpu as pltpu

F32 = jnp.float32
BF16 = jnp.bfloat16

RMS_EPS = 1e-6
NEG_INF = -1e30
HG_CHUNK = 64
REL_MAX_DISTANCE = 128
LANES = 128
VMEM_LIMIT = 56 * 1024 * 1024

_NT = (((1,), (1,)), ((), ()))
_TN = (((0,), (0,)), ((), ()))


def _params(*sem):
    return pltpu.CompilerParams(dimension_semantics=sem, vmem_limit_bytes=VMEM_LIMIT)


def _rms(x, w):
    return x * lax.rsqrt(jnp.mean(x * x, axis=-1, keepdims=True) + RMS_EPS) * w


def _sigmoid(x):
    return 1.0 / (1.0 + jnp.exp(-x))


def _pick(n, prefs):
    for p in prefs:
        if n % p == 0:
            return p
    return n


def _norm_proj_kernel(x_ref, nw_ref, w_ref, o_ref, *rest, jk, jv, v_heads):
    h_ref = rest[-1]
    j = pl.program_id(1)

    @pl.when(j == 0)
    def _():
        h_ref[...] = _rms(x_ref[...], nw_ref[...]).astype(BF16)

    res = jnp.dot(h_ref[...], w_ref[...], preferred_element_type=F32)
    o_ref[...] = res
    if jk is not None:
        kt_ref, v_ref = rest[:2]
        tm, tn = res.shape

        @pl.when(j == jk)
        def _():
            kt_ref[...] = res.T

        @pl.when(j == jv)
        def _():
            dv = tn // v_heads
            for h in range(v_heads):
                v_ref[pl.ds(h, tm, stride=v_heads), :] = res[:, h * dv:(h + 1) * dv]


def _norm_proj(x, nw, w_bf, kv=None):
    m, d = x.shape
    n = w_bf.shape[1]
    tm = _pick(m, (1024, 512, 256, 128, 64, 8))
    tn = _pick(n, (1024, 512, 256, 128))
    out_specs = [pl.BlockSpec((tm, tn), lambda i, j: (i, j))]
    out_shape = [jax.ShapeDtypeStruct((m, n), F32)]
    jk = jv = v_heads = None
    if kv is not None:
        kcol, vcol, width, seqlen, v_heads = kv
        tn = width
        assert n % tn == 0 and kcol % tn == 0 and vcol % tn == 0 and seqlen % tm == 0
        jk, jv = kcol // tn, vcol // tn
        nt = seqlen // tm
        out_specs = [pl.BlockSpec((tm, tn), lambda i, j: (i, j)),
                     pl.BlockSpec((tn, tm), lambda i, j: (i // nt, i % nt)),
                     pl.BlockSpec((tm * v_heads, tn // v_heads), lambda i, j: (i, 0))]
        out_shape += [jax.ShapeDtypeStruct((m // seqlen * tn, seqlen), F32),
                      jax.ShapeDtypeStruct((m * v_heads, tn // v_heads), F32)]
    return pl.pallas_call(
        functools.partial(_norm_proj_kernel, jk=jk, jv=jv, v_heads=v_heads),
        grid=(m // tm, n // tn),
        in_specs=[pl.BlockSpec((tm, d), lambda i, j: (i, 0), pipeline_mode=pl.Buffered(1)),
                  pl.BlockSpec((1, d), lambda i, j: (0, 0)),
                  pl.BlockSpec((d, tn), lambda i, j: (0, j))],
        out_specs=out_specs,
        out_shape=out_shape,
        scratch_shapes=[pltpu.VMEM((tm, d), BF16)],
        compiler_params=_params("parallel", "arbitrary"),
        name="norm_proj",
    )(x, nw, w_bf)


def _hgrn_kernel(*refs, heads, dk, dv, chunk, rows, has_s0):
    if has_s0:
        q_ref, f_ref, i_ref, gate_ref, lb_ref, nw_ref, s0_ref, o_ref, sout_ref, st_ref = refs
    else:
        q_ref, f_ref, i_ref, gate_ref, lb_ref, nw_ref, o_ref, sout_ref, st_ref = refs
    t = pl.program_id(1)
    real = q_ref.shape[0]

    @pl.when(t == 0)
    def _():
        for h in range(heads):
            if has_s0:
                st_ref[h] = s0_ref[h].T
            else:
                st_ref[h] = jnp.zeros((dv, dk), F32)

    lb = lb_ref[...]
    f = lb + (1.0 - lb) * _sigmoid(f_ref[...])
    g = jnp.log(f)
    kin = 1.0 - f
    q = q_ref[...]
    v = i_ref[...]
    if rows > real:
        def pad(a):
            return jnp.concatenate([a, jnp.zeros((rows - real, a.shape[1]), F32)], axis=0)
        g, kin, q, v = pad(g), pad(kin), pad(q), pad(v)

    r = lax.broadcasted_iota(jnp.int32, (rows, rows), 0)
    c = lax.broadcasted_iota(jnp.int32, (rows, rows), 1)
    tri = jnp.where((r // chunk == c // chunk) & (c <= r), 1.0, 0.0).astype(BF16)
    g_hi = g.astype(BF16)
    g_r1 = g - g_hi.astype(F32)
    g_mid = g_r1.astype(BF16)
    g_lo = (g_r1 - g_mid.astype(F32)).astype(BF16)
    G = (jnp.dot(tri, g_hi, preferred_element_type=F32)
         + jnp.dot(tri, g_mid, preferred_element_type=F32)
         + jnp.dot(tri, g_lo, preferred_element_type=F32))

    qg = (q * jnp.exp(G)).astype(BF16)
    kg = (kin * jnp.exp(-G)).astype(BF16)
    vb = v.astype(BF16)
    cr = lax.broadcasted_iota(jnp.int32, (chunk, chunk), 0)
    cc = lax.broadcasted_iota(jnp.int32, (chunk, chunk), 1)
    causal = cc <= cr
    nw = nw_ref[...]

    for ci in range(rows // chunk):
        lo = ci * chunk
        Gc = G[lo:lo + chunk]
        Gl = Gc[chunk - 1:chunk]
        kdec = (kin[lo:lo + chunk] * jnp.exp(Gl - Gc)).astype(BF16)
        decay = jnp.exp(Gl)
        n_out = min(chunk, real - lo)
        for h in range(heads):
            sk = slice(h * dk, (h + 1) * dk)
            sv = slice(h * dv, (h + 1) * dv)
            qg_h = qg[lo:lo + chunk, sk]
            v_h = vb[lo:lo + chunk, sv]
            a = lax.dot_general(qg_h, kg[lo:lo + chunk, sk], _NT, preferred_element_type=F32)
            a = jnp.where(causal, a, 0.0).astype(BF16)
            st = st_ref[h]
            o = (jnp.dot(a, v_h, preferred_element_type=F32)
                 + lax.dot_general(qg_h, st.astype(BF16), _NT, preferred_element_type=F32))
            st_ref[h] = st * decay[:, sk] + lax.dot_general(v_h, kdec[:, sk], _TN,
                                                            preferred_element_type=F32)
            if n_out > 0:
                gt = gate_ref[lo:lo + n_out, sv]
                on = _rms(o[:n_out], nw) * (gt * _sigmoid(gt))
                o_ref[lo:lo + n_out, sv] = on.astype(o_ref.dtype)

    @pl.when(t == pl.num_programs(1) - 1)
    def _():
        for h in range(heads):
            sout_ref[h] = st_ref[h].T


def _hgrn(u, col0, lb, nw, s0, batch, seqlen, heads, dk, dv):
    width = heads * dk
    assert dk == dv and col0 % width == 0
    cb = col0 // width
    chunk = min(HG_CHUNK, seqlen)
    if seqlen >= LANES:
        tb = _pick(seqlen, (256, 128))
        rows = tb
        assert tb % chunk == 0
    else:
        tb = seqlen
        rows = LANES
        chunk = LANES
    nt = seqlen // tb
    has_s0 = s0 is not None

    def col(k):
        return pl.BlockSpec((tb, width), lambda b, t: (b * nt + t, cb + k))

    in_specs = [col(0), col(1), col(2), col(3),
                pl.BlockSpec((1, width), lambda b, t: (0, 0)),
                pl.BlockSpec((1, dv), lambda b, t: (0, 0))]
    args = [u, u, u, u, lb, nw]
    if has_s0:
        in_specs.append(pl.BlockSpec((None, heads, dk, dv), lambda b, t: (b, 0, 0, 0)))
        args.append(s0)
    kern = functools.partial(_hgrn_kernel, heads=heads, dk=dk, dv=dv, chunk=chunk, rows=rows,
                             has_s0=has_s0)
    return pl.pallas_call(
        kern,
        grid=(batch, nt),
        in_specs=in_specs,
        out_specs=[pl.BlockSpec((tb, heads * dv), lambda b, t: (b * nt + t, 0)),
                   pl.BlockSpec((None, heads, dk, dv), lambda b, t: (b, 0, 0, 0))],
        out_shape=[jax.ShapeDtypeStruct((batch * seqlen, heads * dv), BF16),
                   jax.ShapeDtypeStruct((batch, heads, dk, dv), F32)],
        scratch_shapes=[pltpu.VMEM((heads, dv, dk), F32)],
        compiler_params=_params("parallel", "arbitrary"),
        name="hgrn",
    )(*args)


def _bucket(dist, n_buckets):
    n = jnp.maximum(dist, 0)
    max_exact = n_buckets // 2
    nf = jnp.maximum(n, 1).astype(F32)
    large = max_exact + (jnp.log(nf / max_exact) / math.log(REL_MAX_DISTANCE / max_exact)
                         * (n_buckets - max_exact)).astype(jnp.int32)
    large = jnp.minimum(large, n_buckets - 1)
    return jnp.where(n < max_exact, n, large)


def _far_start(n_buckets):
    n = np.arange(1, 8 * REL_MAX_DISTANCE, dtype=np.int64)
    max_exact = n_buckets // 2
    nf = n.astype(np.float32)
    large = max_exact + (np.log(nf / np.float32(max_exact)) / np.float32(math.log(REL_MAX_DISTANCE / max_exact))
                         * np.float32(n_buckets - max_exact)).astype(np.int32)
    b = np.where(n < max_exact, n, np.minimum(large, n_buckets - 1))
    below = n[b < n_buckets - 1]
    return int(below.max()) + 2


def _attn_kernel(tbl_ref, lam_ref, nw_ref, q1_ref, q2_ref, k1_ref, k2_ref, v_ref, o_ref,
                 bnear_ref, bdiag_ref, m_ref, l_ref, acc_ref, *,
                 t, dh, dv, heads, n_buckets, out_scale):
    hp = pl.program_id(0)
    b = pl.program_id(1)
    qb = pl.program_id(2)

    @pl.when((b == 0) & (qb == 0))
    def _():
        row = lax.broadcasted_iota(jnp.int32, (t, t), 0)
        col = lax.broadcasted_iota(jnp.int32, (t, t), 1)
        bk_diag = _bucket(row - col, n_buckets)
        bk_near = _bucket(row - col + t, n_buckets)
        for e in range(2):
            for mp in range(2):
                hcol = mp * heads + 2 * hp + e
                far = tbl_ref[n_buckets - 1, hcol]
                bd = jnp.zeros((t, t), F32)
                bn = jnp.zeros((t, t), F32)
                for k in range(n_buckets - 1):
                    val = tbl_ref[k, hcol] - far
                    bd = jnp.where(bk_diag == k, val, bd)
                    bn = jnp.where(bk_near == k, val, bn)
                rs = slice((2 * e + mp) * t, (2 * e + mp + 1) * t)
                bdiag_ref[rs, :] = jnp.where(col <= row, bd, NEG_INF)
                bnear_ref[rs, :] = bn

    lane = lax.broadcasted_iota(jnp.int32, (t, 2 * dh), 1)
    scale = dh ** -0.5
    q_maps = [q1_ref[...] * scale, q2_ref[...] * scale]
    zero = jnp.zeros((t, 2 * dh), F32)
    blocks = []
    for e in range(2):
        in_e = (lane >= e * dh) & (lane < (e + 1) * dh)
        for mp in range(2):
            qe = jnp.where(in_e, q_maps[mp], 0.0)
            blocks.append(jnp.concatenate([qe, zero] if mp == 0 else [zero, qe], axis=1))
    q_all = jnp.concatenate(blocks, axis=0).astype(BF16)

    m_ref[...] = jnp.full(m_ref.shape, NEG_INF, F32)
    l_ref[...] = jnp.zeros(l_ref.shape, F32)
    acc_ref[...] = jnp.zeros(acc_ref.shape, F32)

    def step(j, bias):
        start = pl.multiple_of(j * t, t)
        kb = jnp.concatenate([k1_ref[:, pl.ds(start, t)], k2_ref[:, pl.ds(start, t)]],
                             axis=0).astype(BF16)
        s = jnp.dot(q_all, kb, preferred_element_type=F32)
        if bias is not None:
            s = s + bias
        m_prev = m_ref[...]
        m_new = jnp.maximum(m_prev, jnp.max(s, axis=1, keepdims=True))
        p = jnp.exp(s - jnp.tile(m_new, (1, t // LANES)))
        alpha = jnp.exp(m_prev - m_new)
        l_ref[...] = alpha * l_ref[...] + jnp.sum(p, axis=1, keepdims=True)
        m_ref[...] = m_new
        pb = p.astype(BF16)
        for e in range(2):
            ve = v_ref[pl.ds(start, t), e * dv:(e + 1) * dv].astype(BF16)
            rs = slice(2 * e * t, 2 * (e + 1) * t)
            acc_ref[rs, :] = acc_ref[rs, :] * alpha[rs] + jnp.dot(
                pb[rs], ve, preferred_element_type=F32)

    def far_body(j, carry):
        step(j, None)
        return carry

    lax.fori_loop(0, qb - 1, far_body, 0)

    @pl.when(qb > 0)
    def _():
        step(qb - 1, bnear_ref[...])

    step(qb, bdiag_ref[...])
    o = acc_ref[...] / l_ref[...]
    lam = lam_ref[0, 0]
    nw = nw_ref[...]
    for e in range(2):
        oe = o[2 * e * t:(2 * e + 1) * t] - lam * o[(2 * e + 1) * t:(2 * e + 2) * t]
        o_ref[:, e * dv:(e + 1) * dv] = (_rms(oe, nw) * out_scale).astype(o_ref.dtype)


def _attn_prompt(u, qcol, vcol, kt, tbl, lam, nw, batch, seqlen, heads, dh, dv, out_scale):
    assert dv == LANES and 2 * dh == LANES and heads % 2 == 0
    n_buckets = tbl.shape[0]
    t = _pick(seqlen, (256, 128))
    assert seqlen % t == 0 and t % LANES == 0 and t + 1 >= _far_start(n_buckets)
    nq = seqlen // t
    nhp = heads // 2
    qc, vc = qcol // LANES, vcol // (2 * dv)
    smem = pl.BlockSpec(memory_space=pltpu.SMEM)
    kern = functools.partial(_attn_kernel, t=t, dh=dh, dv=dv, heads=heads, n_buckets=n_buckets,
                             out_scale=out_scale)
    return pl.pallas_call(
        kern,
        grid=(nhp, batch, nq),
        in_specs=[smem, smem,
                  pl.BlockSpec((1, dv), lambda h, b, i: (0, 0)),
                  pl.BlockSpec((t, LANES), lambda h, b, i: (b * nq + i, qc + h)),
                  pl.BlockSpec((t, LANES), lambda h, b, i: (b * nq + i, qc + nhp + h)),
                  pl.BlockSpec((2 * dh, seqlen), lambda h, b, i: (b * 2 * nhp + h, 0)),
                  pl.BlockSpec((2 * dh, seqlen), lambda h, b, i: (b * 2 * nhp + nhp + h, 0)),
                  pl.BlockSpec((seqlen, 2 * dv), lambda h, b, i: (b, vc + h))],
        out_specs=pl.BlockSpec((t, 2 * dv), lambda h, b, i: (b * nq + i, h)),
        out_shape=jax.ShapeDtypeStruct((batch * seqlen, heads * dv), BF16),
        scratch_shapes=[pltpu.VMEM((4 * t, t), F32), pltpu.VMEM((4 * t, t), F32),
                        pltpu.VMEM((4 * t, LANES), F32), pltpu.VMEM((4 * t, LANES), F32),
                        pltpu.VMEM((4 * t, dv), F32)],
        compiler_params=_params("arbitrary", "arbitrary", "arbitrary"),
        name="attn_prompt",
    )(tbl, lam, nw, u, u, kt, kt, u)


def _decode_kernel(pt_ref, lam_ref, trow_ref, nw_ref, q_ref, kn_ref, vn_ref, *rest,
                   pages, page, lq, heads, dh, dv, n_buckets, past_len, near_slots, out_scale):
    k_refs = rest[:pages]
    v_refs = rest[pages:2 * pages]
    o_ref, m_ref, l_ref, acc_ref, bias_ref = rest[2 * pages:]
    b = pl.program_id(0)
    g = pl.program_id(1)
    ng = pl.num_programs(1)
    mh = 2 * heads
    rows = mh * lq
    n_near = len(near_slots)
    order = [m * heads + h for h in range(heads) for m in range(2)]

    def near_bias(key0, n_valid):
        r = lax.broadcasted_iota(jnp.int32, (rows, page), 0)
        col = lax.broadcasted_iota(jnp.int32, (rows, page), 1)
        dist = past_len + r % lq - (key0 + col)
        bk = _bucket(dist, n_buckets)
        far = trow_ref[:, n_buckets - 1:n_buckets]
        bias = jnp.zeros((rows, page), F32)
        for k in range(n_buckets - 1):
            bias = jnp.where(bk == k, trow_ref[:, k:k + 1] - far, bias)
        return jnp.where((dist >= 0) & (col < n_valid), bias, NEG_INF)

    @pl.when((b == 0) & (g == 0))
    def _():
        for i, slot in enumerate(near_slots):
            bias_ref[i] = near_bias(past_len - (pages - slot) * page, page)
        bias_ref[n_near] = near_bias(past_len, lq)

    @pl.when(g == 0)
    def _():
        m_ref[...] = jnp.full(m_ref.shape, NEG_INF, F32)
        l_ref[...] = jnp.zeros(l_ref.shape, F32)
        acc_ref[...] = jnp.zeros(acc_ref.shape, F32)

    q = q_ref[...] * (dh ** -0.5)
    q_parts = [q[:, j * dh:(j + 1) * dh].astype(BF16) for j in order]

    def attend(score, value_rows, bias):
        s = jnp.concatenate([score(qj, j) for qj, j in zip(q_parts, order)], axis=0)
        if bias is not None:
            s = s + bias
        m_prev = m_ref[...]
        m_new = jnp.maximum(m_prev, jnp.max(s, axis=1, keepdims=True))
        p = jnp.exp(s - jnp.tile(m_new, (1, s.shape[1] // LANES)))
        alpha = jnp.exp(m_prev - m_new)
        l_ref[...] = alpha * l_ref[...] + jnp.sum(p, axis=1, keepdims=True)
        m_ref[...] = m_new
        pv = jnp.concatenate(
            [jnp.dot(p[2 * lq * h:2 * lq * (h + 1)].astype(BF16), value_rows(h).astype(BF16),
                     preferred_element_type=F32) for h in range(heads)], axis=0)
        acc_ref[...] = acc_ref[...] * alpha + pv

    is_last = g == ng - 1
    bias = None
    if near_slots:
        zero = jnp.zeros((rows, page), F32)
        bias = jnp.concatenate(
            [jnp.where(is_last, bias_ref[near_slots.index(s)], 0.0) if s in near_slots else zero
             for s in range(pages)], axis=1)
    def page_score(qj, j):
        kt = jnp.concatenate([r[j * dh:(j + 1) * dh, :] for r in k_refs], axis=1)
        return jnp.dot(qj, kt.astype(BF16), preferred_element_type=F32)

    attend(page_score,
           lambda h: jnp.concatenate([r[pl.ds(h, page, stride=heads), :] for r in v_refs], axis=0),
           bias)

    @pl.when(is_last)
    def _():
        kn = kn_ref[...]
        vn = vn_ref[...]
        zk = jnp.zeros((page - lq, dh), F32)
        zv = jnp.zeros((page - lq, dv), F32)

        def new_score(qj, j):
            kj = jnp.concatenate([kn[:, j * dh:(j + 1) * dh], zk], axis=0).astype(BF16)
            return lax.dot_general(qj, kj, _NT, preferred_element_type=F32)

        attend(new_score,
               lambda h: jnp.concatenate([vn[:, h * dv:(h + 1) * dv], zv], axis=0),
               bias_ref[n_near])
        full = acc_ref[...] / l_ref[...]
        lam = lam_ref[0, 0]
        nw = nw_ref[...]
        for h in range(heads):
            o0 = full[2 * lq * h:2 * lq * h + lq]
            o1 = full[2 * lq * h + lq:2 * lq * (h + 1)]
            o_ref[:, h * dv:(h + 1) * dv] = (_rms(o0 - lam * o1, nw) * out_scale).astype(o_ref.dtype)


def _attn_decode(u, qcol, kcol, vcol, cache_k, cache_v, page_table, tbl, lam, nw,
                 batch, lq, heads, dh, dv, out_scale):
    n_pool, page = cache_k.shape[0], cache_k.shape[1]
    n_pages = page_table.shape[1]
    past_len = n_pages * page
    mh = 2 * heads
    kw, vw = mh * dh, heads * dv
    assert page % LANES == 0 and kw % LANES == 0 and vw % LANES == 0 and lq % 8 == 0
    n_buckets = tbl.shape[0]
    far = _far_start(n_buckets)
    first_near = max(0, -(-(past_len - page + 1 - far + 1) // page))
    pages = _pick(n_pages, (8, 4, 2, 1))
    assert n_pages - pages <= first_near, "near pages must fall in the last grid step"
    near_slots = tuple(s for s in range(pages) if n_pages - pages + s >= first_near)
    ck = jnp.transpose(cache_k, (0, 2, 3, 1)).reshape(n_pool * mh * dh, page)
    cv = cache_v.reshape(n_pool * page * heads, dv)
    trow = jnp.repeat(tbl.T.reshape(2, heads, -1).transpose(1, 0, 2).reshape(mh, -1), lq, axis=0)
    rows = mh * lq

    def page_spec(n_rows, width, slot):
        return pl.BlockSpec((n_rows, width),
                            lambda b, g, pt: (pt[b * n_pages + g * pages + slot], 0))

    smem = pl.BlockSpec(memory_space=pltpu.SMEM)
    in_specs = ([smem,
                 pl.BlockSpec((rows, n_buckets), lambda b, g, pt: (0, 0)),
                 pl.BlockSpec((1, dv), lambda b, g, pt: (0, 0)),
                 pl.BlockSpec((lq, kw), lambda b, g, pt: (b, qcol // kw)),
                 pl.BlockSpec((lq, kw), lambda b, g, pt: (b, kcol // kw)),
                 pl.BlockSpec((lq, vw), lambda b, g, pt: (b, vcol // vw))]
                + [page_spec(mh * dh, page, s) for s in range(pages)]
                + [page_spec(page * heads, dv, s) for s in range(pages)])
    assert qcol % kw == 0 and kcol % kw == 0 and vcol % vw == 0
    kern = functools.partial(_decode_kernel, pages=pages, page=page, lq=lq, heads=heads, dh=dh,
                             dv=dv, n_buckets=n_buckets, past_len=past_len,
                             near_slots=near_slots, out_scale=out_scale)
    grid_spec = pltpu.PrefetchScalarGridSpec(
        num_scalar_prefetch=1,
        grid=(batch, n_pages // pages),
        in_specs=in_specs,
        out_specs=pl.BlockSpec((lq, vw), lambda b, g, pt: (b, 0)),
        scratch_shapes=[pltpu.VMEM((rows, LANES), F32), pltpu.VMEM((rows, LANES), F32),
                        pltpu.VMEM((rows, dv), F32),
                        pltpu.VMEM((len(near_slots) + 1, rows, page), F32)])
    return pl.pallas_call(
        kern,
        grid_spec=grid_spec,
        out_shape=jax.ShapeDtypeStruct((batch * lq, vw), BF16),
        compiler_params=_params("arbitrary", "arbitrary"),
        name="attn_decode",
    )(page_table.reshape(-1), lam, trow, nw, u, u, u, *([ck] * pages), *([cv] * pages))


def _merge_kernel(*refs, n_g):
    ohg_ref, oda_ref = refs[:2]
    g_refs = refs[2:2 + 2 * n_g]
    x_ref, wbh_ref, wbd_ref, wo_ref, nw_ref, o_ref = refs[2 + 2 * n_g:]
    g_hg = jnp.concatenate([r[...] for r in g_refs[:n_g]], axis=1)
    g_da = jnp.concatenate([r[...] for r in g_refs[n_g:]], axis=1)
    y_hg = jnp.dot(ohg_ref[...], wbh_ref[...], preferred_element_type=F32)
    y_da = jnp.dot(oda_ref[...], wbd_ref[...], preferred_element_type=F32)
    mixed = (_sigmoid(g_hg) * y_hg + _sigmoid(g_da) * y_da).astype(BF16)
    z = jnp.dot(mixed, wo_ref[...], preferred_element_type=F32)
    o_ref[...] = x_ref[...] + _rms(z, nw_ref[...])


def _merge(o_hg, o_da, u, gcol, x, wbh, wbd, wo, nw):
    m, d = x.shape
    w = o_hg.shape[1]
    gw = math.gcd(gcol, d)
    n_g = d // gw
    assert gw % LANES == 0
    tm = _pick(m, (256, 128, 64, 8))
    const = lambda i: (0, 0)
    g_specs = [pl.BlockSpec((tm, gw), functools.partial(lambda i, c: (i, c), c=gcol // gw + k))
               for k in range(2 * n_g)]
    return pl.pallas_call(
        functools.partial(_merge_kernel, n_g=n_g),
        grid=(m // tm,),
        in_specs=[pl.BlockSpec((tm, w), lambda i: (i, 0)),
                  pl.BlockSpec((tm, w), lambda i: (i, 0))]
                 + g_specs
                 + [pl.BlockSpec((tm, d), lambda i: (i, 0)),
                    pl.BlockSpec((w, d), const, pipeline_mode=pl.Buffered(1)),
                    pl.BlockSpec((w, d), const, pipeline_mode=pl.Buffered(1)),
                    pl.BlockSpec((d, d), const, pipeline_mode=pl.Buffered(1)),
                    pl.BlockSpec((1, d), const)],
        out_specs=pl.BlockSpec((tm, d), lambda i: (i, 0)),
        out_shape=jax.ShapeDtypeStruct((m, d), F32),
        compiler_params=_params("parallel"),
        name="merge",
    )(o_hg, o_da, *([u] * (2 * n_g)), x, wbh, wbd, wo, nw)


def _ffn_kernel(x_ref, npre_ref, wg_ref, wu_ref, wd_ref, npost_ref, o_ref, h_ref, acc_ref):
    j = pl.program_id(1)

    @pl.when(j == 0)
    def _():
        h_ref[...] = _rms(x_ref[...], npre_ref[...]).astype(BF16)

    h = h_ref[...]
    gate = jnp.dot(h, wg_ref[...], preferred_element_type=F32)
    up = jnp.dot(h, wu_ref[...], preferred_element_type=F32)
    act = (gate * _sigmoid(gate) * up).astype(BF16)
    part = jnp.dot(act, wd_ref[...], preferred_element_type=F32)

    @pl.when(j == 0)
    def _():
        acc_ref[...] = part

    @pl.when(j > 0)
    def _():
        acc_ref[...] = acc_ref[...] + part

    @pl.when(j == pl.num_programs(1) - 1)
    def _():
        o_ref[...] = x_ref[...] + _rms(acc_ref[...], npost_ref[...])


def _ffn(x, npre, w_up, w_down, npost):
    m, d = x.shape
    ff = w_down.shape[0]
    tm = _pick(m, (512, 256, 128, 64, 8))
    tf = _pick(ff, (512, 256, 128))
    nf = ff // tf
    return pl.pallas_call(
        _ffn_kernel,
        grid=(m // tm, nf),
        in_specs=[pl.BlockSpec((tm, d), lambda i, j: (i, 0)),
                  pl.BlockSpec((1, d), lambda i, j: (0, 0)),
                  pl.BlockSpec((d, tf), lambda i, j: (0, j)),
                  pl.BlockSpec((d, tf), lambda i, j: (0, nf + j)),
                  pl.BlockSpec((tf, d), lambda i, j: (j, 0)),
                  pl.BlockSpec((1, d), lambda i, j: (0, 0))],
        out_specs=pl.BlockSpec((tm, d), lambda i, j: (i, 0)),
        out_shape=jax.ShapeDtypeStruct((m, d), F32),
        scratch_shapes=[pltpu.VMEM((tm, d), BF16), pltpu.VMEM((tm, d), F32)],
        compiler_params=_params("parallel", "arbitrary"),
        name="ffn",
    )(x, npre, w_up, w_up, w_down, npost)


def _layer(x, s0, paged, lw, dims):
    batch, seqlen, d = x.shape
    hg_heads, hg_dk, hg_dv, da_heads, da_dh, da_dv = dims
    hg_w, da_qk, da_w = hg_heads * hg_dk, 2 * da_heads * da_dh, da_heads * da_dv
    cols = np.cumsum([0, hg_w, hg_w, hg_heads * hg_dv, hg_heads * hg_dv, da_qk, da_qk, da_w, d, d])
    qcol, kcol, vcol, gcol = int(cols[4]), int(cols[5]), int(cols[6]), int(cols[7])
    x2 = x.reshape(batch * seqlen, d)

    if paged is None:
        assert da_qk == da_w
        u, kt, v2 = _norm_proj(x2, lw["norm_mix_pre"], lw["w_in"],
                               kv=(kcol, vcol, da_qk, seqlen, da_heads))
        o_da = _attn_prompt(u, qcol, vcol, kt, lw["tbl"], lw["lam"], lw["da_subln_w"],
                            batch, seqlen, da_heads, da_dh, da_dv, lw["out_scale"])
        k = kt.reshape(batch, 2 * da_heads, da_dh, seqlen).transpose(0, 3, 1, 2)
    else:
        u, = _norm_proj(x2, lw["norm_mix_pre"], lw["w_in"])
        o_da = _attn_decode(u, qcol, kcol, vcol, paged[0], paged[1], paged[2], lw["tbl"],
                            lw["lam"], lw["da_subln_w"], batch, seqlen, da_heads, da_dh, da_dv,
                            lw["out_scale"])
        k = u[:, kcol:kcol + da_qk].reshape(batch, seqlen, 2 * da_heads, da_dh)
        v2 = u[:, vcol:vcol + da_w]
    o_hg, s_new = _hgrn(u, 0, lw["lb"], lw["hg_norm_w"], s0, batch, seqlen, hg_heads, hg_dk, hg_dv)
    x1 = _merge(o_hg, o_da, u, gcol, x2, lw["w_branch_hg"], lw["w_branch_da"], lw["w_out"],
                lw["norm_mix_post"])
    y = _ffn(x1, lw["norm_ffn_pre"], lw["w_ffn_up"], lw["w_ffn_down"], lw["norm_ffn_post"])
    v = v2.reshape(batch, seqlen, da_heads, da_dv)
    return y.reshape(batch, seqlen, d), k, v, s_new


def kernel(x_prompt, x_sample, cache_k, cache_v, state_hgrn, page_table, norm_mix_pre, norm_mix_post, norm_ffn_pre, norm_ffn_post, w_in, hg_lb_logits, hg_norm_w, da_lambda_q1, da_lambda_k1, da_lambda_q2, da_lambda_k2, da_subln_w, rel_bias_table, w_branch_hg, w_branch_da, w_out, w_ffn_up, w_ffn_down):
    depth = w_in.shape[0]
    _, _, hg_heads, hg_dk, hg_dv = state_hgrn.shape
    da_heads, da_dv = cache_v.shape[3], cache_v.shape[4]
    da_dh = cache_k.shape[4]
    dims = (hg_heads, hg_dk, hg_dv, da_heads, da_dh, da_dv)
    lb_all = jnp.cumsum(jax.nn.softmax(hg_lb_logits.astype(F32), axis=0), axis=0)

    y_p, y_s = x_prompt, x_sample
    outs = [[] for _ in range(6)]
    for l in range(depth):
        lam_init = 0.8 - 0.6 * math.exp(-0.3 * l)
        lam = (jnp.exp(jnp.sum(da_lambda_q1[l] * da_lambda_k1[l]))
               - jnp.exp(jnp.sum(da_lambda_q2[l] * da_lambda_k2[l])) + lam_init)
        lw = {
            "norm_mix_pre": norm_mix_pre[l][None], "norm_mix_post": norm_mix_post[l][None],
            "norm_ffn_pre": norm_ffn_pre[l][None], "norm_ffn_post": norm_ffn_post[l][None],
            "w_in": w_in[l].astype(BF16), "lb": lb_all[l][None], "hg_norm_w": hg_norm_w[l][None],
            "da_subln_w": da_subln_w[l][None], "tbl": rel_bias_table.astype(F32),
            "lam": lam.reshape(1, 1).astype(F32), "out_scale": 1.0 - lam_init,
            "w_branch_hg": w_branch_hg[l].astype(BF16), "w_branch_da": w_branch_da[l].astype(BF16),
            "w_out": w_out[l].astype(BF16), "w_ffn_up": w_ffn_up[l].astype(BF16),
            "w_ffn_down": w_ffn_down[l].astype(BF16),
        }
        y_p, kp, vp, sp = _layer(y_p, None, None, lw, dims)
        y_s, ks, vs, ss = _layer(y_s, state_hgrn[l], (cache_k[l], cache_v[l], page_table), lw, dims)
        for acc, val in zip(outs, (kp, vp, sp, ks, vs, ss)):
            acc.append(val)
    return (y_p, y_s) + tuple(jnp.stack(o) for o in outs)
```

```python
import functools
import math

import jax
import jax.numpy as jnp
import numpy as np
from jax import lax
from jax.experimental import pallas as pl
from jax.experimental.pallas import tpu as pltpu

F32 = jnp.float32
BF16 = jnp.bfloat16

RMS_EPS = 1e-6
NEG_INF = -1e30
HG_CHUNK = 64
REL_MAX_DISTANCE = 128
LANES = 128
VMEM_LIMIT = 56 * 1024 * 1024

_NT = (((1,), (1,)), ((), ()))
_TN = (((0,), (0,)), ((), ()))


def _params(*sem):
    return pltpu.CompilerParams(dimension_semantics=sem, vmem_limit_bytes=VMEM_LIMIT)


def _rms(x, w):
    return x * lax.rsqrt(jnp.mean(x * x, axis=-1, keepdims=True) + RMS_EPS) * w


def _sigmoid(x):
    return 1.0 / (1.0 + jnp.exp(-x))


def _pick(n, prefs):
    for p in prefs:
        if n % p == 0:
            return p
    return n


def _norm_proj_kernel(x_ref, nw_ref, w_ref, o_ref, *rest, jk, jv, v_heads):
    h_ref = rest[-1]
    j = pl.program_id(1)

    @pl.when(j == 0)
    def _():
        h_ref[...] = _rms(x_ref[...], nw_ref[...]).astype(BF16)

    res = jnp.dot(h_ref[...], w_ref[...], preferred_element_type=F32)
    o_ref[...] = res
    if jk is not None:
        kt_ref, v_ref = rest[:2]
        tm, tn = res.shape

        @pl.when(j == jk)
        def _():
            kt_ref[...] = res.T

        @pl.when(j == jv)
        def _():
            dv = tn // v_heads
            for h in range(v_heads):
                v_ref[pl.ds(h, tm, stride=v_heads), :] = res[:, h * dv:(h + 1) * dv]


def _norm_proj(x, nw, w_bf, kv=None):
    m, d = x.shape
    n = w_bf.shape[1]
    tm = _pick(m, (1024, 512, 256, 128, 64, 8))
    tn = _pick(n, (1024, 512, 256, 128))
    out_specs = [pl.BlockSpec((tm, tn), lambda i, j: (i, j))]
    out_shape = [jax.ShapeDtypeStruct((m, n), F32)]
    jk = jv = v_heads = None
    if kv is not None:
        kcol, vcol, width, seqlen, v_heads = kv
        tn = width
        assert n % tn == 0 and kcol % tn == 0 and vcol % tn == 0 and seqlen % tm == 0
        jk, jv = kcol // tn, vcol // tn
        nt = seqlen // tm
        out_specs = [pl.BlockSpec((tm, tn), lambda i, j: (i, j)),
                     pl.BlockSpec((tn, tm), lambda i, j: (i // nt, i % nt)),
                     pl.BlockSpec((tm * v_heads, tn // v_heads), lambda i, j: (i, 0))]
        out_shape += [jax.ShapeDtypeStruct((m // seqlen * tn, seqlen), F32),
                      jax.ShapeDtypeStruct((m * v_heads, tn // v_heads), F32)]
    return pl.pallas_call(
        functools.partial(_norm_proj_kernel, jk=jk, jv=jv, v_heads=v_heads),
        grid=(m // tm, n // tn),
        in_specs=[pl.BlockSpec((tm, d), lambda i, j: (i, 0), pipeline_mode=pl.Buffered(1)),
                  pl.BlockSpec((1, d), lambda i, j: (0, 0)),
                  pl.BlockSpec((d, tn), lambda i, j: (0, j))],
        out_specs=out_specs,
        out_shape=out_shape,
        scratch_shapes=[pltpu.VMEM((tm, d), BF16)],
        compiler_params=_params("parallel", "arbitrary"),
        name="norm_proj",
    )(x, nw, w_bf)


def _hgrn_kernel(*refs, heads, dk, dv, chunk, rows, has_s0):
    if has_s0:
        q_ref, f_ref, i_ref, gate_ref, lb_ref, nw_ref, s0_ref, o_ref, sout_ref, st_ref = refs
    else:
        q_ref, f_ref, i_ref, gate_ref, lb_ref, nw_ref, o_ref, sout_ref, st_ref = refs
    t = pl.program_id(1)
    real = q_ref.shape[0]

    @pl.when(t == 0)
    def _():
        for h in range(heads):
            if has_s0:
                st_ref[h] = s0_ref[h].T
            else:
                st_ref[h] = jnp.zeros((dv, dk), F32)

    lb = lb_ref[...]
    f = lb + (1.0 - lb) * _sigmoid(f_ref[...])
    g = jnp.log(f)
    kin = 1.0 - f
    q = q_ref[...]
    v = i_ref[...]
    if rows > real:
        def pad(a):
            return jnp.concatenate([a, jnp.zeros((rows - real, a.shape[1]), F32)], axis=0)
        g, kin, q, v = pad(g), pad(kin), pad(q), pad(v)

    r = lax.broadcasted_iota(jnp.int32, (rows, rows), 0)
    c = lax.broadcasted_iota(jnp.int32, (rows, rows), 1)
    tri = jnp.where((r // chunk == c // chunk) & (c <= r), 1.0, 0.0).astype(BF16)
    g_hi = g.astype(BF16)
    g_r1 = g - g_hi.astype(F32)
    g_mid = g_r1.astype(BF16)
    g_lo = (g_r1 - g_mid.astype(F32)).astype(BF16)
    G = (jnp.dot(tri, g_hi, preferred_element_type=F32)
         + jnp.dot(tri, g_mid, preferred_element_type=F32)
         + jnp.dot(tri, g_lo, preferred_element_type=F32))

    qg = (q * jnp.exp(G)).astype(BF16)
    kg = (kin * jnp.exp(-G)).astype(BF16)
    vb = v.astype(BF16)
    cr = lax.broadcasted_iota(jnp.int32, (chunk, chunk), 0)
    cc = lax.broadcasted_iota(jnp.int32, (chunk, chunk), 1)
    causal = cc <= cr
    nw = nw_ref[...]

    for ci in range(rows // chunk):
        lo = ci * chunk
        Gc = G[lo:lo + chunk]
        Gl = Gc[chunk - 1:chunk]
        kdec = (kin[lo:lo + chunk] * jnp.exp(Gl - Gc)).astype(BF16)
        decay = jnp.exp(Gl)
        n_out = min(chunk, real - lo)
        for h in range(heads):
            sk = slice(h * dk, (h + 1) * dk)
            sv = slice(h * dv, (h + 1) * dv)
            qg_h = qg[lo:lo + chunk, sk]
            v_h = vb[lo:lo + chunk, sv]
            a = lax.dot_general(qg_h, kg[lo:lo + chunk, sk], _NT, preferred_element_type=F32)
            a = jnp.where(causal, a, 0.0).astype(BF16)
            st = st_ref[h]
            o = (jnp.dot(a, v_h, preferred_element_type=F32)
                 + lax.dot_general(qg_h, st.astype(BF16), _NT, preferred_element_type=F32))
            st_ref[h] = st * decay[:, sk] + lax.dot_general(v_h, kdec[:, sk], _TN,
                                                            preferred_element_type=F32)
            if n_out > 0:
                gt = gate_ref[lo:lo + n_out, sv]
                on = _rms(o[:n_out], nw) * (gt * _sigmoid(gt))
                o_ref[lo:lo + n_out, sv] = on.astype(o_ref.dtype)

    @pl.when(t == pl.num_programs(1) - 1)
    def _():
        for h in range(heads):
            sout_ref[h] = st_ref[h].T


def _hgrn(u, col0, lb, nw, s0, batch, seqlen, heads, dk, dv):
    width = heads * dk
    assert dk == dv and col0 % width == 0
    cb = col0 // width
    chunk = min(HG_CHUNK, seqlen)
    if seqlen >= LANES:
        tb = _pick(seqlen, (256, 128))
        rows = tb
        assert tb % chunk == 0
    else:
        tb = seqlen
        rows = LANES
        chunk = LANES
    nt = seqlen // tb
    has_s0 = s0 is not None

    def col(k):
        return pl.BlockSpec((tb, width), lambda b, t: (b * nt + t, cb + k))

    in_specs = [col(0), col(1), col(2), col(3),
                pl.BlockSpec((1, width), lambda b, t: (0, 0)),
                pl.BlockSpec((1, dv), lambda b, t: (0, 0))]
    args = [u, u, u, u, lb, nw]
    if has_s0:
        in_specs.append(pl.BlockSpec((None, heads, dk, dv), lambda b, t: (b, 0, 0, 0)))
        args.append(s0)
    kern = functools.partial(_hgrn_kernel, heads=heads, dk=dk, dv=dv, chunk=chunk, rows=rows,
                             has_s0=has_s0)
    return pl.pallas_call(
        kern,
        grid=(batch, nt),
        in_specs=in_specs,
        out_specs=[pl.BlockSpec((tb, heads * dv), lambda b, t: (b * nt + t, 0)),
                   pl.BlockSpec((None, heads, dk, dv), lambda b, t: (b, 0, 0, 0))],
        out_shape=[jax.ShapeDtypeStruct((batch * seqlen, heads * dv), BF16),
                   jax.ShapeDtypeStruct((batch, heads, dk, dv), F32)],
        scratch_shapes=[pltpu.VMEM((heads, dv, dk), F32)],
        compiler_params=_params("parallel", "arbitrary"),
        name="hgrn",
    )(*args)


def _bucket(dist, n_buckets):
    n = jnp.maximum(dist, 0)
    max_exact = n_buckets // 2
    nf = jnp.maximum(n, 1).astype(F32)
    large = max_exact + (jnp.log(nf / max_exact) / math.log(REL_MAX_DISTANCE / max_exact)
                         * (n_buckets - max_exact)).astype(jnp.int32)
    large = jnp.minimum(large, n_buckets - 1)
    return jnp.where(n < max_exact, n, large)


def _far_start(n_buckets):
    n = np.arange(1, 8 * REL_MAX_DISTANCE, dtype=np.int64)
    max_exact = n_buckets // 2
    nf = n.astype(np.float32)
    large = max_exact + (np.log(nf / np.float32(max_exact)) / np.float32(math.log(REL_MAX_DISTANCE / max_exact))
                         * np.float32(n_buckets - max_exact)).astype(np.int32)
    b = np.where(n < max_exact, n, np.minimum(large, n_buckets - 1))
    below = n[b < n_buckets - 1]
    return int(below.max()) + 2


def _attn_kernel(tbl_ref, lam_ref, nw_ref, q1_ref, q2_ref, k1_ref, k2_ref, v_ref, o_ref,
                 bnear_ref, bdiag_ref, m_ref, l_ref, acc_ref, *,
                 t, dh, dv, heads, n_buckets, out_scale):
    hp = pl.program_id(0)
    b = pl.program_id(1)
    qb = pl.program_id(2)

    @pl.when((b == 0) & (qb == 0))
    def _():
        row = lax.broadcasted_iota(jnp.int32, (t, t), 0)
        col = lax.broadcasted_iota(jnp.int32, (t, t), 1)
        bk_diag = _bucket(row - col, n_buckets)
        bk_near = _bucket(row - col + t, n_buckets)
        for e in range(2):
            for mp in range(2):
                hcol = mp * heads + 2 * hp + e
                far = tbl_ref[n_buckets - 1, hcol]
                bd = jnp.zeros((t, t), F32)
                bn = jnp.zeros((t, t), F32)
                for k in range(n_buckets - 1):
                    val = tbl_ref[k, hcol] - far
                    bd = jnp.where(bk_diag == k, val, bd)
                    bn = jnp.where(bk_near == k, val, bn)
                rs = slice((2 * e + mp) * t, (2 * e + mp + 1) * t)
                bdiag_ref[rs, :] = jnp.where(col <= row, bd, NEG_INF)
                bnear_ref[rs, :] = bn

    lane = lax.broadcasted_iota(jnp.int32, (t, 2 * dh), 1)
    scale = dh ** -0.5
    q_maps = [q1_ref[...] * scale, q2_ref[...] * scale]
    zero = jnp.zeros((t, 2 * dh), F32)
    blocks = []
    for e in range(2):
        in_e = (lane >= e * dh) & (lane < (e + 1) * dh)
        for mp in range(2):
            qe = jnp.where(in_e, q_maps[mp], 0.0)
            blocks.append(jnp.concatenate([qe, zero] if mp == 0 else [zero, qe], axis=1))
    q_all = jnp.concatenate(blocks, axis=0).astype(BF16)

    m_ref[...] = jnp.full(m_ref.shape, NEG_INF, F32)
    l_ref[...] = jnp.zeros(l_ref.shape, F32)
    acc_ref[...] = jnp.zeros(acc_ref.shape, F32)

    def step(j, bias):
        start = pl.multiple_of(j * t, t)
        kb = jnp.concatenate([k1_ref[:, pl.ds(start, t)], k2_ref[:, pl.ds(start, t)]],
                             axis=0).astype(BF16)
        s = jnp.dot(q_all, kb, preferred_element_type=F32)
        if bias is not None:
            s = s + bias
        m_prev = m_ref[...]
        m_new = jnp.maximum(m_prev, jnp.max(s, axis=1, keepdims=True))
        p = jnp.exp(s - jnp.tile(m_new, (1, t // LANES)))
        alpha = jnp.exp(m_prev - m_new)
        l_ref[...] = alpha * l_ref[...] + jnp.sum(p, axis=1, keepdims=True)
        m_ref[...] = m_new
        pb = p.astype(BF16)
        for e in range(2):
            ve = v_ref[pl.ds(start, t), e * dv:(e + 1) * dv].astype(BF16)
            rs = slice(2 * e * t, 2 * (e + 1) * t)
            acc_ref[rs, :] = acc_ref[rs, :] * alpha[rs] + jnp.dot(
                pb[rs], ve, preferred_element_type=F32)

    def far_body(j, carry):
        step(j, None)
        return carry

    lax.fori_loop(0, qb - 1, far_body, 0)

    @pl.when(qb > 0)
    def _():
        step(qb - 1, bnear_ref[...])

    step(qb, bdiag_ref[...])
    o = acc_ref[...] / l_ref[...]
    lam = lam_ref[0, 0]
    nw = nw_ref[...]
    for e in range(2):
        oe = o[2 * e * t:(2 * e + 1) * t] - lam * o[(2 * e + 1) * t:(2 * e + 2) * t]
        o_ref[:, e * dv:(e + 1) * dv] = (_rms(oe, nw) * out_scale).astype(o_ref.dtype)


def _attn_prompt(u, qcol, vcol, kt, tbl, lam, nw, batch, seqlen, heads, dh, dv, out_scale):
    assert dv == LANES and 2 * dh == LANES and heads % 2 == 0
    n_buckets = tbl.shape[0]
    t = _pick(seqlen, (256, 128))
    assert seqlen % t == 0 and t % LANES == 0 and t + 1 >= _far_start(n_buckets)
    nq = seqlen // t
    nhp = heads // 2
    qc, vc = qcol // LANES, vcol // (2 * dv)
    smem = pl.BlockSpec(memory_space=pltpu.SMEM)
    kern = functools.partial(_attn_kernel, t=t, dh=dh, dv=dv, heads=heads, n_buckets=n_buckets,
                             out_scale=out_scale)
    return pl.pallas_call(
        kern,
        grid=(nhp, batch, nq),
        in_specs=[smem, smem,
                  pl.BlockSpec((1, dv), lambda h, b, i: (0, 0)),
                  pl.BlockSpec((t, LANES), lambda h, b, i: (b * nq + i, qc + h)),
                  pl.BlockSpec((t, LANES), lambda h, b, i: (b * nq + i, qc + nhp + h)),
                  pl.BlockSpec((2 * dh, seqlen), lambda h, b, i: (b * 2 * nhp + h, 0)),
                  pl.BlockSpec((2 * dh, seqlen), lambda h, b, i: (b * 2 * nhp + nhp + h, 0)),
                  pl.BlockSpec((seqlen, 2 * dv), lambda h, b, i: (b, vc + h))],
        out_specs=pl.BlockSpec((t, 2 * dv), lambda h, b, i: (b * nq + i, h)),
        out_shape=jax.ShapeDtypeStruct((batch * seqlen, heads * dv), BF16),
        scratch_shapes=[pltpu.VMEM((4 * t, t), F32), pltpu.VMEM((4 * t, t), F32),
                        pltpu.VMEM((4 * t, LANES), F32), pltpu.VMEM((4 * t, LANES), F32),
                        pltpu.VMEM((4 * t, dv), F32)],
        compiler_params=_params("arbitrary", "arbitrary", "arbitrary"),
        name="attn_prompt",
    )(tbl, lam, nw, u, u, kt, kt, u)


def _decode_kernel(pt_ref, lam_ref, trow_ref, nw_ref, q_ref, kn_ref, vn_ref, *rest,
                   pages, page, lq, heads, dh, dv, n_buckets, past_len, near_slots, out_scale):
    k_refs = rest[:pages]
    v_refs = rest[pages:2 * pages]
    o_ref, m_ref, l_ref, acc_ref, bias_ref = rest[2 * pages:]
    b = pl.program_id(0)
    g = pl.program_id(1)
    ng = pl.num_programs(1)
    mh = 2 * heads
    rows = mh * lq
    n_near = len(near_slots)
    order = [m * heads + h for h in range(heads) for m in range(2)]

    def near_bias(key0, n_valid):
        r = lax.broadcasted_iota(jnp.int32, (rows, page), 0)
        col = lax.broadcasted_iota(jnp.int32, (rows, page), 1)
        dist = past_len + r % lq - (key0 + col)
        bk = _bucket(dist, n_buckets)
        far = trow_ref[:, n_buckets - 1:n_buckets]
        bias = jnp.zeros((rows, page), F32)
        for k in range(n_buckets - 1):
            bias = jnp.where(bk == k, trow_ref[:, k:k + 1] - far, bias)
        return jnp.where((dist >= 0) & (col < n_valid), bias, NEG_INF)

    @pl.when((b == 0) & (g == 0))
    def _():
        for i, slot in enumerate(near_slots):
            bias_ref[i] = near_bias(past_len - (pages - slot) * page, page)
        bias_ref[n_near] = near_bias(past_len, lq)

    @pl.when(g == 0)
    def _():
        m_ref[...] = jnp.full(m_ref.shape, NEG_INF, F32)
        l_ref[...] = jnp.zeros(l_ref.shape, F32)
        acc_ref[...] = jnp.zeros(acc_ref.shape, F32)

    q = q_ref[...] * (dh ** -0.5)
    q_parts = [q[:, j * dh:(j + 1) * dh].astype(BF16) for j in order]

    def attend(score, value_rows, bias):
        s = jnp.concatenate([score(qj, j) for qj, j in zip(q_parts, order)], axis=0)
        if bias is not None:
            s = s + bias
        m_prev = m_ref[...]
        m_new = jnp.maximum(m_prev, jnp.max(s, axis=1, keepdims=True))
        p = jnp.exp(s - jnp.tile(m_new, (1, s.shape[1] // LANES)))
        alpha = jnp.exp(m_prev - m_new)
        l_ref[...] = alpha * l_ref[...] + jnp.sum(p, axis=1, keepdims=True)
        m_ref[...] = m_new
        pv = jnp.concatenate(
            [jnp.dot(p[2 * lq * h:2 * lq * (h + 1)].astype(BF16), value_rows(h).astype(BF16),
                     preferred_element_type=F32) for h in range(heads)], axis=0)
        acc_ref[...] = acc_ref[...] * alpha + pv

    is_last = g == ng - 1
    bias = None
    if near_slots:
        zero = jnp.zeros((rows, page), F32)
        bias = jnp.concatenate(
            [jnp.where(is_last, bias_ref[near_slots.index(s)], 0.0) if s in near_slots else zero
             for s in range(pages)], axis=1)
    def page_score(qj, j):
        kt = jnp.concatenate([r[j * dh:(j + 1) * dh, :] for r in k_refs], axis=1)
        return jnp.dot(qj, kt.astype(BF16), preferred_element_type=F32)

    attend(page_score,
           lambda h: jnp.concatenate([r[pl.ds(h, page, stride=heads), :] for r in v_refs], axis=0),
           bias)

    @pl.when(is_last)
    def _():
        kn = kn_ref[...]
        vn = vn_ref[...]
        zk = jnp.zeros((page - lq, dh), F32)
        zv = jnp.zeros((page - lq, dv), F32)

        def new_score(qj, j):
            kj = jnp.concatenate([kn[:, j * dh:(j + 1) * dh], zk], axis=0).astype(BF16)
            return lax.dot_general(qj, kj, _NT, preferred_element_type=F32)

        attend(new_score,
               lambda h: jnp.concatenate([vn[:, h * dv:(h + 1) * dv], zv], axis=0),
               bias_ref[n_near])
        full = acc_ref[...] / l_ref[...]
        lam = lam_ref[0, 0]
        nw = nw_ref[...]
        for h in range(heads):
            o0 = full[2 * lq * h:2 * lq * h + lq]
            o1 = full[2 * lq * h + lq:2 * lq * (h + 1)]
            o_ref[:, h * dv:(h + 1) * dv] = (_rms(o0 - lam * o1, nw) * out_scale).astype(o_ref.dtype)


def _attn_decode(u, qcol, kcol, vcol, cache_k, cache_v, page_table, tbl, lam, nw,
                 batch, lq, heads, dh, dv, out_scale):
    n_pool, page = cache_k.shape[0], cache_k.shape[1]
    n_pages = page_table.shape[1]
    past_len = n_pages * page
    mh = 2 * heads
    kw, vw = mh * dh, heads * dv
    assert page % LANES == 0 and kw % LANES == 0 and vw % LANES == 0 and lq % 8 == 0
    n_buckets = tbl.shape[0]
    far = _far_start(n_buckets)
    first_near = max(0, -(-(past_len - page + 1 - far + 1) // page))
    pages = _pick(n_pages, (8, 4, 2, 1))
    assert n_pages - pages <= first_near, "near pages must fall in the last grid step"
    near_slots = tuple(s for s in range(pages) if n_pages - pages + s >= first_near)
    ck = jnp.transpose(cache_k, (0, 2, 3, 1)).reshape(n_pool * mh * dh, page)
    cv = cache_v.reshape(n_pool * page * heads, dv)
    trow = jnp.repeat(tbl.T.reshape(2, heads, -1).transpose(1, 0, 2).reshape(mh, -1), lq, axis=0)
    rows = mh * lq

    def page_spec(n_rows, width, slot):
        return pl.BlockSpec((n_rows, width),
                            lambda b, g, pt: (pt[b * n_pages + g * pages + slot], 0))

    smem = pl.BlockSpec(memory_space=pltpu.SMEM)
    in_specs = ([smem,
                 pl.BlockSpec((rows, n_buckets), lambda b, g, pt: (0, 0)),
                 pl.BlockSpec((1, dv), lambda b, g, pt: (0, 0)),
                 pl.BlockSpec((lq, kw), lambda b, g, pt: (b, qcol // kw)),
                 pl.BlockSpec((lq, kw), lambda b, g, pt: (b, kcol // kw)),
                 pl.BlockSpec((lq, vw), lambda b, g, pt: (b, vcol // vw))]
                + [page_spec(mh * dh, page, s) for s in range(pages)]
                + [page_spec(page * heads, dv, s) for s in range(pages)])
    assert qcol % kw == 0 and kcol % kw == 0 and vcol % vw == 0
    kern = functools.partial(_decode_kernel, pages=pages, page=page, lq=lq, heads=heads, dh=dh,
                             dv=dv, n_buckets=n_buckets, past_len=past_len,
                             near_slots=near_slots, out_scale=out_scale)
    grid_spec = pltpu.PrefetchScalarGridSpec(
        num_scalar_prefetch=1,
        grid=(batch, n_pages // pages),
        in_specs=in_specs,
        out_specs=pl.BlockSpec((lq, vw), lambda b, g, pt: (b, 0)),
        scratch_shapes=[pltpu.VMEM((rows, LANES), F32), pltpu.VMEM((rows, LANES), F32),
                        pltpu.VMEM((rows, dv), F32),
                        pltpu.VMEM((len(near_slots) + 1, rows, page), F32)])
    return pl.pallas_call(
        kern,
        grid_spec=grid_spec,
        out_shape=jax.ShapeDtypeStruct((batch * lq, vw), BF16),
        compiler_params=_params("arbitrary", "arbitrary"),
        name="attn_decode",
    )(page_table.reshape(-1), lam, trow, nw, u, u, u, *([ck] * pages), *([cv] * pages))


def _merge_kernel(*refs, n_g):
    ohg_ref, oda_ref = refs[:2]
    g_refs = refs[2:2 + 2 * n_g]
    x_ref, wbh_ref, wbd_ref, wo_ref, nw_ref, o_ref = refs[2 + 2 * n_g:]
    g_hg = jnp.concatenate([r[...] for r in g_refs[:n_g]], axis=1)
    g_da = jnp.concatenate([r[...] for r in g_refs[n_g:]], axis=1)
    y_hg = jnp.dot(ohg_ref[...], wbh_ref[...], preferred_element_type=F32)
    y_da = jnp.dot(oda_ref[...], wbd_ref[...], preferred_element_type=F32)
    mixed = (_sigmoid(g_hg) * y_hg + _sigmoid(g_da) * y_da).astype(BF16)
    z = jnp.dot(mixed, wo_ref[...], preferred_element_type=F32)
    o_ref[...] = x_ref[...] + _rms(z, nw_ref[...])


def _merge(o_hg, o_da, u, gcol, x, wbh, wbd, wo, nw):
    m, d = x.shape
    w = o_hg.shape[1]
    gw = math.gcd(gcol, d)
    n_g = d // gw
    assert gw % LANES == 0
    tm = _pick(m, (256, 128, 64, 8))
    const = lambda i: (0, 0)
    g_specs = [pl.BlockSpec((tm, gw), functools.partial(lambda i, c: (i, c), c=gcol // gw + k))
               for k in range(2 * n_g)]
    return pl.pallas_call(
        functools.partial(_merge_kernel, n_g=n_g),
        grid=(m // tm,),
        in_specs=[pl.BlockSpec((tm, w), lambda i: (i, 0)),
                  pl.BlockSpec((tm, w), lambda i: (i, 0))]
                 + g_specs
                 + [pl.BlockSpec((tm, d), lambda i: (i, 0)),
                    pl.BlockSpec((w, d), const, pipeline_mode=pl.Buffered(1)),
                    pl.BlockSpec((w, d), const, pipeline_mode=pl.Buffered(1)),
                    pl.BlockSpec((d, d), const, pipeline_mode=pl.Buffered(1)),
                    pl.BlockSpec((1, d), const)],
        out_specs=pl.BlockSpec((tm, d), lambda i: (i, 0)),
        out_shape=jax.ShapeDtypeStruct((m, d), F32),
        compiler_params=_params("parallel"),
        name="merge",
    )(o_hg, o_da, *([u] * (2 * n_g)), x, wbh, wbd, wo, nw)


def _ffn_kernel(x_ref, npre_ref, wg_ref, wu_ref, wd_ref, npost_ref, o_ref, h_ref):
    j = pl.program_id(1)

    @pl.when(j == 0)
    def _():
        h_ref[...] = _rms(x_ref[...], npre_ref[...]).astype(BF16)
        o_ref[...] = jnp.zeros(o_ref.shape, F32)

    h = h_ref[...]
    gate = jnp.dot(h, wg_ref[...], preferred_element_type=F32)
    up = jnp.dot(h, wu_ref[...], preferred_element_type=F32)
    act = (gate * _sigmoid(gate) * up).astype(BF16)
    o_ref[...] += jnp.dot(act, wd_ref[...], preferred_element_type=F32)

    @pl.when(j == pl.num_programs(1) - 1)
    def _():
        o_ref[...] = x_ref[...] + _rms(o_ref[...], npost_ref[...])


def _ffn(x, npre, w_up, w_down, npost):
    m, d = x.shape
    ff = w_down.shape[0]
    tm = _pick(m, (1024, 512, 256, 128, 64, 8))
    tf = _pick(ff, (512, 256, 128))
    nf = ff // tf
    return pl.pallas_call(
        _ffn_kernel,
        grid=(m // tm, nf),
        in_specs=[pl.BlockSpec((tm, d), lambda i, j: (i, 0), pipeline_mode=pl.Buffered(1)),
                  pl.BlockSpec((1, d), lambda i, j: (0, 0)),
                  pl.BlockSpec((d, tf), lambda i, j: (0, j)),
                  pl.BlockSpec((d, tf), lambda i, j: (0, nf + j)),
                  pl.BlockSpec((tf, d), lambda i, j: (j, 0)),
                  pl.BlockSpec((1, d), lambda i, j: (0, 0))],
        out_specs=pl.BlockSpec((tm, d), lambda i, j: (i, 0)),
        out_shape=jax.ShapeDtypeStruct((m, d), F32),
        scratch_shapes=[pltpu.VMEM((tm, d), BF16)],
        compiler_params=_params("parallel", "arbitrary"),
        name="ffn",
    )(x, npre, w_up, w_up, w_down, npost)


def _layer(x, s0, paged, lw, dims):
    batch, seqlen, d = x.shape
    hg_heads, hg_dk, hg_dv, da_heads, da_dh, da_dv = dims
    hg_w, da_qk, da_w = hg_heads * hg_dk, 2 * da_heads * da_dh, da_heads * da_dv
    cols = np.cumsum([0, hg_w, hg_w, hg_heads * hg_dv, hg_heads * hg_dv, da_qk, da_qk, da_w, d, d])
    qcol, kcol, vcol, gcol = int(cols[4]), int(cols[5]), int(cols[6]), int(cols[7])
    x2 = x.reshape(batch * seqlen, d)

    if paged is None:
        assert da_qk == da_w
        u, kt, v2 = _norm_proj(x2, lw["norm_mix_pre"], lw["w_in"],
                               kv=(kcol, vcol, da_qk, seqlen, da_heads))
        o_da = _attn_prompt(u, qcol, vcol, kt, lw["tbl"], lw["lam"], lw["da_subln_w"],
                            batch, seqlen, da_heads, da_dh, da_dv, lw["out_scale"])
        k = kt.reshape(batch, 2 * da_heads, da_dh, seqlen).transpose(0, 3, 1, 2)
    else:
        u, = _norm_proj(x2, lw["norm_mix_pre"], lw["w_in"])
        o_da = _attn_decode(u, qcol, kcol, vcol, paged[0], paged[1], paged[2], lw["tbl"],
                            lw["lam"], lw["da_subln_w"], batch, seqlen, da_heads, da_dh, da_dv,
                            lw["out_scale"])
        k = u[:, kcol:kcol + da_qk].reshape(batch, seqlen, 2 * da_heads, da_dh)
        v2 = u[:, vcol:vcol + da_w]
    o_hg, s_new = _hgrn(u, 0, lw["lb"], lw["hg_norm_w"], s0, batch, seqlen, hg_heads, hg_dk, hg_dv)
    x1 = _merge(o_hg, o_da, u, gcol, x2, lw["w_branch_hg"], lw["w_branch_da"], lw["w_out"],
                lw["norm_mix_post"])
    y = _ffn(x1, lw["norm_ffn_pre"], lw["w_ffn_up"], lw["w_ffn_down"], lw["norm_ffn_post"])
    v = v2.reshape(batch, seqlen, da_heads, da_dv)
    return y.reshape(batch, seqlen, d), k, v, s_new


def kernel(x_prompt, x_sample, cache_k, cache_v, state_hgrn, page_table, norm_mix_pre, norm_mix_post, norm_ffn_pre, norm_ffn_post, w_in, hg_lb_logits, hg_norm_w, da_lambda_q1, da_lambda_k1, da_lambda_q2, da_lambda_k2, da_subln_w, rel_bias_table, w_branch_hg, w_branch_da, w_out, w_ffn_up, w_ffn_down):
    depth = w_in.shape[0]
    _, _, hg_heads, hg_dk, hg_dv = state_hgrn.shape
    da_heads, da_dv = cache_v.shape[3], cache_v.shape[4]
    da_dh = cache_k.shape[4]
    dims = (hg_heads, hg_dk, hg_dv, da_heads, da_dh, da_dv)
    lb_all = jnp.cumsum(jax.nn.softmax(hg_lb_logits.astype(F32), axis=0), axis=0)

    y_p, y_s = x_prompt, x_sample
    outs = [[] for _ in range(6)]
    for l in range(depth):
        lam_init = 0.8 - 0.6 * math.exp(-0.3 * l)
        lam = (jnp.exp(jnp.sum(da_lambda_q1[l] * da_lambda_k1[l]))
               - jnp.exp(jnp.sum(da_lambda_q2[l] * da_lambda_k2[l])) + lam_init)
        lw = {
            "norm_mix_pre": norm_mix_pre[l][None], "norm_mix_post": norm_mix_post[l][None],
            "norm_ffn_pre": norm_ffn_pre[l][None], "norm_ffn_post": norm_ffn_post[l][None],
            "w_in": w_in[l].astype(BF16), "lb": lb_all[l][None], "hg_norm_w": hg_norm_w[l][None],
            "da_subln_w": da_subln_w[l][None], "tbl": rel_bias_table.astype(F32),
            "lam": lam.reshape(1, 1).astype(F32), "out_scale": 1.0 - lam_init,
            "w_branch_hg": w_branch_hg[l].astype(BF16), "w_branch_da": w_branch_da[l].astype(BF16),
            "w_out": w_out[l].astype(BF16), "w_ffn_up": w_ffn_up[l].astype(BF16),
            "w_ffn_down": w_ffn_down[l].astype(BF16),
        }
        y_p, kp, vp, sp = _layer(y_p, None, None, lw, dims)
        y_s, ks, vs, ss = _layer(y_s, state_hgrn[l], (cache_k[l], cache_v[l], page_table), lw, dims)
        for acc, val in zip(outs, (kp, vp, sp, ks, vs, ss)):
            acc.append(val)
    return (y_p, y_s) + tuple(jnp.stack(o) for o in outs)
```

```python
import functools
import math

import jax
import jax.numpy as jnp
import numpy as np
from jax import lax
from jax.experimental import pallas as pl
from jax.experimental.pallas import tpu as pltpu

F32 = jnp.float32
BF16 = jnp.bfloat16

RMS_EPS = 1e-6
NEG_INF = -1e30
HG_CHUNK = 64
REL_MAX_DISTANCE = 128
LANES = 128
VMEM_LIMIT = 56 * 1024 * 1024

_NT = (((1,), (1,)), ((), ()))
_TN = (((0,), (0,)), ((), ()))


def _params(*sem):
    return pltpu.CompilerParams(dimension_semantics=sem, vmem_limit_bytes=VMEM_LIMIT)


def _rms(x, w):
    return x * lax.rsqrt(jnp.mean(x * x, axis=-1, keepdims=True) + RMS_EPS) * w


def _sigmoid(x):
    return 1.0 / (1.0 + jnp.exp(-x))


def _pick(n, prefs):
    for p in prefs:
        if n % p == 0:
            return p
    return n


def _bf16_weight(w_ref, copy_ref):
    w = w_ref[...]
    if copy_ref is not None:
        w = w.astype(BF16)
        copy_ref[...] = w
    return w


def _norm_proj_kernel(x_ref, nw_ref, w_ref, o_ref, *rest, jk, jv, v_heads, emit_w):
    h_ref = rest[-1]
    j = pl.program_id(1)

    @pl.when(j == 0)
    def _():
        h_ref[...] = _rms(x_ref[...], nw_ref[...]).astype(BF16)

    w = _bf16_weight(w_ref, rest[-2] if emit_w else None)
    res = jnp.dot(h_ref[...], w, preferred_element_type=F32)
    o_ref[...] = res
    if jk is not None:
        kt_ref, v_ref = rest[:2]
        tm, tn = res.shape

        @pl.when(j == jk)
        def _():
            kt_ref[...] = res.T

        @pl.when(j == jv)
        def _():
            dv = tn // v_heads
            for h in range(v_heads):
                v_ref[pl.ds(h, tm, stride=v_heads), :] = res[:, h * dv:(h + 1) * dv]


def _norm_proj(x, nw, w, kv=None):
    m, d = x.shape
    n = w.shape[1]
    emit_w = w.dtype != BF16
    tm = _pick(m, (1024, 512, 256, 128, 64, 32, 16, 8))
    tn = _pick(n, (1024, 512, 256, 128))
    out_specs = [pl.BlockSpec((tm, tn), lambda i, j: (i, j))]
    out_shape = [jax.ShapeDtypeStruct((m, n), F32)]
    jk = jv = v_heads = None
    if kv is not None:
        kcol, vcol, width, seqlen, v_heads = kv
        tn = width
        assert n % tn == 0 and kcol % tn == 0 and vcol % tn == 0 and seqlen % tm == 0
        jk, jv = kcol // tn, vcol // tn
        nt = seqlen // tm
        out_specs = [pl.BlockSpec((tm, tn), lambda i, j: (i, j)),
                     pl.BlockSpec((tn, tm), lambda i, j: (i // nt, i % nt)),
                     pl.BlockSpec((tm * v_heads, tn // v_heads), lambda i, j: (i, 0))]
        out_shape += [jax.ShapeDtypeStruct((m // seqlen * tn, seqlen), F32),
                      jax.ShapeDtypeStruct((m * v_heads, tn // v_heads), F32)]
    if emit_w:
        out_specs.append(pl.BlockSpec((d, tn), lambda i, j: (0, j)))
        out_shape.append(jax.ShapeDtypeStruct((d, n), BF16))
        assert m == tm, "the bf16 weight copy is written once per column tile"
    return pl.pallas_call(
        functools.partial(_norm_proj_kernel, jk=jk, jv=jv, v_heads=v_heads, emit_w=emit_w),
        grid=(m // tm, n // tn),
        in_specs=[pl.BlockSpec((tm, d), lambda i, j: (i, 0), pipeline_mode=pl.Buffered(1)),
                  pl.BlockSpec((1, d), lambda i, j: (0, 0)),
                  pl.BlockSpec((d, tn), lambda i, j: (0, j))],
        out_specs=out_specs,
        out_shape=out_shape,
        scratch_shapes=[pltpu.VMEM((tm, d), BF16)],
        compiler_params=_params("parallel", "arbitrary"),
        name="norm_proj",
    )(x, nw, w)


def _hgrn_kernel(*refs, heads, dk, dv, chunk, rows, has_s0):
    if has_s0:
        q_ref, f_ref, i_ref, gate_ref, lb_ref, nw_ref, s0_ref, o_ref, sout_ref, st_ref = refs
    else:
        q_ref, f_ref, i_ref, gate_ref, lb_ref, nw_ref, o_ref, sout_ref, st_ref = refs
    t = pl.program_id(1)
    real = q_ref.shape[0]

    @pl.when(t == 0)
    def _():
        for h in range(heads):
            if has_s0:
                st_ref[h] = s0_ref[h].T
            else:
                st_ref[h] = jnp.zeros((dv, dk), F32)

    lb = lb_ref[...]
    f = lb + (1.0 - lb) * _sigmoid(f_ref[...])
    g = jnp.log(f)
    kin = 1.0 - f
    q = q_ref[...]
    v = i_ref[...]
    if rows > real:
        def pad(a):
            return jnp.concatenate([a, jnp.zeros((rows - real, a.shape[1]), F32)], axis=0)
        g, kin, q, v = pad(g), pad(kin), pad(q), pad(v)

    r = lax.broadcasted_iota(jnp.int32, (rows, rows), 0)
    c = lax.broadcasted_iota(jnp.int32, (rows, rows), 1)
    tri = jnp.where((r // chunk == c // chunk) & (c <= r), 1.0, 0.0).astype(BF16)
    g_hi = g.astype(BF16)
    g_r1 = g - g_hi.astype(F32)
    g_mid = g_r1.astype(BF16)
    g_lo = (g_r1 - g_mid.astype(F32)).astype(BF16)
    G = (jnp.dot(tri, g_hi, preferred_element_type=F32)
         + jnp.dot(tri, g_mid, preferred_element_type=F32)
         + jnp.dot(tri, g_lo, preferred_element_type=F32))

    qg = (q * jnp.exp(G)).astype(BF16)
    kg = (kin * jnp.exp(-G)).astype(BF16)
    vb = v.astype(BF16)
    cr = lax.broadcasted_iota(jnp.int32, (chunk, chunk), 0)
    cc = lax.broadcasted_iota(jnp.int32, (chunk, chunk), 1)
    causal = cc <= cr
    nw = nw_ref[...]

    for ci in range(rows // chunk):
        lo = ci * chunk
        Gc = G[lo:lo + chunk]
        Gl = Gc[chunk - 1:chunk]
        kdec = (kin[lo:lo + chunk] * jnp.exp(Gl - Gc)).astype(BF16)
        decay = jnp.exp(Gl)
        n_out = min(chunk, real - lo)
        for h in range(heads):
            sk = slice(h * dk, (h + 1) * dk)
            sv = slice(h * dv, (h + 1) * dv)
            qg_h = qg[lo:lo + chunk, sk]
            v_h = vb[lo:lo + chunk, sv]
            a = lax.dot_general(qg_h, kg[lo:lo + chunk, sk], _NT, preferred_element_type=F32)
            a = jnp.where(causal, a, 0.0).astype(BF16)
            st = st_ref[h]
            o = (jnp.dot(a, v_h, preferred_element_type=F32)
                 + lax.dot_general(qg_h, st.astype(BF16), _NT, preferred_element_type=F32))
            st_ref[h] = st * decay[:, sk] + lax.dot_general(v_h, kdec[:, sk], _TN,
                                                            preferred_element_type=F32)
            if n_out > 0:
                gt = gate_ref[lo:lo + n_out, sv]
                on = _rms(o[:n_out], nw) * (gt * _sigmoid(gt))
                o_ref[lo:lo + n_out, sv] = on.astype(o_ref.dtype)

    @pl.when(t == pl.num_programs(1) - 1)
    def _():
        for h in range(heads):
            sout_ref[h] = st_ref[h].T


def _hgrn(u, col0, lb, nw, s0, batch, seqlen, heads, dk, dv):
    width = heads * dk
    assert dk == dv and col0 % width == 0
    cb = col0 // width
    chunk = min(HG_CHUNK, seqlen)
    if seqlen >= LANES:
        tb = _pick(seqlen, (256, 128))
        rows = tb
        assert tb % chunk == 0
    else:
        tb = seqlen
        rows = LANES
        chunk = LANES
    nt = seqlen // tb
    has_s0 = s0 is not None

    def col(k):
        return pl.BlockSpec((tb, width), lambda b, t: (b * nt + t, cb + k))

    in_specs = [col(0), col(1), col(2), col(3),
                pl.BlockSpec((1, width), lambda b, t: (0, 0)),
                pl.BlockSpec((1, dv), lambda b, t: (0, 0))]
    args = [u, u, u, u, lb, nw]
    if has_s0:
        in_specs.append(pl.BlockSpec((None, heads, dk, dv), lambda b, t: (b, 0, 0, 0)))
        args.append(s0)
    kern = functools.partial(_hgrn_kernel, heads=heads, dk=dk, dv=dv, chunk=chunk, rows=rows,
                             has_s0=has_s0)
    return pl.pallas_call(
        kern,
        grid=(batch, nt),
        in_specs=in_specs,
        out_specs=[pl.BlockSpec((tb, heads * dv), lambda b, t: (b * nt + t, 0)),
                   pl.BlockSpec((None, heads, dk, dv), lambda b, t: (b, 0, 0, 0))],
        out_shape=[jax.ShapeDtypeStruct((batch * seqlen, heads * dv), BF16),
                   jax.ShapeDtypeStruct((batch, heads, dk, dv), F32)],
        scratch_shapes=[pltpu.VMEM((heads, dv, dk), F32)],
        compiler_params=_params("parallel", "arbitrary"),
        name="hgrn",
    )(*args)


def _bucket(dist, n_buckets):
    n = jnp.maximum(dist, 0)
    max_exact = n_buckets // 2
    nf = jnp.maximum(n, 1).astype(F32)
    large = max_exact + (jnp.log(nf / max_exact) / math.log(REL_MAX_DISTANCE / max_exact)
                         * (n_buckets - max_exact)).astype(jnp.int32)
    large = jnp.minimum(large, n_buckets - 1)
    return jnp.where(n < max_exact, n, large)


def _far_start(n_buckets):
    n = np.arange(1, 8 * REL_MAX_DISTANCE, dtype=np.int64)
    max_exact = n_buckets // 2
    nf = n.astype(np.float32)
    large = max_exact + (np.log(nf / np.float32(max_exact)) / np.float32(math.log(REL_MAX_DISTANCE / max_exact))
                         * np.float32(n_buckets - max_exact)).astype(np.int32)
    b = np.where(n < max_exact, n, np.minimum(large, n_buckets - 1))
    below = n[b < n_buckets - 1]
    return int(below.max()) + 2


def _attn_kernel(tbl_ref, lam_ref, nw_ref, q1_ref, q2_ref, k1_ref, k2_ref, v_ref, o_ref,
                 bnear_ref, bdiag_ref, m_ref, l_ref, acc_ref, *,
                 t, dh, dv, heads, n_buckets, out_scale):
    hp = pl.program_id(0)
    b = pl.program_id(1)
    qb = pl.program_id(2)

    @pl.when((b == 0) & (qb == 0))
    def _():
        row = lax.broadcasted_iota(jnp.int32, (t, t), 0)
        col = lax.broadcasted_iota(jnp.int32, (t, t), 1)
        bk_diag = _bucket(row - col, n_buckets)
        bk_near = _bucket(row - col + t, n_buckets)
        for e in range(2):
            for mp in range(2):
                hcol = mp * heads + 2 * hp + e
                far = tbl_ref[n_buckets - 1, hcol]
                bd = jnp.zeros((t, t), F32)
                bn = jnp.zeros((t, t), F32)
                for k in range(n_buckets - 1):
                    val = tbl_ref[k, hcol] - far
                    bd = jnp.where(bk_diag == k, val, bd)
                    bn = jnp.where(bk_near == k, val, bn)
                rs = slice((2 * e + mp) * t, (2 * e + mp + 1) * t)
                bdiag_ref[rs, :] = jnp.where(col <= row, bd, NEG_INF)
                bnear_ref[rs, :] = bn

    lane = lax.broadcasted_iota(jnp.int32, (t, 2 * dh), 1)
    scale = dh ** -0.5
    q_maps = [q1_ref[...] * scale, q2_ref[...] * scale]
    zero = jnp.zeros((t, 2 * dh), F32)
    blocks = []
    for e in range(2):
        in_e = (lane >= e * dh) & (lane < (e + 1) * dh)
        for mp in range(2):
            qe = jnp.where(in_e, q_maps[mp], 0.0)
            blocks.append(jnp.concatenate([qe, zero] if mp == 0 else [zero, qe], axis=1))
    q_all = jnp.concatenate(blocks, axis=0).astype(BF16)

    m_ref[...] = jnp.full(m_ref.shape, NEG_INF, F32)
    l_ref[...] = jnp.zeros(l_ref.shape, F32)
    acc_ref[...] = jnp.zeros(acc_ref.shape, F32)

    def step(j, bias):
        start = pl.multiple_of(j * t, t)
        kb = jnp.concatenate([k1_ref[:, pl.ds(start, t)], k2_ref[:, pl.ds(start, t)]],
                             axis=0).astype(BF16)
        s = jnp.dot(q_all, kb, preferred_element_type=F32)
        if bias is not None:
            s = s + bias
        m_prev = m_ref[...]
        m_new = jnp.maximum(m_prev, jnp.max(s, axis=1, keepdims=True))
        p = jnp.exp(s - jnp.tile(m_new, (1, t // LANES)))
        alpha = jnp.exp(m_prev - m_new)
        l_ref[...] = alpha * l_ref[...] + jnp.sum(p, axis=1, keepdims=True)
        m_ref[...] = m_new
        pb = p.astype(BF16)
        for e in range(2):
            ve = v_ref[pl.ds(start, t), e * dv:(e + 1) * dv].astype(BF16)
            rs = slice(2 * e * t, 2 * (e + 1) * t)
            acc_ref[rs, :] = acc_ref[rs, :] * alpha[rs] + jnp.dot(
                pb[rs], ve, preferred_element_type=F32)

    def far_body(j, carry):
        step(j, None)
        return carry

    lax.fori_loop(0, qb - 1, far_body, 0)

    @pl.when(qb > 0)
    def _():
        step(qb - 1, bnear_ref[...])

    step(qb, bdiag_ref[...])
    o = acc_ref[...] / l_ref[...]
    lam = lam_ref[0, 0]
    nw = nw_ref[...]
    for e in range(2):
        oe = o[2 * e * t:(2 * e + 1) * t] - lam * o[(2 * e + 1) * t:(2 * e + 2) * t]
        o_ref[:, e * dv:(e + 1) * dv] = (_rms(oe, nw) * out_scale).astype(o_ref.dtype)


def _attn_prompt(u, qcol, vcol, kt, tbl, lam, nw, batch, seqlen, heads, dh, dv, out_scale):
    assert dv == LANES and 2 * dh == LANES and heads % 2 == 0
    n_buckets = tbl.shape[0]
    t = _pick(seqlen, (256, 128))
    assert seqlen % t == 0 and t % LANES == 0 and t + 1 >= _far_start(n_buckets)
    nq = seqlen // t
    nhp = heads // 2
    qc, vc = qcol // LANES, vcol // (2 * dv)
    smem = pl.BlockSpec(memory_space=pltpu.SMEM)
    kern = functools.partial(_attn_kernel, t=t, dh=dh, dv=dv, heads=heads, n_buckets=n_buckets,
                             out_scale=out_scale)
    return pl.pallas_call(
        kern,
        grid=(nhp, batch, nq),
        in_specs=[smem, smem,
                  pl.BlockSpec((1, dv), lambda h, b, i: (0, 0)),
                  pl.BlockSpec((t, LANES), lambda h, b, i: (b * nq + i, qc + h)),
                  pl.BlockSpec((t, LANES), lambda h, b, i: (b * nq + i, qc + nhp + h)),
                  pl.BlockSpec((2 * dh, seqlen), lambda h, b, i: (b * 2 * nhp + h, 0)),
                  pl.BlockSpec((2 * dh, seqlen), lambda h, b, i: (b * 2 * nhp + nhp + h, 0)),
                  pl.BlockSpec((seqlen, 2 * dv), lambda h, b, i: (b, vc + h))],
        out_specs=pl.BlockSpec((t, 2 * dv), lambda h, b, i: (b * nq + i, h)),
        out_shape=jax.ShapeDtypeStruct((batch * seqlen, heads * dv), BF16),
        scratch_shapes=[pltpu.VMEM((4 * t, t), F32), pltpu.VMEM((4 * t, t), F32),
                        pltpu.VMEM((4 * t, LANES), F32), pltpu.VMEM((4 * t, LANES), F32),
                        pltpu.VMEM((4 * t, dv), F32)],
        compiler_params=_params("arbitrary", "arbitrary", "arbitrary"),
        name="attn_prompt",
    )(tbl, lam, nw, u, u, kt, kt, u)


def _decode_kernel(pt_ref, lam_ref, trow_ref, nw_ref, q_ref, kn_ref, vn_ref, *rest,
                   pages, page, lq, heads, dh, dv, n_buckets, past_len, near_slots, out_scale):
    k_refs = rest[:pages]
    v_refs = rest[pages:2 * pages]
    o_ref, m_ref, l_ref, acc_ref, bias_ref = rest[2 * pages:]
    b = pl.program_id(0)
    g = pl.program_id(1)
    ng = pl.num_programs(1)
    mh = 2 * heads
    rows = mh * lq
    n_near = len(near_slots)
    order = [m * heads + h for h in range(heads) for m in range(2)]

    def near_bias(key0, n_valid):
        r = lax.broadcasted_iota(jnp.int32, (rows, page), 0)
        col = lax.broadcasted_iota(jnp.int32, (rows, page), 1)
        dist = past_len + r % lq - (key0 + col)
        bk = _bucket(dist, n_buckets)
        far = trow_ref[:, n_buckets - 1:n_buckets]
        bias = jnp.zeros((rows, page), F32)
        for k in range(n_buckets - 1):
            bias = jnp.where(bk == k, trow_ref[:, k:k + 1] - far, bias)
        return jnp.where((dist >= 0) & (col < n_valid), bias, NEG_INF)

    @pl.when((b == 0) & (g == 0))
    def _():
        for i, slot in enumerate(near_slots):
            bias_ref[i] = near_bias(past_len - (pages - slot) * page, page)
        bias_ref[n_near] = near_bias(past_len, lq)

    @pl.when(g == 0)
    def _():
        m_ref[...] = jnp.full(m_ref.shape, NEG_INF, F32)
        l_ref[...] = jnp.zeros(l_ref.shape, F32)
        acc_ref[...] = jnp.zeros(acc_ref.shape, F32)

    q = q_ref[...] * (dh ** -0.5)
    q_parts = [q[:, j * dh:(j + 1) * dh].astype(BF16) for j in order]

    def attend(score, value_rows, bias):
        s = jnp.concatenate([score(qj, j) for qj, j in zip(q_parts, order)], axis=0)
        if bias is not None:
            s = s + bias
        m_prev = m_ref[...]
        m_new = jnp.maximum(m_prev, jnp.max(s, axis=1, keepdims=True))
        p = jnp.exp(s - jnp.tile(m_new, (1, s.shape[1] // LANES)))
        alpha = jnp.exp(m_prev - m_new)
        l_ref[...] = alpha * l_ref[...] + jnp.sum(p, axis=1, keepdims=True)
        m_ref[...] = m_new
        pv = jnp.concatenate(
            [jnp.dot(p[2 * lq * h:2 * lq * (h + 1)].astype(BF16), value_rows(h).astype(BF16),
                     preferred_element_type=F32) for h in range(heads)], axis=0)
        acc_ref[...] = acc_ref[...] * alpha + pv

    is_last = g == ng - 1
    bias = None
    if near_slots:
        zero = jnp.zeros((rows, page), F32)
        bias = jnp.concatenate(
            [jnp.where(is_last, bias_ref[near_slots.index(s)], 0.0) if s in near_slots else zero
             for s in range(pages)], axis=1)
    def page_score(qj, j):
        kt = jnp.concatenate([r[j * dh:(j + 1) * dh, :] for r in k_refs], axis=1)
        return jnp.dot(qj, kt.astype(BF16), preferred_element_type=F32)

    attend(page_score,
           lambda h: jnp.concatenate([r[pl.ds(h, page, stride=heads), :] for r in v_refs], axis=0),
           bias)

    @pl.when(is_last)
    def _():
        kn = kn_ref[...]
        vn = vn_ref[...]
        zk = jnp.zeros((page - lq, dh), F32)
        zv = jnp.zeros((page - lq, dv), F32)

        def new_score(qj, j):
            kj = jnp.concatenate([kn[:, j * dh:(j + 1) * dh], zk], axis=0).astype(BF16)
            return lax.dot_general(qj, kj, _NT, preferred_element_type=F32)

        attend(new_score,
               lambda h: jnp.concatenate([vn[:, h * dv:(h + 1) * dv], zv], axis=0),
               bias_ref[n_near])
        full = acc_ref[...] / l_ref[...]
        lam = lam_ref[0, 0]
        nw = nw_ref[...]
        for h in range(heads):
            o0 = full[2 * lq * h:2 * lq * h + lq]
            o1 = full[2 * lq * h + lq:2 * lq * (h + 1)]
            o_ref[:, h * dv:(h + 1) * dv] = (_rms(o0 - lam * o1, nw) * out_scale).astype(o_ref.dtype)


def _attn_decode(u, qcol, kcol, vcol, cache_k, cache_v, page_table, tbl, lam, nw,
                 batch, lq, heads, dh, dv, out_scale):
    n_pool, page = cache_k.shape[0], cache_k.shape[1]
    n_pages = page_table.shape[1]
    past_len = n_pages * page
    mh = 2 * heads
    kw, vw = mh * dh, heads * dv
    assert page % LANES == 0 and kw % LANES == 0 and vw % LANES == 0 and lq % 8 == 0
    n_buckets = tbl.shape[0]
    far = _far_start(n_buckets)
    first_near = max(0, -(-(past_len - page + 1 - far + 1) // page))
    pages = _pick(n_pages, (8, 4, 2, 1))
    assert n_pages - pages <= first_near, "near pages must fall in the last grid step"
    near_slots = tuple(s for s in range(pages) if n_pages - pages + s >= first_near)
    ck = jnp.transpose(cache_k, (0, 2, 3, 1)).reshape(n_pool * mh * dh, page)
    cv = cache_v.reshape(n_pool * page * heads, dv)
    trow = jnp.repeat(tbl.T.reshape(2, heads, -1).transpose(1, 0, 2).reshape(mh, -1), lq, axis=0)
    rows = mh * lq

    def page_spec(n_rows, width, slot):
        return pl.BlockSpec((n_rows, width),
                            lambda b, g, pt: (pt[b * n_pages + g * pages + slot], 0))

    smem = pl.BlockSpec(memory_space=pltpu.SMEM)
    in_specs = ([smem,
                 pl.BlockSpec((rows, n_buckets), lambda b, g, pt: (0, 0)),
                 pl.BlockSpec((1, dv), lambda b, g, pt: (0, 0)),
                 pl.BlockSpec((lq, kw), lambda b, g, pt: (b, qcol // kw)),
                 pl.BlockSpec((lq, kw), lambda b, g, pt: (b, kcol // kw)),
                 pl.BlockSpec((lq, vw), lambda b, g, pt: (b, vcol // vw))]
                + [page_spec(mh * dh, page, s) for s in range(pages)]
                + [page_spec(page * heads, dv, s) for s in range(pages)])
    assert qcol % kw == 0 and kcol % kw == 0 and vcol % vw == 0
    kern = functools.partial(_decode_kernel, pages=pages, page=page, lq=lq, heads=heads, dh=dh,
                             dv=dv, n_buckets=n_buckets, past_len=past_len,
                             near_slots=near_slots, out_scale=out_scale)
    grid_spec = pltpu.PrefetchScalarGridSpec(
        num_scalar_prefetch=1,
        grid=(batch, n_pages // pages),
        in_specs=in_specs,
        out_specs=pl.BlockSpec((lq, vw), lambda b, g, pt: (b, 0)),
        scratch_shapes=[pltpu.VMEM((rows, LANES), F32), pltpu.VMEM((rows, LANES), F32),
                        pltpu.VMEM((rows, dv), F32),
                        pltpu.VMEM((len(near_slots) + 1, rows, page), F32)])
    return pl.pallas_call(
        kern,
        grid_spec=grid_spec,
        out_shape=jax.ShapeDtypeStruct((batch * lq, vw), BF16),
        compiler_params=_params("arbitrary", "arbitrary"),
        name="attn_decode",
    )(page_table.reshape(-1), lam, trow, nw, u, u, u, *([ck] * pages), *([cv] * pages))


def _merge_kernel(ohg_ref, oda_ref, ghg_ref, gda_ref, x_ref, wbh_ref, wbd_ref, wo_ref, nw_ref,
                  o_ref, *copies, emit_w):
    copies = copies if emit_w else (None, None, None)
    n = pl.program_id(1)

    @pl.when(n == 0)
    def _():
        o_ref[...] = jnp.zeros(o_ref.shape, F32)

    y_hg = jnp.dot(ohg_ref[...], _bf16_weight(wbh_ref, copies[0]), preferred_element_type=F32)
    y_da = jnp.dot(oda_ref[...], _bf16_weight(wbd_ref, copies[1]), preferred_element_type=F32)
    mixed = (_sigmoid(ghg_ref[...]) * y_hg + _sigmoid(gda_ref[...]) * y_da).astype(BF16)
    o_ref[...] += jnp.dot(mixed, _bf16_weight(wo_ref, copies[2]), preferred_element_type=F32)

    @pl.when(n == pl.num_programs(1) - 1)
    def _():
        o_ref[...] = x_ref[...] + _rms(o_ref[...], nw_ref[...])


def _merge(o_hg, o_da, u, gcol, x, wbh, wbd, wo, nw):
    m, d = x.shape
    w = o_hg.shape[1]
    emit_w = wo.dtype != BF16
    tm = _pick(m, (1024, 512, 256, 128, 64, 32, 16, 8))
    tn = _pick(math.gcd(gcol, d), (512, 256, 128))
    nn = d // tn
    gc = gcol // tn
    once = dict(pipeline_mode=pl.Buffered(1))
    out_specs = [pl.BlockSpec((tm, d), lambda i, n: (i, 0))]
    out_shape = [jax.ShapeDtypeStruct((m, d), F32)]
    if emit_w:
        assert m == tm, "the bf16 weight copies are written once per column tile"
        out_specs += [pl.BlockSpec((w, tn), lambda i, n: (0, n)),
                      pl.BlockSpec((w, tn), lambda i, n: (0, n)),
                      pl.BlockSpec((tn, d), lambda i, n: (n, 0))]
        out_shape += [jax.ShapeDtypeStruct((w, d), BF16), jax.ShapeDtypeStruct((w, d), BF16),
                      jax.ShapeDtypeStruct((d, d), BF16)]
    return pl.pallas_call(
        functools.partial(_merge_kernel, emit_w=emit_w),
        grid=(m // tm, nn),
        in_specs=[pl.BlockSpec((tm, w), lambda i, n: (i, 0), **once),
                  pl.BlockSpec((tm, w), lambda i, n: (i, 0), **once),
                  pl.BlockSpec((tm, tn), lambda i, n: (i, gc + n)),
                  pl.BlockSpec((tm, tn), lambda i, n: (i, gc + nn + n)),
                  pl.BlockSpec((tm, d), lambda i, n: (i, 0), **once),
                  pl.BlockSpec((w, tn), lambda i, n: (0, n)),
                  pl.BlockSpec((w, tn), lambda i, n: (0, n)),
                  pl.BlockSpec((tn, d), lambda i, n: (n, 0)),
                  pl.BlockSpec((1, d), lambda i, n: (0, 0))],
        out_specs=out_specs,
        out_shape=out_shape,
        compiler_params=_params("parallel", "arbitrary"),
        name="merge",
    )(o_hg, o_da, u, u, x, wbh, wbd, wo, nw)


def _ffn_kernel(x_ref, npre_ref, wg_ref, wu_ref, wd_ref, npost_ref, o_ref, *rest, emit_w):
    h_ref = rest[-1]
    copies = rest[:3] if emit_w else (None, None, None)
    j = pl.program_id(1)

    @pl.when(j == 0)
    def _():
        h_ref[...] = _rms(x_ref[...], npre_ref[...]).astype(BF16)
        o_ref[...] = jnp.zeros(o_ref.shape, F32)

    h = h_ref[...]
    gate = jnp.dot(h, _bf16_weight(wg_ref, copies[0]), preferred_element_type=F32)
    up = jnp.dot(h, _bf16_weight(wu_ref, copies[1]), preferred_element_type=F32)
    act = (gate * _sigmoid(gate) * up).astype(BF16)
    o_ref[...] += jnp.dot(act, _bf16_weight(wd_ref, copies[2]), preferred_element_type=F32)

    @pl.when(j == pl.num_programs(1) - 1)
    def _():
        o_ref[...] = x_ref[...] + _rms(o_ref[...], npost_ref[...])


def _ffn(x, npre, w_gate, w_up, w_down, npost):
    m, d = x.shape
    ff = w_down.shape[0]
    emit_w = w_down.dtype != BF16
    tm = _pick(m, (1024, 512, 256, 128, 64, 32, 16, 8))
    tf = _pick(ff, (512, 256, 128))
    nf = ff // tf
    (wg, gcol), (wu, ucol) = w_gate, w_up
    assert gcol % tf == 0 and ucol % tf == 0
    gc, uc = gcol // tf, ucol // tf
    out_specs = [pl.BlockSpec((tm, d), lambda i, j: (i, 0))]
    out_shape = [jax.ShapeDtypeStruct((m, d), F32)]
    if emit_w:
        assert m == tm, "the bf16 weight copies are written once per hidden-dim tile"
        out_specs += [pl.BlockSpec((d, tf), lambda i, j: (0, j)),
                      pl.BlockSpec((d, tf), lambda i, j: (0, j)),
                      pl.BlockSpec((tf, d), lambda i, j: (j, 0))]
        out_shape += [jax.ShapeDtypeStruct((d, ff), BF16), jax.ShapeDtypeStruct((d, ff), BF16),
                      jax.ShapeDtypeStruct((ff, d), BF16)]
    return pl.pallas_call(
        functools.partial(_ffn_kernel, emit_w=emit_w),
        grid=(m // tm, nf),
        in_specs=[pl.BlockSpec((tm, d), lambda i, j: (i, 0), pipeline_mode=pl.Buffered(1)),
                  pl.BlockSpec((1, d), lambda i, j: (0, 0)),
                  pl.BlockSpec((d, tf), lambda i, j: (0, gc + j)),
                  pl.BlockSpec((d, tf), lambda i, j: (0, uc + j)),
                  pl.BlockSpec((tf, d), lambda i, j: (j, 0)),
                  pl.BlockSpec((1, d), lambda i, j: (0, 0))],
        out_specs=out_specs,
        out_shape=out_shape,
        scratch_shapes=[pltpu.VMEM((tm, d), BF16)],
        compiler_params=_params("parallel", "arbitrary"),
        name="ffn",
    )(x, npre, wg, wu, w_down, npost)


def _layer(x, s0, paged, lw, wts, dims):
    batch, seqlen, d = x.shape
    wts = dict(wts)
    hg_heads, hg_dk, hg_dv, da_heads, da_dh, da_dv = dims
    hg_w, da_qk, da_w = hg_heads * hg_dk, 2 * da_heads * da_dh, da_heads * da_dv
    cols = np.cumsum([0, hg_w, hg_w, hg_heads * hg_dv, hg_heads * hg_dv, da_qk, da_qk, da_w, d, d])
    qcol, kcol, vcol, gcol = int(cols[4]), int(cols[5]), int(cols[6]), int(cols[7])
    x2 = x.reshape(batch * seqlen, d)

    if paged is None:
        assert da_qk == da_w
        u, kt, v2 = _norm_proj(x2, lw["norm_mix_pre"], wts["w_in"],
                               kv=(kcol, vcol, da_qk, seqlen, da_heads))
        o_da = _attn_prompt(u, qcol, vcol, kt, lw["tbl"], lw["lam"], lw["da_subln_w"],
                            batch, seqlen, da_heads, da_dh, da_dv, lw["out_scale"])
        k = kt.reshape(batch, 2 * da_heads, da_dh, seqlen).transpose(0, 3, 1, 2)
    else:
        u, *copy = _norm_proj(x2, lw["norm_mix_pre"], wts["w_in"])
        if copy:
            wts["w_in"], = copy
        o_da = _attn_decode(u, qcol, kcol, vcol, paged[0], paged[1], paged[2], lw["tbl"],
                            lw["lam"], lw["da_subln_w"], batch, seqlen, da_heads, da_dh, da_dv,
                            lw["out_scale"])
        k = u[:, kcol:kcol + da_qk].reshape(batch, seqlen, 2 * da_heads, da_dh)
        v2 = u[:, vcol:vcol + da_w]
    o_hg, s_new = _hgrn(u, 0, lw["lb"], lw["hg_norm_w"], s0, batch, seqlen, hg_heads, hg_dk, hg_dv)
    x1, *copy = _merge(o_hg, o_da, u, gcol, x2, wts["w_branch_hg"], wts["w_branch_da"],
                       wts["w_out"], lw["norm_mix_post"])
    if copy:
        wts["w_branch_hg"], wts["w_branch_da"], wts["w_out"] = copy
    y, *copy = _ffn(x1, lw["norm_ffn_pre"], wts["w_gate"], wts["w_up"], wts["w_down"],
                    lw["norm_ffn_post"])
    if copy:
        wts["w_gate"], wts["w_up"], wts["w_down"] = (copy[0], 0), (copy[1], 0), copy[2]
    v = v2.reshape(batch, seqlen, da_heads, da_dv)
    return y.reshape(batch, seqlen, d), k, v, s_new, wts


def kernel(x_prompt, x_sample, cache_k, cache_v, state_hgrn, page_table, norm_mix_pre, norm_mix_post, norm_ffn_pre, norm_ffn_post, w_in, hg_lb_logits, hg_norm_w, da_lambda_q1, da_lambda_k1, da_lambda_q2, da_lambda_k2, da_subln_w, rel_bias_table, w_branch_hg, w_branch_da, w_out, w_ffn_up, w_ffn_down):
    depth = w_in.shape[0]
    _, _, hg_heads, hg_dk, hg_dv = state_hgrn.shape
    da_heads, da_dv = cache_v.shape[3], cache_v.shape[4]
    da_dh = cache_k.shape[4]
    dims = (hg_heads, hg_dk, hg_dv, da_heads, da_dh, da_dv)
    lb_all = jnp.cumsum(jax.nn.softmax(hg_lb_logits.astype(F32), axis=0), axis=0)

    y_p, y_s = x_prompt, x_sample
    outs = [[] for _ in range(6)]
    for l in range(depth):
        lam_init = 0.8 - 0.6 * math.exp(-0.3 * l)
        lam = (jnp.exp(jnp.sum(da_lambda_q1[l] * da_lambda_k1[l]))
               - jnp.exp(jnp.sum(da_lambda_q2[l] * da_lambda_k2[l])) + lam_init)
        lw = {
            "norm_mix_pre": norm_mix_pre[l][None], "norm_mix_post": norm_mix_post[l][None],
            "norm_ffn_pre": norm_ffn_pre[l][None], "norm_ffn_post": norm_ffn_post[l][None],
            "lb": lb_all[l][None], "hg_norm_w": hg_norm_w[l][None],
            "da_subln_w": da_subln_w[l][None], "tbl": rel_bias_table.astype(F32),
            "lam": lam.reshape(1, 1).astype(F32), "out_scale": 1.0 - lam_init,
        }
        ff = w_ffn_down.shape[1]
        wts = {"w_in": w_in[l], "w_branch_hg": w_branch_hg[l], "w_branch_da": w_branch_da[l],
               "w_out": w_out[l], "w_gate": (w_ffn_up[l], 0), "w_up": (w_ffn_up[l], ff),
               "w_down": w_ffn_down[l]}
        y_s, ks, vs, ss, wts = _layer(y_s, state_hgrn[l], (cache_k[l], cache_v[l], page_table),
                                      lw, wts, dims)
        y_p, kp, vp, sp, _ = _layer(y_p, None, None, lw, wts, dims)
        for acc, val in zip(outs, (kp, vp, sp, ks, vs, ss)):
            acc.append(val)
    return (y_p, y_s) + tuple(jnp.stack(o) for o in outs)
```

```python
import functools
import math

import jax
import jax.numpy as jnp
import numpy as np
from jax import lax
from jax.experimental import pallas as pl
from jax.experimental.pallas import tpu as pltpu

F32 = jnp.float32
BF16 = jnp.bfloat16

RMS_EPS = 1e-6
NEG_INF = -1e30
HG_CHUNK = 64
REL_MAX_DISTANCE = 128
LANES = 128
VMEM_LIMIT = 56 * 1024 * 1024

_NT = (((1,), (1,)), ((), ()))
_TN = (((0,), (0,)), ((), ()))


def _params(*sem):
    return pltpu.CompilerParams(dimension_semantics=sem, vmem_limit_bytes=VMEM_LIMIT)


def _rms(x, w):
    return x * lax.rsqrt(jnp.mean(x * x, axis=-1, keepdims=True) + RMS_EPS) * w


def _sigmoid(x):
    return 1.0 / (1.0 + jnp.exp(-x))


def _pick(n, prefs):
    for p in prefs:
        if n % p == 0:
            return p
    return n


def _bf16_weight(w_ref, copy_ref):
    w = w_ref[...]
    if copy_ref is not None:
        w = w.astype(BF16)
        copy_ref[...] = w
    return w


def _norm_proj_kernel(x_ref, nw_ref, w_ref, o_ref, *rest, jk, jv, v_heads, emit_w):
    h_ref = rest[-1]
    j = pl.program_id(1)

    @pl.when(j == 0)
    def _():
        h_ref[...] = _rms(x_ref[...], nw_ref[...]).astype(BF16)

    w = _bf16_weight(w_ref, rest[-2] if emit_w else None)
    res = jnp.dot(h_ref[...], w, preferred_element_type=F32)
    o_ref[...] = res
    if jk is not None:
        kt_ref, v_ref = rest[:2]
        tm, tn = res.shape

        @pl.when(j == jk)
        def _():
            kt_ref[...] = res.T

        @pl.when(j == jv)
        def _():
            dv = tn // v_heads
            for h in range(v_heads):
                v_ref[pl.ds(h, tm, stride=v_heads), :] = res[:, h * dv:(h + 1) * dv]


def _norm_proj(x, nw, w, kv=None):
    m, d = x.shape
    n = w.shape[1]
    emit_w = w.dtype != BF16
    tm = _pick(m, (1024, 512, 256, 128, 64, 32, 16, 8))
    tn = _pick(n, (1024, 512, 256, 128))
    out_specs = [pl.BlockSpec((tm, tn), lambda i, j: (i, j))]
    out_shape = [jax.ShapeDtypeStruct((m, n), F32)]
    jk = jv = v_heads = None
    if kv is not None:
        kcol, vcol, width, seqlen, v_heads = kv
        tn = width
        assert n % tn == 0 and kcol % tn == 0 and vcol % tn == 0 and seqlen % tm == 0
        jk, jv = kcol // tn, vcol // tn
        nt = seqlen // tm
        out_specs = [pl.BlockSpec((tm, tn), lambda i, j: (i, j)),
                     pl.BlockSpec((tn, tm), lambda i, j: (i // nt, i % nt)),
                     pl.BlockSpec((tm * v_heads, tn // v_heads), lambda i, j: (i, 0))]
        out_shape += [jax.ShapeDtypeStruct((m // seqlen * tn, seqlen), F32),
                      jax.ShapeDtypeStruct((m * v_heads, tn // v_heads), F32)]
    if emit_w:
        out_specs.append(pl.BlockSpec((d, tn), lambda i, j: (0, j)))
        out_shape.append(jax.ShapeDtypeStruct((d, n), BF16))
        assert m == tm, "the bf16 weight copy is written once per column tile"
    return pl.pallas_call(
        functools.partial(_norm_proj_kernel, jk=jk, jv=jv, v_heads=v_heads, emit_w=emit_w),
        grid=(m // tm, n // tn),
        in_specs=[pl.BlockSpec((tm, d), lambda i, j: (i, 0), pipeline_mode=pl.Buffered(1)),
                  pl.BlockSpec((1, d), lambda i, j: (0, 0)),
                  pl.BlockSpec((d, tn), lambda i, j: (0, j))],
        out_specs=out_specs,
        out_shape=out_shape,
        scratch_shapes=[pltpu.VMEM((tm, d), BF16)],
        compiler_params=_params("parallel", "arbitrary"),
        name="norm_proj",
    )(x, nw, w)


def _hgrn_kernel(*refs, heads, dk, dv, chunk, rows, has_s0):
    if has_s0:
        q_ref, f_ref, i_ref, gate_ref, lb_ref, nw_ref, s0_ref, o_ref, sout_ref, st_ref = refs
    else:
        q_ref, f_ref, i_ref, gate_ref, lb_ref, nw_ref, o_ref, sout_ref, st_ref = refs
    t = pl.program_id(1)
    real = q_ref.shape[0]

    @pl.when(t == 0)
    def _():
        for h in range(heads):
            if has_s0:
                st_ref[h] = s0_ref[h].T
            else:
                st_ref[h] = jnp.zeros((dv, dk), F32)

    lb = lb_ref[...]
    f = lb + (1.0 - lb) * _sigmoid(f_ref[...])
    g = jnp.log(f)
    kin = 1.0 - f
    q = q_ref[...]
    v = i_ref[...]
    if rows > real:
        def pad(a):
            return jnp.concatenate([a, jnp.zeros((rows - real, a.shape[1]), F32)], axis=0)
        g, kin, q, v = pad(g), pad(kin), pad(q), pad(v)

    r = lax.broadcasted_iota(jnp.int32, (rows, rows), 0)
    c = lax.broadcasted_iota(jnp.int32, (rows, rows), 1)
    tri = jnp.where((r // chunk == c // chunk) & (c <= r), 1.0, 0.0).astype(BF16)
    g_hi = g.astype(BF16)
    g_r1 = g - g_hi.astype(F32)
    g_mid = g_r1.astype(BF16)
    g_lo = (g_r1 - g_mid.astype(F32)).astype(BF16)
    G = (jnp.dot(tri, g_hi, preferred_element_type=F32)
         + jnp.dot(tri, g_mid, preferred_element_type=F32)
         + jnp.dot(tri, g_lo, preferred_element_type=F32))

    qg = (q * jnp.exp(G)).astype(BF16)
    kg = (kin * jnp.exp(-G)).astype(BF16)
    vb = v.astype(BF16)
    cr = lax.broadcasted_iota(jnp.int32, (chunk, chunk), 0)
    cc = lax.broadcasted_iota(jnp.int32, (chunk, chunk), 1)
    causal = cc <= cr
    nw = nw_ref[...]

    for ci in range(rows // chunk):
        lo = ci * chunk
        Gc = G[lo:lo + chunk]
        Gl = Gc[chunk - 1:chunk]
        kdec = (kin[lo:lo + chunk] * jnp.exp(Gl - Gc)).astype(BF16)
        decay = jnp.exp(Gl)
        n_out = min(chunk, real - lo)
        for h in range(heads):
            sk = slice(h * dk, (h + 1) * dk)
            sv = slice(h * dv, (h + 1) * dv)
            qg_h = qg[lo:lo + chunk, sk]
            v_h = vb[lo:lo + chunk, sv]
            a = lax.dot_general(qg_h, kg[lo:lo + chunk, sk], _NT, preferred_element_type=F32)
            a = jnp.where(causal, a, 0.0).astype(BF16)
            st = st_ref[h]
            o = (jnp.dot(a, v_h, preferred_element_type=F32)
                 + lax.dot_general(qg_h, st.astype(BF16), _NT, preferred_element_type=F32))
            st_ref[h] = st * decay[:, sk] + lax.dot_general(v_h, kdec[:, sk], _TN,
                                                            preferred_element_type=F32)
            if n_out > 0:
                gt = gate_ref[lo:lo + n_out, sv]
                on = _rms(o[:n_out], nw) * (gt * _sigmoid(gt))
                o_ref[lo:lo + n_out, sv] = on.astype(o_ref.dtype)

    @pl.when(t == pl.num_programs(1) - 1)
    def _():
        for h in range(heads):
            sout_ref[h] = st_ref[h].T


def _hgrn(u, col0, lb, nw, s0, batch, seqlen, heads, dk, dv):
    width = heads * dk
    assert dk == dv and col0 % width == 0
    cb = col0 // width
    chunk = min(HG_CHUNK, seqlen)
    if seqlen >= LANES:
        tb = _pick(seqlen, (256, 128))
        rows = tb
        assert tb % chunk == 0
    else:
        tb = seqlen
        rows = LANES
        chunk = LANES
    nt = seqlen // tb
    has_s0 = s0 is not None

    def col(k):
        return pl.BlockSpec((tb, width), lambda b, t: (b * nt + t, cb + k))

    in_specs = [col(0), col(1), col(2), col(3),
                pl.BlockSpec((1, width), lambda b, t: (0, 0)),
                pl.BlockSpec((1, dv), lambda b, t: (0, 0))]
    args = [u, u, u, u, lb, nw]
    if has_s0:
        in_specs.append(pl.BlockSpec((None, heads, dk, dv), lambda b, t: (b, 0, 0, 0)))
        args.append(s0)
    kern = functools.partial(_hgrn_kernel, heads=heads, dk=dk, dv=dv, chunk=chunk, rows=rows,
                             has_s0=has_s0)
    return pl.pallas_call(
        kern,
        grid=(batch, nt),
        in_specs=in_specs,
        out_specs=[pl.BlockSpec((tb, heads * dv), lambda b, t: (b * nt + t, 0)),
                   pl.BlockSpec((None, heads, dk, dv), lambda b, t: (b, 0, 0, 0))],
        out_shape=[jax.ShapeDtypeStruct((batch * seqlen, heads * dv), BF16),
                   jax.ShapeDtypeStruct((batch, heads, dk, dv), F32)],
        scratch_shapes=[pltpu.VMEM((heads, dv, dk), F32)],
        compiler_params=_params("parallel", "arbitrary"),
        name="hgrn",
    )(*args)


def _bucket(dist, n_buckets):
    n = jnp.maximum(dist, 0)
    max_exact = n_buckets // 2
    nf = jnp.maximum(n, 1).astype(F32)
    large = max_exact + (jnp.log(nf / max_exact) / math.log(REL_MAX_DISTANCE / max_exact)
                         * (n_buckets - max_exact)).astype(jnp.int32)
    large = jnp.minimum(large, n_buckets - 1)
    return jnp.where(n < max_exact, n, large)


def _far_start(n_buckets):
    n = np.arange(1, 8 * REL_MAX_DISTANCE, dtype=np.int64)
    max_exact = n_buckets // 2
    nf = n.astype(np.float32)
    large = max_exact + (np.log(nf / np.float32(max_exact)) / np.float32(math.log(REL_MAX_DISTANCE / max_exact))
                         * np.float32(n_buckets - max_exact)).astype(np.int32)
    b = np.where(n < max_exact, n, np.minimum(large, n_buckets - 1))
    below = n[b < n_buckets - 1]
    return int(below.max()) + 2


def _attn_kernel(tbl_ref, lam_ref, nw_ref, q1_ref, q2_ref, k1_ref, k2_ref, v_ref, o_ref,
                 bnear_ref, bdiag_ref, m_ref, l_ref, acc_ref, *,
                 t, dh, dv, heads, n_buckets, out_scale):
    hp = pl.program_id(0)
    b = pl.program_id(1)
    qb = pl.program_id(2)

    @pl.when((b == 0) & (qb == 0))
    def _():
        row = lax.broadcasted_iota(jnp.int32, (t, t), 0)
        col = lax.broadcasted_iota(jnp.int32, (t, t), 1)
        bk_diag = _bucket(row - col, n_buckets)
        bk_near = _bucket(row - col + t, n_buckets)
        for e in range(2):
            for mp in range(2):
                hcol = mp * heads + 2 * hp + e
                far = tbl_ref[n_buckets - 1, hcol]
                bd = jnp.zeros((t, t), F32)
                bn = jnp.zeros((t, t), F32)
                for k in range(n_buckets - 1):
                    val = tbl_ref[k, hcol] - far
                    bd = jnp.where(bk_diag == k, val, bd)
                    bn = jnp.where(bk_near == k, val, bn)
                rs = slice((2 * e + mp) * t, (2 * e + mp + 1) * t)
                bdiag_ref[rs, :] = jnp.where(col <= row, bd, NEG_INF)
                bnear_ref[rs, :] = bn

    lane = lax.broadcasted_iota(jnp.int32, (t, 2 * dh), 1)
    scale = dh ** -0.5
    q_maps = [q1_ref[...] * scale, q2_ref[...] * scale]
    zero = jnp.zeros((t, 2 * dh), F32)
    blocks = []
    for e in range(2):
        in_e = (lane >= e * dh) & (lane < (e + 1) * dh)
        for mp in range(2):
            qe = jnp.where(in_e, q_maps[mp], 0.0)
            blocks.append(jnp.concatenate([qe, zero] if mp == 0 else [zero, qe], axis=1))
    q_all = jnp.concatenate(blocks, axis=0).astype(BF16)

    m_ref[...] = jnp.full(m_ref.shape, NEG_INF, F32)
    l_ref[...] = jnp.zeros(l_ref.shape, F32)
    acc_ref[...] = jnp.zeros(acc_ref.shape, F32)

    def step(j, bias):
        start = pl.multiple_of(j * t, t)
        kb = jnp.concatenate([k1_ref[:, pl.ds(start, t)], k2_ref[:, pl.ds(start, t)]],
                             axis=0).astype(BF16)
        s = jnp.dot(q_all, kb, preferred_element_type=F32)
        if bias is not None:
            s = s + bias
        m_prev = m_ref[...]
        m_new = jnp.maximum(m_prev, jnp.max(s, axis=1, keepdims=True))
        p = jnp.exp(s - jnp.tile(m_new, (1, t // LANES)))
        alpha = jnp.exp(m_prev - m_new)
        l_ref[...] = alpha * l_ref[...] + jnp.sum(p, axis=1, keepdims=True)
        m_ref[...] = m_new
        pb = p.astype(BF16)
        for e in range(2):
            ve = v_ref[pl.ds(start, t), e * dv:(e + 1) * dv].astype(BF16)
            rs = slice(2 * e * t, 2 * (e + 1) * t)
            acc_ref[rs, :] = acc_ref[rs, :] * alpha[rs] + jnp.dot(
                pb[rs], ve, preferred_element_type=F32)

    def far_body(j, carry):
        step(j, None)
        return carry

    lax.fori_loop(0, qb - 1, far_body, 0)

    @pl.when(qb > 0)
    def _():
        step(qb - 1, bnear_ref[...])

    step(qb, bdiag_ref[...])
    o = acc_ref[...] / l_ref[...]
    lam = lam_ref[0, 0]
    nw = nw_ref[...]
    for e in range(2):
        oe = o[2 * e * t:(2 * e + 1) * t] - lam * o[(2 * e + 1) * t:(2 * e + 2) * t]
        o_ref[:, e * dv:(e + 1) * dv] = (_rms(oe, nw) * out_scale).astype(o_ref.dtype)


def _attn_prompt(u, qcol, vcol, kt, tbl, lam, nw, batch, seqlen, heads, dh, dv, out_scale):
    assert dv == LANES and 2 * dh == LANES and heads % 2 == 0
    n_buckets = tbl.shape[0]
    t = _pick(seqlen, (256, 128))
    assert seqlen % t == 0 and t % LANES == 0 and t + 1 >= _far_start(n_buckets)
    nq = seqlen // t
    nhp = heads // 2
    qc, vc = qcol // LANES, vcol // (2 * dv)
    smem = pl.BlockSpec(memory_space=pltpu.SMEM)
    kern = functools.partial(_attn_kernel, t=t, dh=dh, dv=dv, heads=heads, n_buckets=n_buckets,
                             out_scale=out_scale)
    return pl.pallas_call(
        kern,
        grid=(nhp, batch, nq),
        in_specs=[smem, smem,
                  pl.BlockSpec((1, dv), lambda h, b, i: (0, 0)),
                  pl.BlockSpec((t, LANES), lambda h, b, i: (b * nq + i, qc + h)),
                  pl.BlockSpec((t, LANES), lambda h, b, i: (b * nq + i, qc + nhp + h)),
                  pl.BlockSpec((2 * dh, seqlen), lambda h, b, i: (b * 2 * nhp + h, 0)),
                  pl.BlockSpec((2 * dh, seqlen), lambda h, b, i: (b * 2 * nhp + nhp + h, 0)),
                  pl.BlockSpec((seqlen, 2 * dv), lambda h, b, i: (b, vc + h))],
        out_specs=pl.BlockSpec((t, 2 * dv), lambda h, b, i: (b * nq + i, h)),
        out_shape=jax.ShapeDtypeStruct((batch * seqlen, heads * dv), BF16),
        scratch_shapes=[pltpu.VMEM((4 * t, t), F32), pltpu.VMEM((4 * t, t), F32),
                        pltpu.VMEM((4 * t, LANES), F32), pltpu.VMEM((4 * t, LANES), F32),
                        pltpu.VMEM((4 * t, dv), F32)],
        compiler_params=_params("arbitrary", "arbitrary", "arbitrary"),
        name="attn_prompt",
    )(tbl, lam, nw, u, u, kt, kt, u)


def _decode_kernel(pt_ref, lam_ref, trow_ref, nw_ref, q_ref, kn_ref, vn_ref, *rest,
                   pages, page, lq, heads, dh, dv, n_buckets, past_len, near_slots, out_scale):
    k_refs = rest[:pages]
    v_refs = rest[pages:2 * pages]
    o_ref, m_ref, l_ref, acc_ref, bias_ref = rest[2 * pages:]
    b = pl.program_id(0)
    g = pl.program_id(1)
    ng = pl.num_programs(1)
    mh = 2 * heads
    rows = mh * lq
    n_near = len(near_slots)
    order = [m * heads + h for h in range(heads) for m in range(2)]

    def near_bias(key0, n_valid):
        r = lax.broadcasted_iota(jnp.int32, (rows, page), 0)
        col = lax.broadcasted_iota(jnp.int32, (rows, page), 1)
        dist = past_len + r % lq - (key0 + col)
        bk = _bucket(dist, n_buckets)
        far = trow_ref[:, n_buckets - 1:n_buckets]
        bias = jnp.zeros((rows, page), F32)
        for k in range(n_buckets - 1):
            bias = jnp.where(bk == k, trow_ref[:, k:k + 1] - far, bias)
        return jnp.where((dist >= 0) & (col < n_valid), bias, NEG_INF)

    @pl.when((b == 0) & (g == 0))
    def _():
        for i, slot in enumerate(near_slots):
            bias_ref[i] = near_bias(past_len - (pages - slot) * page, page)
        bias_ref[n_near] = near_bias(past_len, lq)

    @pl.when(g == 0)
    def _():
        m_ref[...] = jnp.full(m_ref.shape, NEG_INF, F32)
        l_ref[...] = jnp.zeros(l_ref.shape, F32)
        acc_ref[...] = jnp.zeros(acc_ref.shape, F32)

    q = q_ref[...] * (dh ** -0.5)
    q_parts = [q[:, j * dh:(j + 1) * dh].astype(BF16) for j in order]

    def attend(score, value_rows, bias):
        s = jnp.concatenate([score(qj, j) for qj, j in zip(q_parts, order)], axis=0)
        if bias is not None:
            s = s + bias
        m_prev = m_ref[...]
        m_new = jnp.maximum(m_prev, jnp.max(s, axis=1, keepdims=True))
        p = jnp.exp(s - jnp.tile(m_new, (1, s.shape[1] // LANES)))
        alpha = jnp.exp(m_prev - m_new)
        l_ref[...] = alpha * l_ref[...] + jnp.sum(p, axis=1, keepdims=True)
        m_ref[...] = m_new
        pv = jnp.concatenate(
            [jnp.dot(p[2 * lq * h:2 * lq * (h + 1)].astype(BF16), value_rows(h).astype(BF16),
                     preferred_element_type=F32) for h in range(heads)], axis=0)
        acc_ref[...] = acc_ref[...] * alpha + pv

    is_last = g == ng - 1
    bias = None
    if near_slots:
        zero = jnp.zeros((rows, page), F32)
        bias = jnp.concatenate(
            [jnp.where(is_last, bias_ref[near_slots.index(s)], 0.0) if s in near_slots else zero
             for s in range(pages)], axis=1)
    def page_score(qj, j):
        kt = jnp.concatenate([r[j * dh:(j + 1) * dh, :] for r in k_refs], axis=1)
        return jnp.dot(qj, kt.astype(BF16), preferred_element_type=F32)

    attend(page_score,
           lambda h: jnp.concatenate([r[pl.ds(h, page, stride=heads), :] for r in v_refs], axis=0),
           bias)

    @pl.when(is_last)
    def _():
        kn = kn_ref[...]
        vn = vn_ref[...]
        zk = jnp.zeros((page - lq, dh), F32)
        zv = jnp.zeros((page - lq, dv), F32)

        def new_score(qj, j):
            kj = jnp.concatenate([kn[:, j * dh:(j + 1) * dh], zk], axis=0).astype(BF16)
            return lax.dot_general(qj, kj, _NT, preferred_element_type=F32)

        attend(new_score,
               lambda h: jnp.concatenate([vn[:, h * dv:(h + 1) * dv], zv], axis=0),
               bias_ref[n_near])
        full = acc_ref[...] / l_ref[...]
        lam = lam_ref[0, 0]
        nw = nw_ref[...]
        for h in range(heads):
            o0 = full[2 * lq * h:2 * lq * h + lq]
            o1 = full[2 * lq * h + lq:2 * lq * (h + 1)]
            o_ref[:, h * dv:(h + 1) * dv] = (_rms(o0 - lam * o1, nw) * out_scale).astype(o_ref.dtype)


def _attn_decode(u, qcol, kcol, vcol, cache_k, cache_v, page_table, tbl, lam, nw,
                 batch, lq, heads, dh, dv, out_scale):
    n_pool, page = cache_k.shape[0], cache_k.shape[1]
    n_pages = page_table.shape[1]
    past_len = n_pages * page
    mh = 2 * heads
    kw, vw = mh * dh, heads * dv
    assert page % LANES == 0 and kw % LANES == 0 and vw % LANES == 0 and lq % 8 == 0
    n_buckets = tbl.shape[0]
    far = _far_start(n_buckets)
    first_near = max(0, -(-(past_len - page + 1 - far + 1) // page))
    pages = _pick(n_pages, (16, 8, 4, 2, 1))
    assert n_pages - pages <= first_near, "near pages must fall in the last grid step"
    near_slots = tuple(s for s in range(pages) if n_pages - pages + s >= first_near)
    ck = jnp.transpose(cache_k, (0, 2, 3, 1)).reshape(n_pool * mh * dh, page)
    cv = cache_v.reshape(n_pool * page * heads, dv)
    trow = jnp.repeat(tbl.T.reshape(2, heads, -1).transpose(1, 0, 2).reshape(mh, -1), lq, axis=0)
    rows = mh * lq

    def page_spec(n_rows, width, slot):
        return pl.BlockSpec((n_rows, width),
                            lambda b, g, pt: (pt[b * n_pages + g * pages + slot], 0))

    smem = pl.BlockSpec(memory_space=pltpu.SMEM)
    in_specs = ([smem,
                 pl.BlockSpec((rows, n_buckets), lambda b, g, pt: (0, 0)),
                 pl.BlockSpec((1, dv), lambda b, g, pt: (0, 0)),
                 pl.BlockSpec((lq, kw), lambda b, g, pt: (b, qcol // kw)),
                 pl.BlockSpec((lq, kw), lambda b, g, pt: (b, kcol // kw)),
                 pl.BlockSpec((lq, vw), lambda b, g, pt: (b, vcol // vw))]
                + [page_spec(mh * dh, page, s) for s in range(pages)]
                + [page_spec(page * heads, dv, s) for s in range(pages)])
    assert qcol % kw == 0 and kcol % kw == 0 and vcol % vw == 0
    kern = functools.partial(_decode_kernel, pages=pages, page=page, lq=lq, heads=heads, dh=dh,
                             dv=dv, n_buckets=n_buckets, past_len=past_len,
                             near_slots=near_slots, out_scale=out_scale)
    grid_spec = pltpu.PrefetchScalarGridSpec(
        num_scalar_prefetch=1,
        grid=(batch, n_pages // pages),
        in_specs=in_specs,
        out_specs=pl.BlockSpec((lq, vw), lambda b, g, pt: (b, 0)),
        scratch_shapes=[pltpu.VMEM((rows, LANES), F32), pltpu.VMEM((rows, LANES), F32),
                        pltpu.VMEM((rows, dv), F32),
                        pltpu.VMEM((len(near_slots) + 1, rows, page), F32)])
    return pl.pallas_call(
        kern,
        grid_spec=grid_spec,
        out_shape=jax.ShapeDtypeStruct((batch * lq, vw), BF16),
        compiler_params=_params("arbitrary", "arbitrary"),
        name="attn_decode",
    )(page_table.reshape(-1), lam, trow, nw, u, u, u, *([ck] * pages), *([cv] * pages))


def _merge_resident_kernel(*refs, n_g):
    ohg_ref, oda_ref = refs[:2]
    g_refs = refs[2:2 + 2 * n_g]
    x_ref, wbh_ref, wbd_ref, wo_ref, nw_ref, o_ref = refs[2 + 2 * n_g:]
    g_hg = jnp.concatenate([r[...] for r in g_refs[:n_g]], axis=1)
    g_da = jnp.concatenate([r[...] for r in g_refs[n_g:]], axis=1)
    y_hg = jnp.dot(ohg_ref[...], wbh_ref[...], preferred_element_type=F32)
    y_da = jnp.dot(oda_ref[...], wbd_ref[...], preferred_element_type=F32)
    mixed = (_sigmoid(g_hg) * y_hg + _sigmoid(g_da) * y_da).astype(BF16)
    z = jnp.dot(mixed, wo_ref[...], preferred_element_type=F32)
    o_ref[...] = x_ref[...] + _rms(z, nw_ref[...])


def _merge_resident(o_hg, o_da, u, gcol, x, wbh, wbd, wo, nw):
    m, d = x.shape
    w = o_hg.shape[1]
    gw = math.gcd(gcol, d)
    n_g = d // gw
    assert gw % LANES == 0
    tm = _pick(m, (256, 128, 64, 8))
    const = lambda i: (0, 0)
    g_specs = [pl.BlockSpec((tm, gw), functools.partial(lambda i, c: (i, c), c=gcol // gw + k))
               for k in range(2 * n_g)]
    return pl.pallas_call(
        functools.partial(_merge_resident_kernel, n_g=n_g),
        grid=(m // tm,),
        in_specs=[pl.BlockSpec((tm, w), lambda i: (i, 0)),
                  pl.BlockSpec((tm, w), lambda i: (i, 0))]
                 + g_specs
                 + [pl.BlockSpec((tm, d), lambda i: (i, 0)),
                    pl.BlockSpec((w, d), const, pipeline_mode=pl.Buffered(1)),
                    pl.BlockSpec((w, d), const, pipeline_mode=pl.Buffered(1)),
                    pl.BlockSpec((d, d), const, pipeline_mode=pl.Buffered(1)),
                    pl.BlockSpec((1, d), const)],
        out_specs=pl.BlockSpec((tm, d), lambda i: (i, 0)),
        out_shape=jax.ShapeDtypeStruct((m, d), F32),
        compiler_params=_params("parallel"),
        name="merge",
    )(o_hg, o_da, *([u] * (2 * n_g)), x, wbh, wbd, wo, nw)


def _merge_kernel(ohg_ref, oda_ref, ghg_ref, gda_ref, x_ref, wbh_ref, wbd_ref, wo_ref, nw_ref,
                  o_ref, *copies, emit_w):
    copies = copies if emit_w else (None, None, None)
    n = pl.program_id(1)

    @pl.when(n == 0)
    def _():
        o_ref[...] = jnp.zeros(o_ref.shape, F32)

    y_hg = jnp.dot(ohg_ref[...], _bf16_weight(wbh_ref, copies[0]), preferred_element_type=F32)
    y_da = jnp.dot(oda_ref[...], _bf16_weight(wbd_ref, copies[1]), preferred_element_type=F32)
    mixed = (_sigmoid(ghg_ref[...]) * y_hg + _sigmoid(gda_ref[...]) * y_da).astype(BF16)
    o_ref[...] += jnp.dot(mixed, _bf16_weight(wo_ref, copies[2]), preferred_element_type=F32)

    @pl.when(n == pl.num_programs(1) - 1)
    def _():
        o_ref[...] = x_ref[...] + _rms(o_ref[...], nw_ref[...])


def _merge(o_hg, o_da, u, gcol, x, wbh, wbd, wo, nw):
    m, d = x.shape
    w = o_hg.shape[1]
    emit_w = wo.dtype != BF16
    tm = _pick(m, (1024, 512, 256, 128, 64, 32, 16, 8))
    tn = _pick(math.gcd(gcol, d), (512, 256, 128))
    nn = d // tn
    gc = gcol // tn
    once = dict(pipeline_mode=pl.Buffered(1))
    out_specs = [pl.BlockSpec((tm, d), lambda i, n: (i, 0))]
    out_shape = [jax.ShapeDtypeStruct((m, d), F32)]
    if emit_w:
        assert m == tm, "the bf16 weight copies are written once per column tile"
        out_specs += [pl.BlockSpec((w, tn), lambda i, n: (0, n)),
                      pl.BlockSpec((w, tn), lambda i, n: (0, n)),
                      pl.BlockSpec((tn, d), lambda i, n: (n, 0))]
        out_shape += [jax.ShapeDtypeStruct((w, d), BF16), jax.ShapeDtypeStruct((w, d), BF16),
                      jax.ShapeDtypeStruct((d, d), BF16)]
    return pl.pallas_call(
        functools.partial(_merge_kernel, emit_w=emit_w),
        grid=(m // tm, nn),
        in_specs=[pl.BlockSpec((tm, w), lambda i, n: (i, 0), **once),
                  pl.BlockSpec((tm, w), lambda i, n: (i, 0), **once),
                  pl.BlockSpec((tm, tn), lambda i, n: (i, gc + n)),
                  pl.BlockSpec((tm, tn), lambda i, n: (i, gc + nn + n)),
                  pl.BlockSpec((tm, d), lambda i, n: (i, 0), **once),
                  pl.BlockSpec((w, tn), lambda i, n: (0, n)),
                  pl.BlockSpec((w, tn), lambda i, n: (0, n)),
                  pl.BlockSpec((tn, d), lambda i, n: (n, 0)),
                  pl.BlockSpec((1, d), lambda i, n: (0, 0))],
        out_specs=out_specs,
        out_shape=out_shape,
        compiler_params=_params("parallel", "arbitrary"),
        name="merge",
    )(o_hg, o_da, u, u, x, wbh, wbd, wo, nw)


def _ffn_kernel(x_ref, npre_ref, wg_ref, wu_ref, wd_ref, npost_ref, o_ref, *rest, emit_w):
    h_ref = rest[-1]
    copies = rest[:3] if emit_w else (None, None, None)
    j = pl.program_id(1)

    @pl.when(j == 0)
    def _():
        h_ref[...] = _rms(x_ref[...], npre_ref[...]).astype(BF16)
        o_ref[...] = jnp.zeros(o_ref.shape, F32)

    h = h_ref[...]
    gate = jnp.dot(h, _bf16_weight(wg_ref, copies[0]), preferred_element_type=F32)
    up = jnp.dot(h, _bf16_weight(wu_ref, copies[1]), preferred_element_type=F32)
    act = (gate * _sigmoid(gate) * up).astype(BF16)
    o_ref[...] += jnp.dot(act, _bf16_weight(wd_ref, copies[2]), preferred_element_type=F32)

    @pl.when(j == pl.num_programs(1) - 1)
    def _():
        o_ref[...] = x_ref[...] + _rms(o_ref[...], npost_ref[...])


def _ffn(x, npre, w_gate, w_up, w_down, npost):
    m, d = x.shape
    ff = w_down.shape[0]
    emit_w = w_down.dtype != BF16
    tm = _pick(m, (1024, 512, 256, 128, 64, 32, 16, 8))
    tf = _pick(ff, (512, 256, 128))
    nf = ff // tf
    (wg, gcol), (wu, ucol) = w_gate, w_up
    assert gcol % tf == 0 and ucol % tf == 0
    gc, uc = gcol // tf, ucol // tf
    out_specs = [pl.BlockSpec((tm, d), lambda i, j: (i, 0))]
    out_shape = [jax.ShapeDtypeStruct((m, d), F32)]
    if emit_w:
        assert m == tm, "the bf16 weight copies are written once per hidden-dim tile"
        out_specs += [pl.BlockSpec((d, tf), lambda i, j: (0, j)),
                      pl.BlockSpec((d, tf), lambda i, j: (0, j)),
                      pl.BlockSpec((tf, d), lambda i, j: (j, 0))]
        out_shape += [jax.ShapeDtypeStruct((d, ff), BF16), jax.ShapeDtypeStruct((d, ff), BF16),
                      jax.ShapeDtypeStruct((ff, d), BF16)]
    return pl.pallas_call(
        functools.partial(_ffn_kernel, emit_w=emit_w),
        grid=(m // tm, nf),
        in_specs=[pl.BlockSpec((tm, d), lambda i, j: (i, 0), pipeline_mode=pl.Buffered(1)),
                  pl.BlockSpec((1, d), lambda i, j: (0, 0)),
                  pl.BlockSpec((d, tf), lambda i, j: (0, gc + j)),
                  pl.BlockSpec((d, tf), lambda i, j: (0, uc + j)),
                  pl.BlockSpec((tf, d), lambda i, j: (j, 0)),
                  pl.BlockSpec((1, d), lambda i, j: (0, 0))],
        out_specs=out_specs,
        out_shape=out_shape,
        scratch_shapes=[pltpu.VMEM((tm, d), BF16)],
        compiler_params=_params("parallel", "arbitrary"),
        name="ffn",
    )(x, npre, wg, wu, w_down, npost)


def _layer(x, s0, paged, lw, wts, dims):
    batch, seqlen, d = x.shape
    wts = dict(wts)
    hg_heads, hg_dk, hg_dv, da_heads, da_dh, da_dv = dims
    hg_w, da_qk, da_w = hg_heads * hg_dk, 2 * da_heads * da_dh, da_heads * da_dv
    cols = np.cumsum([0, hg_w, hg_w, hg_heads * hg_dv, hg_heads * hg_dv, da_qk, da_qk, da_w, d, d])
    qcol, kcol, vcol, gcol = int(cols[4]), int(cols[5]), int(cols[6]), int(cols[7])
    x2 = x.reshape(batch * seqlen, d)

    if paged is None:
        assert da_qk == da_w
        u, kt, v2 = _norm_proj(x2, lw["norm_mix_pre"], wts["w_in"],
                               kv=(kcol, vcol, da_qk, seqlen, da_heads))
        o_da = _attn_prompt(u, qcol, vcol, kt, lw["tbl"], lw["lam"], lw["da_subln_w"],
                            batch, seqlen, da_heads, da_dh, da_dv, lw["out_scale"])
        k = kt.reshape(batch, 2 * da_heads, da_dh, seqlen).transpose(0, 3, 1, 2)
    else:
        u, *copy = _norm_proj(x2, lw["norm_mix_pre"], wts["w_in"])
        if copy:
            wts["w_in"], = copy
        o_da = _attn_decode(u, qcol, kcol, vcol, paged[0], paged[1], paged[2], lw["tbl"],
                            lw["lam"], lw["da_subln_w"], batch, seqlen, da_heads, da_dh, da_dv,
                            lw["out_scale"])
        k = u[:, kcol:kcol + da_qk].reshape(batch, seqlen, 2 * da_heads, da_dh)
        v2 = u[:, vcol:vcol + da_w]
    o_hg, s_new = _hgrn(u, 0, lw["lb"], lw["hg_norm_w"], s0, batch, seqlen, hg_heads, hg_dk, hg_dv)
    merge_args = (o_hg, o_da, u, gcol, x2, wts["w_branch_hg"], wts["w_branch_da"], wts["w_out"],
                  lw["norm_mix_post"])
    if wts["w_out"].dtype == BF16:
        x1 = _merge_resident(*merge_args)
    else:
        x1, wts["w_branch_hg"], wts["w_branch_da"], wts["w_out"] = _merge(*merge_args)
    y, *copy = _ffn(x1, lw["norm_ffn_pre"], wts["w_gate"], wts["w_up"], wts["w_down"],
                    lw["norm_ffn_post"])
    if copy:
        wts["w_gate"], wts["w_up"], wts["w_down"] = (copy[0], 0), (copy[1], 0), copy[2]
    v = v2.reshape(batch, seqlen, da_heads, da_dv)
    return y.reshape(batch, seqlen, d), k, v, s_new, wts


def kernel(x_prompt, x_sample, cache_k, cache_v, state_hgrn, page_table, norm_mix_pre, norm_mix_post, norm_ffn_pre, norm_ffn_post, w_in, hg_lb_logits, hg_norm_w, da_lambda_q1, da_lambda_k1, da_lambda_q2, da_lambda_k2, da_subln_w, rel_bias_table, w_branch_hg, w_branch_da, w_out, w_ffn_up, w_ffn_down):
    depth = w_in.shape[0]
    _, _, hg_heads, hg_dk, hg_dv = state_hgrn.shape
    da_heads, da_dv = cache_v.shape[3], cache_v.shape[4]
    da_dh = cache_k.shape[4]
    dims = (hg_heads, hg_dk, hg_dv, da_heads, da_dh, da_dv)
    lb_all = jnp.cumsum(jax.nn.softmax(hg_lb_logits.astype(F32), axis=0), axis=0)

    y_p, y_s = x_prompt, x_sample
    outs = [[] for _ in range(6)]
    for l in range(depth):
        lam_init = 0.8 - 0.6 * math.exp(-0.3 * l)
        lam = (jnp.exp(jnp.sum(da_lambda_q1[l] * da_lambda_k1[l]))
               - jnp.exp(jnp.sum(da_lambda_q2[l] * da_lambda_k2[l])) + lam_init)
        lw = {
            "norm_mix_pre": norm_mix_pre[l][None], "norm_mix_post": norm_mix_post[l][None],
            "norm_ffn_pre": norm_ffn_pre[l][None], "norm_ffn_post": norm_ffn_post[l][None],
            "lb": lb_all[l][None], "hg_norm_w": hg_norm_w[l][None],
            "da_subln_w": da_subln_w[l][None], "tbl": rel_bias_table.astype(F32),
            "lam": lam.reshape(1, 1).astype(F32), "out_scale": 1.0 - lam_init,
        }
        ff = w_ffn_down.shape[1]
        wts = {"w_in": w_in[l], "w_branch_hg": w_branch_hg[l], "w_branch_da": w_branch_da[l],
               "w_out": w_out[l], "w_gate": (w_ffn_up[l], 0), "w_up": (w_ffn_up[l], ff),
               "w_down": w_ffn_down[l]}
        y_s, ks, vs, ss, wts = _layer(y_s, state_hgrn[l], (cache_k[l], cache_v[l], page_table),
                                      lw, wts, dims)
        y_p, kp, vp, sp, _ = _layer(y_p, None, None, lw, wts, dims)
        for acc, val in zip(outs, (kp, vp, sp, ks, vs, ss)):
            acc.append(val)
    return (y_p, y_s) + tuple(jnp.stack(o) for o in outs)
```

```python
import functools
import math

import jax
import jax.numpy as jnp
import numpy as np
from jax import lax
from jax.experimental import pallas as pl
from jax.experimental.pallas import tpu as pltpu

F32 = jnp.float32
BF16 = jnp.bfloat16

RMS_EPS = 1e-6
NEG_INF = -1e30
HG_CHUNK = 64
REL_MAX_DISTANCE = 128
LANES = 128
VMEM_LIMIT = 62 * 1024 * 1024

_NT = (((1,), (1,)), ((), ()))
_TN = (((0,), (0,)), ((), ()))


def _params(*sem):
    return pltpu.CompilerParams(dimension_semantics=sem, vmem_limit_bytes=VMEM_LIMIT)


def _rms(x, w):
    return x * lax.rsqrt(jnp.mean(x * x, axis=-1, keepdims=True) + RMS_EPS) * w


def _sigmoid(x):
    return 1.0 / (1.0 + jnp.exp(-x))


def _pick(n, prefs):
    for p in prefs:
        if n % p == 0:
            return p
    return n


def _bf16_weight(w_ref, copy_ref):
    w = w_ref[...]
    if copy_ref is not None:
        w = w.astype(BF16)
        copy_ref[...] = w
    return w


def _norm_proj_kernel(x_ref, nw_ref, w_ref, o_ref, *rest, jk, jv, v_heads, emit_w):
    h_ref = rest[-1]
    j = pl.program_id(1)

    @pl.when(j == 0)
    def _():
        h_ref[...] = _rms(x_ref[...], nw_ref[...]).astype(BF16)

    w = _bf16_weight(w_ref, rest[-2] if emit_w else None)
    res = jnp.dot(h_ref[...], w, preferred_element_type=F32)
    o_ref[...] = res
    if jk is not None:
        kt_ref, v_ref = rest[:2]
        tm, tn = res.shape

        @pl.when(j == jk)
        def _():
            kt_ref[...] = res.T

        @pl.when(j == jv)
        def _():
            dv = tn // v_heads
            for h in range(v_heads):
                v_ref[pl.ds(h, tm, stride=v_heads), :] = res[:, h * dv:(h + 1) * dv]


def _norm_proj(x, nw, w, kv=None):
    m, d = x.shape
    n = w.shape[1]
    emit_w = w.dtype != BF16
    tm = _pick(m, (1024, 512, 256, 128, 64, 32, 16, 8))
    tn = _pick(n, (1024, 512, 256, 128))
    out_specs = [pl.BlockSpec((tm, tn), lambda i, j: (i, j))]
    out_shape = [jax.ShapeDtypeStruct((m, n), F32)]
    jk = jv = v_heads = None
    if kv is not None:
        kcol, vcol, width, seqlen, v_heads = kv
        tn = width
        assert n % tn == 0 and kcol % tn == 0 and vcol % tn == 0 and seqlen % tm == 0
        jk, jv = kcol // tn, vcol // tn
        nt = seqlen // tm
        out_specs = [pl.BlockSpec((tm, tn), lambda i, j: (i, j)),
                     pl.BlockSpec((tn, tm), lambda i, j: (i // nt, i % nt)),
                     pl.BlockSpec((tm * v_heads, tn // v_heads), lambda i, j: (i, 0))]
        out_shape += [jax.ShapeDtypeStruct((m // seqlen * tn, seqlen), F32),
                      jax.ShapeDtypeStruct((m * v_heads, tn // v_heads), F32)]
    if emit_w:
        out_specs.append(pl.BlockSpec((d, tn), lambda i, j: (0, j)))
        out_shape.append(jax.ShapeDtypeStruct((d, n), BF16))
        assert m == tm, "the bf16 weight copy is written once per column tile"
    return pl.pallas_call(
        functools.partial(_norm_proj_kernel, jk=jk, jv=jv, v_heads=v_heads, emit_w=emit_w),
        grid=(m // tm, n // tn),
        in_specs=[pl.BlockSpec((tm, d), lambda i, j: (i, 0)),
                  pl.BlockSpec((1, d), lambda i, j: (0, 0)),
                  pl.BlockSpec((d, tn), lambda i, j: (0, j))],
        out_specs=out_specs,
        out_shape=out_shape,
        scratch_shapes=[pltpu.VMEM((tm, d), BF16)],
        compiler_params=_params("parallel", "arbitrary"),
        name="norm_proj",
    )(x, nw, w)


def _hgrn_kernel(*refs, heads, dk, dv, chunk, rows, has_s0):
    if has_s0:
        q_ref, f_ref, i_ref, gate_ref, lb_ref, nw_ref, s0_ref, o_ref, sout_ref, st_ref = refs
    else:
        q_ref, f_ref, i_ref, gate_ref, lb_ref, nw_ref, o_ref, sout_ref, st_ref = refs
    t = pl.program_id(1)
    real = q_ref.shape[0]

    @pl.when(t == 0)
    def _():
        for h in range(heads):
            if has_s0:
                st_ref[h] = s0_ref[h].T
            else:
                st_ref[h] = jnp.zeros((dv, dk), F32)

    lb = lb_ref[...]
    f = lb + (1.0 - lb) * _sigmoid(f_ref[...])
    g = jnp.log(f)
    kin = 1.0 - f
    q = q_ref[...]
    v = i_ref[...]
    if rows > real:
        def pad(a):
            return jnp.concatenate([a, jnp.zeros((rows - real, a.shape[1]), F32)], axis=0)
        g, kin, q, v = pad(g), pad(kin), pad(q), pad(v)

    r = lax.broadcasted_iota(jnp.int32, (rows, rows), 0)
    c = lax.broadcasted_iota(jnp.int32, (rows, rows), 1)
    tri = jnp.where((r // chunk == c // chunk) & (c <= r), 1.0, 0.0).astype(BF16)
    g_hi = g.astype(BF16)
    g_r1 = g - g_hi.astype(F32)
    g_mid = g_r1.astype(BF16)
    g_lo = (g_r1 - g_mid.astype(F32)).astype(BF16)
    G = (jnp.dot(tri, g_hi, preferred_element_type=F32)
         + jnp.dot(tri, g_mid, preferred_element_type=F32)
         + jnp.dot(tri, g_lo, preferred_element_type=F32))

    qg = (q * jnp.exp(G)).astype(BF16)
    kg = (kin * jnp.exp(-G)).astype(BF16)
    vb = v.astype(BF16)
    cr = lax.broadcasted_iota(jnp.int32, (chunk, chunk), 0)
    cc = lax.broadcasted_iota(jnp.int32, (chunk, chunk), 1)
    causal = cc <= cr
    nw = nw_ref[...]

    states = [st_ref[h] for h in range(heads)]
    for ci in range(rows // chunk):
        lo = ci * chunk
        Gc = G[lo:lo + chunk]
        Gl = Gc[chunk - 1:chunk]
        kdec = (kin[lo:lo + chunk] * jnp.exp(Gl - Gc)).astype(BF16)
        decay = jnp.exp(Gl)
        n_out = min(chunk, real - lo)
        for h in range(heads):
            sk = slice(h * dk, (h + 1) * dk)
            sv = slice(h * dv, (h + 1) * dv)
            qg_h = qg[lo:lo + chunk, sk]
            v_h = vb[lo:lo + chunk, sv]
            a = lax.dot_general(qg_h, kg[lo:lo + chunk, sk], _NT, preferred_element_type=F32)
            a = jnp.where(causal, a, 0.0).astype(BF16)
            st = states[h]
            o = (jnp.dot(a, v_h, preferred_element_type=F32)
                 + lax.dot_general(qg_h, st.astype(BF16), _NT, preferred_element_type=F32))
            states[h] = st * decay[:, sk] + lax.dot_general(v_h, kdec[:, sk], _TN,
                                                            preferred_element_type=F32)
            if n_out > 0:
                gt = gate_ref[lo:lo + n_out, sv]
                on = _rms(o[:n_out], nw) * (gt * _sigmoid(gt))
                o_ref[lo:lo + n_out, sv] = on.astype(o_ref.dtype)
    for h in range(heads):
        st_ref[h] = states[h]

    @pl.when(t == pl.num_programs(1) - 1)
    def _():
        for h in range(heads):
            sout_ref[h] = st_ref[h].T


def _hgrn(u, col0, lb, nw, s0, batch, seqlen, heads, dk, dv):
    width = heads * dk
    assert dk == dv and col0 % width == 0
    cb = col0 // width
    chunk = min(HG_CHUNK, seqlen)
    if seqlen >= LANES:
        tb = _pick(seqlen, (256, 128))
        rows = tb
        assert tb % chunk == 0
    else:
        tb = seqlen
        rows = LANES
        chunk = LANES
    nt = seqlen // tb
    has_s0 = s0 is not None

    def col(k):
        return pl.BlockSpec((tb, width), lambda b, t: (b * nt + t, cb + k))

    in_specs = [col(0), col(1), col(2), col(3),
                pl.BlockSpec((1, width), lambda b, t: (0, 0)),
                pl.BlockSpec((1, dv), lambda b, t: (0, 0))]
    args = [u, u, u, u, lb, nw]
    if has_s0:
        in_specs.append(pl.BlockSpec((None, heads, dk, dv), lambda b, t: (b, 0, 0, 0)))
        args.append(s0)
    kern = functools.partial(_hgrn_kernel, heads=heads, dk=dk, dv=dv, chunk=chunk, rows=rows,
                             has_s0=has_s0)
    return pl.pallas_call(
        kern,
        grid=(batch, nt),
        in_specs=in_specs,
        out_specs=[pl.BlockSpec((tb, heads * dv), lambda b, t: (b * nt + t, 0)),
                   pl.BlockSpec((None, heads, dk, dv), lambda b, t: (b, 0, 0, 0))],
        out_shape=[jax.ShapeDtypeStruct((batch * seqlen, heads * dv), BF16),
                   jax.ShapeDtypeStruct((batch, heads, dk, dv), F32)],
        scratch_shapes=[pltpu.VMEM((heads, dv, dk), F32)],
        compiler_params=_params("parallel", "arbitrary"),
        name="hgrn",
    )(*args)


def _bucket(dist, n_buckets):
    n = jnp.maximum(dist, 0)
    max_exact = n_buckets // 2
    nf = jnp.maximum(n, 1).astype(F32)
    large = max_exact + (jnp.log(nf / max_exact) / math.log(REL_MAX_DISTANCE / max_exact)
                         * (n_buckets - max_exact)).astype(jnp.int32)
    large = jnp.minimum(large, n_buckets - 1)
    return jnp.where(n < max_exact, n, large)


def _far_start(n_buckets):
    n = np.arange(1, 8 * REL_MAX_DISTANCE, dtype=np.int64)
    max_exact = n_buckets // 2
    nf = n.astype(np.float32)
    large = max_exact + (np.log(nf / np.float32(max_exact)) / np.float32(math.log(REL_MAX_DISTANCE / max_exact))
                         * np.float32(n_buckets - max_exact)).astype(np.int32)
    b = np.where(n < max_exact, n, np.minimum(large, n_buckets - 1))
    below = n[b < n_buckets - 1]
    return int(below.max()) + 2


def _attn_kernel(tbl_ref, lam_ref, nw_ref, q1_ref, q2_ref, k1_ref, k2_ref, v_ref, o_ref,
                 bnear_ref, bdiag_ref, m_ref, l_ref, acc_ref, *,
                 t, dh, dv, heads, n_buckets, out_scale):
    hp = pl.program_id(0)
    b = pl.program_id(1)
    qb = pl.program_id(2)

    @pl.when((b == 0) & (qb == 0))
    def _():
        row = lax.broadcasted_iota(jnp.int32, (t, t), 0)
        col = lax.broadcasted_iota(jnp.int32, (t, t), 1)
        bk_diag = _bucket(row - col, n_buckets)
        bk_near = _bucket(row - col + t, n_buckets)
        for e in range(2):
            for mp in range(2):
                hcol = mp * heads + 2 * hp + e
                far = tbl_ref[n_buckets - 1, hcol]
                bd = jnp.zeros((t, t), F32)
                bn = jnp.zeros((t, t), F32)
                for k in range(n_buckets - 1):
                    val = tbl_ref[k, hcol] - far
                    bd = jnp.where(bk_diag == k, val, bd)
                    bn = jnp.where(bk_near == k, val, bn)
                rs = slice((2 * e + mp) * t, (2 * e + mp + 1) * t)
                bdiag_ref[rs, :] = jnp.where(col <= row, bd, NEG_INF)
                bnear_ref[rs, :] = bn

    lane = lax.broadcasted_iota(jnp.int32, (t, 2 * dh), 1)
    scale = dh ** -0.5
    q_maps = [q1_ref[...] * scale, q2_ref[...] * scale]
    zero = jnp.zeros((t, 2 * dh), F32)
    blocks = []
    for e in range(2):
        in_e = (lane >= e * dh) & (lane < (e + 1) * dh)
        for mp in range(2):
            qe = jnp.where(in_e, q_maps[mp], 0.0)
            blocks.append(jnp.concatenate([qe, zero] if mp == 0 else [zero, qe], axis=1))
    q_all = jnp.concatenate(blocks, axis=0).astype(BF16)

    m_ref[...] = jnp.full(m_ref.shape, NEG_INF, F32)
    l_ref[...] = jnp.zeros(l_ref.shape, F32)
    acc_ref[...] = jnp.zeros(acc_ref.shape, F32)

    def step(j, n, bias):
        start = pl.multiple_of(j * t, t)
        kb = jnp.concatenate([k1_ref[:, pl.ds(start, n * t)], k2_ref[:, pl.ds(start, n * t)]],
                             axis=0).astype(BF16)
        s = jnp.dot(q_all, kb, preferred_element_type=F32)
        if bias is not None:
            s = s + bias
        m_prev = m_ref[...]
        m_new = jnp.maximum(m_prev, jnp.max(s, axis=1, keepdims=True))
        p = jnp.exp(s - jnp.tile(m_new, (1, n * t // LANES)))
        alpha = jnp.exp(m_prev - m_new)
        l_ref[...] = alpha * l_ref[...] + jnp.sum(p, axis=1, keepdims=True)
        m_ref[...] = m_new
        pb = p.astype(BF16)
        for e in range(2):
            ve = v_ref[pl.ds(start, n * t), e * dv:(e + 1) * dv].astype(BF16)
            rs = slice(2 * e * t, 2 * (e + 1) * t)
            acc_ref[rs, :] = acc_ref[rs, :] * alpha[rs] + jnp.dot(
                pb[rs], ve, preferred_element_type=F32)

    n_far = jnp.maximum(qb - 1, 0)

    def far_body(i, carry):
        step(2 * i, 2, None)
        return carry

    lax.fori_loop(0, n_far // 2, far_body, 0)

    @pl.when(n_far % 2 == 1)
    def _():
        step(n_far - 1, 1, None)

    @pl.when(qb > 0)
    def _():
        step(qb - 1, 1, bnear_ref[...])

    step(qb, 1, bdiag_ref[...])
    o = acc_ref[...] / l_ref[...]
    lam = lam_ref[0, 0]
    nw = nw_ref[...]
    for e in range(2):
        oe = o[2 * e * t:(2 * e + 1) * t] - lam * o[(2 * e + 1) * t:(2 * e + 2) * t]
        o_ref[:, e * dv:(e + 1) * dv] = (_rms(oe, nw) * out_scale).astype(o_ref.dtype)


def _attn_prompt(u, qcol, vcol, kt, tbl, lam, nw, batch, seqlen, heads, dh, dv, out_scale):
    assert dv == LANES and 2 * dh == LANES and heads % 2 == 0
    n_buckets = tbl.shape[0]
    t = _pick(seqlen, (256, 128))
    assert seqlen % t == 0 and t % LANES == 0 and t + 1 >= _far_start(n_buckets)
    nq = seqlen // t
    nhp = heads // 2
    qc, vc = qcol // LANES, vcol // (2 * dv)
    smem = pl.BlockSpec(memory_space=pltpu.SMEM)
    kern = functools.partial(_attn_kernel, t=t, dh=dh, dv=dv, heads=heads, n_buckets=n_buckets,
                             out_scale=out_scale)
    return pl.pallas_call(
        kern,
        grid=(nhp, batch, nq),
        in_specs=[smem, smem,
                  pl.BlockSpec((1, dv), lambda h, b, i: (0, 0)),
                  pl.BlockSpec((t, LANES), lambda h, b, i: (b * nq + i, qc + h)),
                  pl.BlockSpec((t, LANES), lambda h, b, i: (b * nq + i, qc + nhp + h)),
                  pl.BlockSpec((2 * dh, seqlen), lambda h, b, i: (b * 2 * nhp + h, 0)),
                  pl.BlockSpec((2 * dh, seqlen), lambda h, b, i: (b * 2 * nhp + nhp + h, 0)),
                  pl.BlockSpec((seqlen, 2 * dv), lambda h, b, i: (b, vc + h))],
        out_specs=pl.BlockSpec((t, 2 * dv), lambda h, b, i: (b * nq + i, h)),
        out_shape=jax.ShapeDtypeStruct((batch * seqlen, heads * dv), BF16),
        scratch_shapes=[pltpu.VMEM((4 * t, t), F32), pltpu.VMEM((4 * t, t), F32),
                        pltpu.VMEM((4 * t, LANES), F32), pltpu.VMEM((4 * t, LANES), F32),
                        pltpu.VMEM((4 * t, dv), F32)],
        compiler_params=_params("arbitrary", "arbitrary", "arbitrary"),
        name="attn_prompt",
    )(tbl, lam, nw, u, u, kt, kt, u)


def _decode_kernel(pt_ref, lam_ref, trow_ref, nw_ref, q_ref, kn_ref, vn_ref, *rest,
                   pages, page, lq, heads, dh, dv, n_buckets, past_len, near_slots, out_scale):
    k_refs = rest[:pages]
    v_refs = rest[pages:2 * pages]
    o_ref, m_ref, l_ref, acc_ref, bias_ref = rest[2 * pages:]
    b = pl.program_id(0)
    g = pl.program_id(1)
    ng = pl.num_programs(1)
    mh = 2 * heads
    rows = mh * lq
    n_near = len(near_slots)
    order = [m * heads + h for h in range(heads) for m in range(2)]

    def near_bias(key0, n_valid):
        r = lax.broadcasted_iota(jnp.int32, (rows, page), 0)
        col = lax.broadcasted_iota(jnp.int32, (rows, page), 1)
        dist = past_len + r % lq - (key0 + col)
        bk = _bucket(dist, n_buckets)
        far = trow_ref[:, n_buckets - 1:n_buckets]
        bias = jnp.zeros((rows, page), F32)
        for k in range(n_buckets - 1):
            bias = jnp.where(bk == k, trow_ref[:, k:k + 1] - far, bias)
        return jnp.where((dist >= 0) & (col < n_valid), bias, NEG_INF)

    @pl.when((b == 0) & (g == 0))
    def _():
        for i, slot in enumerate(near_slots):
            bias_ref[i] = near_bias(past_len - (pages - slot) * page, page)
        bias_ref[n_near] = near_bias(past_len, lq)

    @pl.when(g == 0)
    def _():
        m_ref[...] = jnp.full(m_ref.shape, NEG_INF, F32)
        l_ref[...] = jnp.zeros(l_ref.shape, F32)
        acc_ref[...] = jnp.zeros(acc_ref.shape, F32)

    q = q_ref[...] * (dh ** -0.5)
    q_parts = [q[:, j * dh:(j + 1) * dh].astype(BF16) for j in order]

    def attend(score, value_rows, bias):
        s = jnp.concatenate([score(qj, j) for qj, j in zip(q_parts, order)], axis=0)
        if bias is not None:
            s = s + bias
        m_prev = m_ref[...]
        m_new = jnp.maximum(m_prev, jnp.max(s, axis=1, keepdims=True))
        p = jnp.exp(s - jnp.tile(m_new, (1, s.shape[1] // LANES)))
        alpha = jnp.exp(m_prev - m_new)
        l_ref[...] = alpha * l_ref[...] + jnp.sum(p, axis=1, keepdims=True)
        m_ref[...] = m_new
        pv = jnp.concatenate(
            [jnp.dot(p[2 * lq * h:2 * lq * (h + 1)].astype(BF16), value_rows(h).astype(BF16),
                     preferred_element_type=F32) for h in range(heads)], axis=0)
        acc_ref[...] = acc_ref[...] * alpha + pv

    is_last = g == ng - 1
    bias = None
    if near_slots:
        zero = jnp.zeros((rows, page), F32)
        bias = jnp.concatenate(
            [jnp.where(is_last, bias_ref[near_slots.index(s)], 0.0) if s in near_slots else zero
             for s in range(pages)], axis=1)
    def page_score(qj, j):
        kt = jnp.concatenate([r[j * dh:(j + 1) * dh, :] for r in k_refs], axis=1)
        return jnp.dot(qj, kt.astype(BF16), preferred_element_type=F32)

    attend(page_score,
           lambda h: jnp.concatenate([r[pl.ds(h, page, stride=heads), :] for r in v_refs], axis=0),
           bias)

    @pl.when(is_last)
    def _():
        kn = kn_ref[...]
        vn = vn_ref[...]
        zk = jnp.zeros((page - lq, dh), F32)
        zv = jnp.zeros((page - lq, dv), F32)

        def new_score(qj, j):
            kj = jnp.concatenate([kn[:, j * dh:(j + 1) * dh], zk], axis=0).astype(BF16)
            return lax.dot_general(qj, kj, _NT, preferred_element_type=F32)

        attend(new_score,
               lambda h: jnp.concatenate([vn[:, h * dv:(h + 1) * dv], zv], axis=0),
               bias_ref[n_near])
        full = acc_ref[...] / l_ref[...]
        lam = lam_ref[0, 0]
        nw = nw_ref[...]
        for h in range(heads):
            o0 = full[2 * lq * h:2 * lq * h + lq]
            o1 = full[2 * lq * h + lq:2 * lq * (h + 1)]
            o_ref[:, h * dv:(h + 1) * dv] = (_rms(o0 - lam * o1, nw) * out_scale).astype(o_ref.dtype)


def _attn_decode(u, qcol, kcol, vcol, cache_k, cache_v, page_table, tbl, lam, nw,
                 batch, lq, heads, dh, dv, out_scale):
    n_pool, page = cache_k.shape[0], cache_k.shape[1]
    n_pages = page_table.shape[1]
    past_len = n_pages * page
    mh = 2 * heads
    kw, vw = mh * dh, heads * dv
    assert page % LANES == 0 and kw % LANES == 0 and vw % LANES == 0 and lq % 8 == 0
    n_buckets = tbl.shape[0]
    far = _far_start(n_buckets)
    first_near = max(0, -(-(past_len - page + 1 - far + 1) // page))
    pages = _pick(n_pages, (16, 8, 4, 2, 1))
    assert n_pages - pages <= first_near, "near pages must fall in the last grid step"
    near_slots = tuple(s for s in range(pages) if n_pages - pages + s >= first_near)
    ck = jnp.transpose(cache_k, (0, 2, 3, 1)).reshape(n_pool * mh * dh, page)
    cv = cache_v.reshape(n_pool * page * heads, dv)
    trow = jnp.repeat(tbl.T.reshape(2, heads, -1).transpose(1, 0, 2).reshape(mh, -1), lq, axis=0)
    rows = mh * lq

    def page_spec(n_rows, width, slot):
        return pl.BlockSpec((n_rows, width),
                            lambda b, g, pt: (pt[b * n_pages + g * pages + slot], 0))

    smem = pl.BlockSpec(memory_space=pltpu.SMEM)
    in_specs = ([smem,
                 pl.BlockSpec((rows, n_buckets), lambda b, g, pt: (0, 0)),
                 pl.BlockSpec((1, dv), lambda b, g, pt: (0, 0)),
                 pl.BlockSpec((lq, kw), lambda b, g, pt: (b, qcol // kw)),
                 pl.BlockSpec((lq, kw), lambda b, g, pt: (b, kcol // kw)),
                 pl.BlockSpec((lq, vw), lambda b, g, pt: (b, vcol // vw))]
                + [page_spec(mh * dh, page, s) for s in range(pages)]
                + [page_spec(page * heads, dv, s) for s in range(pages)])
    assert qcol % kw == 0 and kcol % kw == 0 and vcol % vw == 0
    kern = functools.partial(_decode_kernel, pages=pages, page=page, lq=lq, heads=heads, dh=dh,
                             dv=dv, n_buckets=n_buckets, past_len=past_len,
                             near_slots=near_slots, out_scale=out_scale)
    grid_spec = pltpu.PrefetchScalarGridSpec(
        num_scalar_prefetch=1,
        grid=(batch, n_pages // pages),
        in_specs=in_specs,
        out_specs=pl.BlockSpec((lq, vw), lambda b, g, pt: (b, 0)),
        scratch_shapes=[pltpu.VMEM((rows, LANES), F32), pltpu.VMEM((rows, LANES), F32),
                        pltpu.VMEM((rows, dv), F32),
                        pltpu.VMEM((len(near_slots) + 1, rows, page), F32)])
    return pl.pallas_call(
        kern,
        grid_spec=grid_spec,
        out_shape=jax.ShapeDtypeStruct((batch * lq, vw), BF16),
        compiler_params=_params("arbitrary", "arbitrary"),
        name="attn_decode",
    )(page_table.reshape(-1), lam, trow, nw, u, u, u, *([ck] * pages), *([cv] * pages))


def _merge_resident_kernel(*refs, n_g):
    ohg_ref, oda_ref = refs[:2]
    g_refs = refs[2:2 + 2 * n_g]
    x_ref, wbh_ref, wbd_ref, wo_ref, nw_ref, o_ref = refs[2 + 2 * n_g:]
    g_hg = jnp.concatenate([r[...] for r in g_refs[:n_g]], axis=1)
    g_da = jnp.concatenate([r[...] for r in g_refs[n_g:]], axis=1)
    y_hg = jnp.dot(ohg_ref[...], wbh_ref[...], preferred_element_type=F32)
    y_da = jnp.dot(oda_ref[...], wbd_ref[...], preferred_element_type=F32)
    mixed = (_sigmoid(g_hg) * y_hg + _sigmoid(g_da) * y_da).astype(BF16)
    z = jnp.dot(mixed, wo_ref[...], preferred_element_type=F32)
    o_ref[...] = x_ref[...] + _rms(z, nw_ref[...])


def _merge_resident(o_hg, o_da, u, gcol, x, wbh, wbd, wo, nw):
    m, d = x.shape
    w = o_hg.shape[1]
    gw = math.gcd(gcol, d)
    n_g = d // gw
    assert gw % LANES == 0
    tm = _pick(m, (256, 128, 64, 8))
    const = lambda i: (0, 0)
    g_specs = [pl.BlockSpec((tm, gw), functools.partial(lambda i, c: (i, c), c=gcol // gw + k))
               for k in range(2 * n_g)]
    return pl.pallas_call(
        functools.partial(_merge_resident_kernel, n_g=n_g),
        grid=(m // tm,),
        in_specs=[pl.BlockSpec((tm, w), lambda i: (i, 0)),
                  pl.BlockSpec((tm, w), lambda i: (i, 0))]
                 + g_specs
                 + [pl.BlockSpec((tm, d), lambda i: (i, 0)),
                    pl.BlockSpec((w, d), const, pipeline_mode=pl.Buffered(1)),
                    pl.BlockSpec((w, d), const, pipeline_mode=pl.Buffered(1)),
                    pl.BlockSpec((d, d), const, pipeline_mode=pl.Buffered(1)),
                    pl.BlockSpec((1, d), const)],
        out_specs=pl.BlockSpec((tm, d), lambda i: (i, 0)),
        out_shape=jax.ShapeDtypeStruct((m, d), F32),
        compiler_params=_params("parallel"),
        name="merge",
    )(o_hg, o_da, *([u] * (2 * n_g)), x, wbh, wbd, wo, nw)


def _merge_kernel(ohg_ref, oda_ref, ghg_ref, gda_ref, x_ref, wbh_ref, wbd_ref, wo_ref, nw_ref,
                  o_ref, *copies, emit_w):
    copies = copies if emit_w else (None, None, None)
    n = pl.program_id(1)

    @pl.when(n == 0)
    def _():
        o_ref[...] = jnp.zeros(o_ref.shape, F32)

    y_hg = jnp.dot(ohg_ref[...], _bf16_weight(wbh_ref, copies[0]), preferred_element_type=F32)
    y_da = jnp.dot(oda_ref[...], _bf16_weight(wbd_ref, copies[1]), preferred_element_type=F32)
    mixed = (_sigmoid(ghg_ref[...]) * y_hg + _sigmoid(gda_ref[...]) * y_da).astype(BF16)
    o_ref[...] += jnp.dot(mixed, _bf16_weight(wo_ref, copies[2]), preferred_element_type=F32)

    @pl.when(n == pl.num_programs(1) - 1)
    def _():
        o_ref[...] = x_ref[...] + _rms(o_ref[...], nw_ref[...])


def _merge(o_hg, o_da, u, gcol, x, wbh, wbd, wo, nw):
    m, d = x.shape
    w = o_hg.shape[1]
    emit_w = wo.dtype != BF16
    tm = _pick(m, (1024, 512, 256, 128, 64, 32, 16, 8))
    tn = _pick(math.gcd(gcol, d), (512, 256, 128))
    nn = d // tn
    gc = gcol // tn
    once = dict(pipeline_mode=pl.Buffered(1))
    out_specs = [pl.BlockSpec((tm, d), lambda i, n: (i, 0))]
    out_shape = [jax.ShapeDtypeStruct((m, d), F32)]
    if emit_w:
        assert m == tm, "the bf16 weight copies are written once per column tile"
        out_specs += [pl.BlockSpec((w, tn), lambda i, n: (0, n)),
                      pl.BlockSpec((w, tn), lambda i, n: (0, n)),
                      pl.BlockSpec((tn, d), lambda i, n: (n, 0))]
        out_shape += [jax.ShapeDtypeStruct((w, d), BF16), jax.ShapeDtypeStruct((w, d), BF16),
                      jax.ShapeDtypeStruct((d, d), BF16)]
    return pl.pallas_call(
        functools.partial(_merge_kernel, emit_w=emit_w),
        grid=(m // tm, nn),
        in_specs=[pl.BlockSpec((tm, w), lambda i, n: (i, 0), **once),
                  pl.BlockSpec((tm, w), lambda i, n: (i, 0), **once),
                  pl.BlockSpec((tm, tn), lambda i, n: (i, gc + n)),
                  pl.BlockSpec((tm, tn), lambda i, n: (i, gc + nn + n)),
                  pl.BlockSpec((tm, d), lambda i, n: (i, 0), **once),
                  pl.BlockSpec((w, tn), lambda i, n: (0, n)),
                  pl.BlockSpec((w, tn), lambda i, n: (0, n)),
                  pl.BlockSpec((tn, d), lambda i, n: (n, 0)),
                  pl.BlockSpec((1, d), lambda i, n: (0, 0))],
        out_specs=out_specs,
        out_shape=out_shape,
        compiler_params=_params("parallel", "arbitrary"),
        name="merge",
    )(o_hg, o_da, u, u, x, wbh, wbd, wo, nw)


def _ffn_kernel(x_ref, npre_ref, wg_ref, wu_ref, wd_ref, npost_ref, o_ref, *rest, emit_w):
    h_ref = rest[-1]
    copies = rest[:3] if emit_w else (None, None, None)
    j = pl.program_id(1)

    @pl.when(j == 0)
    def _():
        h_ref[...] = _rms(x_ref[...], npre_ref[...]).astype(BF16)
        o_ref[...] = jnp.zeros(o_ref.shape, F32)

    h = h_ref[...]
    gate = jnp.dot(h, _bf16_weight(wg_ref, copies[0]), preferred_element_type=F32)
    up = jnp.dot(h, _bf16_weight(wu_ref, copies[1]), preferred_element_type=F32)
    act = (gate * _sigmoid(gate) * up).astype(BF16)
    o_ref[...] += jnp.dot(act, _bf16_weight(wd_ref, copies[2]), preferred_element_type=F32)

    @pl.when(j == pl.num_programs(1) - 1)
    def _():
        o_ref[...] = x_ref[...] + _rms(o_ref[...], npost_ref[...])


def _ffn(x, npre, w_gate, w_up, w_down, npost):
    m, d = x.shape
    ff = w_down.shape[0]
    emit_w = w_down.dtype != BF16
    tm = _pick(m, (1024, 512, 256, 128, 64, 32, 16, 8))
    tf = _pick(ff, (512, 256, 128))
    nf = ff // tf
    (wg, gcol), (wu, ucol) = w_gate, w_up
    assert gcol % tf == 0 and ucol % tf == 0
    gc, uc = gcol // tf, ucol // tf
    out_specs = [pl.BlockSpec((tm, d), lambda i, j: (i, 0))]
    out_shape = [jax.ShapeDtypeStruct((m, d), F32)]
    if emit_w:
        assert m == tm, "the bf16 weight copies are written once per hidden-dim tile"
        out_specs += [pl.BlockSpec((d, tf), lambda i, j: (0, j)),
                      pl.BlockSpec((d, tf), lambda i, j: (0, j)),
                      pl.BlockSpec((tf, d), lambda i, j: (j, 0))]
        out_shape += [jax.ShapeDtypeStruct((d, ff), BF16), jax.ShapeDtypeStruct((d, ff), BF16),
                      jax.ShapeDtypeStruct((ff, d), BF16)]
    return pl.pallas_call(
        functools.partial(_ffn_kernel, emit_w=emit_w),
        grid=(m // tm, nf),
        in_specs=[pl.BlockSpec((tm, d), lambda i, j: (i, 0)),
                  pl.BlockSpec((1, d), lambda i, j: (0, 0)),
                  pl.BlockSpec((d, tf), lambda i, j: (0, gc + j)),
                  pl.BlockSpec((d, tf), lambda i, j: (0, uc + j)),
                  pl.BlockSpec((tf, d), lambda i, j: (j, 0)),
                  pl.BlockSpec((1, d), lambda i, j: (0, 0))],
        out_specs=out_specs,
        out_shape=out_shape,
        scratch_shapes=[pltpu.VMEM((tm, d), BF16)],
        compiler_params=_params("parallel", "arbitrary"),
        name="ffn",
    )(x, npre, wg, wu, w_down, npost)


def _layer(x, s0, paged, lw, wts, dims):
    batch, seqlen, d = x.shape
    wts = dict(wts)
    hg_heads, hg_dk, hg_dv, da_heads, da_dh, da_dv = dims
    hg_w, da_qk, da_w = hg_heads * hg_dk, 2 * da_heads * da_dh, da_heads * da_dv
    cols = np.cumsum([0, hg_w, hg_w, hg_heads * hg_dv, hg_heads * hg_dv, da_qk, da_qk, da_w, d, d])
    qcol, kcol, vcol, gcol = int(cols[4]), int(cols[5]), int(cols[6]), int(cols[7])
    x2 = x.reshape(batch * seqlen, d)

    if paged is None:
        assert da_qk == da_w
        u, kt, v2 = _norm_proj(x2, lw["norm_mix_pre"], wts["w_in"],
                               kv=(kcol, vcol, da_qk, seqlen, da_heads))
        o_da = _attn_prompt(u, qcol, vcol, kt, lw["tbl"], lw["lam"], lw["da_subln_w"],
                            batch, seqlen, da_heads, da_dh, da_dv, lw["out_scale"])
        k = kt.reshape(batch, 2 * da_heads, da_dh, seqlen).transpose(0, 3, 1, 2)
    else:
        u, *copy = _norm_proj(x2, lw["norm_mix_pre"], wts["w_in"])
        if copy:
            wts["w_in"], = copy
        o_da = _attn_decode(u, qcol, kcol, vcol, paged[0], paged[1], paged[2], lw["tbl"],
                            lw["lam"], lw["da_subln_w"], batch, seqlen, da_heads, da_dh, da_dv,
                            lw["out_scale"])
        k = u[:, kcol:kcol + da_qk].reshape(batch, seqlen, 2 * da_heads, da_dh)
        v2 = u[:, vcol:vcol + da_w]
    o_hg, s_new = _hgrn(u, 0, lw["lb"], lw["hg_norm_w"], s0, batch, seqlen, hg_heads, hg_dk, hg_dv)
    merge_args = (o_hg, o_da, u, gcol, x2, wts["w_branch_hg"], wts["w_branch_da"], wts["w_out"],
                  lw["norm_mix_post"])
    if wts["w_out"].dtype == BF16:
        x1 = _merge_resident(*merge_args)
    else:
        x1, wts["w_branch_hg"], wts["w_branch_da"], wts["w_out"] = _merge(*merge_args)
    y, *copy = _ffn(x1, lw["norm_ffn_pre"], wts["w_gate"], wts["w_up"], wts["w_down"],
                    lw["norm_ffn_post"])
    if copy:
        wts["w_gate"], wts["w_up"], wts["w_down"] = (copy[0], 0), (copy[1], 0), copy[2]
    v = v2.reshape(batch, seqlen, da_heads, da_dv)
    return y.reshape(batch, seqlen, d), k, v, s_new, wts


def kernel(x_prompt, x_sample, cache_k, cache_v, state_hgrn, page_table, norm_mix_pre, norm_mix_post, norm_ffn_pre, norm_ffn_post, w_in, hg_lb_logits, hg_norm_w, da_lambda_q1, da_lambda_k1, da_lambda_q2, da_lambda_k2, da_subln_w, rel_bias_table, w_branch_hg, w_branch_da, w_out, w_ffn_up, w_ffn_down):
    depth = w_in.shape[0]
    _, _, hg_heads, hg_dk, hg_dv = state_hgrn.shape
    da_heads, da_dv = cache_v.shape[3], cache_v.shape[4]
    da_dh = cache_k.shape[4]
    dims = (hg_heads, hg_dk, hg_dv, da_heads, da_dh, da_dv)
    lb_all = jnp.cumsum(jax.nn.softmax(hg_lb_logits.astype(F32), axis=0), axis=0)

    y_p, y_s = x_prompt, x_sample
    outs = [[] for _ in range(6)]
    for l in range(depth):
        lam_init = 0.8 - 0.6 * math.exp(-0.3 * l)
        lam = (jnp.exp(jnp.sum(da_lambda_q1[l] * da_lambda_k1[l]))
               - jnp.exp(jnp.sum(da_lambda_q2[l] * da_lambda_k2[l])) + lam_init)
        lw = {
            "norm_mix_pre": norm_mix_pre[l][None], "norm_mix_post": norm_mix_post[l][None],
            "norm_ffn_pre": norm_ffn_pre[l][None], "norm_ffn_post": norm_ffn_post[l][None],
            "lb": lb_all[l][None], "hg_norm_w": hg_norm_w[l][None],
            "da_subln_w": da_subln_w[l][None], "tbl": rel_bias_table.astype(F32),
            "lam": lam.reshape(1, 1).astype(F32), "out_scale": 1.0 - lam_init,
        }
        ff = w_ffn_down.shape[1]
        wts = {"w_in": w_in[l], "w_branch_hg": w_branch_hg[l], "w_branch_da": w_branch_da[l],
               "w_out": w_out[l], "w_gate": (w_ffn_up[l], 0), "w_up": (w_ffn_up[l], ff),
               "w_down": w_ffn_down[l]}
        y_s, ks, vs, ss, wts = _layer(y_s, state_hgrn[l], (cache_k[l], cache_v[l], page_table),
                                      lw, wts, dims)
        y_p, kp, vp, sp, _ = _layer(y_p, None, None, lw, wts, dims)
        for acc, val in zip(outs, (kp, vp, sp, ks, vs, ss)):
            acc.append(val)
    return (y_p, y_s) + tuple(jnp.stack(o) for o in outs)
```

```python
import functools
import math

import jax
import jax.numpy as jnp
import numpy as np
from jax import lax
from jax.experimental import pallas as pl
from jax.experimental.pallas import tpu as pltpu

F32 = jnp.float32
BF16 = jnp.bfloat16

RMS_EPS = 1e-6
NEG_INF = -1e30
HG_CHUNK = 64
REL_MAX_DISTANCE = 128
LANES = 128
VMEM_LIMIT = 62 * 1024 * 1024

_NT = (((1,), (1,)), ((), ()))
_TN = (((0,), (0,)), ((), ()))


def _params(*sem):
    return pltpu.CompilerParams(dimension_semantics=sem, vmem_limit_bytes=VMEM_LIMIT)


def _rms(x, w):
    return x * lax.rsqrt(jnp.mean(x * x, axis=-1, keepdims=True) + RMS_EPS) * w


def _sigmoid(x):
    return 1.0 / (1.0 + jnp.exp(-x))


def _pick(n, prefs):
    for p in prefs:
        if n % p == 0:
            return p
    return n


def _bf16_weight(w_ref, copy_ref):
    w = w_ref[...]
    if copy_ref is not None:
        w = w.astype(BF16)
        copy_ref[...] = w
    return w


def _norm_proj_kernel(x_ref, nw_ref, w_ref, o_ref, *rest, jk, jv, v_heads, emit_w):
    h_ref = rest[-1]
    j = pl.program_id(1)

    @pl.when(j == 0)
    def _():
        h_ref[...] = _rms(x_ref[...], nw_ref[...]).astype(BF16)

    w = _bf16_weight(w_ref, rest[-2] if emit_w else None)
    res = jnp.dot(h_ref[...], w, preferred_element_type=F32)
    o_ref[...] = res
    if jk is not None:
        kt_ref, v_ref = rest[:2]
        tm, tn = res.shape

        @pl.when(j == jk)
        def _():
            kt_ref[...] = res.T

        @pl.when(j == jv)
        def _():
            dv = tn // v_heads
            for h in range(v_heads):
                v_ref[pl.ds(h, tm, stride=v_heads), :] = res[:, h * dv:(h + 1) * dv]


def _norm_proj(x, nw, w, kv=None):
    m, d = x.shape
    n = w.shape[1]
    emit_w = w.dtype != BF16
    tm = _pick(m, (1024, 512, 256, 128, 64, 32, 16, 8))
    tn = _pick(n, (1024, 512, 256, 128))
    out_specs = [pl.BlockSpec((tm, tn), lambda i, j: (i, j))]
    out_shape = [jax.ShapeDtypeStruct((m, n), F32)]
    jk = jv = v_heads = None
    if kv is not None:
        kcol, vcol, width, seqlen, v_heads = kv
        tn = width
        assert n % tn == 0 and kcol % tn == 0 and vcol % tn == 0 and seqlen % tm == 0
        jk, jv = kcol // tn, vcol // tn
        nt = seqlen // tm
        out_specs = [pl.BlockSpec((tm, tn), lambda i, j: (i, j)),
                     pl.BlockSpec((tn, tm), lambda i, j: (i // nt, i % nt)),
                     pl.BlockSpec((tm * v_heads, tn // v_heads), lambda i, j: (i, 0))]
        out_shape += [jax.ShapeDtypeStruct((m // seqlen * tn, seqlen), F32),
                      jax.ShapeDtypeStruct((m * v_heads, tn // v_heads), F32)]
    if emit_w:
        out_specs.append(pl.BlockSpec((d, tn), lambda i, j: (0, j)))
        out_shape.append(jax.ShapeDtypeStruct((d, n), BF16))
        assert m == tm, "the bf16 weight copy is written once per column tile"
    return pl.pallas_call(
        functools.partial(_norm_proj_kernel, jk=jk, jv=jv, v_heads=v_heads, emit_w=emit_w),
        grid=(m // tm, n // tn),
        in_specs=[pl.BlockSpec((tm, d), lambda i, j: (i, 0)),
                  pl.BlockSpec((1, d), lambda i, j: (0, 0)),
                  pl.BlockSpec((d, tn), lambda i, j: (0, j))],
        out_specs=out_specs,
        out_shape=out_shape,
        scratch_shapes=[pltpu.VMEM((tm, d), BF16)],
        compiler_params=_params("parallel", "arbitrary"),
        name="norm_proj",
    )(x, nw, w)


def _hgrn_kernel(*refs, heads, dk, dv, chunk, rows, has_s0):
    if has_s0:
        q_ref, f_ref, i_ref, gate_ref, lb_ref, nw_ref, s0_ref, o_ref, sout_ref, st_ref = refs
    else:
        q_ref, f_ref, i_ref, gate_ref, lb_ref, nw_ref, o_ref, sout_ref, st_ref = refs
    t = pl.program_id(1)
    real = q_ref.shape[0]

    @pl.when(t == 0)
    def _():
        for h in range(heads):
            if has_s0:
                st_ref[h] = s0_ref[h].T
            else:
                st_ref[h] = jnp.zeros((dv, dk), F32)

    lb = lb_ref[...]
    f = lb + (1.0 - lb) * _sigmoid(f_ref[...])
    g = jnp.log(f)
    kin = 1.0 - f
    q = q_ref[...]
    v = i_ref[...]
    if rows > real:
        def pad(a):
            return jnp.concatenate([a, jnp.zeros((rows - real, a.shape[1]), F32)], axis=0)
        g, kin, q, v = pad(g), pad(kin), pad(q), pad(v)

    cs = math.gcd(rows, 2 * LANES)
    r = lax.broadcasted_iota(jnp.int32, (cs, cs), 0)
    c = lax.broadcasted_iota(jnp.int32, (cs, cs), 1)
    within = (r // chunk == c // chunk) & (c <= r)
    tri = jnp.where(within, 1.0, 0.0).astype(BF16)
    g_hi = g.astype(BF16)
    g_r1 = g - g_hi.astype(F32)
    g_mid = g_r1.astype(BF16)
    g_lo = (g_r1 - g_mid.astype(F32)).astype(BF16)
    G = jnp.concatenate(
        [jnp.dot(tri, g_hi[i:i + cs], preferred_element_type=F32)
         + jnp.dot(tri, g_mid[i:i + cs], preferred_element_type=F32)
         + jnp.dot(tri, g_lo[i:i + cs], preferred_element_type=F32)
         for i in range(0, rows, cs)], axis=0)

    qg = (q * jnp.exp(G)).astype(BF16)
    kg = (kin * jnp.exp(-G)).astype(BF16)
    vb = v.astype(BF16)
    nw = nw_ref[...]

    states = [st_ref[h] for h in range(heads)]
    for g0 in range(0, rows, cs):
        intra = []
        for h in range(heads):
            sk = slice(h * dk, (h + 1) * dk)
            a = lax.dot_general(qg[g0:g0 + cs, sk], kg[g0:g0 + cs, sk], _NT,
                                preferred_element_type=F32)
            a = jnp.where(within, a, 0.0).astype(BF16)
            intra.append(jnp.dot(a, vb[g0:g0 + cs, h * dv:(h + 1) * dv],
                                 preferred_element_type=F32))
        for ci in range(cs // chunk):
            lo = g0 + ci * chunk
            Gc = G[lo:lo + chunk]
            Gl = Gc[chunk - 1:chunk]
            kdec = (kin[lo:lo + chunk] * jnp.exp(Gl - Gc)).astype(BF16)
            decay = jnp.exp(Gl)
            n_out = min(chunk, real - lo)
            for h in range(heads):
                sk = slice(h * dk, (h + 1) * dk)
                sv = slice(h * dv, (h + 1) * dv)
                st = states[h]
                o = intra[h][ci * chunk:(ci + 1) * chunk] + lax.dot_general(
                    qg[lo:lo + chunk, sk], st.astype(BF16), _NT, preferred_element_type=F32)
                states[h] = st * decay[:, sk] + lax.dot_general(
                    vb[lo:lo + chunk, sv], kdec[:, sk], _TN, preferred_element_type=F32)
                if n_out > 0:
                    gt = gate_ref[lo:lo + n_out, sv]
                    on = _rms(o[:n_out], nw) * (gt * _sigmoid(gt))
                    o_ref[lo:lo + n_out, sv] = on.astype(o_ref.dtype)
    for h in range(heads):
        st_ref[h] = states[h]

    @pl.when(t == pl.num_programs(1) - 1)
    def _():
        for h in range(heads):
            sout_ref[h] = st_ref[h].T


def _hgrn(u, col0, lb, nw, s0, batch, seqlen, heads, dk, dv):
    width = heads * dk
    assert dk == dv and col0 % width == 0
    cb = col0 // width
    chunk = min(HG_CHUNK, seqlen)
    if seqlen >= LANES:
        tb = _pick(seqlen, (512, 256, 128))
        rows = tb
        assert tb % chunk == 0
    else:
        tb = seqlen
        rows = LANES
        chunk = LANES
    nt = seqlen // tb
    has_s0 = s0 is not None

    def col(k):
        return pl.BlockSpec((tb, width), lambda b, t: (b * nt + t, cb + k))

    in_specs = [col(0), col(1), col(2), col(3),
                pl.BlockSpec((1, width), lambda b, t: (0, 0)),
                pl.BlockSpec((1, dv), lambda b, t: (0, 0))]
    args = [u, u, u, u, lb, nw]
    if has_s0:
        in_specs.append(pl.BlockSpec((None, heads, dk, dv), lambda b, t: (b, 0, 0, 0)))
        args.append(s0)
    kern = functools.partial(_hgrn_kernel, heads=heads, dk=dk, dv=dv, chunk=chunk, rows=rows,
                             has_s0=has_s0)
    return pl.pallas_call(
        kern,
        grid=(batch, nt),
        in_specs=in_specs,
        out_specs=[pl.BlockSpec((tb, heads * dv), lambda b, t: (b * nt + t, 0)),
                   pl.BlockSpec((None, heads, dk, dv), lambda b, t: (b, 0, 0, 0))],
        out_shape=[jax.ShapeDtypeStruct((batch * seqlen, heads * dv), BF16),
                   jax.ShapeDtypeStruct((batch, heads, dk, dv), F32)],
        scratch_shapes=[pltpu.VMEM((heads, dv, dk), F32)],
        compiler_params=_params("parallel", "arbitrary"),
        name="hgrn",
    )(*args)


def _rel_bias(dist, value, n_buckets):
    n = jnp.maximum(dist, 0)
    max_exact = n_buckets // 2
    nf = jnp.maximum(n, 1).astype(F32)
    x = (jnp.log(nf / max_exact) / math.log(REL_MAX_DISTANCE / max_exact)
         * (n_buckets - max_exact))
    bias = jnp.zeros(dist.shape, F32)
    for k in range(n_buckets - 2, max_exact - 1, -1):
        bias = jnp.where(x < k - max_exact + 1, value(k), bias)
    for k in range(max_exact - 1, -1, -1):
        bias = jnp.where(n <= k, value(k), bias)
    return bias


def _far_start(n_buckets):
    n = np.arange(1, 8 * REL_MAX_DISTANCE, dtype=np.int64)
    max_exact = n_buckets // 2
    nf = n.astype(np.float32)
    large = max_exact + (np.log(nf / np.float32(max_exact)) / np.float32(math.log(REL_MAX_DISTANCE / max_exact))
                         * np.float32(n_buckets - max_exact)).astype(np.int32)
    b = np.where(n < max_exact, n, np.minimum(large, n_buckets - 1))
    below = n[b < n_buckets - 1]
    return int(below.max()) + 2


def _attn_kernel(tbl_ref, lam_ref, nw_ref, q1_ref, q2_ref, k1_ref, k2_ref, v_ref, o_ref,
                 bias_ref, m_ref, acc_ref, *,
                 t, dh, dv, heads, n_buckets, out_scale):
    hp = pl.program_id(0)
    b = pl.program_id(1)
    qb = pl.program_id(2)

    @pl.when((b == 0) & (qb == 0))
    def _():
        row = lax.broadcasted_iota(jnp.int32, (t, t), 0)
        col = lax.broadcasted_iota(jnp.int32, (t, t), 1)
        for e in range(2):
            for mp in range(2):
                hcol = mp * heads + 2 * hp + e
                far = tbl_ref[n_buckets - 1, hcol]
                value = lambda k: tbl_ref[k, hcol] - far
                bd = _rel_bias(row - col, value, n_buckets)
                bn = _rel_bias(row - col + t, value, n_buckets)
                rs = slice((2 * e + mp) * t, (2 * e + mp + 1) * t)
                bias_ref[rs, :t] = bn
                bias_ref[rs, t:] = jnp.where(col <= row, bd, NEG_INF)

    lane = lax.broadcasted_iota(jnp.int32, (t, 2 * dh), 1)
    scale = dh ** -0.5
    q_maps = [q1_ref[...] * scale, q2_ref[...] * scale]
    zero = jnp.zeros((t, 2 * dh), F32)
    blocks = []
    for e in range(2):
        in_e = (lane >= e * dh) & (lane < (e + 1) * dh)
        for mp in range(2):
            qe = jnp.where(in_e, q_maps[mp], 0.0)
            blocks.append(jnp.concatenate([qe, zero] if mp == 0 else [zero, qe], axis=1))
    q_all = jnp.concatenate(blocks, axis=0).astype(BF16)

    m_ref[...] = jnp.full(m_ref.shape, NEG_INF, F32)
    acc_ref[...] = jnp.zeros(acc_ref.shape, F32)

    def step(j, n, bias=None):
        start = pl.multiple_of(j * t, t)
        kb = jnp.concatenate([k1_ref[:, pl.ds(start, n * t)], k2_ref[:, pl.ds(start, n * t)]],
                             axis=0).astype(BF16)
        s = jnp.dot(q_all, kb, preferred_element_type=F32)
        if bias is not None:
            s = s + bias
        m_prev = m_ref[...]
        m_new = jnp.maximum(m_prev, jnp.max(s, axis=1, keepdims=True))
        pb = jnp.exp(s - jnp.tile(m_new, (1, n * t // LANES))).astype(BF16)
        alpha = jnp.tile(jnp.exp(m_prev - m_new), (1, 2))
        m_ref[...] = m_new
        ones = jnp.ones((n * t, dv), BF16)
        for e in range(2):
            ve = v_ref[pl.ds(start, n * t), e * dv:(e + 1) * dv].astype(BF16)
            rs = slice(2 * e * t, 2 * (e + 1) * t)
            acc_ref[rs, :] = acc_ref[rs, :] * alpha[rs] + jnp.dot(
                pb[rs], jnp.concatenate([ve, ones], axis=1), preferred_element_type=F32)

    n_far = jnp.maximum(qb - 1, 0)

    def far_body(i, carry):
        step(2 * i, 2)
        return carry

    lax.fori_loop(0, n_far // 2, far_body, 0)

    @pl.when(n_far % 2 == 1)
    def _():
        step(n_far - 1, 1)

    @pl.when(qb > 0)
    def _():
        step(qb - 1, 2, bias_ref[...])

    @pl.when(qb == 0)
    def _():
        step(0, 1, bias_ref[:, t:])

    o = acc_ref[:, :dv] / acc_ref[:, dv:]
    lam = lam_ref[0, 0]
    nw = nw_ref[...]
    for e in range(2):
        oe = o[2 * e * t:(2 * e + 1) * t] - lam * o[(2 * e + 1) * t:(2 * e + 2) * t]
        o_ref[:, e * dv:(e + 1) * dv] = (_rms(oe, nw) * out_scale).astype(o_ref.dtype)


def _attn_prompt(u, qcol, vcol, kt, tbl, lam, nw, batch, seqlen, heads, dh, dv, out_scale):
    assert dv == LANES and 2 * dh == LANES and heads % 2 == 0
    n_buckets = tbl.shape[0]
    t = _pick(seqlen, (256, 128))
    assert seqlen % t == 0 and t % LANES == 0 and t + 1 >= _far_start(n_buckets)
    nq = seqlen // t
    nhp = heads // 2
    qc, vc = qcol // LANES, vcol // (2 * dv)
    smem = pl.BlockSpec(memory_space=pltpu.SMEM)
    kern = functools.partial(_attn_kernel, t=t, dh=dh, dv=dv, heads=heads, n_buckets=n_buckets,
                             out_scale=out_scale)
    return pl.pallas_call(
        kern,
        grid=(nhp, batch, nq),
        in_specs=[smem, smem,
                  pl.BlockSpec((1, dv), lambda h, b, i: (0, 0)),
                  pl.BlockSpec((t, LANES), lambda h, b, i: (b * nq + i, qc + h)),
                  pl.BlockSpec((t, LANES), lambda h, b, i: (b * nq + i, qc + nhp + h)),
                  pl.BlockSpec((2 * dh, seqlen), lambda h, b, i: (b * 2 * nhp + h, 0)),
                  pl.BlockSpec((2 * dh, seqlen), lambda h, b, i: (b * 2 * nhp + nhp + h, 0)),
                  pl.BlockSpec((seqlen, 2 * dv), lambda h, b, i: (b, vc + h))],
        out_specs=pl.BlockSpec((t, 2 * dv), lambda h, b, i: (b * nq + i, h)),
        out_shape=jax.ShapeDtypeStruct((batch * seqlen, heads * dv), BF16),
        scratch_shapes=[pltpu.VMEM((4 * t, 2 * t), F32),
                        pltpu.VMEM((4 * t, LANES), F32), pltpu.VMEM((4 * t, 2 * dv), F32)],
        compiler_params=_params("arbitrary", "arbitrary", "arbitrary"),
        name="attn_prompt",
    )(tbl, lam, nw, u, u, kt, kt, u)


def _decode_kernel(pt_ref, lam_ref, trow_ref, nw_ref, q_ref, kn_ref, vn_ref, *rest,
                   pages, page, lq, heads, dh, dv, n_buckets, past_len, near_slots, out_scale):
    k_refs = rest[:pages]
    v_refs = rest[pages:2 * pages]
    o_ref, m_ref, l_ref, acc_ref, bias_ref = rest[2 * pages:]
    b = pl.program_id(0)
    g = pl.program_id(1)
    ng = pl.num_programs(1)
    mh = 2 * heads
    rows = mh * lq
    n_near = len(near_slots)
    order = [m * heads + h for h in range(heads) for m in range(2)]

    def near_bias(key0, n_valid):
        r = lax.broadcasted_iota(jnp.int32, (rows, page), 0)
        col = lax.broadcasted_iota(jnp.int32, (rows, page), 1)
        dist = past_len + r % lq - (key0 + col)
        far = trow_ref[:, n_buckets - 1:n_buckets]
        bias = _rel_bias(dist, lambda k: trow_ref[:, k:k + 1] - far, n_buckets)
        return jnp.where((dist >= 0) & (col < n_valid), bias, NEG_INF)

    @pl.when((b == 0) & (g == 0))
    def _():
        for i, slot in enumerate(near_slots):
            bias_ref[i] = near_bias(past_len - (pages - slot) * page, page)
        bias_ref[n_near] = near_bias(past_len, lq)

    @pl.when(g == 0)
    def _():
        m_ref[...] = jnp.full(m_ref.shape, NEG_INF, F32)
        l_ref[...] = jnp.zeros(l_ref.shape, F32)
        acc_ref[...] = jnp.zeros(acc_ref.shape, F32)

    q = q_ref[...] * (dh ** -0.5)
    q_parts = [q[:, j * dh:(j + 1) * dh].astype(BF16) for j in order]

    def attend(score, value_rows, bias):
        s = jnp.concatenate([score(qj, j) for qj, j in zip(q_parts, order)], axis=0)
        if bias is not None:
            s = s + bias
        m_prev = m_ref[...]
        m_new = jnp.maximum(m_prev, jnp.max(s, axis=1, keepdims=True))
        p = jnp.exp(s - jnp.tile(m_new, (1, s.shape[1] // LANES)))
        alpha = jnp.exp(m_prev - m_new)
        l_ref[...] = alpha * l_ref[...] + jnp.sum(p, axis=1, keepdims=True)
        m_ref[...] = m_new
        pv = jnp.concatenate(
            [jnp.dot(p[2 * lq * h:2 * lq * (h + 1)].astype(BF16), value_rows(h).astype(BF16),
                     preferred_element_type=F32) for h in range(heads)], axis=0)
        acc_ref[...] = acc_ref[...] * alpha + pv

    is_last = g == ng - 1
    bias = None
    if near_slots:
        zero = jnp.zeros((rows, page), F32)
        bias = jnp.concatenate(
            [jnp.where(is_last, bias_ref[near_slots.index(s)], 0.0) if s in near_slots else zero
             for s in range(pages)], axis=1)
    def page_score(qj, j):
        kt = jnp.concatenate([r[j * dh:(j + 1) * dh, :] for r in k_refs], axis=1)
        return jnp.dot(qj, kt.astype(BF16), preferred_element_type=F32)

    attend(page_score,
           lambda h: jnp.concatenate([r[pl.ds(h, page, stride=heads), :] for r in v_refs], axis=0),
           bias)

    @pl.when(is_last)
    def _():
        kn = kn_ref[...]
        vn = vn_ref[...]
        zk = jnp.zeros((page - lq, dh), F32)
        zv = jnp.zeros((page - lq, dv), F32)

        def new_score(qj, j):
            kj = jnp.concatenate([kn[:, j * dh:(j + 1) * dh], zk], axis=0).astype(BF16)
            return lax.dot_general(qj, kj, _NT, preferred_element_type=F32)

        attend(new_score,
               lambda h: jnp.concatenate([vn[:, h * dv:(h + 1) * dv], zv], axis=0),
               bias_ref[n_near])
        full = acc_ref[...] / l_ref[...]
        lam = lam_ref[0, 0]
        nw = nw_ref[...]
        for h in range(heads):
            o0 = full[2 * lq * h:2 * lq * h + lq]
            o1 = full[2 * lq * h + lq:2 * lq * (h + 1)]
            o_ref[:, h * dv:(h + 1) * dv] = (_rms(o0 - lam * o1, nw) * out_scale).astype(o_ref.dtype)


def _attn_decode(u, qcol, kcol, vcol, cache_k, cache_v, page_table, tbl, lam, nw,
                 batch, lq, heads, dh, dv, out_scale):
    n_pool, page = cache_k.shape[0], cache_k.shape[1]
    n_pages = page_table.shape[1]
    past_len = n_pages * page
    mh = 2 * heads
    kw, vw = mh * dh, heads * dv
    assert page % LANES == 0 and kw % LANES == 0 and vw % LANES == 0 and lq % 8 == 0
    n_buckets = tbl.shape[0]
    far = _far_start(n_buckets)
    first_near = max(0, -(-(past_len - page + 1 - far + 1) // page))
    pages = _pick(n_pages, (16, 8, 4, 2, 1))
    assert n_pages - pages <= first_near, "near pages must fall in the last grid step"
    near_slots = tuple(s for s in range(pages) if n_pages - pages + s >= first_near)
    ck = jnp.transpose(cache_k, (0, 2, 3, 1)).reshape(n_pool * mh * dh, page)
    cv = cache_v.reshape(n_pool * page * heads, dv)
    trow = jnp.repeat(tbl.T.reshape(2, heads, -1).transpose(1, 0, 2).reshape(mh, -1), lq, axis=0)
    rows = mh * lq

    def page_spec(n_rows, width, slot):
        return pl.BlockSpec((n_rows, width),
                            lambda b, g, pt: (pt[b * n_pages + g * pages + slot], 0))

    smem = pl.BlockSpec(memory_space=pltpu.SMEM)
    in_specs = ([smem,
                 pl.BlockSpec((rows, n_buckets), lambda b, g, pt: (0, 0)),
                 pl.BlockSpec((1, dv), lambda b, g, pt: (0, 0)),
                 pl.BlockSpec((lq, kw), lambda b, g, pt: (b, qcol // kw)),
                 pl.BlockSpec((lq, kw), lambda b, g, pt: (b, kcol // kw)),
                 pl.BlockSpec((lq, vw), lambda b, g, pt: (b, vcol // vw))]
                + [page_spec(mh * dh, page, s) for s in range(pages)]
                + [page_spec(page * heads, dv, s) for s in range(pages)])
    assert qcol % kw == 0 and kcol % kw == 0 and vcol % vw == 0
    kern = functools.partial(_decode_kernel, pages=pages, page=page, lq=lq, heads=heads, dh=dh,
                             dv=dv, n_buckets=n_buckets, past_len=past_len,
                             near_slots=near_slots, out_scale=out_scale)
    grid_spec = pltpu.PrefetchScalarGridSpec(
        num_scalar_prefetch=1,
        grid=(batch, n_pages // pages),
        in_specs=in_specs,
        out_specs=pl.BlockSpec((lq, vw), lambda b, g, pt: (b, 0)),
        scratch_shapes=[pltpu.VMEM((rows, LANES), F32), pltpu.VMEM((rows, LANES), F32),
                        pltpu.VMEM((rows, dv), F32),
                        pltpu.VMEM((len(near_slots) + 1, rows, page), F32)])
    return pl.pallas_call(
        kern,
        grid_spec=grid_spec,
        out_shape=jax.ShapeDtypeStruct((batch * lq, vw), BF16),
        compiler_params=_params("arbitrary", "arbitrary"),
        name="attn_decode",
    )(page_table.reshape(-1), lam, trow, nw, u, u, u, *([ck] * pages), *([cv] * pages))


def _merge_resident_kernel(*refs, n_g):
    ohg_ref, oda_ref = refs[:2]
    g_refs = refs[2:2 + 2 * n_g]
    x_ref, wbh_ref, wbd_ref, wo_ref, nw_ref, o_ref = refs[2 + 2 * n_g:]
    g_hg = jnp.concatenate([r[...] for r in g_refs[:n_g]], axis=1)
    g_da = jnp.concatenate([r[...] for r in g_refs[n_g:]], axis=1)
    y_hg = jnp.dot(ohg_ref[...], wbh_ref[...], preferred_element_type=F32)
    y_da = jnp.dot(oda_ref[...], wbd_ref[...], preferred_element_type=F32)
    mixed = (_sigmoid(g_hg) * y_hg + _sigmoid(g_da) * y_da).astype(BF16)
    z = jnp.dot(mixed, wo_ref[...], preferred_element_type=F32)
    o_ref[...] = x_ref[...] + _rms(z, nw_ref[...])


def _merge_resident(o_hg, o_da, u, gcol, x, wbh, wbd, wo, nw):
    m, d = x.shape
    w = o_hg.shape[1]
    gw = math.gcd(gcol, d)
    n_g = d // gw
    assert gw % LANES == 0
    tm = _pick(m, (256, 128, 64, 8))
    const = lambda i: (0, 0)
    g_specs = [pl.BlockSpec((tm, gw), functools.partial(lambda i, c: (i, c), c=gcol // gw + k))
               for k in range(2 * n_g)]
    return pl.pallas_call(
        functools.partial(_merge_resident_kernel, n_g=n_g),
        grid=(m // tm,),
        in_specs=[pl.BlockSpec((tm, w), lambda i: (i, 0)),
                  pl.BlockSpec((tm, w), lambda i: (i, 0))]
                 + g_specs
                 + [pl.BlockSpec((tm, d), lambda i: (i, 0)),
                    pl.BlockSpec((w, d), const, pipeline_mode=pl.Buffered(1)),
                    pl.BlockSpec((w, d), const, pipeline_mode=pl.Buffered(1)),
                    pl.BlockSpec((d, d), const, pipeline_mode=pl.Buffered(1)),
                    pl.BlockSpec((1, d), const)],
        out_specs=pl.BlockSpec((tm, d), lambda i: (i, 0)),
        out_shape=jax.ShapeDtypeStruct((m, d), F32),
        compiler_params=_params("parallel"),
        name="merge",
    )(o_hg, o_da, *([u] * (2 * n_g)), x, wbh, wbd, wo, nw)


def _merge_kernel(ohg_ref, oda_ref, ghg_ref, gda_ref, x_ref, wbh_ref, wbd_ref, wo_ref, nw_ref,
                  o_ref, *copies, emit_w):
    copies = copies if emit_w else (None, None, None)
    n = pl.program_id(1)

    @pl.when(n == 0)
    def _():
        o_ref[...] = jnp.zeros(o_ref.shape, F32)

    y_hg = jnp.dot(ohg_ref[...], _bf16_weight(wbh_ref, copies[0]), preferred_element_type=F32)
    y_da = jnp.dot(oda_ref[...], _bf16_weight(wbd_ref, copies[1]), preferred_element_type=F32)
    mixed = (_sigmoid(ghg_ref[...]) * y_hg + _sigmoid(gda_ref[...]) * y_da).astype(BF16)
    o_ref[...] += jnp.dot(mixed, _bf16_weight(wo_ref, copies[2]), preferred_element_type=F32)

    @pl.when(n == pl.num_programs(1) - 1)
    def _():
        o_ref[...] = x_ref[...] + _rms(o_ref[...], nw_ref[...])


def _merge(o_hg, o_da, u, gcol, x, wbh, wbd, wo, nw):
    m, d = x.shape
    w = o_hg.shape[1]
    emit_w = wo.dtype != BF16
    tm = _pick(m, (1024, 512, 256, 128, 64, 32, 16, 8))
    tn = _pick(math.gcd(gcol, d), (512, 256, 128))
    nn = d // tn
    gc = gcol // tn
    once = dict(pipeline_mode=pl.Buffered(1))
    out_specs = [pl.BlockSpec((tm, d), lambda i, n: (i, 0))]
    out_shape = [jax.ShapeDtypeStruct((m, d), F32)]
    if emit_w:
        assert m == tm, "the bf16 weight copies are written once per column tile"
        out_specs += [pl.BlockSpec((w, tn), lambda i, n: (0, n)),
                      pl.BlockSpec((w, tn), lambda i, n: (0, n)),
                      pl.BlockSpec((tn, d), lambda i, n: (n, 0))]
        out_shape += [jax.ShapeDtypeStruct((w, d), BF16), jax.ShapeDtypeStruct((w, d), BF16),
                      jax.ShapeDtypeStruct((d, d), BF16)]
    return pl.pallas_call(
        functools.partial(_merge_kernel, emit_w=emit_w),
        grid=(m // tm, nn),
        in_specs=[pl.BlockSpec((tm, w), lambda i, n: (i, 0), **once),
                  pl.BlockSpec((tm, w), lambda i, n: (i, 0), **once),
                  pl.BlockSpec((tm, tn), lambda i, n: (i, gc + n)),
                  pl.BlockSpec((tm, tn), lambda i, n: (i, gc + nn + n)),
                  pl.BlockSpec((tm, d), lambda i, n: (i, 0), **once),
                  pl.BlockSpec((w, tn), lambda i, n: (0, n)),
                  pl.BlockSpec((w, tn), lambda i, n: (0, n)),
                  pl.BlockSpec((tn, d), lambda i, n: (n, 0)),
                  pl.BlockSpec((1, d), lambda i, n: (0, 0))],
        out_specs=out_specs,
        out_shape=out_shape,
        compiler_params=_params("parallel", "arbitrary"),
        name="merge",
    )(o_hg, o_da, u, u, x, wbh, wbd, wo, nw)


def _ffn_kernel(x_ref, npre_ref, wg_ref, wu_ref, wd_ref, npost_ref, o_ref, *rest, emit_w):
    h_ref = rest[-1]
    copies = rest[:3] if emit_w else (None, None, None)
    j = pl.program_id(1)

    @pl.when(j == 0)
    def _():
        h_ref[...] = _rms(x_ref[...], npre_ref[...]).astype(BF16)
        o_ref[...] = jnp.zeros(o_ref.shape, F32)

    h = h_ref[...]
    gate = jnp.dot(h, _bf16_weight(wg_ref, copies[0]), preferred_element_type=F32)
    up = jnp.dot(h, _bf16_weight(wu_ref, copies[1]), preferred_element_type=F32)
    act = (gate * _sigmoid(gate) * up).astype(BF16)
    o_ref[...] += jnp.dot(act, _bf16_weight(wd_ref, copies[2]), preferred_element_type=F32)

    @pl.when(j == pl.num_programs(1) - 1)
    def _():
        o_ref[...] = x_ref[...] + _rms(o_ref[...], npost_ref[...])


def _ffn(x, npre, w_gate, w_up, w_down, npost):
    m, d = x.shape
    ff = w_down.shape[0]
    emit_w = w_down.dtype != BF16
    tm = _pick(m, (1024, 512, 256, 128, 64, 32, 16, 8))
    tf = _pick(ff, (512, 256, 128))
    nf = ff // tf
    (wg, gcol), (wu, ucol) = w_gate, w_up
    assert gcol % tf == 0 and ucol % tf == 0
    gc, uc = gcol // tf, ucol // tf
    out_specs = [pl.BlockSpec((tm, d), lambda i, j: (i, 0))]
    out_shape = [jax.ShapeDtypeStruct((m, d), F32)]
    if emit_w:
        assert m == tm, "the bf16 weight copies are written once per hidden-dim tile"
        out_specs += [pl.BlockSpec((d, tf), lambda i, j: (0, j)),
                      pl.BlockSpec((d, tf), lambda i, j: (0, j)),
                      pl.BlockSpec((tf, d), lambda i, j: (j, 0))]
        out_shape += [jax.ShapeDtypeStruct((d, ff), BF16), jax.ShapeDtypeStruct((d, ff), BF16),
                      jax.ShapeDtypeStruct((ff, d), BF16)]
    return pl.pallas_call(
        functools.partial(_ffn_kernel, emit_w=emit_w),
        grid=(m // tm, nf),
        in_specs=[pl.BlockSpec((tm, d), lambda i, j: (i, 0)),
                  pl.BlockSpec((1, d), lambda i, j: (0, 0)),
                  pl.BlockSpec((d, tf), lambda i, j: (0, gc + j)),
                  pl.BlockSpec((d, tf), lambda i, j: (0, uc + j)),
                  pl.BlockSpec((tf, d), lambda i, j: (j, 0)),
                  pl.BlockSpec((1, d), lambda i, j: (0, 0))],
        out_specs=out_specs,
        out_shape=out_shape,
        scratch_shapes=[pltpu.VMEM((tm, d), BF16)],
        compiler_params=_params("parallel", "arbitrary"),
        name="ffn",
    )(x, npre, wg, wu, w_down, npost)


def _layer(x, s0, paged, lw, wts, dims):
    batch, seqlen, d = x.shape
    wts = dict(wts)
    hg_heads, hg_dk, hg_dv, da_heads, da_dh, da_dv = dims
    hg_w, da_qk, da_w = hg_heads * hg_dk, 2 * da_heads * da_dh, da_heads * da_dv
    cols = np.cumsum([0, hg_w, hg_w, hg_heads * hg_dv, hg_heads * hg_dv, da_qk, da_qk, da_w, d, d])
    qcol, kcol, vcol, gcol = int(cols[4]), int(cols[5]), int(cols[6]), int(cols[7])
    x2 = x.reshape(batch * seqlen, d)

    if paged is None:
        assert da_qk == da_w
        u, kt, v2 = _norm_proj(x2, lw["norm_mix_pre"], wts["w_in"],
                               kv=(kcol, vcol, da_qk, seqlen, da_heads))
        o_da = _attn_prompt(u, qcol, vcol, kt, lw["tbl"], lw["lam"], lw["da_subln_w"],
                            batch, seqlen, da_heads, da_dh, da_dv, lw["out_scale"])
        k = kt.reshape(batch, 2 * da_heads, da_dh, seqlen).transpose(0, 3, 1, 2)
    else:
        u, *copy = _norm_proj(x2, lw["norm_mix_pre"], wts["w_in"])
        if copy:
            wts["w_in"], = copy
        o_da = _attn_decode(u, qcol, kcol, vcol, paged[0], paged[1], paged[2], lw["tbl"],
                            lw["lam"], lw["da_subln_w"], batch, seqlen, da_heads, da_dh, da_dv,
                            lw["out_scale"])
        k = u[:, kcol:kcol + da_qk].reshape(batch, seqlen, 2 * da_heads, da_dh)
        v2 = u[:, vcol:vcol + da_w]
    o_hg, s_new = _hgrn(u, 0, lw["lb"], lw["hg_norm_w"], s0, batch, seqlen, hg_heads, hg_dk, hg_dv)
    merge_args = (o_hg, o_da, u, gcol, x2, wts["w_branch_hg"], wts["w_branch_da"], wts["w_out"],
                  lw["norm_mix_post"])
    if wts["w_out"].dtype == BF16:
        x1 = _merge_resident(*merge_args)
    else:
        x1, wts["w_branch_hg"], wts["w_branch_da"], wts["w_out"] = _merge(*merge_args)
    y, *copy = _ffn(x1, lw["norm_ffn_pre"], wts["w_gate"], wts["w_up"], wts["w_down"],
                    lw["norm_ffn_post"])
    if copy:
        wts["w_gate"], wts["w_up"], wts["w_down"] = (copy[0], 0), (copy[1], 0), copy[2]
    v = v2.reshape(batch, seqlen, da_heads, da_dv)
    return y.reshape(batch, seqlen, d), k, v, s_new, wts


def kernel(x_prompt, x_sample, cache_k, cache_v, state_hgrn, page_table, norm_mix_pre, norm_mix_post, norm_ffn_pre, norm_ffn_post, w_in, hg_lb_logits, hg_norm_w, da_lambda_q1, da_lambda_k1, da_lambda_q2, da_lambda_k2, da_subln_w, rel_bias_table, w_branch_hg, w_branch_da, w_out, w_ffn_up, w_ffn_down):
    depth = w_in.shape[0]
    _, _, hg_heads, hg_dk, hg_dv = state_hgrn.shape
    da_heads, da_dv = cache_v.shape[3], cache_v.shape[4]
    da_dh = cache_k.shape[4]
    dims = (hg_heads, hg_dk, hg_dv, da_heads, da_dh, da_dv)
    lb_all = jnp.cumsum(jax.nn.softmax(hg_lb_logits.astype(F32), axis=0), axis=0)

    y_p, y_s = x_prompt, x_sample
    outs = [[] for _ in range(6)]
    for l in range(depth):
        lam_init = 0.8 - 0.6 * math.exp(-0.3 * l)
        lam = (jnp.exp(jnp.sum(da_lambda_q1[l] * da_lambda_k1[l]))
               - jnp.exp(jnp.sum(da_lambda_q2[l] * da_lambda_k2[l])) + lam_init)
        lw = {
            "norm_mix_pre": norm_mix_pre[l][None], "norm_mix_post": norm_mix_post[l][None],
            "norm_ffn_pre": norm_ffn_pre[l][None], "norm_ffn_post": norm_ffn_post[l][None],
            "lb": lb_all[l][None], "hg_norm_w": hg_norm_w[l][None],
            "da_subln_w": da_subln_w[l][None], "tbl": rel_bias_table.astype(F32),
            "lam": lam.reshape(1, 1).astype(F32), "out_scale": 1.0 - lam_init,
        }
        ff = w_ffn_down.shape[1]
        wts = {"w_in": w_in[l], "w_branch_hg": w_branch_hg[l], "w_branch_da": w_branch_da[l],
               "w_out": w_out[l], "w_gate": (w_ffn_up[l], 0), "w_up": (w_ffn_up[l], ff),
               "w_down": w_ffn_down[l]}
        y_s, ks, vs, ss, wts = _layer(y_s, state_hgrn[l], (cache_k[l], cache_v[l], page_table),
                                      lw, wts, dims)
        y_p, kp, vp, sp, _ = _layer(y_p, None, None, lw, wts, dims)
        for acc, val in zip(outs, (kp, vp, sp, ks, vs, ss)):
            acc.append(val)
    return (y_p, y_s) + tuple(jnp.stack(o) for o in outs)
```

```python
import functools
import math

import jax
import jax.numpy as jnp
import numpy as np
from jax import lax
from jax.experimental import pallas as pl
from jax.experimental.pallas import tpu as pltpu

F32 = jnp.float32
BF16 = jnp.bfloat16

RMS_EPS = 1e-6
NEG_INF = -1e30
HG_CHUNK = 64
REL_MAX_DISTANCE = 128
LANES = 128
VMEM_LIMIT = 62 * 1024 * 1024

_NT = (((1,), (1,)), ((), ()))
_TN = (((0,), (0,)), ((), ()))


def _params(*sem):
    return pltpu.CompilerParams(dimension_semantics=sem, vmem_limit_bytes=VMEM_LIMIT)


def _rms(x, w):
    return x * lax.rsqrt(jnp.mean(x * x, axis=-1, keepdims=True) + RMS_EPS) * w


def _sigmoid(x):
    return 1.0 / (1.0 + jnp.exp(-x))


def _pick(n, prefs):
    for p in prefs:
        if n % p == 0:
            return p
    return n


def _bf16_weight(w_ref, copy_ref):
    w = w_ref[...]
    if copy_ref is not None:
        w = w.astype(BF16)
        copy_ref[...] = w
    return w


def _norm_proj_kernel(x_ref, nw_ref, w_ref, o_ref, *rest, jk, jv, v_heads, emit_w):
    h_ref = rest[-1]
    j = pl.program_id(1)

    @pl.when(j == 0)
    def _():
        h_ref[...] = _rms(x_ref[...], nw_ref[...]).astype(BF16)

    w = _bf16_weight(w_ref, rest[-2] if emit_w else None)
    res = jnp.dot(h_ref[...], w, preferred_element_type=F32)
    o_ref[...] = res
    if jk is not None:
        kt_ref, v_ref = rest[:2]
        tm, tn = res.shape

        @pl.when(j == jk)
        def _():
            kt_ref[...] = res.T

        @pl.when(j == jv)
        def _():
            dv = tn // v_heads
            for h in range(v_heads):
                v_ref[pl.ds(h, tm, stride=v_heads), :] = res[:, h * dv:(h + 1) * dv]


def _norm_proj(x, nw, w, kv=None):
    m, d = x.shape
    n = w.shape[1]
    emit_w = w.dtype != BF16
    tm = _pick(m, (1024, 512, 256, 128, 64, 32, 16, 8))
    tn = _pick(n, (1024, 512, 256, 128))
    out_specs = [pl.BlockSpec((tm, tn), lambda i, j: (i, j))]
    out_shape = [jax.ShapeDtypeStruct((m, n), F32)]
    jk = jv = v_heads = None
    if kv is not None:
        kcol, vcol, width, seqlen, v_heads = kv
        tn = width
        assert n % tn == 0 and kcol % tn == 0 and vcol % tn == 0 and seqlen % tm == 0
        jk, jv = kcol // tn, vcol // tn
        nt = seqlen // tm
        out_specs = [pl.BlockSpec((tm, tn), lambda i, j: (i, j)),
                     pl.BlockSpec((tn, tm), lambda i, j: (i // nt, i % nt)),
                     pl.BlockSpec((tm * v_heads, tn // v_heads), lambda i, j: (i, 0))]
        out_shape += [jax.ShapeDtypeStruct((m // seqlen * tn, seqlen), F32),
                      jax.ShapeDtypeStruct((m * v_heads, tn // v_heads), F32)]
    if emit_w:
        out_specs.append(pl.BlockSpec((d, tn), lambda i, j: (0, j)))
        out_shape.append(jax.ShapeDtypeStruct((d, n), BF16))
        assert m == tm, "the bf16 weight copy is written once per column tile"
    return pl.pallas_call(
        functools.partial(_norm_proj_kernel, jk=jk, jv=jv, v_heads=v_heads, emit_w=emit_w),
        grid=(m // tm, n // tn),
        in_specs=[pl.BlockSpec((tm, d), lambda i, j: (i, 0)),
                  pl.BlockSpec((1, d), lambda i, j: (0, 0)),
                  pl.BlockSpec((d, tn), lambda i, j: (0, j))],
        out_specs=out_specs,
        out_shape=out_shape,
        scratch_shapes=[pltpu.VMEM((tm, d), BF16)],
        compiler_params=_params("parallel", "arbitrary"),
        name="norm_proj",
    )(x, nw, w)


def _hgrn_kernel(*refs, heads, dk, dv, chunk, rows, has_s0):
    if has_s0:
        q_ref, f_ref, i_ref, gate_ref, lb_ref, nw_ref, s0_ref, o_ref, sout_ref, st_ref = refs
    else:
        q_ref, f_ref, i_ref, gate_ref, lb_ref, nw_ref, o_ref, sout_ref, st_ref = refs
    t = pl.program_id(1)
    real = q_ref.shape[0]

    @pl.when(t == 0)
    def _():
        for h in range(heads):
            if has_s0:
                st_ref[h] = s0_ref[h].T
            else:
                st_ref[h] = jnp.zeros((dv, dk), F32)

    lb = lb_ref[...]
    f = lb + (1.0 - lb) * _sigmoid(f_ref[...])
    g = jnp.log(f)
    kin = 1.0 - f
    q = q_ref[...]
    v = i_ref[...]
    if rows > real:
        def pad(a):
            return jnp.concatenate([a, jnp.zeros((rows - real, a.shape[1]), F32)], axis=0)
        g, kin, q, v = pad(g), pad(kin), pad(q), pad(v)

    cs = math.gcd(rows, 2 * LANES)
    r = lax.broadcasted_iota(jnp.int32, (cs, cs), 0)
    c = lax.broadcasted_iota(jnp.int32, (cs, cs), 1)
    within = (r // chunk == c // chunk) & (c <= r)
    tri = jnp.where(within, 1.0, 0.0).astype(BF16)
    g_hi = g.astype(BF16)
    g_r1 = g - g_hi.astype(F32)
    g_mid = g_r1.astype(BF16)
    g_lo = (g_r1 - g_mid.astype(F32)).astype(BF16)
    G = jnp.concatenate(
        [jnp.dot(tri, g_hi[i:i + cs], preferred_element_type=F32)
         + jnp.dot(tri, g_mid[i:i + cs], preferred_element_type=F32)
         + jnp.dot(tri, g_lo[i:i + cs], preferred_element_type=F32)
         for i in range(0, rows, cs)], axis=0)

    qg = (q * jnp.exp(G)).astype(BF16)
    kg = (kin * jnp.exp(-G)).astype(BF16)
    vb = v.astype(BF16)
    nw = nw_ref[...]

    states = [st_ref[h] for h in range(heads)]
    for g0 in range(0, rows, cs):
        intra = []
        for h in range(heads):
            sk = slice(h * dk, (h + 1) * dk)
            a = lax.dot_general(qg[g0:g0 + cs, sk], kg[g0:g0 + cs, sk], _NT,
                                preferred_element_type=F32)
            a = jnp.where(within, a, 0.0).astype(BF16)
            intra.append(jnp.dot(a, vb[g0:g0 + cs, h * dv:(h + 1) * dv],
                                 preferred_element_type=F32))
        for ci in range(cs // chunk):
            lo = g0 + ci * chunk
            Gc = G[lo:lo + chunk]
            Gl = Gc[chunk - 1:chunk]
            kdec = (kin[lo:lo + chunk] * jnp.exp(Gl - Gc)).astype(BF16)
            decay = jnp.exp(Gl)
            n_out = min(chunk, real - lo)
            for h in range(heads):
                sk = slice(h * dk, (h + 1) * dk)
                sv = slice(h * dv, (h + 1) * dv)
                st = states[h]
                o = intra[h][ci * chunk:(ci + 1) * chunk] + lax.dot_general(
                    qg[lo:lo + chunk, sk], st.astype(BF16), _NT, preferred_element_type=F32)
                states[h] = st * decay[:, sk] + lax.dot_general(
                    vb[lo:lo + chunk, sv], kdec[:, sk], _TN, preferred_element_type=F32)
                if n_out > 0:
                    gt = gate_ref[lo:lo + n_out, sv]
                    on = _rms(o[:n_out], nw) * (gt * _sigmoid(gt))
                    o_ref[lo:lo + n_out, sv] = on.astype(o_ref.dtype)
    for h in range(heads):
        st_ref[h] = states[h]

    @pl.when(t == pl.num_programs(1) - 1)
    def _():
        for h in range(heads):
            sout_ref[h] = st_ref[h].T


def _hgrn(u, col0, lb, nw, s0, batch, seqlen, heads, dk, dv):
    width = heads * dk
    assert dk == dv and col0 % width == 0
    cb = col0 // width
    chunk = min(HG_CHUNK, seqlen)
    if seqlen >= LANES:
        tb = _pick(seqlen, (512, 256, 128))
        rows = tb
        assert tb % chunk == 0
    else:
        tb = seqlen
        rows = LANES
        chunk = LANES
    nt = seqlen // tb
    has_s0 = s0 is not None

    def col(k):
        return pl.BlockSpec((tb, width), lambda b, t: (b * nt + t, cb + k))

    in_specs = [col(0), col(1), col(2), col(3),
                pl.BlockSpec((1, width), lambda b, t: (0, 0)),
                pl.BlockSpec((1, dv), lambda b, t: (0, 0))]
    args = [u, u, u, u, lb, nw]
    if has_s0:
        in_specs.append(pl.BlockSpec((None, heads, dk, dv), lambda b, t: (b, 0, 0, 0)))
        args.append(s0)
    kern = functools.partial(_hgrn_kernel, heads=heads, dk=dk, dv=dv, chunk=chunk, rows=rows,
                             has_s0=has_s0)
    return pl.pallas_call(
        kern,
        grid=(batch, nt),
        in_specs=in_specs,
        out_specs=[pl.BlockSpec((tb, heads * dv), lambda b, t: (b * nt + t, 0)),
                   pl.BlockSpec((None, heads, dk, dv), lambda b, t: (b, 0, 0, 0))],
        out_shape=[jax.ShapeDtypeStruct((batch * seqlen, heads * dv), BF16),
                   jax.ShapeDtypeStruct((batch, heads, dk, dv), F32)],
        scratch_shapes=[pltpu.VMEM((heads, dv, dk), F32)],
        compiler_params=_params("parallel", "arbitrary"),
        name="hgrn",
    )(*args)


def _rel_bias(dist, value, n_buckets):
    n = jnp.maximum(dist, 0)
    max_exact = n_buckets // 2
    nf = jnp.maximum(n, 1).astype(F32)
    x = (jnp.log(nf / max_exact) / math.log(REL_MAX_DISTANCE / max_exact)
         * (n_buckets - max_exact))
    bias = jnp.zeros(dist.shape, F32)
    for k in range(n_buckets - 2, max_exact - 1, -1):
        bias = jnp.where(x < k - max_exact + 1, value(k), bias)
    for k in range(max_exact - 1, -1, -1):
        bias = jnp.where(n <= k, value(k), bias)
    return bias


def _far_start(n_buckets):
    n = np.arange(1, 8 * REL_MAX_DISTANCE, dtype=np.int64)
    max_exact = n_buckets // 2
    nf = n.astype(np.float32)
    large = max_exact + (np.log(nf / np.float32(max_exact)) / np.float32(math.log(REL_MAX_DISTANCE / max_exact))
                         * np.float32(n_buckets - max_exact)).astype(np.int32)
    b = np.where(n < max_exact, n, np.minimum(large, n_buckets - 1))
    below = n[b < n_buckets - 1]
    return int(below.max()) + 2


def _attn_kernel(tbl_ref, lam_ref, nw_ref, q1_ref, q2_ref, k1_ref, k2_ref, v_ref, o_ref,
                 bias_ref, m_ref, acc_ref, *,
                 t, dh, dv, heads, n_buckets, out_scale):
    hp = pl.program_id(0)
    b = pl.program_id(1)

    @pl.when(b == 0)
    def _():
        row = lax.broadcasted_iota(jnp.int32, (t, t), 0)
        col = lax.broadcasted_iota(jnp.int32, (t, t), 1)
        for e in range(2):
            for mp in range(2):
                hcol = mp * heads + 2 * hp + e
                far = tbl_ref[n_buckets - 1, hcol]
                value = lambda k: tbl_ref[k, hcol] - far
                bd = _rel_bias(row - col, value, n_buckets)
                bn = _rel_bias(row - col + t, value, n_buckets)
                rs = slice((2 * e + mp) * t, (2 * e + mp + 1) * t)
                bias_ref[rs, :t] = bn
                bias_ref[rs, t:] = jnp.where(col <= row, bd, NEG_INF)

    lane = lax.broadcasted_iota(jnp.int32, (t, 2 * dh), 1)
    scale = dh ** -0.5
    zero = jnp.zeros((t, 2 * dh), F32)
    lam = lam_ref[0, 0]
    nw = nw_ref[...]

    def q_block(qb, carry):
        q0 = pl.multiple_of(qb * t, t)
        q_maps = [q1_ref[pl.ds(q0, t), :] * scale, q2_ref[pl.ds(q0, t), :] * scale]
        blocks = []
        for e in range(2):
            in_e = (lane >= e * dh) & (lane < (e + 1) * dh)
            for mp in range(2):
                qe = jnp.where(in_e, q_maps[mp], 0.0)
                blocks.append(jnp.concatenate([qe, zero] if mp == 0 else [zero, qe], axis=1))
        q_all = jnp.concatenate(blocks, axis=0).astype(BF16)

        m_ref[...] = jnp.full(m_ref.shape, NEG_INF, F32)
        acc_ref[...] = jnp.zeros(acc_ref.shape, F32)

        def step(j, n, bias=None):
            start = pl.multiple_of(j * t, t)
            kb = jnp.concatenate([k1_ref[:, pl.ds(start, n * t)], k2_ref[:, pl.ds(start, n * t)]],
                                 axis=0).astype(BF16)
            s = jnp.dot(q_all, kb, preferred_element_type=F32)
            if bias is not None:
                s = s + bias
            m_prev = m_ref[...]
            m_new = jnp.maximum(m_prev, jnp.max(s, axis=1, keepdims=True))
            pb = jnp.exp(s - jnp.tile(m_new, (1, n * t // LANES))).astype(BF16)
            alpha = jnp.tile(jnp.exp(m_prev - m_new), (1, 2))
            m_ref[...] = m_new
            ones = jnp.ones((n * t, dv), BF16)
            for e in range(2):
                ve = v_ref[pl.ds(start, n * t), e * dv:(e + 1) * dv].astype(BF16)
                rs = slice(2 * e * t, 2 * (e + 1) * t)
                acc_ref[rs, :] = acc_ref[rs, :] * alpha[rs] + jnp.dot(
                    pb[rs], jnp.concatenate([ve, ones], axis=1), preferred_element_type=F32)

        n_far = jnp.maximum(qb - 1, 0)

        def far_body(i, c):
            step(2 * i, 2)
            return c

        lax.fori_loop(0, n_far // 2, far_body, 0)

        @pl.when(n_far % 2 == 1)
        def _():
            step(n_far - 1, 1)

        @pl.when(qb > 0)
        def _():
            step(qb - 1, 2, bias_ref[...])

        @pl.when(qb == 0)
        def _():
            step(0, 1, bias_ref[:, t:])

        o = acc_ref[:, :dv] / acc_ref[:, dv:]
        for e in range(2):
            oe = o[2 * e * t:(2 * e + 1) * t] - lam * o[(2 * e + 1) * t:(2 * e + 2) * t]
            o_ref[pl.ds(q0, t), e * dv:(e + 1) * dv] = (
                _rms(oe, nw) * out_scale).astype(o_ref.dtype)
        return carry

    lax.fori_loop(0, q1_ref.shape[0] // t, q_block, 0)


def _attn_prompt(u, qcol, vcol, kt, tbl, lam, nw, batch, seqlen, heads, dh, dv, out_scale):
    assert dv == LANES and 2 * dh == LANES and heads % 2 == 0
    n_buckets = tbl.shape[0]
    t = _pick(seqlen, (256, 128))
    assert seqlen % t == 0 and t % LANES == 0 and t + 1 >= _far_start(n_buckets)
    nhp = heads // 2
    qc, vc = qcol // LANES, vcol // (2 * dv)
    smem = pl.BlockSpec(memory_space=pltpu.SMEM)
    kern = functools.partial(_attn_kernel, t=t, dh=dh, dv=dv, heads=heads, n_buckets=n_buckets,
                             out_scale=out_scale)
    return pl.pallas_call(
        kern,
        grid=(nhp, batch),
        in_specs=[smem, smem,
                  pl.BlockSpec((1, dv), lambda h, b: (0, 0)),
                  pl.BlockSpec((seqlen, LANES), lambda h, b: (b, qc + h)),
                  pl.BlockSpec((seqlen, LANES), lambda h, b: (b, qc + nhp + h)),
                  pl.BlockSpec((2 * dh, seqlen), lambda h, b: (b * 2 * nhp + h, 0)),
                  pl.BlockSpec((2 * dh, seqlen), lambda h, b: (b * 2 * nhp + nhp + h, 0)),
                  pl.BlockSpec((seqlen, 2 * dv), lambda h, b: (b, vc + h))],
        out_specs=pl.BlockSpec((seqlen, 2 * dv), lambda h, b: (b, h)),
        out_shape=jax.ShapeDtypeStruct((batch * seqlen, heads * dv), BF16),
        scratch_shapes=[pltpu.VMEM((4 * t, 2 * t), F32),
                        pltpu.VMEM((4 * t, LANES), F32), pltpu.VMEM((4 * t, 2 * dv), F32)],
        compiler_params=_params("arbitrary", "arbitrary"),
        name="attn_prompt",
    )(tbl, lam, nw, u, u, kt, kt, u)


def _decode_kernel(pt_ref, lam_ref, trow_ref, nw_ref, q_ref, kn_ref, vn_ref, *rest,
                   pages, page, lq, heads, dh, dv, n_buckets, past_len, near_slots, out_scale):
    k_refs = rest[:pages]
    v_refs = rest[pages:2 * pages]
    o_ref, m_ref, l_ref, acc_ref, bias_ref = rest[2 * pages:]
    b = pl.program_id(0)
    g = pl.program_id(1)
    ng = pl.num_programs(1)
    mh = 2 * heads
    rows = mh * lq
    n_near = len(near_slots)
    order = [m * heads + h for h in range(heads) for m in range(2)]

    def near_bias(key0, n_valid):
        r = lax.broadcasted_iota(jnp.int32, (rows, page), 0)
        col = lax.broadcasted_iota(jnp.int32, (rows, page), 1)
        dist = past_len + r % lq - (key0 + col)
        far = trow_ref[:, n_buckets - 1:n_buckets]
        bias = _rel_bias(dist, lambda k: trow_ref[:, k:k + 1] - far, n_buckets)
        return jnp.where((dist >= 0) & (col < n_valid), bias, NEG_INF)

    @pl.when((b == 0) & (g == 0))
    def _():
        for i, slot in enumerate(near_slots):
            bias_ref[i] = near_bias(past_len - (pages - slot) * page, page)
        bias_ref[n_near] = near_bias(past_len, lq)

    @pl.when(g == 0)
    def _():
        m_ref[...] = jnp.full(m_ref.shape, NEG_INF, F32)
        l_ref[...] = jnp.zeros(l_ref.shape, F32)
        acc_ref[...] = jnp.zeros(acc_ref.shape, F32)

    q = q_ref[...] * (dh ** -0.5)
    q_parts = [q[:, j * dh:(j + 1) * dh].astype(BF16) for j in order]

    def attend(score, value_rows, bias):
        s = jnp.concatenate([score(qj, j) for qj, j in zip(q_parts, order)], axis=0)
        if bias is not None:
            s = s + bias
        m_prev = m_ref[...]
        m_new = jnp.maximum(m_prev, jnp.max(s, axis=1, keepdims=True))
        p = jnp.exp(s - jnp.tile(m_new, (1, s.shape[1] // LANES)))
        alpha = jnp.exp(m_prev - m_new)
        l_ref[...] = alpha * l_ref[...] + jnp.sum(p, axis=1, keepdims=True)
        m_ref[...] = m_new
        pv = jnp.concatenate(
            [jnp.dot(p[2 * lq * h:2 * lq * (h + 1)].astype(BF16), value_rows(h).astype(BF16),
                     preferred_element_type=F32) for h in range(heads)], axis=0)
        acc_ref[...] = acc_ref[...] * alpha + pv

    is_last = g == ng - 1
    bias = None
    if near_slots:
        zero = jnp.zeros((rows, page), F32)
        bias = jnp.concatenate(
            [jnp.where(is_last, bias_ref[near_slots.index(s)], 0.0) if s in near_slots else zero
             for s in range(pages)], axis=1)
    def page_score(qj, j):
        kt = jnp.concatenate([r[j * dh:(j + 1) * dh, :] for r in k_refs], axis=1)
        return jnp.dot(qj, kt.astype(BF16), preferred_element_type=F32)

    attend(page_score,
           lambda h: jnp.concatenate([r[pl.ds(h, page, stride=heads), :] for r in v_refs], axis=0),
           bias)

    @pl.when(is_last)
    def _():
        kn = kn_ref[...]
        vn = vn_ref[...]
        zk = jnp.zeros((page - lq, dh), F32)
        zv = jnp.zeros((page - lq, dv), F32)

        def new_score(qj, j):
            kj = jnp.concatenate([kn[:, j * dh:(j + 1) * dh], zk], axis=0).astype(BF16)
            return lax.dot_general(qj, kj, _NT, preferred_element_type=F32)

        attend(new_score,
               lambda h: jnp.concatenate([vn[:, h * dv:(h + 1) * dv], zv], axis=0),
               bias_ref[n_near])
        full = acc_ref[...] / l_ref[...]
        lam = lam_ref[0, 0]
        nw = nw_ref[...]
        for h in range(heads):
            o0 = full[2 * lq * h:2 * lq * h + lq]
            o1 = full[2 * lq * h + lq:2 * lq * (h + 1)]
            o_ref[:, h * dv:(h + 1) * dv] = (_rms(o0 - lam * o1, nw) * out_scale).astype(o_ref.dtype)


def _attn_decode(u, qcol, kcol, vcol, cache_k, cache_v, page_table, tbl, lam, nw,
                 batch, lq, heads, dh, dv, out_scale):
    n_pool, page = cache_k.shape[0], cache_k.shape[1]
    n_pages = page_table.shape[1]
    past_len = n_pages * page
    mh = 2 * heads
    kw, vw = mh * dh, heads * dv
    assert page % LANES == 0 and kw % LANES == 0 and vw % LANES == 0 and lq % 8 == 0
    n_buckets = tbl.shape[0]
    far = _far_start(n_buckets)
    first_near = max(0, -(-(past_len - page + 1 - far + 1) // page))
    pages = _pick(n_pages, (16, 8, 4, 2, 1))
    assert n_pages - pages <= first_near, "near pages must fall in the last grid step"
    near_slots = tuple(s for s in range(pages) if n_pages - pages + s >= first_near)
    ck = jnp.transpose(cache_k, (0, 2, 3, 1)).reshape(n_pool * mh * dh, page)
    cv = cache_v.reshape(n_pool * page * heads, dv)
    trow = jnp.repeat(tbl.T.reshape(2, heads, -1).transpose(1, 0, 2).reshape(mh, -1), lq, axis=0)
    rows = mh * lq

    def page_spec(n_rows, width, slot):
        return pl.BlockSpec((n_rows, width),
                            lambda b, g, pt: (pt[b * n_pages + g * pages + slot], 0))

    smem = pl.BlockSpec(memory_space=pltpu.SMEM)
    in_specs = ([smem,
                 pl.BlockSpec((rows, n_buckets), lambda b, g, pt: (0, 0)),
                 pl.BlockSpec((1, dv), lambda b, g, pt: (0, 0)),
                 pl.BlockSpec((lq, kw), lambda b, g, pt: (b, qcol // kw)),
                 pl.BlockSpec((lq, kw), lambda b, g, pt: (b, kcol // kw)),
                 pl.BlockSpec((lq, vw), lambda b, g, pt: (b, vcol // vw))]
                + [page_spec(mh * dh, page, s) for s in range(pages)]
                + [page_spec(page * heads, dv, s) for s in range(pages)])
    assert qcol % kw == 0 and kcol % kw == 0 and vcol % vw == 0
    kern = functools.partial(_decode_kernel, pages=pages, page=page, lq=lq, heads=heads, dh=dh,
                             dv=dv, n_buckets=n_buckets, past_len=past_len,
                             near_slots=near_slots, out_scale=out_scale)
    grid_spec = pltpu.PrefetchScalarGridSpec(
        num_scalar_prefetch=1,
        grid=(batch, n_pages // pages),
        in_specs=in_specs,
        out_specs=pl.BlockSpec((lq, vw), lambda b, g, pt: (b, 0)),
        scratch_shapes=[pltpu.VMEM((rows, LANES), F32), pltpu.VMEM((rows, LANES), F32),
                        pltpu.VMEM((rows, dv), F32),
                        pltpu.VMEM((len(near_slots) + 1, rows, page), F32)])
    return pl.pallas_call(
        kern,
        grid_spec=grid_spec,
        out_shape=jax.ShapeDtypeStruct((batch * lq, vw), BF16),
        compiler_params=_params("arbitrary", "arbitrary"),
        name="attn_decode",
    )(page_table.reshape(-1), lam, trow, nw, u, u, u, *([ck] * pages), *([cv] * pages))


def _merge_resident_kernel(*refs, n_g):
    ohg_ref, oda_ref = refs[:2]
    g_refs = refs[2:2 + 2 * n_g]
    x_ref, wbh_ref, wbd_ref, wo_ref, nw_ref, o_ref = refs[2 + 2 * n_g:]
    g_hg = jnp.concatenate([r[...] for r in g_refs[:n_g]], axis=1)
    g_da = jnp.concatenate([r[...] for r in g_refs[n_g:]], axis=1)
    y_hg = jnp.dot(ohg_ref[...], wbh_ref[...], preferred_element_type=F32)
    y_da = jnp.dot(oda_ref[...], wbd_ref[...], preferred_element_type=F32)
    mixed = (_sigmoid(g_hg) * y_hg + _sigmoid(g_da) * y_da).astype(BF16)
    z = jnp.dot(mixed, wo_ref[...], preferred_element_type=F32)
    o_ref[...] = x_ref[...] + _rms(z, nw_ref[...])


def _merge_resident(o_hg, o_da, u, gcol, x, wbh, wbd, wo, nw):
    m, d = x.shape
    w = o_hg.shape[1]
    gw = math.gcd(gcol, d)
    n_g = d // gw
    assert gw % LANES == 0
    tm = _pick(m, (256, 128, 64, 8))
    const = lambda i: (0, 0)
    g_specs = [pl.BlockSpec((tm, gw), functools.partial(lambda i, c: (i, c), c=gcol // gw + k))
               for k in range(2 * n_g)]
    return pl.pallas_call(
        functools.partial(_merge_resident_kernel, n_g=n_g),
        grid=(m // tm,),
        in_specs=[pl.BlockSpec((tm, w), lambda i: (i, 0)),
                  pl.BlockSpec((tm, w), lambda i: (i, 0))]
                 + g_specs
                 + [pl.BlockSpec((tm, d), lambda i: (i, 0)),
                    pl.BlockSpec((w, d), const, pipeline_mode=pl.Buffered(1)),
                    pl.BlockSpec((w, d), const, pipeline_mode=pl.Buffered(1)),
                    pl.BlockSpec((d, d), const, pipeline_mode=pl.Buffered(1)),
                    pl.BlockSpec((1, d), const)],
        out_specs=pl.BlockSpec((tm, d), lambda i: (i, 0)),
        out_shape=jax.ShapeDtypeStruct((m, d), F32),
        compiler_params=_params("parallel"),
        name="merge",
    )(o_hg, o_da, *([u] * (2 * n_g)), x, wbh, wbd, wo, nw)


def _merge_kernel(ohg_ref, oda_ref, ghg_ref, gda_ref, x_ref, wbh_ref, wbd_ref, wo_ref, nw_ref,
                  o_ref, *copies, emit_w):
    copies = copies if emit_w else (None, None, None)
    n = pl.program_id(1)

    @pl.when(n == 0)
    def _():
        o_ref[...] = jnp.zeros(o_ref.shape, F32)

    y_hg = jnp.dot(ohg_ref[...], _bf16_weight(wbh_ref, copies[0]), preferred_element_type=F32)
    y_da = jnp.dot(oda_ref[...], _bf16_weight(wbd_ref, copies[1]), preferred_element_type=F32)
    mixed = (_sigmoid(ghg_ref[...]) * y_hg + _sigmoid(gda_ref[...]) * y_da).astype(BF16)
    o_ref[...] += jnp.dot(mixed, _bf16_weight(wo_ref, copies[2]), preferred_element_type=F32)

    @pl.when(n == pl.num_programs(1) - 1)
    def _():
        o_ref[...] = x_ref[...] + _rms(o_ref[...], nw_ref[...])


def _merge(o_hg, o_da, u, gcol, x, wbh, wbd, wo, nw):
    m, d = x.shape
    w = o_hg.shape[1]
    emit_w = wo.dtype != BF16
    tm = _pick(m, (1024, 512, 256, 128, 64, 32, 16, 8))
    tn = _pick(math.gcd(gcol, d), (512, 256, 128))
    nn = d // tn
    gc = gcol // tn
    once = dict(pipeline_mode=pl.Buffered(1))
    out_specs = [pl.BlockSpec((tm, d), lambda i, n: (i, 0))]
    out_shape = [jax.ShapeDtypeStruct((m, d), F32)]
    if emit_w:
        assert m == tm, "the bf16 weight copies are written once per column tile"
        out_specs += [pl.BlockSpec((w, tn), lambda i, n: (0, n)),
                      pl.BlockSpec((w, tn), lambda i, n: (0, n)),
                      pl.BlockSpec((tn, d), lambda i, n: (n, 0))]
        out_shape += [jax.ShapeDtypeStruct((w, d), BF16), jax.ShapeDtypeStruct((w, d), BF16),
                      jax.ShapeDtypeStruct((d, d), BF16)]
    return pl.pallas_call(
        functools.partial(_merge_kernel, emit_w=emit_w),
        grid=(m // tm, nn),
        in_specs=[pl.BlockSpec((tm, w), lambda i, n: (i, 0), **once),
                  pl.BlockSpec((tm, w), lambda i, n: (i, 0), **once),
                  pl.BlockSpec((tm, tn), lambda i, n: (i, gc + n)),
                  pl.BlockSpec((tm, tn), lambda i, n: (i, gc + nn + n)),
                  pl.BlockSpec((tm, d), lambda i, n: (i, 0), **once),
                  pl.BlockSpec((w, tn), lambda i, n: (0, n)),
                  pl.BlockSpec((w, tn), lambda i, n: (0, n)),
                  pl.BlockSpec((tn, d), lambda i, n: (n, 0)),
                  pl.BlockSpec((1, d), lambda i, n: (0, 0))],
        out_specs=out_specs,
        out_shape=out_shape,
        compiler_params=_params("parallel", "arbitrary"),
        name="merge",
    )(o_hg, o_da, u, u, x, wbh, wbd, wo, nw)


def _ffn_kernel(x_ref, npre_ref, wg_ref, wu_ref, wd_ref, npost_ref, o_ref, *rest, emit_w):
    h_ref = rest[-1]
    copies = rest[:3] if emit_w else (None, None, None)
    j = pl.program_id(1)

    @pl.when(j == 0)
    def _():
        h_ref[...] = _rms(x_ref[...], npre_ref[...]).astype(BF16)
        o_ref[...] = jnp.zeros(o_ref.shape, F32)

    h = h_ref[...]
    gate = jnp.dot(h, _bf16_weight(wg_ref, copies[0]), preferred_element_type=F32)
    up = jnp.dot(h, _bf16_weight(wu_ref, copies[1]), preferred_element_type=F32)
    act = (gate * _sigmoid(gate) * up).astype(BF16)
    o_ref[...] += jnp.dot(act, _bf16_weight(wd_ref, copies[2]), preferred_element_type=F32)

    @pl.when(j == pl.num_programs(1) - 1)
    def _():
        o_ref[...] = x_ref[...] + _rms(o_ref[...], npost_ref[...])


def _ffn(x, npre, w_gate, w_up, w_down, npost):
    m, d = x.shape
    ff = w_down.shape[0]
    emit_w = w_down.dtype != BF16
    tm = _pick(m, (1024, 512, 256, 128, 64, 32, 16, 8))
    tf = _pick(ff, (512, 256, 128))
    nf = ff // tf
    (wg, gcol), (wu, ucol) = w_gate, w_up
    assert gcol % tf == 0 and ucol % tf == 0
    gc, uc = gcol // tf, ucol // tf
    out_specs = [pl.BlockSpec((tm, d), lambda i, j: (i, 0))]
    out_shape = [jax.ShapeDtypeStruct((m, d), F32)]
    if emit_w:
        assert m == tm, "the bf16 weight copies are written once per hidden-dim tile"
        out_specs += [pl.BlockSpec((d, tf), lambda i, j: (0, j)),
                      pl.BlockSpec((d, tf), lambda i, j: (0, j)),
                      pl.BlockSpec((tf, d), lambda i, j: (j, 0))]
        out_shape += [jax.ShapeDtypeStruct((d, ff), BF16), jax.ShapeDtypeStruct((d, ff), BF16),
                      jax.ShapeDtypeStruct((ff, d), BF16)]
    return pl.pallas_call(
        functools.partial(_ffn_kernel, emit_w=emit_w),
        grid=(m // tm, nf),
        in_specs=[pl.BlockSpec((tm, d), lambda i, j: (i, 0)),
                  pl.BlockSpec((1, d), lambda i, j: (0, 0)),
                  pl.BlockSpec((d, tf), lambda i, j: (0, gc + j)),
                  pl.BlockSpec((d, tf), lambda i, j: (0, uc + j)),
                  pl.BlockSpec((tf, d), lambda i, j: (j, 0)),
                  pl.BlockSpec((1, d), lambda i, j: (0, 0))],
        out_specs=out_specs,
        out_shape=out_shape,
        scratch_shapes=[pltpu.VMEM((tm, d), BF16)],
        compiler_params=_params("parallel", "arbitrary"),
        name="ffn",
    )(x, npre, wg, wu, w_down, npost)


def _layer(x, s0, paged, lw, wts, dims):
    batch, seqlen, d = x.shape
    wts = dict(wts)
    hg_heads, hg_dk, hg_dv, da_heads, da_dh, da_dv = dims
    hg_w, da_qk, da_w = hg_heads * hg_dk, 2 * da_heads * da_dh, da_heads * da_dv
    cols = np.cumsum([0, hg_w, hg_w, hg_heads * hg_dv, hg_heads * hg_dv, da_qk, da_qk, da_w, d, d])
    qcol, kcol, vcol, gcol = int(cols[4]), int(cols[5]), int(cols[6]), int(cols[7])
    x2 = x.reshape(batch * seqlen, d)

    if paged is None:
        assert da_qk == da_w
        u, kt, v2 = _norm_proj(x2, lw["norm_mix_pre"], wts["w_in"],
                               kv=(kcol, vcol, da_qk, seqlen, da_heads))
        o_da = _attn_prompt(u, qcol, vcol, kt, lw["tbl"], lw["lam"], lw["da_subln_w"],
                            batch, seqlen, da_heads, da_dh, da_dv, lw["out_scale"])
        k = kt.reshape(batch, 2 * da_heads, da_dh, seqlen).transpose(0, 3, 1, 2)
    else:
        u, *copy = _norm_proj(x2, lw["norm_mix_pre"], wts["w_in"])
        if copy:
            wts["w_in"], = copy
        o_da = _attn_decode(u, qcol, kcol, vcol, paged[0], paged[1], paged[2], lw["tbl"],
                            lw["lam"], lw["da_subln_w"], batch, seqlen, da_heads, da_dh, da_dv,
                            lw["out_scale"])
        k = u[:, kcol:kcol + da_qk].reshape(batch, seqlen, 2 * da_heads, da_dh)
        v2 = u[:, vcol:vcol + da_w]
    o_hg, s_new = _hgrn(u, 0, lw["lb"], lw["hg_norm_w"], s0, batch, seqlen, hg_heads, hg_dk, hg_dv)
    merge_args = (o_hg, o_da, u, gcol, x2, wts["w_branch_hg"], wts["w_branch_da"], wts["w_out"],
                  lw["norm_mix_post"])
    if wts["w_out"].dtype == BF16:
        x1 = _merge_resident(*merge_args)
    else:
        x1, wts["w_branch_hg"], wts["w_branch_da"], wts["w_out"] = _merge(*merge_args)
    y, *copy = _ffn(x1, lw["norm_ffn_pre"], wts["w_gate"], wts["w_up"], wts["w_down"],
                    lw["norm_ffn_post"])
    if copy:
        wts["w_gate"], wts["w_up"], wts["w_down"] = (copy[0], 0), (copy[1], 0), copy[2]
    v = v2.reshape(batch, seqlen, da_heads, da_dv)
    return y.reshape(batch, seqlen, d), k, v, s_new, wts


def kernel(x_prompt, x_sample, cache_k, cache_v, state_hgrn, page_table, norm_mix_pre, norm_mix_post, norm_ffn_pre, norm_ffn_post, w_in, hg_lb_logits, hg_norm_w, da_lambda_q1, da_lambda_k1, da_lambda_q2, da_lambda_k2, da_subln_w, rel_bias_table, w_branch_hg, w_branch_da, w_out, w_ffn_up, w_ffn_down):
    depth = w_in.shape[0]
    _, _, hg_heads, hg_dk, hg_dv = state_hgrn.shape
    da_heads, da_dv = cache_v.shape[3], cache_v.shape[4]
    da_dh = cache_k.shape[4]
    dims = (hg_heads, hg_dk, hg_dv, da_heads, da_dh, da_dv)
    lb_all = jnp.cumsum(jax.nn.softmax(hg_lb_logits.astype(F32), axis=0), axis=0)

    y_p, y_s = x_prompt, x_sample
    outs = [[] for _ in range(6)]
    for l in range(depth):
        lam_init = 0.8 - 0.6 * math.exp(-0.3 * l)
        lam = (jnp.exp(jnp.sum(da_lambda_q1[l] * da_lambda_k1[l]))
               - jnp.exp(jnp.sum(da_lambda_q2[l] * da_lambda_k2[l])) + lam_init)
        lw = {
            "norm_mix_pre": norm_mix_pre[l][None], "norm_mix_post": norm_mix_post[l][None],
            "norm_ffn_pre": norm_ffn_pre[l][None], "norm_ffn_post": norm_ffn_post[l][None],
            "lb": lb_all[l][None], "hg_norm_w": hg_norm_w[l][None],
            "da_subln_w": da_subln_w[l][None], "tbl": rel_bias_table.astype(F32),
            "lam": lam.reshape(1, 1).astype(F32), "out_scale": 1.0 - lam_init,
        }
        ff = w_ffn_down.shape[1]
        wts = {"w_in": w_in[l], "w_branch_hg": w_branch_hg[l], "w_branch_da": w_branch_da[l],
               "w_out": w_out[l], "w_gate": (w_ffn_up[l], 0), "w_up": (w_ffn_up[l], ff),
               "w_down": w_ffn_down[l]}
        y_s, ks, vs, ss, wts = _layer(y_s, state_hgrn[l], (cache_k[l], cache_v[l], page_table),
                                      lw, wts, dims)
        y_p, kp, vp, sp, _ = _layer(y_p, None, None, lw, wts, dims)
        for acc, val in zip(outs, (kp, vp, sp, ks, vs, ss)):
            acc.append(val)
    return (y_p, y_s) + tuple(jnp.stack(o) for o in outs)
```

```python
import functools
import math

import jax
import jax.numpy as jnp
import numpy as np
from jax import lax
from jax.experimental import pallas as pl
from jax.experimental.pallas import tpu as pltpu

F32 = jnp.float32
BF16 = jnp.bfloat16

RMS_EPS = 1e-6
NEG_INF = -1e30
HG_CHUNK = 64
REL_MAX_DISTANCE = 128
LANES = 128
VMEM_LIMIT = 62 * 1024 * 1024

_NT = (((1,), (1,)), ((), ()))
_TN = (((0,), (0,)), ((), ()))


def _params(*sem):
    return pltpu.CompilerParams(dimension_semantics=sem, vmem_limit_bytes=VMEM_LIMIT)


def _rms(x, w):
    return x * lax.rsqrt(jnp.mean(x * x, axis=-1, keepdims=True) + RMS_EPS) * w


def _sigmoid(x):
    return 1.0 / (1.0 + jnp.exp(-x))


def _pick(n, prefs):
    for p in prefs:
        if n % p == 0:
            return p
    return n


def _bf16_weight(w_ref, copy_ref):
    w = w_ref[...]
    if copy_ref is not None:
        w = w.astype(BF16)
        copy_ref[...] = w
    return w


def _norm_proj_kernel(x_ref, nw_ref, w_ref, o_ref, *rest, jk, jv, v_heads, emit_w):
    h_ref = rest[-1]
    j = pl.program_id(1)

    @pl.when(j == 0)
    def _():
        h_ref[...] = _rms(x_ref[...], nw_ref[...]).astype(BF16)

    w = _bf16_weight(w_ref, rest[-2] if emit_w else None)
    res = jnp.dot(h_ref[...], w, preferred_element_type=F32)
    o_ref[...] = res
    if jk is not None:
        kt_ref, v_ref = rest[:2]
        tm, tn = res.shape

        @pl.when(j == jk)
        def _():
            kt_ref[...] = res.T

        @pl.when(j == jv)
        def _():
            dv = tn // v_heads
            for h in range(v_heads):
                v_ref[pl.ds(h, tm, stride=v_heads), :] = res[:, h * dv:(h + 1) * dv]


def _norm_proj(x, nw, w, kv=None):
    m, d = x.shape
    n = w.shape[1]
    emit_w = w.dtype != BF16
    tm = _pick(m, (1024, 512, 256, 128, 64, 32, 16, 8))
    tn = _pick(n, (1024, 512, 256, 128))
    out_specs = [pl.BlockSpec((tm, tn), lambda i, j: (i, j))]
    out_shape = [jax.ShapeDtypeStruct((m, n), F32)]
    jk = jv = v_heads = None
    if kv is not None:
        kcol, vcol, width, seqlen, v_heads = kv
        tn = width
        assert n % tn == 0 and kcol % tn == 0 and vcol % tn == 0 and seqlen % tm == 0
        jk, jv = kcol // tn, vcol // tn
        nt = seqlen // tm
        out_specs = [pl.BlockSpec((tm, tn), lambda i, j: (i, j)),
                     pl.BlockSpec((tn, tm), lambda i, j: (i // nt, i % nt)),
                     pl.BlockSpec((tm * v_heads, tn // v_heads), lambda i, j: (i, 0))]
        out_shape += [jax.ShapeDtypeStruct((m // seqlen * tn, seqlen), F32),
                      jax.ShapeDtypeStruct((m * v_heads, tn // v_heads), F32)]
    if emit_w:
        out_specs.append(pl.BlockSpec((d, tn), lambda i, j: (0, j)))
        out_shape.append(jax.ShapeDtypeStruct((d, n), BF16))
        assert m == tm, "the bf16 weight copy is written once per column tile"
    return pl.pallas_call(
        functools.partial(_norm_proj_kernel, jk=jk, jv=jv, v_heads=v_heads, emit_w=emit_w),
        grid=(m // tm, n // tn),
        in_specs=[pl.BlockSpec((tm, d), lambda i, j: (i, 0)),
                  pl.BlockSpec((1, d), lambda i, j: (0, 0)),
                  pl.BlockSpec((d, tn), lambda i, j: (0, j))],
        out_specs=out_specs,
        out_shape=out_shape,
        scratch_shapes=[pltpu.VMEM((tm, d), BF16)],
        compiler_params=_params("parallel", "arbitrary"),
        name="norm_proj",
    )(x, nw, w)


def _hgrn_kernel(*refs, heads, dk, dv, chunk, rows, has_s0):
    if has_s0:
        q_ref, f_ref, i_ref, gate_ref, lb_ref, nw_ref, s0_ref, o_ref, sout_ref, st_ref = refs
    else:
        q_ref, f_ref, i_ref, gate_ref, lb_ref, nw_ref, o_ref, sout_ref, st_ref = refs
    t = pl.program_id(1)
    real = q_ref.shape[0]

    @pl.when(t == 0)
    def _():
        for h in range(heads):
            if has_s0:
                st_ref[h] = s0_ref[h].T
            else:
                st_ref[h] = jnp.zeros((dv, dk), F32)

    lb = lb_ref[...]
    f = lb + (1.0 - lb) * _sigmoid(f_ref[...])
    g = jnp.log(f)
    kin = 1.0 - f
    q = q_ref[...]
    v = i_ref[...]
    if rows > real:
        def pad(a):
            return jnp.concatenate([a, jnp.zeros((rows - real, a.shape[1]), F32)], axis=0)
        g, kin, q, v = pad(g), pad(kin), pad(q), pad(v)

    cs = math.gcd(rows, 2 * LANES)
    r = lax.broadcasted_iota(jnp.int32, (cs, cs), 0)
    c = lax.broadcasted_iota(jnp.int32, (cs, cs), 1)
    within = (r // chunk == c // chunk) & (c <= r)
    tri = jnp.where(within, 1.0, 0.0).astype(BF16)
    g_hi = g.astype(BF16)
    g_r1 = g - g_hi.astype(F32)
    g_mid = g_r1.astype(BF16)
    g_lo = (g_r1 - g_mid.astype(F32)).astype(BF16)
    G = jnp.concatenate(
        [jnp.dot(tri, g_hi[i:i + cs], preferred_element_type=F32)
         + jnp.dot(tri, g_mid[i:i + cs], preferred_element_type=F32)
         + jnp.dot(tri, g_lo[i:i + cs], preferred_element_type=F32)
         for i in range(0, rows, cs)], axis=0)

    qg = (q * jnp.exp(G)).astype(BF16)
    kg = (kin * jnp.exp(-G)).astype(BF16)
    vb = v.astype(BF16)
    nw = nw_ref[...]

    states = [st_ref[h] for h in range(heads)]
    for g0 in range(0, rows, cs):
        intra = []
        for h in range(heads):
            sk = slice(h * dk, (h + 1) * dk)
            a = lax.dot_general(qg[g0:g0 + cs, sk], kg[g0:g0 + cs, sk], _NT,
                                preferred_element_type=F32)
            a = jnp.where(within, a, 0.0).astype(BF16)
            intra.append(jnp.dot(a, vb[g0:g0 + cs, h * dv:(h + 1) * dv],
                                 preferred_element_type=F32))
        for ci in range(cs // chunk):
            lo = g0 + ci * chunk
            Gc = G[lo:lo + chunk]
            Gl = Gc[chunk - 1:chunk]
            kdec = (kin[lo:lo + chunk] * jnp.exp(Gl - Gc)).astype(BF16)
            decay = jnp.exp(Gl)
            n_out = min(chunk, real - lo)
            for h in range(heads):
                sk = slice(h * dk, (h + 1) * dk)
                sv = slice(h * dv, (h + 1) * dv)
                st = states[h]
                o = intra[h][ci * chunk:(ci + 1) * chunk] + lax.dot_general(
                    qg[lo:lo + chunk, sk], st.astype(BF16), _NT, preferred_element_type=F32)
                states[h] = st * decay[:, sk] + lax.dot_general(
                    vb[lo:lo + chunk, sv], kdec[:, sk], _TN, preferred_element_type=F32)
                if n_out > 0:
                    gt = gate_ref[lo:lo + n_out, sv]
                    on = _rms(o[:n_out], nw) * (gt * _sigmoid(gt))
                    o_ref[lo:lo + n_out, sv] = on.astype(o_ref.dtype)
    for h in range(heads):
        st_ref[h] = states[h]

    @pl.when(t == pl.num_programs(1) - 1)
    def _():
        for h in range(heads):
            sout_ref[h] = st_ref[h].T


def _hgrn(u, col0, lb, nw, s0, batch, seqlen, heads, dk, dv):
    width = heads * dk
    assert dk == dv and col0 % width == 0
    cb = col0 // width
    chunk = min(HG_CHUNK, seqlen)
    if seqlen >= LANES:
        tb = _pick(seqlen, (512, 256, 128))
        rows = tb
        assert tb % chunk == 0
    else:
        tb = seqlen
        rows = LANES
        chunk = LANES
    nt = seqlen // tb
    has_s0 = s0 is not None

    def col(k):
        return pl.BlockSpec((tb, width), lambda b, t: (b * nt + t, cb + k))

    in_specs = [col(0), col(1), col(2), col(3),
                pl.BlockSpec((1, width), lambda b, t: (0, 0)),
                pl.BlockSpec((1, dv), lambda b, t: (0, 0))]
    args = [u, u, u, u, lb, nw]
    if has_s0:
        in_specs.append(pl.BlockSpec((None, heads, dk, dv), lambda b, t: (b, 0, 0, 0)))
        args.append(s0)
    kern = functools.partial(_hgrn_kernel, heads=heads, dk=dk, dv=dv, chunk=chunk, rows=rows,
                             has_s0=has_s0)
    return pl.pallas_call(
        kern,
        grid=(batch, nt),
        in_specs=in_specs,
        out_specs=[pl.BlockSpec((tb, heads * dv), lambda b, t: (b * nt + t, 0)),
                   pl.BlockSpec((None, heads, dk, dv), lambda b, t: (b, 0, 0, 0))],
        out_shape=[jax.ShapeDtypeStruct((batch * seqlen, heads * dv), BF16),
                   jax.ShapeDtypeStruct((batch, heads, dk, dv), F32)],
        scratch_shapes=[pltpu.VMEM((heads, dv, dk), F32)],
        compiler_params=_params("parallel", "arbitrary"),
        name="hgrn",
    )(*args)


def _rel_bias(dist, value, n_buckets):
    n = jnp.maximum(dist, 0)
    max_exact = n_buckets // 2
    nf = jnp.maximum(n, 1).astype(F32)
    x = (jnp.log(nf / max_exact) / math.log(REL_MAX_DISTANCE / max_exact)
         * (n_buckets - max_exact))
    bias = jnp.zeros(dist.shape, F32)
    for k in range(n_buckets - 2, max_exact - 1, -1):
        bias = jnp.where(x < k - max_exact + 1, value(k), bias)
    for k in range(max_exact - 1, -1, -1):
        bias = jnp.where(n <= k, value(k), bias)
    return bias


def _far_start(n_buckets):
    n = np.arange(1, 8 * REL_MAX_DISTANCE, dtype=np.int64)
    max_exact = n_buckets // 2
    nf = n.astype(np.float32)
    large = max_exact + (np.log(nf / np.float32(max_exact)) / np.float32(math.log(REL_MAX_DISTANCE / max_exact))
                         * np.float32(n_buckets - max_exact)).astype(np.int32)
    b = np.where(n < max_exact, n, np.minimum(large, n_buckets - 1))
    below = n[b < n_buckets - 1]
    return int(below.max()) + 2


def _attn_kernel(*refs, **kw):
    _attn_body(pl.program_id(0), pl.program_id(1) == 0, None, *refs, **kw)


def _attn_body(hp, init, qb_only, tbl_ref, lam_ref, nw_ref, q1_ref, q2_ref, k1_ref, k2_ref, v_ref,
               o_ref, bias_ref, m_ref, acc_ref, *, t, dh, dv, heads, n_buckets, out_scale):
    @pl.when(init)
    def _():
        row = lax.broadcasted_iota(jnp.int32, (t, t), 0)
        col = lax.broadcasted_iota(jnp.int32, (t, t), 1)
        for e in range(2):
            for mp in range(2):
                hcol = mp * heads + 2 * hp + e
                far = tbl_ref[n_buckets - 1, hcol]
                value = lambda k: tbl_ref[k, hcol] - far
                bd = _rel_bias(row - col, value, n_buckets)
                bn = _rel_bias(row - col + t, value, n_buckets)
                rs = slice((2 * e + mp) * t, (2 * e + mp + 1) * t)
                bias_ref[rs, :t] = bn
                bias_ref[rs, t:] = jnp.where(col <= row, bd, NEG_INF)

    lane = lax.broadcasted_iota(jnp.int32, (t, 2 * dh), 1)
    scale = dh ** -0.5
    zero = jnp.zeros((t, 2 * dh), F32)
    lam = lam_ref[0, 0]
    nw = nw_ref[...]

    def q_block(qb, carry):
        rows = slice(None) if qb_only is not None else pl.ds(pl.multiple_of(qb * t, t), t)
        q_maps = [q1_ref[rows, :] * scale, q2_ref[rows, :] * scale]
        blocks = []
        for e in range(2):
            in_e = (lane >= e * dh) & (lane < (e + 1) * dh)
            for mp in range(2):
                qe = jnp.where(in_e, q_maps[mp], 0.0)
                blocks.append(jnp.concatenate([qe, zero] if mp == 0 else [zero, qe], axis=1))
        q_all = jnp.concatenate(blocks, axis=0).astype(BF16)

        m_ref[...] = jnp.full(m_ref.shape, NEG_INF, F32)
        acc_ref[...] = jnp.zeros(acc_ref.shape, F32)

        def step(j, n, bias=None):
            start = pl.multiple_of(j * t, t)
            kb = jnp.concatenate([k1_ref[:, pl.ds(start, n * t)], k2_ref[:, pl.ds(start, n * t)]],
                                 axis=0).astype(BF16)
            s = jnp.dot(q_all, kb, preferred_element_type=F32)
            if bias is not None:
                s = s + bias
            m_prev = m_ref[...]
            m_new = jnp.maximum(m_prev, jnp.max(s, axis=1, keepdims=True))
            pb = jnp.exp(s - jnp.tile(m_new, (1, n * t // LANES))).astype(BF16)
            alpha = jnp.tile(jnp.exp(m_prev - m_new), (1, 2))
            m_ref[...] = m_new
            ones = jnp.ones((n * t, dv), BF16)
            for e in range(2):
                ve = v_ref[pl.ds(start, n * t), e * dv:(e + 1) * dv].astype(BF16)
                rs = slice(2 * e * t, 2 * (e + 1) * t)
                acc_ref[rs, :] = acc_ref[rs, :] * alpha[rs] + jnp.dot(
                    pb[rs], jnp.concatenate([ve, ones], axis=1), preferred_element_type=F32)

        n_far = jnp.maximum(qb - 1, 0)

        def far_body(i, c):
            step(2 * i, 2)
            return c

        lax.fori_loop(0, n_far // 2, far_body, 0)

        @pl.when(n_far % 2 == 1)
        def _():
            step(n_far - 1, 1)

        @pl.when(qb > 0)
        def _():
            step(qb - 1, 2, bias_ref[...])

        @pl.when(qb == 0)
        def _():
            step(0, 1, bias_ref[:, t:])

        o = acc_ref[:, :dv] / acc_ref[:, dv:]
        for e in range(2):
            oe = o[2 * e * t:(2 * e + 1) * t] - lam * o[(2 * e + 1) * t:(2 * e + 2) * t]
            o_ref[rows, e * dv:(e + 1) * dv] = (_rms(oe, nw) * out_scale).astype(o_ref.dtype)
        return carry

    if qb_only is None:
        lax.fori_loop(0, q1_ref.shape[0] // t, q_block, 0)
    else:
        q_block(qb_only, 0)


def _attn_prompt(u, qcol, vcol, kt, tbl, lam, nw, batch, seqlen, heads, dh, dv, out_scale):
    assert dv == LANES and 2 * dh == LANES and heads % 2 == 0
    n_buckets = tbl.shape[0]
    t = _pick(seqlen, (256, 128))
    assert seqlen % t == 0 and t % LANES == 0 and t + 1 >= _far_start(n_buckets)
    nhp = heads // 2
    qc, vc = qcol // LANES, vcol // (2 * dv)
    smem = pl.BlockSpec(memory_space=pltpu.SMEM)
    kern = functools.partial(_attn_kernel, t=t, dh=dh, dv=dv, heads=heads, n_buckets=n_buckets,
                             out_scale=out_scale)
    return pl.pallas_call(
        kern,
        grid=(nhp, batch),
        in_specs=[smem, smem,
                  pl.BlockSpec((1, dv), lambda h, b: (0, 0)),
                  pl.BlockSpec((seqlen, LANES), lambda h, b: (b, qc + h)),
                  pl.BlockSpec((seqlen, LANES), lambda h, b: (b, qc + nhp + h)),
                  pl.BlockSpec((2 * dh, seqlen), lambda h, b: (b * 2 * nhp + h, 0)),
                  pl.BlockSpec((2 * dh, seqlen), lambda h, b: (b * 2 * nhp + nhp + h, 0)),
                  pl.BlockSpec((seqlen, 2 * dv), lambda h, b: (b, vc + h))],
        out_specs=pl.BlockSpec((seqlen, 2 * dv), lambda h, b: (b, h)),
        out_shape=jax.ShapeDtypeStruct((batch * seqlen, heads * dv), BF16),
        scratch_shapes=[pltpu.VMEM((4 * t, 2 * t), F32),
                        pltpu.VMEM((4 * t, LANES), F32), pltpu.VMEM((4 * t, 2 * dv), F32)],
        compiler_params=_params("arbitrary", "arbitrary"),
        name="attn_prompt",
    )(tbl, lam, nw, u, u, kt, kt, u)


def _decode_kernel(pt_ref, *refs, **kw):
    first = (pl.program_id(0) == 0) & (pl.program_id(1) == 0)
    _decode_body(first, pl.program_id(1), pl.num_programs(1), *refs, **kw)


def _decode_body(first, g, ng, lam_ref, trow_ref, nw_ref, q_ref, kn_ref, vn_ref, *rest,
                 pages, page, lq, heads, dh, dv, n_buckets, past_len, near_slots, out_scale):
    k_refs = rest[:pages]
    v_refs = rest[pages:2 * pages]
    o_ref, m_ref, l_ref, acc_ref, bias_ref = rest[2 * pages:]
    mh = 2 * heads
    rows = mh * lq
    n_near = len(near_slots)
    order = [m * heads + h for h in range(heads) for m in range(2)]

    def near_bias(key0, n_valid):
        r = lax.broadcasted_iota(jnp.int32, (rows, page), 0)
        col = lax.broadcasted_iota(jnp.int32, (rows, page), 1)
        dist = past_len + r % lq - (key0 + col)
        far = trow_ref[:, n_buckets - 1:n_buckets]
        bias = _rel_bias(dist, lambda k: trow_ref[:, k:k + 1] - far, n_buckets)
        return jnp.where((dist >= 0) & (col < n_valid), bias, NEG_INF)

    @pl.when(first)
    def _():
        for i, slot in enumerate(near_slots):
            bias_ref[i] = near_bias(past_len - (pages - slot) * page, page)
        bias_ref[n_near] = near_bias(past_len, lq)

    @pl.when(g == 0)
    def _():
        m_ref[...] = jnp.full(m_ref.shape, NEG_INF, F32)
        l_ref[...] = jnp.zeros(l_ref.shape, F32)
        acc_ref[...] = jnp.zeros(acc_ref.shape, F32)

    q = q_ref[...] * (dh ** -0.5)
    q_parts = [q[:, j * dh:(j + 1) * dh].astype(BF16) for j in order]

    def attend(score, value_rows, bias):
        s = jnp.concatenate([score(qj, j) for qj, j in zip(q_parts, order)], axis=0)
        if bias is not None:
            s = s + bias
        m_prev = m_ref[...]
        m_new = jnp.maximum(m_prev, jnp.max(s, axis=1, keepdims=True))
        p = jnp.exp(s - jnp.tile(m_new, (1, s.shape[1] // LANES)))
        alpha = jnp.exp(m_prev - m_new)
        l_ref[...] = alpha * l_ref[...] + jnp.sum(p, axis=1, keepdims=True)
        m_ref[...] = m_new
        pv = jnp.concatenate(
            [jnp.dot(p[2 * lq * h:2 * lq * (h + 1)].astype(BF16), value_rows(h).astype(BF16),
                     preferred_element_type=F32) for h in range(heads)], axis=0)
        acc_ref[...] = acc_ref[...] * alpha + pv

    is_last = g == ng - 1
    bias = None
    if near_slots:
        zero = jnp.zeros((rows, page), F32)
        bias = jnp.concatenate(
            [jnp.where(is_last, bias_ref[near_slots.index(s)], 0.0) if s in near_slots else zero
             for s in range(pages)], axis=1)
    def page_score(qj, j):
        kt = jnp.concatenate([r[j * dh:(j + 1) * dh, :] for r in k_refs], axis=1)
        return jnp.dot(qj, kt.astype(BF16), preferred_element_type=F32)

    attend(page_score,
           lambda h: jnp.concatenate([r[pl.ds(h, page, stride=heads), :] for r in v_refs], axis=0),
           bias)

    @pl.when(is_last)
    def _():
        kn = kn_ref[...]
        vn = vn_ref[...]
        zk = jnp.zeros((page - lq, dh), F32)
        zv = jnp.zeros((page - lq, dv), F32)

        def new_score(qj, j):
            kj = jnp.concatenate([kn[:, j * dh:(j + 1) * dh], zk], axis=0).astype(BF16)
            return lax.dot_general(qj, kj, _NT, preferred_element_type=F32)

        attend(new_score,
               lambda h: jnp.concatenate([vn[:, h * dv:(h + 1) * dv], zv], axis=0),
               bias_ref[n_near])
        full = acc_ref[...] / l_ref[...]
        lam = lam_ref[0, 0]
        nw = nw_ref[...]
        for h in range(heads):
            o0 = full[2 * lq * h:2 * lq * h + lq]
            o1 = full[2 * lq * h + lq:2 * lq * (h + 1)]
            o_ref[:, h * dv:(h + 1) * dv] = (_rms(o0 - lam * o1, nw) * out_scale).astype(o_ref.dtype)


def _attn_decode(u, qcol, kcol, vcol, cache_k, cache_v, page_table, tbl, lam, nw,
                 batch, lq, heads, dh, dv, out_scale):
    n_pool, page = cache_k.shape[0], cache_k.shape[1]
    n_pages = page_table.shape[1]
    past_len = n_pages * page
    mh = 2 * heads
    kw, vw = mh * dh, heads * dv
    assert page % LANES == 0 and kw % LANES == 0 and vw % LANES == 0 and lq % 8 == 0
    n_buckets = tbl.shape[0]
    pages = next(p for p in (16, 8, 4, 2, 1)
                 if _decode_geometry(n_pages, page, n_buckets, p) is not None)
    near_slots = _decode_geometry(n_pages, page, n_buckets, pages)
    ck = jnp.transpose(cache_k, (0, 2, 3, 1)).reshape(n_pool * mh * dh, page)
    cv = cache_v.reshape(n_pool * page * heads, dv)
    trow = jnp.repeat(tbl.T.reshape(2, heads, -1).transpose(1, 0, 2).reshape(mh, -1), lq, axis=0)
    rows = mh * lq

    def page_spec(n_rows, width, slot):
        return pl.BlockSpec((n_rows, width),
                            lambda b, g, pt: (pt[b * n_pages + g * pages + slot], 0))

    smem = pl.BlockSpec(memory_space=pltpu.SMEM)
    in_specs = ([smem,
                 pl.BlockSpec((rows, n_buckets), lambda b, g, pt: (0, 0)),
                 pl.BlockSpec((1, dv), lambda b, g, pt: (0, 0)),
                 pl.BlockSpec((lq, kw), lambda b, g, pt: (b, qcol // kw)),
                 pl.BlockSpec((lq, kw), lambda b, g, pt: (b, kcol // kw)),
                 pl.BlockSpec((lq, vw), lambda b, g, pt: (b, vcol // vw))]
                + [page_spec(mh * dh, page, s) for s in range(pages)]
                + [page_spec(page * heads, dv, s) for s in range(pages)])
    assert qcol % kw == 0 and kcol % kw == 0 and vcol % vw == 0
    kern = functools.partial(_decode_kernel, pages=pages, page=page, lq=lq, heads=heads, dh=dh,
                             dv=dv, n_buckets=n_buckets, past_len=past_len,
                             near_slots=near_slots, out_scale=out_scale)
    grid_spec = pltpu.PrefetchScalarGridSpec(
        num_scalar_prefetch=1,
        grid=(batch, n_pages // pages),
        in_specs=in_specs,
        out_specs=pl.BlockSpec((lq, vw), lambda b, g, pt: (b, 0)),
        scratch_shapes=[pltpu.VMEM((rows, LANES), F32), pltpu.VMEM((rows, LANES), F32),
                        pltpu.VMEM((rows, dv), F32),
                        pltpu.VMEM((len(near_slots) + 1, rows, page), F32)])
    return pl.pallas_call(
        kern,
        grid_spec=grid_spec,
        out_shape=jax.ShapeDtypeStruct((batch * lq, vw), BF16),
        compiler_params=_params("arbitrary", "arbitrary"),
        name="attn_decode",
    )(page_table.reshape(-1), lam, trow, nw, u, u, u, *([ck] * pages), *([cv] * pages))


def _decode_geometry(n_pages, page, n_buckets, pages):
    past_len = n_pages * page
    far = _far_start(n_buckets)
    first_near = max(0, -(-(past_len - page + 1 - far + 1) // page))
    if pages < 1 or n_pages % pages or n_pages - pages > first_near:
        return None
    return tuple(s for s in range(pages) if n_pages - pages + s >= first_near)


def _fused_pages(batch_p, seqlen, heads, batch_s, n_pages):
    steps = (heads // 2) * batch_p * (seqlen // _pick(seqlen, (256, 128)))
    total = batch_s * n_pages
    return total // steps if total % steps == 0 else 0


def _attn_fused_kernel(pt_ref, *refs, n_dec_in, nb, nq, ng, attn_kw, dec_kw):
    hp, b, qb = pl.program_id(0), pl.program_id(1), pl.program_id(2)
    step = (hp * nb + b) * nq + qb
    a_in, rest = refs[:8], refs[8:]
    d_in, rest = rest[:n_dec_in], rest[n_dec_in:]
    o_attn, o_dec = rest[:2]
    a_scr, d_scr = rest[2:5], rest[5:]
    _attn_body(hp, (b == 0) & (qb == 0), qb, *a_in, o_attn, *a_scr, **attn_kw)
    _decode_body(step == 0, step % ng, ng, *d_in, o_dec, *d_scr, **dec_kw)


def _attn_fused(up, kt, us, qcol, kcol, vcol, cache_k, cache_v, page_table, tbl, lam, nw,
                batch_p, seqlen, batch_s, lq, heads, dh, dv, out_scale, pages):
    assert dv == LANES and 2 * dh == LANES and heads % 2 == 0
    n_buckets = tbl.shape[0]
    t = _pick(seqlen, (256, 128))
    assert seqlen % t == 0 and t % LANES == 0 and t + 1 >= _far_start(n_buckets)
    nq, nhp = seqlen // t, heads // 2
    qc, vc = qcol // LANES, vcol // (2 * dv)
    n_pool, page = cache_k.shape[0], cache_k.shape[1]
    n_pages = page_table.shape[1]
    ng = n_pages // pages
    mh = 2 * heads
    kw, vw = mh * dh, heads * dv
    near_slots = _decode_geometry(n_pages, page, n_buckets, pages)
    assert near_slots is not None and nhp * batch_p * nq == batch_s * ng
    assert page % LANES == 0 and lq % 8 == 0
    assert qcol % kw == 0 and kcol % kw == 0 and vcol % vw == 0
    ck = jnp.transpose(cache_k, (0, 2, 3, 1)).reshape(n_pool * mh * dh, page)
    cv = cache_v.reshape(n_pool * page * heads, dv)
    trow = jnp.repeat(tbl.T.reshape(2, heads, -1).transpose(1, 0, 2).reshape(mh, -1), lq, axis=0)
    rows = mh * lq

    def seq_grp(h, b, i):
        step = (h * batch_p + b) * nq + i
        return step // ng, step % ng

    def page_spec(n_rows, width, slot):
        def index(h, b, i, pt):
            seq, grp = seq_grp(h, b, i)
            return pt[seq * n_pages + grp * pages + slot], 0
        return pl.BlockSpec((n_rows, width), index)

    def sample_rows(width, col):
        return pl.BlockSpec((lq, width), lambda h, b, i, pt: (seq_grp(h, b, i)[0], col))

    smem = pl.BlockSpec(memory_space=pltpu.SMEM)
    const = lambda h, b, i, pt: (0, 0)
    attn_specs = [smem, smem, pl.BlockSpec((1, dv), const),
                  pl.BlockSpec((t, LANES), lambda h, b, i, pt: (b * nq + i, qc + h)),
                  pl.BlockSpec((t, LANES), lambda h, b, i, pt: (b * nq + i, qc + nhp + h)),
                  pl.BlockSpec((2 * dh, seqlen), lambda h, b, i, pt: (b * 2 * nhp + h, 0)),
                  pl.BlockSpec((2 * dh, seqlen), lambda h, b, i, pt: (b * 2 * nhp + nhp + h, 0)),
                  pl.BlockSpec((seqlen, 2 * dv), lambda h, b, i, pt: (b, vc + h))]
    dec_specs = ([smem, pl.BlockSpec((rows, n_buckets), const), pl.BlockSpec((1, dv), const),
                  sample_rows(kw, qcol // kw), sample_rows(kw, kcol // kw),
                  sample_rows(vw, vcol // vw)]
                 + [page_spec(mh * dh, page, s) for s in range(pages)]
                 + [page_spec(page * heads, dv, s) for s in range(pages)])
    attn_kw = dict(t=t, dh=dh, dv=dv, heads=heads, n_buckets=n_buckets, out_scale=out_scale)
    dec_kw = dict(pages=pages, page=page, lq=lq, heads=heads, dh=dh, dv=dv, n_buckets=n_buckets,
                  past_len=n_pages * page, near_slots=near_slots, out_scale=out_scale)
    kern = functools.partial(_attn_fused_kernel, n_dec_in=len(dec_specs), nb=batch_p, nq=nq,
                             ng=ng, attn_kw=attn_kw, dec_kw=dec_kw)
    grid_spec = pltpu.PrefetchScalarGridSpec(
        num_scalar_prefetch=1,
        grid=(nhp, batch_p, nq),
        in_specs=attn_specs + dec_specs,
        out_specs=[pl.BlockSpec((t, 2 * dv), lambda h, b, i, pt: (b * nq + i, h)),
                   pl.BlockSpec((lq, vw), lambda h, b, i, pt: (seq_grp(h, b, i)[0], 0))],
        scratch_shapes=[pltpu.VMEM((4 * t, 2 * t), F32), pltpu.VMEM((4 * t, LANES), F32),
                        pltpu.VMEM((4 * t, 2 * dv), F32),
                        pltpu.VMEM((rows, LANES), F32), pltpu.VMEM((rows, LANES), F32),
                        pltpu.VMEM((rows, dv), F32),
                        pltpu.VMEM((len(near_slots) + 1, rows, page), F32)])
    return pl.pallas_call(
        kern,
        grid_spec=grid_spec,
        out_shape=[jax.ShapeDtypeStruct((batch_p * seqlen, heads * dv), BF16),
                   jax.ShapeDtypeStruct((batch_s * lq, vw), BF16)],
        compiler_params=_params("arbitrary", "arbitrary", "arbitrary"),
        name="attn_fused",
    )(page_table.reshape(-1), tbl, lam, nw, up, up, kt, kt, up,
      lam, trow, nw, us, us, us, *([ck] * pages), *([cv] * pages))


def _merge_resident_kernel(*refs, n_g):
    ohg_ref, oda_ref = refs[:2]
    g_refs = refs[2:2 + 2 * n_g]
    x_ref, wbh_ref, wbd_ref, wo_ref, nw_ref, o_ref = refs[2 + 2 * n_g:]
    g_hg = jnp.concatenate([r[...] for r in g_refs[:n_g]], axis=1)
    g_da = jnp.concatenate([r[...] for r in g_refs[n_g:]], axis=1)
    y_hg = jnp.dot(ohg_ref[...], wbh_ref[...], preferred_element_type=F32)
    y_da = jnp.dot(oda_ref[...], wbd_ref[...], preferred_element_type=F32)
    mixed = (_sigmoid(g_hg) * y_hg + _sigmoid(g_da) * y_da).astype(BF16)
    z = jnp.dot(mixed, wo_ref[...], preferred_element_type=F32)
    o_ref[...] = x_ref[...] + _rms(z, nw_ref[...])


def _merge_resident(o_hg, o_da, u, gcol, x, wbh, wbd, wo, nw):
    m, d = x.shape
    w = o_hg.shape[1]
    gw = math.gcd(gcol, d)
    n_g = d // gw
    assert gw % LANES == 0
    tm = _pick(m, (256, 128, 64, 8))
    const = lambda i: (0, 0)
    g_specs = [pl.BlockSpec((tm, gw), functools.partial(lambda i, c: (i, c), c=gcol // gw + k))
               for k in range(2 * n_g)]
    return pl.pallas_call(
        functools.partial(_merge_resident_kernel, n_g=n_g),
        grid=(m // tm,),
        in_specs=[pl.BlockSpec((tm, w), lambda i: (i, 0)),
                  pl.BlockSpec((tm, w), lambda i: (i, 0))]
                 + g_specs
                 + [pl.BlockSpec((tm, d), lambda i: (i, 0)),
                    pl.BlockSpec((w, d), const, pipeline_mode=pl.Buffered(1)),
                    pl.BlockSpec((w, d), const, pipeline_mode=pl.Buffered(1)),
                    pl.BlockSpec((d, d), const, pipeline_mode=pl.Buffered(1)),
                    pl.BlockSpec((1, d), const)],
        out_specs=pl.BlockSpec((tm, d), lambda i: (i, 0)),
        out_shape=jax.ShapeDtypeStruct((m, d), F32),
        compiler_params=_params("parallel"),
        name="merge",
    )(o_hg, o_da, *([u] * (2 * n_g)), x, wbh, wbd, wo, nw)


def _merge_kernel(ohg_ref, oda_ref, ghg_ref, gda_ref, x_ref, wbh_ref, wbd_ref, wo_ref, nw_ref,
                  o_ref, *copies, emit_w):
    copies = copies if emit_w else (None, None, None)
    n = pl.program_id(1)

    @pl.when(n == 0)
    def _():
        o_ref[...] = jnp.zeros(o_ref.shape, F32)

    y_hg = jnp.dot(ohg_ref[...], _bf16_weight(wbh_ref, copies[0]), preferred_element_type=F32)
    y_da = jnp.dot(oda_ref[...], _bf16_weight(wbd_ref, copies[1]), preferred_element_type=F32)
    mixed = (_sigmoid(ghg_ref[...]) * y_hg + _sigmoid(gda_ref[...]) * y_da).astype(BF16)
    o_ref[...] += jnp.dot(mixed, _bf16_weight(wo_ref, copies[2]), preferred_element_type=F32)

    @pl.when(n == pl.num_programs(1) - 1)
    def _():
        o_ref[...] = x_ref[...] + _rms(o_ref[...], nw_ref[...])


def _merge(o_hg, o_da, u, gcol, x, wbh, wbd, wo, nw):
    m, d = x.shape
    w = o_hg.shape[1]
    emit_w = wo.dtype != BF16
    tm = _pick(m, (1024, 512, 256, 128, 64, 32, 16, 8))
    tn = _pick(math.gcd(gcol, d), (512, 256, 128))
    nn = d // tn
    gc = gcol // tn
    once = dict(pipeline_mode=pl.Buffered(1))
    out_specs = [pl.BlockSpec((tm, d), lambda i, n: (i, 0))]
    out_shape = [jax.ShapeDtypeStruct((m, d), F32)]
    if emit_w:
        assert m == tm, "the bf16 weight copies are written once per column tile"
        out_specs += [pl.BlockSpec((w, tn), lambda i, n: (0, n)),
                      pl.BlockSpec((w, tn), lambda i, n: (0, n)),
                      pl.BlockSpec((tn, d), lambda i, n: (n, 0))]
        out_shape += [jax.ShapeDtypeStruct((w, d), BF16), jax.ShapeDtypeStruct((w, d), BF16),
                      jax.ShapeDtypeStruct((d, d), BF16)]
    return pl.pallas_call(
        functools.partial(_merge_kernel, emit_w=emit_w),
        grid=(m // tm, nn),
        in_specs=[pl.BlockSpec((tm, w), lambda i, n: (i, 0), **once),
                  pl.BlockSpec((tm, w), lambda i, n: (i, 0), **once),
                  pl.BlockSpec((tm, tn), lambda i, n: (i, gc + n)),
                  pl.BlockSpec((tm, tn), lambda i, n: (i, gc + nn + n)),
                  pl.BlockSpec((tm, d), lambda i, n: (i, 0), **once),
                  pl.BlockSpec((w, tn), lambda i, n: (0, n)),
                  pl.BlockSpec((w, tn), lambda i, n: (0, n)),
                  pl.BlockSpec((tn, d), lambda i, n: (n, 0)),
                  pl.BlockSpec((1, d), lambda i, n: (0, 0))],
        out_specs=out_specs,
        out_shape=out_shape,
        compiler_params=_params("parallel", "arbitrary"),
        name="merge",
    )(o_hg, o_da, u, u, x, wbh, wbd, wo, nw)


def _ffn_kernel(x_ref, npre_ref, wg_ref, wu_ref, wd_ref, npost_ref, o_ref, *rest, emit_w):
    h_ref = rest[-1]
    copies = rest[:3] if emit_w else (None, None, None)
    j = pl.program_id(1)

    @pl.when(j == 0)
    def _():
        h_ref[...] = _rms(x_ref[...], npre_ref[...]).astype(BF16)
        o_ref[...] = jnp.zeros(o_ref.shape, F32)

    h = h_ref[...]
    gate = jnp.dot(h, _bf16_weight(wg_ref, copies[0]), preferred_element_type=F32)
    up = jnp.dot(h, _bf16_weight(wu_ref, copies[1]), preferred_element_type=F32)
    act = (gate * _sigmoid(gate) * up).astype(BF16)
    o_ref[...] += jnp.dot(act, _bf16_weight(wd_ref, copies[2]), preferred_element_type=F32)

    @pl.when(j == pl.num_programs(1) - 1)
    def _():
        o_ref[...] = x_ref[...] + _rms(o_ref[...], npost_ref[...])


def _ffn(x, npre, w_gate, w_up, w_down, npost):
    m, d = x.shape
    ff = w_down.shape[0]
    emit_w = w_down.dtype != BF16
    tm = _pick(m, (1024, 512, 256, 128, 64, 32, 16, 8))
    tf = _pick(ff, (512, 256, 128))
    nf = ff // tf
    (wg, gcol), (wu, ucol) = w_gate, w_up
    assert gcol % tf == 0 and ucol % tf == 0
    gc, uc = gcol // tf, ucol // tf
    out_specs = [pl.BlockSpec((tm, d), lambda i, j: (i, 0))]
    out_shape = [jax.ShapeDtypeStruct((m, d), F32)]
    if emit_w:
        assert m == tm, "the bf16 weight copies are written once per hidden-dim tile"
        out_specs += [pl.BlockSpec((d, tf), lambda i, j: (0, j)),
                      pl.BlockSpec((d, tf), lambda i, j: (0, j)),
                      pl.BlockSpec((tf, d), lambda i, j: (j, 0))]
        out_shape += [jax.ShapeDtypeStruct((d, ff), BF16), jax.ShapeDtypeStruct((d, ff), BF16),
                      jax.ShapeDtypeStruct((ff, d), BF16)]
    return pl.pallas_call(
        functools.partial(_ffn_kernel, emit_w=emit_w),
        grid=(m // tm, nf),
        in_specs=[pl.BlockSpec((tm, d), lambda i, j: (i, 0)),
                  pl.BlockSpec((1, d), lambda i, j: (0, 0)),
                  pl.BlockSpec((d, tf), lambda i, j: (0, gc + j)),
                  pl.BlockSpec((d, tf), lambda i, j: (0, uc + j)),
                  pl.BlockSpec((tf, d), lambda i, j: (j, 0)),
                  pl.BlockSpec((1, d), lambda i, j: (0, 0))],
        out_specs=out_specs,
        out_shape=out_shape,
        scratch_shapes=[pltpu.VMEM((tm, d), BF16)],
        compiler_params=_params("parallel", "arbitrary"),
        name="ffn",
    )(x, npre, wg, wu, w_down, npost)


def _mixer_tail(x2, u, o_da, s0, lw, wts, dims, batch, seqlen, gcol):
    hg_heads, hg_dk, hg_dv = dims[:3]
    o_hg, s_new = _hgrn(u, 0, lw["lb"], lw["hg_norm_w"], s0, batch, seqlen, hg_heads, hg_dk, hg_dv)
    merge_args = (o_hg, o_da, u, gcol, x2, wts["w_branch_hg"], wts["w_branch_da"], wts["w_out"],
                  lw["norm_mix_post"])
    if wts["w_out"].dtype == BF16:
        x1 = _merge_resident(*merge_args)
    else:
        x1, wts["w_branch_hg"], wts["w_branch_da"], wts["w_out"] = _merge(*merge_args)
    y, *copy = _ffn(x1, lw["norm_ffn_pre"], wts["w_gate"], wts["w_up"], wts["w_down"],
                    lw["norm_ffn_post"])
    if copy:
        wts["w_gate"], wts["w_up"], wts["w_down"] = (copy[0], 0), (copy[1], 0), copy[2]
    return y, s_new, wts


def _layer(x_p, x_s, s0_s, cache_k, cache_v, page_table, lw, wts, dims):
    (bp, lp, d), (bs, ls, _) = x_p.shape, x_s.shape
    wts = dict(wts)
    hg_heads, hg_dk, hg_dv, da_heads, da_dh, da_dv = dims
    hg_w, da_qk, da_w = hg_heads * hg_dk, 2 * da_heads * da_dh, da_heads * da_dv
    assert da_qk == da_w
    cols = np.cumsum([0, hg_w, hg_w, hg_heads * hg_dv, hg_heads * hg_dv, da_qk, da_qk, da_w, d, d])
    qcol, kcol, vcol, gcol = int(cols[4]), int(cols[5]), int(cols[6]), int(cols[7])
    xp2, xs2 = x_p.reshape(bp * lp, d), x_s.reshape(bs * ls, d)

    us, wts["w_in"] = _norm_proj(xs2, lw["norm_mix_pre"], wts["w_in"])
    up, kt, vp2 = _norm_proj(xp2, lw["norm_mix_pre"], wts["w_in"],
                             kv=(kcol, vcol, da_qk, lp, da_heads))

    attn = (lw["tbl"], lw["lam"], lw["da_subln_w"])
    shape = (da_heads, da_dh, da_dv, lw["out_scale"])
    n_pages, page = page_table.shape[1], cache_k.shape[1]
    pages = _fused_pages(bp, lp, da_heads, bs, n_pages)
    if _decode_geometry(n_pages, page, lw["tbl"].shape[0], pages) is not None:
        o_da_p, o_da_s = _attn_fused(up, kt, us, qcol, kcol, vcol, cache_k, cache_v, page_table,
                                     *attn, bp, lp, bs, ls, *shape, pages)
    else:
        o_da_p = _attn_prompt(up, qcol, vcol, kt, *attn, bp, lp, *shape)
        o_da_s = _attn_decode(us, qcol, kcol, vcol, cache_k, cache_v, page_table, *attn,
                              bs, ls, *shape)

    y_s, s_s, wts = _mixer_tail(xs2, us, o_da_s, s0_s, lw, wts, dims, bs, ls, gcol)
    y_p, s_p, _ = _mixer_tail(xp2, up, o_da_p, None, lw, wts, dims, bp, lp, gcol)

    k_p = kt.reshape(bp, 2 * da_heads, da_dh, lp).transpose(0, 3, 1, 2)
    v_p = vp2.reshape(bp, lp, da_heads, da_dv)
    k_s = us[:, kcol:kcol + da_qk].reshape(bs, ls, 2 * da_heads, da_dh)
    v_s = us[:, vcol:vcol + da_w].reshape(bs, ls, da_heads, da_dv)
    return (y_p.reshape(bp, lp, d), y_s.reshape(bs, ls, d)), (k_p, v_p, s_p, k_s, v_s, s_s)


def kernel(x_prompt, x_sample, cache_k, cache_v, state_hgrn, page_table, norm_mix_pre, norm_mix_post, norm_ffn_pre, norm_ffn_post, w_in, hg_lb_logits, hg_norm_w, da_lambda_q1, da_lambda_k1, da_lambda_q2, da_lambda_k2, da_subln_w, rel_bias_table, w_branch_hg, w_branch_da, w_out, w_ffn_up, w_ffn_down):
    depth = w_in.shape[0]
    _, _, hg_heads, hg_dk, hg_dv = state_hgrn.shape
    da_heads, da_dv = cache_v.shape[3], cache_v.shape[4]
    da_dh = cache_k.shape[4]
    dims = (hg_heads, hg_dk, hg_dv, da_heads, da_dh, da_dv)
    lb_all = jnp.cumsum(jax.nn.softmax(hg_lb_logits.astype(F32), axis=0), axis=0)

    y_p, y_s = x_prompt, x_sample
    outs = [[] for _ in range(6)]
    for l in range(depth):
        lam_init = 0.8 - 0.6 * math.exp(-0.3 * l)
        lam = (jnp.exp(jnp.sum(da_lambda_q1[l] * da_lambda_k1[l]))
               - jnp.exp(jnp.sum(da_lambda_q2[l] * da_lambda_k2[l])) + lam_init)
        lw = {
            "norm_mix_pre": norm_mix_pre[l][None], "norm_mix_post": norm_mix_post[l][None],
            "norm_ffn_pre": norm_ffn_pre[l][None], "norm_ffn_post": norm_ffn_post[l][None],
            "lb": lb_all[l][None], "hg_norm_w": hg_norm_w[l][None],
            "da_subln_w": da_subln_w[l][None], "tbl": rel_bias_table.astype(F32),
            "lam": lam.reshape(1, 1).astype(F32), "out_scale": 1.0 - lam_init,
        }
        ff = w_ffn_down.shape[1]
        wts = {"w_in": w_in[l], "w_branch_hg": w_branch_hg[l], "w_branch_da": w_branch_da[l],
               "w_out": w_out[l], "w_gate": (w_ffn_up[l], 0), "w_up": (w_ffn_up[l], ff),
               "w_down": w_ffn_down[l]}
        (y_p, y_s), caches = _layer(y_p, y_s, state_hgrn[l], cache_k[l], cache_v[l], page_table,
                                    lw, wts, dims)
        for acc, val in zip(outs, caches):
            acc.append(val)
    return (y_p, y_s) + tuple(jnp.stack(o) for o in outs)
```

```python
import functools
import math

import jax
import jax.numpy as jnp
import numpy as np
from jax import lax
from jax.experimental import pallas as pl
from jax.experimental.pallas import tpu as pltpu

F32 = jnp.float32
BF16 = jnp.bfloat16

RMS_EPS = 1e-6
NEG_INF = -1e30
HG_CHUNK = 64
REL_MAX_DISTANCE = 128
LANES = 128
VMEM_LIMIT = 62 * 1024 * 1024

_NT = (((1,), (1,)), ((), ()))
_TN = (((0,), (0,)), ((), ()))


def _params(*sem):
    return pltpu.CompilerParams(dimension_semantics=sem, vmem_limit_bytes=VMEM_LIMIT)


def _rms(x, w):
    return x * lax.rsqrt(jnp.mean(x * x, axis=-1, keepdims=True) + RMS_EPS) * w


def _sigmoid(x):
    return 1.0 / (1.0 + jnp.exp(-x))


def _pick(n, prefs):
    for p in prefs:
        if n % p == 0:
            return p
    return n


def _bf16_weight(w_ref, copy_ref):
    w = w_ref[...]
    if copy_ref is not None:
        w = w.astype(BF16)
        copy_ref[...] = w
    return w


def _norm_proj_kernel(x_ref, nw_ref, w_ref, o_ref, *rest, jk, jv, v_heads, emit_w):
    h_ref = rest[-1]
    j = pl.program_id(1)

    @pl.when(j == 0)
    def _():
        h_ref[...] = _rms(x_ref[...], nw_ref[...]).astype(BF16)

    w = _bf16_weight(w_ref, rest[-2] if emit_w else None)
    res = jnp.dot(h_ref[...], w, preferred_element_type=F32)
    o_ref[...] = res
    if jk is not None:
        kt_ref, v_ref = rest[:2]
        tm, tn = res.shape

        @pl.when(j == jk)
        def _():
            kt_ref[...] = res.T

        @pl.when(j == jv)
        def _():
            dv = tn // v_heads
            for h in range(v_heads):
                v_ref[pl.ds(h, tm, stride=v_heads), :] = res[:, h * dv:(h + 1) * dv]


def _norm_proj(x, nw, w, kv=None):
    m, d = x.shape
    n = w.shape[1]
    emit_w = w.dtype != BF16
    tm = _pick(m, (1024, 512, 256, 128, 64, 32, 16, 8))
    tn = _pick(n, (1024, 512, 256, 128))
    out_specs = [pl.BlockSpec((tm, tn), lambda i, j: (i, j))]
    out_shape = [jax.ShapeDtypeStruct((m, n), F32)]
    jk = jv = v_heads = None
    if kv is not None:
        kcol, vcol, width, seqlen, v_heads = kv
        tn = width
        assert n % tn == 0 and kcol % tn == 0 and vcol % tn == 0 and seqlen % tm == 0
        jk, jv = kcol // tn, vcol // tn
        nt = seqlen // tm
        out_specs = [pl.BlockSpec((tm, tn), lambda i, j: (i, j)),
                     pl.BlockSpec((tn, tm), lambda i, j: (i // nt, i % nt)),
                     pl.BlockSpec((tm * v_heads, tn // v_heads), lambda i, j: (i, 0))]
        out_shape += [jax.ShapeDtypeStruct((m // seqlen * tn, seqlen), F32),
                      jax.ShapeDtypeStruct((m * v_heads, tn // v_heads), F32)]
    if emit_w:
        out_specs.append(pl.BlockSpec((d, tn), lambda i, j: (0, j)))
        out_shape.append(jax.ShapeDtypeStruct((d, n), BF16))
        assert m == tm, "the bf16 weight copy is written once per column tile"
    return pl.pallas_call(
        functools.partial(_norm_proj_kernel, jk=jk, jv=jv, v_heads=v_heads, emit_w=emit_w),
        grid=(m // tm, n // tn),
        in_specs=[pl.BlockSpec((tm, d), lambda i, j: (i, 0)),
                  pl.BlockSpec((1, d), lambda i, j: (0, 0)),
                  pl.BlockSpec((d, tn), lambda i, j: (0, j))],
        out_specs=out_specs,
        out_shape=out_shape,
        scratch_shapes=[pltpu.VMEM((tm, d), BF16)],
        compiler_params=_params("parallel", "arbitrary"),
        name="norm_proj",
    )(x, nw, w)


def _hgrn_kernel(*refs, heads, dk, dv, chunk, rows, has_s0):
    if has_s0:
        q_ref, f_ref, i_ref, gate_ref, lb_ref, nw_ref, s0_ref, o_ref, sout_ref, st_ref = refs
    else:
        q_ref, f_ref, i_ref, gate_ref, lb_ref, nw_ref, o_ref, sout_ref, st_ref = refs
    t = pl.program_id(1)
    real = q_ref.shape[0]

    @pl.when(t == 0)
    def _():
        for h in range(heads):
            if has_s0:
                st_ref[h] = s0_ref[h].T
            else:
                st_ref[h] = jnp.zeros((dv, dk), F32)

    lb = lb_ref[...]
    f = lb + (1.0 - lb) * _sigmoid(f_ref[...])
    g = jnp.log(f)
    kin = 1.0 - f
    q = q_ref[...]
    v = i_ref[...]
    if rows > real:
        def pad(a):
            return jnp.concatenate([a, jnp.zeros((rows - real, a.shape[1]), F32)], axis=0)
        g, kin, q, v = pad(g), pad(kin), pad(q), pad(v)

    cs = math.gcd(rows, 2 * LANES)
    r = lax.broadcasted_iota(jnp.int32, (cs, cs), 0)
    c = lax.broadcasted_iota(jnp.int32, (cs, cs), 1)
    within = (r // chunk == c // chunk) & (c <= r)
    tri = jnp.where(within, 1.0, 0.0).astype(BF16)
    g_hi = g.astype(BF16)
    g_r1 = g - g_hi.astype(F32)
    g_mid = g_r1.astype(BF16)
    g_lo = (g_r1 - g_mid.astype(F32)).astype(BF16)
    G = jnp.concatenate(
        [jnp.dot(tri, g_hi[i:i + cs], preferred_element_type=F32)
         + jnp.dot(tri, g_mid[i:i + cs], preferred_element_type=F32)
         + jnp.dot(tri, g_lo[i:i + cs], preferred_element_type=F32)
         for i in range(0, rows, cs)], axis=0)

    qg = (q * jnp.exp(G)).astype(BF16)
    kg = (kin * jnp.exp(-G)).astype(BF16)
    vb = v.astype(BF16)
    nw = nw_ref[...]

    states = [st_ref[h] for h in range(heads)]
    for g0 in range(0, rows, cs):
        intra = []
        for h in range(heads):
            sk = slice(h * dk, (h + 1) * dk)
            a = lax.dot_general(qg[g0:g0 + cs, sk], kg[g0:g0 + cs, sk], _NT,
                                preferred_element_type=F32)
            a = jnp.where(within, a, 0.0).astype(BF16)
            intra.append(jnp.dot(a, vb[g0:g0 + cs, h * dv:(h + 1) * dv],
                                 preferred_element_type=F32))
        for ci in range(cs // chunk):
            lo = g0 + ci * chunk
            Gc = G[lo:lo + chunk]
            Gl = Gc[chunk - 1:chunk]
            kdec = (kin[lo:lo + chunk] * jnp.exp(Gl - Gc)).astype(BF16)
            decay = jnp.exp(Gl)
            n_out = min(chunk, real - lo)
            for h in range(heads):
                sk = slice(h * dk, (h + 1) * dk)
                sv = slice(h * dv, (h + 1) * dv)
                st = states[h]
                o = intra[h][ci * chunk:(ci + 1) * chunk] + lax.dot_general(
                    qg[lo:lo + chunk, sk], st.astype(BF16), _NT, preferred_element_type=F32)
                states[h] = st * decay[:, sk] + lax.dot_general(
                    vb[lo:lo + chunk, sv], kdec[:, sk], _TN, preferred_element_type=F32)
                if n_out > 0:
                    gt = gate_ref[lo:lo + n_out, sv]
                    on = _rms(o[:n_out], nw) * (gt * _sigmoid(gt))
                    o_ref[lo:lo + n_out, sv] = on.astype(o_ref.dtype)
    for h in range(heads):
        st_ref[h] = states[h]

    @pl.when(t == pl.num_programs(1) - 1)
    def _():
        for h in range(heads):
            sout_ref[h] = st_ref[h].T


def _hgrn(u, col0, lb, nw, s0, batch, seqlen, heads, dk, dv):
    width = heads * dk
    assert dk == dv and col0 % width == 0
    cb = col0 // width
    chunk = min(HG_CHUNK, seqlen)
    if seqlen >= LANES:
        tb = _pick(seqlen, (512, 256, 128))
        rows = tb
        assert tb % chunk == 0
    else:
        tb = seqlen
        rows = LANES
        chunk = LANES
    nt = seqlen // tb
    has_s0 = s0 is not None

    def col(k):
        return pl.BlockSpec((tb, width), lambda b, t: (b * nt + t, cb + k))

    in_specs = [col(0), col(1), col(2), col(3),
                pl.BlockSpec((1, width), lambda b, t: (0, 0)),
                pl.BlockSpec((1, dv), lambda b, t: (0, 0))]
    args = [u, u, u, u, lb, nw]
    if has_s0:
        in_specs.append(pl.BlockSpec((None, heads, dk, dv), lambda b, t: (b, 0, 0, 0)))
        args.append(s0)
    kern = functools.partial(_hgrn_kernel, heads=heads, dk=dk, dv=dv, chunk=chunk, rows=rows,
                             has_s0=has_s0)
    return pl.pallas_call(
        kern,
        grid=(batch, nt),
        in_specs=in_specs,
        out_specs=[pl.BlockSpec((tb, heads * dv), lambda b, t: (b * nt + t, 0)),
                   pl.BlockSpec((None, heads, dk, dv), lambda b, t: (b, 0, 0, 0))],
        out_shape=[jax.ShapeDtypeStruct((batch * seqlen, heads * dv), BF16),
                   jax.ShapeDtypeStruct((batch, heads, dk, dv), F32)],
        scratch_shapes=[pltpu.VMEM((heads, dv, dk), F32)],
        compiler_params=_params("parallel", "arbitrary"),
        name="hgrn",
    )(*args)


def _rel_bias(dist, value, n_buckets):
    n = jnp.maximum(dist, 0)
    max_exact = n_buckets // 2
    nf = jnp.maximum(n, 1).astype(F32)
    x = (jnp.log(nf / max_exact) / math.log(REL_MAX_DISTANCE / max_exact)
         * (n_buckets - max_exact))
    bias = jnp.zeros(dist.shape, F32)
    for k in range(n_buckets - 2, max_exact - 1, -1):
        bias = jnp.where(x < k - max_exact + 1, value(k), bias)
    for k in range(max_exact - 1, -1, -1):
        bias = jnp.where(n <= k, value(k), bias)
    return bias


def _far_start(n_buckets):
    n = np.arange(1, 8 * REL_MAX_DISTANCE, dtype=np.int64)
    max_exact = n_buckets // 2
    nf = n.astype(np.float32)
    large = max_exact + (np.log(nf / np.float32(max_exact)) / np.float32(math.log(REL_MAX_DISTANCE / max_exact))
                         * np.float32(n_buckets - max_exact)).astype(np.int32)
    b = np.where(n < max_exact, n, np.minimum(large, n_buckets - 1))
    below = n[b < n_buckets - 1]
    return int(below.max()) + 2


def _attn_kernel(*refs, **kw):
    _attn_body(pl.program_id(0), pl.program_id(1) == 0, None, *refs, **kw)


def _attn_body(hp, init, qb_only, tbl_ref, lam_ref, nw_ref, q1_ref, q2_ref, k1_ref, k2_ref, v_ref,
               o_ref, bias_ref, m_ref, acc_ref, *, t, dh, dv, heads, n_buckets, out_scale):
    @pl.when(init)
    def _():
        row = lax.broadcasted_iota(jnp.int32, (t, t), 0)
        col = lax.broadcasted_iota(jnp.int32, (t, t), 1)
        for e in range(2):
            for mp in range(2):
                hcol = mp * heads + 2 * hp + e
                far = tbl_ref[n_buckets - 1, hcol]
                value = lambda k: tbl_ref[k, hcol] - far
                bd = _rel_bias(row - col, value, n_buckets)
                bn = _rel_bias(row - col + t, value, n_buckets)
                rs = slice((2 * e + mp) * t, (2 * e + mp + 1) * t)
                bias_ref[rs, :t] = bn
                bias_ref[rs, t:] = jnp.where(col <= row, bd, NEG_INF)

    lane = lax.broadcasted_iota(jnp.int32, (t, 2 * dh), 1)
    scale = dh ** -0.5
    zero = jnp.zeros((t, 2 * dh), F32)
    lam = lam_ref[0, 0]
    nw = nw_ref[...]

    def q_block(qb, carry):
        rows = slice(None) if qb_only is not None else pl.ds(pl.multiple_of(qb * t, t), t)
        q_maps = [q1_ref[rows, :] * scale, q2_ref[rows, :] * scale]
        blocks = []
        for e in range(2):
            in_e = (lane >= e * dh) & (lane < (e + 1) * dh)
            for mp in range(2):
                qe = jnp.where(in_e, q_maps[mp], 0.0)
                blocks.append(jnp.concatenate([qe, zero] if mp == 0 else [zero, qe], axis=1))
        q_all = jnp.concatenate(blocks, axis=0).astype(BF16)

        m_ref[...] = jnp.full(m_ref.shape, NEG_INF, F32)
        acc_ref[...] = jnp.zeros(acc_ref.shape, F32)

        def step(j, n, bias=None):
            start = pl.multiple_of(j * t, t)
            kb = jnp.concatenate([k1_ref[:, pl.ds(start, n * t)], k2_ref[:, pl.ds(start, n * t)]],
                                 axis=0).astype(BF16)
            s = jnp.dot(q_all, kb, preferred_element_type=F32)
            if bias is not None:
                s = s + bias
            m_prev = m_ref[...]
            m_new = jnp.maximum(m_prev, jnp.max(s, axis=1, keepdims=True))
            pb = jnp.exp(s - jnp.tile(m_new, (1, n * t // LANES))).astype(BF16)
            alpha = jnp.tile(jnp.exp(m_prev - m_new), (1, 2))
            m_ref[...] = m_new
            ones = jnp.ones((n * t, dv), BF16)
            for e in range(2):
                ve = v_ref[pl.ds(start, n * t), e * dv:(e + 1) * dv].astype(BF16)
                rs = slice(2 * e * t, 2 * (e + 1) * t)
                acc_ref[rs, :] = acc_ref[rs, :] * alpha[rs] + jnp.dot(
                    pb[rs], jnp.concatenate([ve, ones], axis=1), preferred_element_type=F32)

        n_far = jnp.maximum(qb - 1, 0)

        def far_body(i, c):
            step(2 * i, 2)
            return c

        lax.fori_loop(0, n_far // 2, far_body, 0)

        @pl.when(n_far % 2 == 1)
        def _():
            step(n_far - 1, 1)

        @pl.when(qb > 0)
        def _():
            step(qb - 1, 2, bias_ref[...])

        @pl.when(qb == 0)
        def _():
            step(0, 1, bias_ref[:, t:])

        o = acc_ref[:, :dv] / acc_ref[:, dv:]
        for e in range(2):
            oe = o[2 * e * t:(2 * e + 1) * t] - lam * o[(2 * e + 1) * t:(2 * e + 2) * t]
            o_ref[rows, e * dv:(e + 1) * dv] = (_rms(oe, nw) * out_scale).astype(o_ref.dtype)
        return carry

    if qb_only is None:
        lax.fori_loop(0, q1_ref.shape[0] // t, q_block, 0)
    else:
        q_block(qb_only, 0)


def _attn_prompt(u, qcol, vcol, kt, tbl, lam, nw, batch, seqlen, heads, dh, dv, out_scale):
    assert dv == LANES and 2 * dh == LANES and heads % 2 == 0
    n_buckets = tbl.shape[0]
    t = _pick(seqlen, (256, 128))
    assert seqlen % t == 0 and t % LANES == 0 and t + 1 >= _far_start(n_buckets)
    nhp = heads // 2
    qc, vc = qcol // LANES, vcol // (2 * dv)
    smem = pl.BlockSpec(memory_space=pltpu.SMEM)
    kern = functools.partial(_attn_kernel, t=t, dh=dh, dv=dv, heads=heads, n_buckets=n_buckets,
                             out_scale=out_scale)
    return pl.pallas_call(
        kern,
        grid=(nhp, batch),
        in_specs=[smem, smem,
                  pl.BlockSpec((1, dv), lambda h, b: (0, 0)),
                  pl.BlockSpec((seqlen, LANES), lambda h, b: (b, qc + h)),
                  pl.BlockSpec((seqlen, LANES), lambda h, b: (b, qc + nhp + h)),
                  pl.BlockSpec((2 * dh, seqlen), lambda h, b: (b * 2 * nhp + h, 0)),
                  pl.BlockSpec((2 * dh, seqlen), lambda h, b: (b * 2 * nhp + nhp + h, 0)),
                  pl.BlockSpec((seqlen, 2 * dv), lambda h, b: (b, vc + h))],
        out_specs=pl.BlockSpec((seqlen, 2 * dv), lambda h, b: (b, h)),
        out_shape=jax.ShapeDtypeStruct((batch * seqlen, heads * dv), BF16),
        scratch_shapes=[pltpu.VMEM((4 * t, 2 * t), F32),
                        pltpu.VMEM((4 * t, LANES), F32), pltpu.VMEM((4 * t, 2 * dv), F32)],
        compiler_params=_params("arbitrary", "arbitrary"),
        name="attn_prompt",
    )(tbl, lam, nw, u, u, kt, kt, u)


def _decode_kernel(pt_ref, *refs, **kw):
    first = (pl.program_id(0) == 0) & (pl.program_id(1) == 0)
    _decode_body(first, pl.program_id(1), pl.num_programs(1), *refs, **kw)


def _decode_body(first, g, ng, lam_ref, trow_ref, nw_ref, q_ref, kn_ref, vn_ref, *rest,
                 pages, page, lq, heads, dh, dv, n_buckets, past_len, near_slots, out_scale):
    k_refs = rest[:pages]
    v_refs = rest[pages:2 * pages]
    o_ref, m_ref, l_ref, acc_ref, bias_ref = rest[2 * pages:]
    mh = 2 * heads
    rows = mh * lq
    n_near = len(near_slots)
    order = [m * heads + h for h in range(heads) for m in range(2)]

    def near_bias(key0, n_valid):
        r = lax.broadcasted_iota(jnp.int32, (rows, page), 0)
        col = lax.broadcasted_iota(jnp.int32, (rows, page), 1)
        dist = past_len + r % lq - (key0 + col)
        far = trow_ref[:, n_buckets - 1:n_buckets]
        bias = _rel_bias(dist, lambda k: trow_ref[:, k:k + 1] - far, n_buckets)
        return jnp.where((dist >= 0) & (col < n_valid), bias, NEG_INF)

    @pl.when(first)
    def _():
        for i, slot in enumerate(near_slots):
            bias_ref[i] = near_bias(past_len - (pages - slot) * page, page)
        bias_ref[n_near] = near_bias(past_len, lq)

    @pl.when(g == 0)
    def _():
        m_ref[...] = jnp.full(m_ref.shape, NEG_INF, F32)
        l_ref[...] = jnp.zeros(l_ref.shape, F32)
        acc_ref[...] = jnp.zeros(acc_ref.shape, F32)

    q = q_ref[...] * (dh ** -0.5)
    q_parts = [q[:, j * dh:(j + 1) * dh].astype(BF16) for j in order]

    def attend(score, value_rows, bias):
        s = jnp.concatenate([score(qj, j) for qj, j in zip(q_parts, order)], axis=0)
        if bias is not None:
            s = s + bias
        m_prev = m_ref[...]
        m_new = jnp.maximum(m_prev, jnp.max(s, axis=1, keepdims=True))
        p = jnp.exp(s - jnp.tile(m_new, (1, s.shape[1] // LANES)))
        alpha = jnp.exp(m_prev - m_new)
        l_ref[...] = alpha * l_ref[...] + jnp.sum(p, axis=1, keepdims=True)
        m_ref[...] = m_new
        pv = jnp.concatenate(
            [jnp.dot(p[2 * lq * h:2 * lq * (h + 1)].astype(BF16), value_rows(h).astype(BF16),
                     preferred_element_type=F32) for h in range(heads)], axis=0)
        acc_ref[...] = acc_ref[...] * alpha + pv

    is_last = g == ng - 1
    bias = None
    if near_slots:
        zero = jnp.zeros((rows, page), F32)
        bias = jnp.concatenate(
            [jnp.where(is_last, bias_ref[near_slots.index(s)], 0.0) if s in near_slots else zero
             for s in range(pages)], axis=1)
    def page_score(qj, j):
        kt = jnp.concatenate([r[j * dh:(j + 1) * dh, :] for r in k_refs], axis=1)
        return jnp.dot(qj, kt.astype(BF16), preferred_element_type=F32)

    v_by_head = [pltpu.einshape("phd->hpd", r[...].reshape(page, heads, dv)) for r in v_refs]
    attend(page_score,
           lambda h: jnp.concatenate([vh[h] for vh in v_by_head], axis=0),
           bias)

    @pl.when(is_last)
    def _():
        kn = kn_ref[...]
        vn = vn_ref[...]
        zk = jnp.zeros((page - lq, dh), F32)
        zv = jnp.zeros((page - lq, dv), F32)

        def new_score(qj, j):
            kj = jnp.concatenate([kn[:, j * dh:(j + 1) * dh], zk], axis=0).astype(BF16)
            return lax.dot_general(qj, kj, _NT, preferred_element_type=F32)

        attend(new_score,
               lambda h: jnp.concatenate([vn[:, h * dv:(h + 1) * dv], zv], axis=0),
               bias_ref[n_near])
        full = acc_ref[...] / l_ref[...]
        lam = lam_ref[0, 0]
        nw = nw_ref[...]
        for h in range(heads):
            o0 = full[2 * lq * h:2 * lq * h + lq]
            o1 = full[2 * lq * h + lq:2 * lq * (h + 1)]
            o_ref[:, h * dv:(h + 1) * dv] = (_rms(o0 - lam * o1, nw) * out_scale).astype(o_ref.dtype)


def _attn_decode(u, qcol, kcol, vcol, cache_k, cache_v, page_table, tbl, lam, nw,
                 batch, lq, heads, dh, dv, out_scale):
    n_pool, page = cache_k.shape[0], cache_k.shape[1]
    n_pages = page_table.shape[1]
    past_len = n_pages * page
    mh = 2 * heads
    kw, vw = mh * dh, heads * dv
    assert page % LANES == 0 and kw % LANES == 0 and vw % LANES == 0 and lq % 8 == 0
    n_buckets = tbl.shape[0]
    pages = next(p for p in (16, 8, 4, 2, 1)
                 if _decode_geometry(n_pages, page, n_buckets, p) is not None)
    near_slots = _decode_geometry(n_pages, page, n_buckets, pages)
    ck = jnp.transpose(cache_k, (0, 2, 3, 1)).reshape(n_pool * mh * dh, page)
    cv = cache_v.reshape(n_pool * page * heads, dv)
    trow = jnp.repeat(tbl.T.reshape(2, heads, -1).transpose(1, 0, 2).reshape(mh, -1), lq, axis=0)
    rows = mh * lq

    def page_spec(n_rows, width, slot):
        return pl.BlockSpec((n_rows, width),
                            lambda b, g, pt: (pt[b * n_pages + g * pages + slot], 0))

    smem = pl.BlockSpec(memory_space=pltpu.SMEM)
    in_specs = ([smem,
                 pl.BlockSpec((rows, n_buckets), lambda b, g, pt: (0, 0)),
                 pl.BlockSpec((1, dv), lambda b, g, pt: (0, 0)),
                 pl.BlockSpec((lq, kw), lambda b, g, pt: (b, qcol // kw)),
                 pl.BlockSpec((lq, kw), lambda b, g, pt: (b, kcol // kw)),
                 pl.BlockSpec((lq, vw), lambda b, g, pt: (b, vcol // vw))]
                + [page_spec(mh * dh, page, s) for s in range(pages)]
                + [page_spec(page * heads, dv, s) for s in range(pages)])
    assert qcol % kw == 0 and kcol % kw == 0 and vcol % vw == 0
    kern = functools.partial(_decode_kernel, pages=pages, page=page, lq=lq, heads=heads, dh=dh,
                             dv=dv, n_buckets=n_buckets, past_len=past_len,
                             near_slots=near_slots, out_scale=out_scale)
    grid_spec = pltpu.PrefetchScalarGridSpec(
        num_scalar_prefetch=1,
        grid=(batch, n_pages // pages),
        in_specs=in_specs,
        out_specs=pl.BlockSpec((lq, vw), lambda b, g, pt: (b, 0)),
        scratch_shapes=[pltpu.VMEM((rows, LANES), F32), pltpu.VMEM((rows, LANES), F32),
                        pltpu.VMEM((rows, dv), F32),
                        pltpu.VMEM((len(near_slots) + 1, rows, page), F32)])
    return pl.pallas_call(
        kern,
        grid_spec=grid_spec,
        out_shape=jax.ShapeDtypeStruct((batch * lq, vw), BF16),
        compiler_params=_params("arbitrary", "arbitrary"),
        name="attn_decode",
    )(page_table.reshape(-1), lam, trow, nw, u, u, u, *([ck] * pages), *([cv] * pages))


def _decode_geometry(n_pages, page, n_buckets, pages):
    past_len = n_pages * page
    far = _far_start(n_buckets)
    first_near = max(0, -(-(past_len - page + 1 - far + 1) // page))
    if pages < 1 or n_pages % pages or n_pages - pages > first_near:
        return None
    return tuple(s for s in range(pages) if n_pages - pages + s >= first_near)


def _fused_pages(batch_p, seqlen, heads, batch_s, n_pages):
    steps = (heads // 2) * batch_p * (seqlen // _pick(seqlen, (256, 128)))
    total = batch_s * n_pages
    return total // steps if total % steps == 0 else 0


def _attn_fused_kernel(pt_ref, *refs, n_dec_in, nb, nq, ng, attn_kw, dec_kw):
    hp, b, qb = pl.program_id(0), pl.program_id(1), pl.program_id(2)
    step = (hp * nb + b) * nq + qb
    a_in, rest = refs[:8], refs[8:]
    d_in, rest = rest[:n_dec_in], rest[n_dec_in:]
    o_attn, o_dec = rest[:2]
    a_scr, d_scr = rest[2:5], rest[5:]
    _attn_body(hp, (b == 0) & (qb == 0), qb, *a_in, o_attn, *a_scr, **attn_kw)
    _decode_body(step == 0, step % ng, ng, *d_in, o_dec, *d_scr, **dec_kw)


def _attn_fused(up, kt, us, qcol, kcol, vcol, cache_k, cache_v, page_table, tbl, lam, nw,
                batch_p, seqlen, batch_s, lq, heads, dh, dv, out_scale, pages):
    assert dv == LANES and 2 * dh == LANES and heads % 2 == 0
    n_buckets = tbl.shape[0]
    t = _pick(seqlen, (256, 128))
    assert seqlen % t == 0 and t % LANES == 0 and t + 1 >= _far_start(n_buckets)
    nq, nhp = seqlen // t, heads // 2
    qc, vc = qcol // LANES, vcol // (2 * dv)
    n_pool, page = cache_k.shape[0], cache_k.shape[1]
    n_pages = page_table.shape[1]
    ng = n_pages // pages
    mh = 2 * heads
    kw, vw = mh * dh, heads * dv
    near_slots = _decode_geometry(n_pages, page, n_buckets, pages)
    assert near_slots is not None and nhp * batch_p * nq == batch_s * ng
    assert page % LANES == 0 and lq % 8 == 0
    assert qcol % kw == 0 and kcol % kw == 0 and vcol % vw == 0
    ck = jnp.transpose(cache_k, (0, 2, 3, 1)).reshape(n_pool * mh * dh, page)
    cv = cache_v.reshape(n_pool * page * heads, dv)
    trow = jnp.repeat(tbl.T.reshape(2, heads, -1).transpose(1, 0, 2).reshape(mh, -1), lq, axis=0)
    rows = mh * lq

    def seq_grp(h, b, i):
        step = (h * batch_p + b) * nq + i
        return step // ng, step % ng

    def page_spec(n_rows, width, slot):
        def index(h, b, i, pt):
            seq, grp = seq_grp(h, b, i)
            return pt[seq * n_pages + grp * pages + slot], 0
        return pl.BlockSpec((n_rows, width), index)

    def sample_rows(width, col):
        return pl.BlockSpec((lq, width), lambda h, b, i, pt: (seq_grp(h, b, i)[0], col))

    smem = pl.BlockSpec(memory_space=pltpu.SMEM)
    const = lambda h, b, i, pt: (0, 0)
    attn_specs = [smem, smem, pl.BlockSpec((1, dv), const),
                  pl.BlockSpec((t, LANES), lambda h, b, i, pt: (b * nq + i, qc + h)),
                  pl.BlockSpec((t, LANES), lambda h, b, i, pt: (b * nq + i, qc + nhp + h)),
                  pl.BlockSpec((2 * dh, seqlen), lambda h, b, i, pt: (b * 2 * nhp + h, 0)),
                  pl.BlockSpec((2 * dh, seqlen), lambda h, b, i, pt: (b * 2 * nhp + nhp + h, 0)),
                  pl.BlockSpec((seqlen, 2 * dv), lambda h, b, i, pt: (b, vc + h))]
    dec_specs = ([smem, pl.BlockSpec((rows, n_buckets), const), pl.BlockSpec((1, dv), const),
                  sample_rows(kw, qcol // kw), sample_rows(kw, kcol // kw),
                  sample_rows(vw, vcol // vw)]
                 + [page_spec(mh * dh, page, s) for s in range(pages)]
                 + [page_spec(page * heads, dv, s) for s in range(pages)])
    attn_kw = dict(t=t, dh=dh, dv=dv, heads=heads, n_buckets=n_buckets, out_scale=out_scale)
    dec_kw = dict(pages=pages, page=page, lq=lq, heads=heads, dh=dh, dv=dv, n_buckets=n_buckets,
                  past_len=n_pages * page, near_slots=near_slots, out_scale=out_scale)
    kern = functools.partial(_attn_fused_kernel, n_dec_in=len(dec_specs), nb=batch_p, nq=nq,
                             ng=ng, attn_kw=attn_kw, dec_kw=dec_kw)
    grid_spec = pltpu.PrefetchScalarGridSpec(
        num_scalar_prefetch=1,
        grid=(nhp, batch_p, nq),
        in_specs=attn_specs + dec_specs,
        out_specs=[pl.BlockSpec((t, 2 * dv), lambda h, b, i, pt: (b * nq + i, h)),
                   pl.BlockSpec((lq, vw), lambda h, b, i, pt: (seq_grp(h, b, i)[0], 0))],
        scratch_shapes=[pltpu.VMEM((4 * t, 2 * t), F32), pltpu.VMEM((4 * t, LANES), F32),
                        pltpu.VMEM((4 * t, 2 * dv), F32),
                        pltpu.VMEM((rows, LANES), F32), pltpu.VMEM((rows, LANES), F32),
                        pltpu.VMEM((rows, dv), F32),
                        pltpu.VMEM((len(near_slots) + 1, rows, page), F32)])
    return pl.pallas_call(
        kern,
        grid_spec=grid_spec,
        out_shape=[jax.ShapeDtypeStruct((batch_p * seqlen, heads * dv), BF16),
                   jax.ShapeDtypeStruct((batch_s * lq, vw), BF16)],
        compiler_params=_params("arbitrary", "arbitrary", "arbitrary"),
        name="attn_fused",
    )(page_table.reshape(-1), tbl, lam, nw, up, up, kt, kt, up,
      lam, trow, nw, us, us, us, *([ck] * pages), *([cv] * pages))


def _merge_resident_kernel(*refs, n_g):
    ohg_ref, oda_ref = refs[:2]
    g_refs = refs[2:2 + 2 * n_g]
    x_ref, wbh_ref, wbd_ref, wo_ref, nw_ref, o_ref = refs[2 + 2 * n_g:]
    g_hg = jnp.concatenate([r[...] for r in g_refs[:n_g]], axis=1)
    g_da = jnp.concatenate([r[...] for r in g_refs[n_g:]], axis=1)
    y_hg = jnp.dot(ohg_ref[...], wbh_ref[...], preferred_element_type=F32)
    y_da = jnp.dot(oda_ref[...], wbd_ref[...], preferred_element_type=F32)
    mixed = (_sigmoid(g_hg) * y_hg + _sigmoid(g_da) * y_da).astype(BF16)
    z = jnp.dot(mixed, wo_ref[...], preferred_element_type=F32)
    o_ref[...] = x_ref[...] + _rms(z, nw_ref[...])


def _merge_resident(o_hg, o_da, u, gcol, x, wbh, wbd, wo, nw):
    m, d = x.shape
    w = o_hg.shape[1]
    gw = math.gcd(gcol, d)
    n_g = d // gw
    assert gw % LANES == 0
    tm = _pick(m, (256, 128, 64, 8))
    const = lambda i: (0, 0)
    g_specs = [pl.BlockSpec((tm, gw), functools.partial(lambda i, c: (i, c), c=gcol // gw + k))
               for k in range(2 * n_g)]
    return pl.pallas_call(
        functools.partial(_merge_resident_kernel, n_g=n_g),
        grid=(m // tm,),
        in_specs=[pl.BlockSpec((tm, w), lambda i: (i, 0)),
                  pl.BlockSpec((tm, w), lambda i: (i, 0))]
                 + g_specs
                 + [pl.BlockSpec((tm, d), lambda i: (i, 0)),
                    pl.BlockSpec((w, d), const, pipeline_mode=pl.Buffered(1)),
                    pl.BlockSpec((w, d), const, pipeline_mode=pl.Buffered(1)),
                    pl.BlockSpec((d, d), const, pipeline_mode=pl.Buffered(1)),
                    pl.BlockSpec((1, d), const)],
        out_specs=pl.BlockSpec((tm, d), lambda i: (i, 0)),
        out_shape=jax.ShapeDtypeStruct((m, d), F32),
        compiler_params=_params("parallel"),
        name="merge",
    )(o_hg, o_da, *([u] * (2 * n_g)), x, wbh, wbd, wo, nw)


def _merge_kernel(ohg_ref, oda_ref, ghg_ref, gda_ref, x_ref, wbh_ref, wbd_ref, wo_ref, nw_ref,
                  o_ref, *copies, emit_w):
    copies = copies if emit_w else (None, None, None)
    n = pl.program_id(1)

    @pl.when(n == 0)
    def _():
        o_ref[...] = jnp.zeros(o_ref.shape, F32)

    y_hg = jnp.dot(ohg_ref[...], _bf16_weight(wbh_ref, copies[0]), preferred_element_type=F32)
    y_da = jnp.dot(oda_ref[...], _bf16_weight(wbd_ref, copies[1]), preferred_element_type=F32)
    mixed = (_sigmoid(ghg_ref[...]) * y_hg + _sigmoid(gda_ref[...]) * y_da).astype(BF16)
    o_ref[...] += jnp.dot(mixed, _bf16_weight(wo_ref, copies[2]), preferred_element_type=F32)

    @pl.when(n == pl.num_programs(1) - 1)
    def _():
        o_ref[...] = x_ref[...] + _rms(o_ref[...], nw_ref[...])


def _merge(o_hg, o_da, u, gcol, x, wbh, wbd, wo, nw):
    m, d = x.shape
    w = o_hg.shape[1]
    emit_w = wo.dtype != BF16
    tm = _pick(m, (1024, 512, 256, 128, 64, 32, 16, 8))
    tn = _pick(math.gcd(gcol, d), (512, 256, 128))
    nn = d // tn
    gc = gcol // tn
    once = dict(pipeline_mode=pl.Buffered(1))
    out_specs = [pl.BlockSpec((tm, d), lambda i, n: (i, 0))]
    out_shape = [jax.ShapeDtypeStruct((m, d), F32)]
    if emit_w:
        assert m == tm, "the bf16 weight copies are written once per column tile"
        out_specs += [pl.BlockSpec((w, tn), lambda i, n: (0, n)),
                      pl.BlockSpec((w, tn), lambda i, n: (0, n)),
                      pl.BlockSpec((tn, d), lambda i, n: (n, 0))]
        out_shape += [jax.ShapeDtypeStruct((w, d), BF16), jax.ShapeDtypeStruct((w, d), BF16),
                      jax.ShapeDtypeStruct((d, d), BF16)]
    return pl.pallas_call(
        functools.partial(_merge_kernel, emit_w=emit_w),
        grid=(m // tm, nn),
        in_specs=[pl.BlockSpec((tm, w), lambda i, n: (i, 0), **once),
                  pl.BlockSpec((tm, w), lambda i, n: (i, 0), **once),
                  pl.BlockSpec((tm, tn), lambda i, n: (i, gc + n)),
                  pl.BlockSpec((tm, tn), lambda i, n: (i, gc + nn + n)),
                  pl.BlockSpec((tm, d), lambda i, n: (i, 0), **once),
                  pl.BlockSpec((w, tn), lambda i, n: (0, n)),
                  pl.BlockSpec((w, tn), lambda i, n: (0, n)),
                  pl.BlockSpec((tn, d), lambda i, n: (n, 0)),
                  pl.BlockSpec((1, d), lambda i, n: (0, 0))],
        out_specs=out_specs,
        out_shape=out_shape,
        compiler_params=_params("parallel", "arbitrary"),
        name="merge",
    )(o_hg, o_da, u, u, x, wbh, wbd, wo, nw)


def _ffn_kernel(x_ref, npre_ref, wg_ref, wu_ref, wd_ref, npost_ref, o_ref, *rest, emit_w):
    h_ref = rest[-1]
    copies = rest[:3] if emit_w else (None, None, None)
    j = pl.program_id(1)

    @pl.when(j == 0)
    def _():
        h_ref[...] = _rms(x_ref[...], npre_ref[...]).astype(BF16)
        o_ref[...] = jnp.zeros(o_ref.shape, F32)

    h = h_ref[...]
    gate = jnp.dot(h, _bf16_weight(wg_ref, copies[0]), preferred_element_type=F32)
    up = jnp.dot(h, _bf16_weight(wu_ref, copies[1]), preferred_element_type=F32)
    act = (gate * _sigmoid(gate) * up).astype(BF16)
    o_ref[...] += jnp.dot(act, _bf16_weight(wd_ref, copies[2]), preferred_element_type=F32)

    @pl.when(j == pl.num_programs(1) - 1)
    def _():
        o_ref[...] = x_ref[...] + _rms(o_ref[...], npost_ref[...])


def _ffn(x, npre, w_gate, w_up, w_down, npost):
    m, d = x.shape
    ff = w_down.shape[0]
    emit_w = w_down.dtype != BF16
    tm = _pick(m, (1024, 512, 256, 128, 64, 32, 16, 8))
    tf = _pick(ff, (512, 256, 128))
    nf = ff // tf
    (wg, gcol), (wu, ucol) = w_gate, w_up
    assert gcol % tf == 0 and ucol % tf == 0
    gc, uc = gcol // tf, ucol // tf
    out_specs = [pl.BlockSpec((tm, d), lambda i, j: (i, 0))]
    out_shape = [jax.ShapeDtypeStruct((m, d), F32)]
    if emit_w:
        assert m == tm, "the bf16 weight copies are written once per hidden-dim tile"
        out_specs += [pl.BlockSpec((d, tf), lambda i, j: (0, j)),
                      pl.BlockSpec((d, tf), lambda i, j: (0, j)),
                      pl.BlockSpec((tf, d), lambda i, j: (j, 0))]
        out_shape += [jax.ShapeDtypeStruct((d, ff), BF16), jax.ShapeDtypeStruct((d, ff), BF16),
                      jax.ShapeDtypeStruct((ff, d), BF16)]
    return pl.pallas_call(
        functools.partial(_ffn_kernel, emit_w=emit_w),
        grid=(m // tm, nf),
        in_specs=[pl.BlockSpec((tm, d), lambda i, j: (i, 0)),
                  pl.BlockSpec((1, d), lambda i, j: (0, 0)),
                  pl.BlockSpec((d, tf), lambda i, j: (0, gc + j)),
                  pl.BlockSpec((d, tf), lambda i, j: (0, uc + j)),
                  pl.BlockSpec((tf, d), lambda i, j: (j, 0)),
                  pl.BlockSpec((1, d), lambda i, j: (0, 0))],
        out_specs=out_specs,
        out_shape=out_shape,
        scratch_shapes=[pltpu.VMEM((tm, d), BF16)],
        compiler_params=_params("parallel", "arbitrary"),
        name="ffn",
    )(x, npre, wg, wu, w_down, npost)


def _mixer_tail(x2, u, o_da, s0, lw, wts, dims, batch, seqlen, gcol):
    hg_heads, hg_dk, hg_dv = dims[:3]
    o_hg, s_new = _hgrn(u, 0, lw["lb"], lw["hg_norm_w"], s0, batch, seqlen, hg_heads, hg_dk, hg_dv)
    merge_args = (o_hg, o_da, u, gcol, x2, wts["w_branch_hg"], wts["w_branch_da"], wts["w_out"],
                  lw["norm_mix_post"])
    if wts["w_out"].dtype == BF16:
        x1 = _merge_resident(*merge_args)
    else:
        x1, wts["w_branch_hg"], wts["w_branch_da"], wts["w_out"] = _merge(*merge_args)
    y, *copy = _ffn(x1, lw["norm_ffn_pre"], wts["w_gate"], wts["w_up"], wts["w_down"],
                    lw["norm_ffn_post"])
    if copy:
        wts["w_gate"], wts["w_up"], wts["w_down"] = (copy[0], 0), (copy[1], 0), copy[2]
    return y, s_new, wts


def _layer(x_p, x_s, s0_s, cache_k, cache_v, page_table, lw, wts, dims):
    (bp, lp, d), (bs, ls, _) = x_p.shape, x_s.shape
    wts = dict(wts)
    hg_heads, hg_dk, hg_dv, da_heads, da_dh, da_dv = dims
    hg_w, da_qk, da_w = hg_heads * hg_dk, 2 * da_heads * da_dh, da_heads * da_dv
    assert da_qk == da_w
    cols = np.cumsum([0, hg_w, hg_w, hg_heads * hg_dv, hg_heads * hg_dv, da_qk, da_qk, da_w, d, d])
    qcol, kcol, vcol, gcol = int(cols[4]), int(cols[5]), int(cols[6]), int(cols[7])
    xp2, xs2 = x_p.reshape(bp * lp, d), x_s.reshape(bs * ls, d)

    us, wts["w_in"] = _norm_proj(xs2, lw["norm_mix_pre"], wts["w_in"])
    up, kt, vp2 = _norm_proj(xp2, lw["norm_mix_pre"], wts["w_in"],
                             kv=(kcol, vcol, da_qk, lp, da_heads))

    attn = (lw["tbl"], lw["lam"], lw["da_subln_w"])
    shape = (da_heads, da_dh, da_dv, lw["out_scale"])
    n_pages, page = page_table.shape[1], cache_k.shape[1]
    pages = _fused_pages(bp, lp, da_heads, bs, n_pages)
    if _decode_geometry(n_pages, page, lw["tbl"].shape[0], pages) is not None:
        o_da_p, o_da_s = _attn_fused(up, kt, us, qcol, kcol, vcol, cache_k, cache_v, page_table,
                                     *attn, bp, lp, bs, ls, *shape, pages)
    else:
        o_da_p = _attn_prompt(up, qcol, vcol, kt, *attn, bp, lp, *shape)
        o_da_s = _attn_decode(us, qcol, kcol, vcol, cache_k, cache_v, page_table, *attn,
                              bs, ls, *shape)

    y_s, s_s, wts = _mixer_tail(xs2, us, o_da_s, s0_s, lw, wts, dims, bs, ls, gcol)
    y_p, s_p, _ = _mixer_tail(xp2, up, o_da_p, None, lw, wts, dims, bp, lp, gcol)

    k_p = kt.reshape(bp, 2 * da_heads, da_dh, lp).transpose(0, 3, 1, 2)
    v_p = vp2.reshape(bp, lp, da_heads, da_dv)
    k_s = us[:, kcol:kcol + da_qk].reshape(bs, ls, 2 * da_heads, da_dh)
    v_s = us[:, vcol:vcol + da_w].reshape(bs, ls, da_heads, da_dv)
    return (y_p.reshape(bp, lp, d), y_s.reshape(bs, ls, d)), (k_p, v_p, s_p, k_s, v_s, s_s)


def kernel(x_prompt, x_sample, cache_k, cache_v, state_hgrn, page_table, norm_mix_pre, norm_mix_post, norm_ffn_pre, norm_ffn_post, w_in, hg_lb_logits, hg_norm_w, da_lambda_q1, da_lambda_k1, da_lambda_q2, da_lambda_k2, da_subln_w, rel_bias_table, w_branch_hg, w_branch_da, w_out, w_ffn_up, w_ffn_down):
    depth = w_in.shape[0]
    _, _, hg_heads, hg_dk, hg_dv = state_hgrn.shape
    da_heads, da_dv = cache_v.shape[3], cache_v.shape[4]
    da_dh = cache_k.shape[4]
    dims = (hg_heads, hg_dk, hg_dv, da_heads, da_dh, da_dv)
    lb_all = jnp.cumsum(jax.nn.softmax(hg_lb_logits.astype(F32), axis=0), axis=0)

    y_p, y_s = x_prompt, x_sample
    outs = [[] for _ in range(6)]
    for l in range(depth):
        lam_init = 0.8 - 0.6 * math.exp(-0.3 * l)
        lam = (jnp.exp(jnp.sum(da_lambda_q1[l] * da_lambda_k1[l]))
               - jnp.exp(jnp.sum(da_lambda_q2[l] * da_lambda_k2[l])) + lam_init)
        lw = {
            "norm_mix_pre": norm_mix_pre[l][None], "norm_mix_post": norm_mix_post[l][None],
            "norm_ffn_pre": norm_ffn_pre[l][None], "norm_ffn_post": norm_ffn_post[l][None],
            "lb": lb_all[l][None], "hg_norm_w": hg_norm_w[l][None],
            "da_subln_w": da_subln_w[l][None], "tbl": rel_bias_table.astype(F32),
            "lam": lam.reshape(1, 1).astype(F32), "out_scale": 1.0 - lam_init,
        }
        ff = w_ffn_down.shape[1]
        wts = {"w_in": w_in[l], "w_branch_hg": w_branch_hg[l], "w_branch_da": w_branch_da[l],
               "w_out": w_out[l], "w_gate": (w_ffn_up[l], 0), "w_up": (w_ffn_up[l], ff),
               "w_down": w_ffn_down[l]}
        (y_p, y_s), caches = _layer(y_p, y_s, state_hgrn[l], cache_k[l], cache_v[l], page_table,
                                    lw, wts, dims)
        for acc, val in zip(outs, caches):
            acc.append(val)
    return (y_p, y_s) + tuple(jnp.stack(o) for o in outs)
```

```python
import functools
import math

import jax
import jax.numpy as jnp
import numpy as np
from jax import lax
from jax.experimental import pallas as pl
from jax.experimental.pallas import tpu as pltpu

F32 = jnp.float32
BF16 = jnp.bfloat16

RMS_EPS = 1e-6
NEG_INF = -1e30
HG_CHUNK = 64
REL_MAX_DISTANCE = 128
LANES = 128
VMEM_LIMIT = 62 * 1024 * 1024

_NT = (((1,), (1,)), ((), ()))
_TN = (((0,), (0,)), ((), ()))


def _params(*sem):
    return pltpu.CompilerParams(dimension_semantics=sem, vmem_limit_bytes=VMEM_LIMIT)


def _rms(x, w):
    return x * lax.rsqrt(jnp.mean(x * x, axis=-1, keepdims=True) + RMS_EPS) * w


def _sigmoid(x):
    return 1.0 / (1.0 + jnp.exp(-x))


def _pick(n, prefs):
    for p in prefs:
        if n % p == 0:
            return p
    return n


def _bf16_weight(w_ref, copy_ref):
    w = w_ref[...]
    if copy_ref is not None:
        w = w.astype(BF16)
        copy_ref[...] = w
    return w


def _norm_proj_kernel(x_ref, nw_ref, w_ref, o_ref, *rest, jk, jv, v_heads, emit_w):
    h_ref = rest[-1]
    j = pl.program_id(1)

    @pl.when(j == 0)
    def _():
        h_ref[...] = _rms(x_ref[...], nw_ref[...]).astype(BF16)

    w = _bf16_weight(w_ref, rest[-2] if emit_w else None)
    res = jnp.dot(h_ref[...], w, preferred_element_type=F32)
    o_ref[...] = res
    if jk is not None:
        kt_ref, v_ref = rest[:2]
        tm, tn = res.shape

        @pl.when(j == jk)
        def _():
            kt_ref[...] = res.T

        @pl.when(j == jv)
        def _():
            dv = tn // v_heads
            for h in range(v_heads):
                v_ref[pl.ds(h, tm, stride=v_heads), :] = res[:, h * dv:(h + 1) * dv]


def _norm_proj(x, nw, w, kv=None):
    m, d = x.shape
    n = w.shape[1]
    emit_w = w.dtype != BF16
    tm = _pick(m, (1024, 512, 256, 128, 64, 32, 16, 8))
    tn = _pick(n, (1024, 512, 256, 128))
    out_specs = [pl.BlockSpec((tm, tn), lambda i, j: (i, j))]
    out_shape = [jax.ShapeDtypeStruct((m, n), F32)]
    jk = jv = v_heads = None
    if kv is not None:
        kcol, vcol, width, seqlen, v_heads = kv
        tn = width
        assert n % tn == 0 and kcol % tn == 0 and vcol % tn == 0 and seqlen % tm == 0
        jk, jv = kcol // tn, vcol // tn
        nt = seqlen // tm
        out_specs = [pl.BlockSpec((tm, tn), lambda i, j: (i, j)),
                     pl.BlockSpec((tn, tm), lambda i, j: (i // nt, i % nt)),
                     pl.BlockSpec((tm * v_heads, tn // v_heads), lambda i, j: (i, 0))]
        out_shape += [jax.ShapeDtypeStruct((m // seqlen * tn, seqlen), F32),
                      jax.ShapeDtypeStruct((m * v_heads, tn // v_heads), F32)]
    if emit_w:
        out_specs.append(pl.BlockSpec((d, tn), lambda i, j: (0, j)))
        out_shape.append(jax.ShapeDtypeStruct((d, n), BF16))
        assert m == tm, "the bf16 weight copy is written once per column tile"
    return pl.pallas_call(
        functools.partial(_norm_proj_kernel, jk=jk, jv=jv, v_heads=v_heads, emit_w=emit_w),
        grid=(m // tm, n // tn),
        in_specs=[pl.BlockSpec((tm, d), lambda i, j: (i, 0)),
                  pl.BlockSpec((1, d), lambda i, j: (0, 0)),
                  pl.BlockSpec((d, tn), lambda i, j: (0, j))],
        out_specs=out_specs,
        out_shape=out_shape,
        scratch_shapes=[pltpu.VMEM((tm, d), BF16)],
        compiler_params=_params("parallel", "arbitrary"),
        name="norm_proj",
    )(x, nw, w)


def _hgrn_kernel(*refs, heads, dk, dv, chunk, rows, has_s0):
    if has_s0:
        q_ref, f_ref, i_ref, gate_ref, lb_ref, nw_ref, s0_ref, o_ref, sout_ref, st_ref = refs
    else:
        q_ref, f_ref, i_ref, gate_ref, lb_ref, nw_ref, o_ref, sout_ref, st_ref = refs
    t = pl.program_id(1)
    real = q_ref.shape[0]

    @pl.when(t == 0)
    def _():
        for h in range(heads):
            if has_s0:
                st_ref[h] = s0_ref[h].T
            else:
                st_ref[h] = jnp.zeros((dv, dk), F32)

    lb = lb_ref[...]
    f = lb + (1.0 - lb) * _sigmoid(f_ref[...])
    g = jnp.log(f)
    kin = 1.0 - f
    q = q_ref[...]
    v = i_ref[...]
    if rows > real:
        def pad(a):
            return jnp.concatenate([a, jnp.zeros((rows - real, a.shape[1]), F32)], axis=0)
        g, kin, q, v = pad(g), pad(kin), pad(q), pad(v)

    cs = math.gcd(rows, 2 * LANES)
    r = lax.broadcasted_iota(jnp.int32, (cs, cs), 0)
    c = lax.broadcasted_iota(jnp.int32, (cs, cs), 1)
    within = (r // chunk == c // chunk) & (c <= r)
    tri = jnp.where(within, 1.0, 0.0).astype(BF16)
    g_hi = g.astype(BF16)
    g_r1 = g - g_hi.astype(F32)
    g_mid = g_r1.astype(BF16)
    g_lo = (g_r1 - g_mid.astype(F32)).astype(BF16)
    G = jnp.concatenate(
        [jnp.dot(tri, g_hi[i:i + cs], preferred_element_type=F32)
         + jnp.dot(tri, g_mid[i:i + cs], preferred_element_type=F32)
         + jnp.dot(tri, g_lo[i:i + cs], preferred_element_type=F32)
         for i in range(0, rows, cs)], axis=0)

    qg = (q * jnp.exp(G)).astype(BF16)
    kg = (kin * jnp.exp(-G)).astype(BF16)
    vb = v.astype(BF16)
    nw = nw_ref[...]

    states = [st_ref[h] for h in range(heads)]
    for g0 in range(0, rows, cs):
        intra = []
        for h in range(heads):
            sk = slice(h * dk, (h + 1) * dk)
            a = lax.dot_general(qg[g0:g0 + cs, sk], kg[g0:g0 + cs, sk], _NT,
                                preferred_element_type=F32)
            a = jnp.where(within, a, 0.0).astype(BF16)
            intra.append(jnp.dot(a, vb[g0:g0 + cs, h * dv:(h + 1) * dv],
                                 preferred_element_type=F32))
        for ci in range(cs // chunk):
            lo = g0 + ci * chunk
            Gc = G[lo:lo + chunk]
            Gl = Gc[chunk - 1:chunk]
            kdec = (kin[lo:lo + chunk] * jnp.exp(Gl - Gc)).astype(BF16)
            decay = jnp.exp(Gl)
            n_out = min(chunk, real - lo)
            for h in range(heads):
                sk = slice(h * dk, (h + 1) * dk)
                sv = slice(h * dv, (h + 1) * dv)
                st = states[h]
                o = intra[h][ci * chunk:(ci + 1) * chunk] + lax.dot_general(
                    qg[lo:lo + chunk, sk], st.astype(BF16), _NT, preferred_element_type=F32)
                states[h] = st * decay[:, sk] + lax.dot_general(
                    vb[lo:lo + chunk, sv], kdec[:, sk], _TN, preferred_element_type=F32)
                if n_out > 0:
                    gt = gate_ref[lo:lo + n_out, sv]
                    on = _rms(o[:n_out], nw) * (gt * _sigmoid(gt))
                    o_ref[lo:lo + n_out, sv] = on.astype(o_ref.dtype)
    for h in range(heads):
        st_ref[h] = states[h]

    @pl.when(t == pl.num_programs(1) - 1)
    def _():
        for h in range(heads):
            sout_ref[h] = st_ref[h].T


def _hgrn(u, col0, lb, nw, s0, batch, seqlen, heads, dk, dv):
    width = heads * dk
    assert dk == dv and col0 % width == 0
    cb = col0 // width
    chunk = min(HG_CHUNK, seqlen)
    if seqlen >= LANES:
        tb = _pick(seqlen, (512, 256, 128))
        rows = tb
        assert tb % chunk == 0
    else:
        tb = seqlen
        rows = LANES
        chunk = LANES
    nt = seqlen // tb
    has_s0 = s0 is not None

    def col(k):
        return pl.BlockSpec((tb, width), lambda b, t: (b * nt + t, cb + k))

    in_specs = [col(0), col(1), col(2), col(3),
                pl.BlockSpec((1, width), lambda b, t: (0, 0)),
                pl.BlockSpec((1, dv), lambda b, t: (0, 0))]
    args = [u, u, u, u, lb, nw]
    if has_s0:
        in_specs.append(pl.BlockSpec((None, heads, dk, dv), lambda b, t: (b, 0, 0, 0)))
        args.append(s0)
    kern = functools.partial(_hgrn_kernel, heads=heads, dk=dk, dv=dv, chunk=chunk, rows=rows,
                             has_s0=has_s0)
    return pl.pallas_call(
        kern,
        grid=(batch, nt),
        in_specs=in_specs,
        out_specs=[pl.BlockSpec((tb, heads * dv), lambda b, t: (b * nt + t, 0)),
                   pl.BlockSpec((None, heads, dk, dv), lambda b, t: (b, 0, 0, 0))],
        out_shape=[jax.ShapeDtypeStruct((batch * seqlen, heads * dv), BF16),
                   jax.ShapeDtypeStruct((batch, heads, dk, dv), F32)],
        scratch_shapes=[pltpu.VMEM((heads, dv, dk), F32)],
        compiler_params=_params("parallel", "arbitrary"),
        name="hgrn",
    )(*args)


def _rel_bias(dist, value, n_buckets):
    n = jnp.maximum(dist, 0)
    max_exact = n_buckets // 2
    nf = jnp.maximum(n, 1).astype(F32)
    x = (jnp.log(nf / max_exact) / math.log(REL_MAX_DISTANCE / max_exact)
         * (n_buckets - max_exact))
    bias = jnp.zeros(dist.shape, F32)
    for k in range(n_buckets - 2, max_exact - 1, -1):
        bias = jnp.where(x < k - max_exact + 1, value(k), bias)
    for k in range(max_exact - 1, -1, -1):
        bias = jnp.where(n <= k, value(k), bias)
    return bias


def _far_start(n_buckets):
    n = np.arange(1, 8 * REL_MAX_DISTANCE, dtype=np.int64)
    max_exact = n_buckets // 2
    nf = n.astype(np.float32)
    large = max_exact + (np.log(nf / np.float32(max_exact)) / np.float32(math.log(REL_MAX_DISTANCE / max_exact))
                         * np.float32(n_buckets - max_exact)).astype(np.int32)
    b = np.where(n < max_exact, n, np.minimum(large, n_buckets - 1))
    below = n[b < n_buckets - 1]
    return int(below.max()) + 2


def _attn_kernel(*refs, **kw):
    _attn_body(pl.program_id(0), pl.program_id(1) == 0, None, *refs, **kw)


def _attn_body(hp, init, qb_only, tbl_ref, lam_ref, nw_ref, q1_ref, q2_ref, k1_ref, k2_ref, v_ref,
               o_ref, bias_ref, m_ref, acc_ref, *, t, dh, dv, heads, n_buckets, out_scale,
               alongside=None):
    @pl.when(init)
    def _():
        row = lax.broadcasted_iota(jnp.int32, (t, t), 0)
        col = lax.broadcasted_iota(jnp.int32, (t, t), 1)
        for e in range(2):
            for mp in range(2):
                hcol = mp * heads + 2 * hp + e
                far = tbl_ref[n_buckets - 1, hcol]
                value = lambda k: tbl_ref[k, hcol] - far
                bd = _rel_bias(row - col, value, n_buckets)
                bn = _rel_bias(row - col + t, value, n_buckets)
                rs = slice((2 * e + mp) * t, (2 * e + mp + 1) * t)
                bias_ref[rs, :t] = bn
                bias_ref[rs, t:] = jnp.where(col <= row, bd, NEG_INF)

    lane = lax.broadcasted_iota(jnp.int32, (t, 2 * dh), 1)
    scale = dh ** -0.5
    zero = jnp.zeros((t, 2 * dh), F32)
    lam = lam_ref[0, 0]
    nw = nw_ref[...]

    def q_block(qb, carry):
        rows = slice(None) if qb_only is not None else pl.ds(pl.multiple_of(qb * t, t), t)
        q_maps = [q1_ref[rows, :] * scale, q2_ref[rows, :] * scale]
        blocks = []
        for e in range(2):
            in_e = (lane >= e * dh) & (lane < (e + 1) * dh)
            for mp in range(2):
                qe = jnp.where(in_e, q_maps[mp], 0.0)
                blocks.append(jnp.concatenate([qe, zero] if mp == 0 else [zero, qe], axis=1))
        q_all = jnp.concatenate(blocks, axis=0).astype(BF16)

        m_ref[...] = jnp.full(m_ref.shape, NEG_INF, F32)
        acc_ref[...] = jnp.zeros(acc_ref.shape, F32)

        def step(j, n, bias=None):
            start = pl.multiple_of(j * t, t)
            kb = jnp.concatenate([k1_ref[:, pl.ds(start, n * t)], k2_ref[:, pl.ds(start, n * t)]],
                                 axis=0).astype(BF16)
            s = jnp.dot(q_all, kb, preferred_element_type=F32)
            if bias is not None:
                s = s + bias
            m_prev = m_ref[...]
            m_new = jnp.maximum(m_prev, jnp.max(s, axis=1, keepdims=True))
            pb = jnp.exp(s - jnp.tile(m_new, (1, n * t // LANES))).astype(BF16)
            alpha = jnp.tile(jnp.exp(m_prev - m_new), (1, 2))
            m_ref[...] = m_new
            ones = jnp.ones((n * t, dv), BF16)
            for e in range(2):
                ve = v_ref[pl.ds(start, n * t), e * dv:(e + 1) * dv].astype(BF16)
                rs = slice(2 * e * t, 2 * (e + 1) * t)
                acc_ref[rs, :] = acc_ref[rs, :] * alpha[rs] + jnp.dot(
                    pb[rs], jnp.concatenate([ve, ones], axis=1), preferred_element_type=F32)

        n_far = jnp.maximum(qb - 1, 0)

        def far_body(i, c):
            step(2 * i, 2)
            return c

        lax.fori_loop(0, n_far // 2, far_body, 0)

        @pl.when(n_far % 2 == 1)
        def _():
            step(n_far - 1, 1)

        @pl.when(qb > 0)
        def _():
            step(qb - 1, 2, bias_ref[...])
            if alongside is not None:
                alongside()

        @pl.when(qb == 0)
        def _():
            step(0, 1, bias_ref[:, t:])
            if alongside is not None:
                alongside()

        o = acc_ref[:, :dv] / acc_ref[:, dv:]
        for e in range(2):
            oe = o[2 * e * t:(2 * e + 1) * t] - lam * o[(2 * e + 1) * t:(2 * e + 2) * t]
            o_ref[rows, e * dv:(e + 1) * dv] = (_rms(oe, nw) * out_scale).astype(o_ref.dtype)
        return carry

    if qb_only is None:
        lax.fori_loop(0, q1_ref.shape[0] // t, q_block, 0)
    else:
        q_block(qb_only, 0)


def _attn_prompt(u, qcol, vcol, kt, tbl, lam, nw, batch, seqlen, heads, dh, dv, out_scale):
    assert dv == LANES and 2 * dh == LANES and heads % 2 == 0
    n_buckets = tbl.shape[0]
    t = _pick(seqlen, (256, 128))
    assert seqlen % t == 0 and t % LANES == 0 and t + 1 >= _far_start(n_buckets)
    nhp = heads // 2
    qc, vc = qcol // LANES, vcol // (2 * dv)
    smem = pl.BlockSpec(memory_space=pltpu.SMEM)
    kern = functools.partial(_attn_kernel, t=t, dh=dh, dv=dv, heads=heads, n_buckets=n_buckets,
                             out_scale=out_scale)
    return pl.pallas_call(
        kern,
        grid=(nhp, batch),
        in_specs=[smem, smem,
                  pl.BlockSpec((1, dv), lambda h, b: (0, 0)),
                  pl.BlockSpec((seqlen, LANES), lambda h, b: (b, qc + h)),
                  pl.BlockSpec((seqlen, LANES), lambda h, b: (b, qc + nhp + h)),
                  pl.BlockSpec((2 * dh, seqlen), lambda h, b: (b * 2 * nhp + h, 0)),
                  pl.BlockSpec((2 * dh, seqlen), lambda h, b: (b * 2 * nhp + nhp + h, 0)),
                  pl.BlockSpec((seqlen, 2 * dv), lambda h, b: (b, vc + h))],
        out_specs=pl.BlockSpec((seqlen, 2 * dv), lambda h, b: (b, h)),
        out_shape=jax.ShapeDtypeStruct((batch * seqlen, heads * dv), BF16),
        scratch_shapes=[pltpu.VMEM((4 * t, 2 * t), F32),
                        pltpu.VMEM((4 * t, LANES), F32), pltpu.VMEM((4 * t, 2 * dv), F32)],
        compiler_params=_params("arbitrary", "arbitrary"),
        name="attn_prompt",
    )(tbl, lam, nw, u, u, kt, kt, u)


def _decode_kernel(pt_ref, *refs, **kw):
    first = (pl.program_id(0) == 0) & (pl.program_id(1) == 0)
    _decode_body(first, pl.program_id(1), pl.num_programs(1), *refs, **kw)


def _decode_body(first, g, ng, lam_ref, trow_ref, nw_ref, q_ref, kn_ref, vn_ref, *rest,
                 pages, page, lq, heads, dh, dv, n_buckets, past_len, near_slots, out_scale,
                 phases=("pre", "main", "post")):
    k_refs = rest[:pages]
    v_refs = rest[pages:2 * pages]
    o_ref, m_ref, l_ref, acc_ref, bias_ref = rest[2 * pages:]
    mh = 2 * heads
    rows = mh * lq
    n_near = len(near_slots)
    order = [m * heads + h for h in range(heads) for m in range(2)]

    def near_bias(key0, n_valid):
        r = lax.broadcasted_iota(jnp.int32, (rows, page), 0)
        col = lax.broadcasted_iota(jnp.int32, (rows, page), 1)
        dist = past_len + r % lq - (key0 + col)
        far = trow_ref[:, n_buckets - 1:n_buckets]
        bias = _rel_bias(dist, lambda k: trow_ref[:, k:k + 1] - far, n_buckets)
        return jnp.where((dist >= 0) & (col < n_valid), bias, NEG_INF)

    if "pre" in phases:
        @pl.when(first)
        def _():
            for i, slot in enumerate(near_slots):
                bias_ref[i] = near_bias(past_len - (pages - slot) * page, page)
            bias_ref[n_near] = near_bias(past_len, lq)

        @pl.when(g == 0)
        def _():
            m_ref[...] = jnp.full(m_ref.shape, NEG_INF, F32)
            l_ref[...] = jnp.zeros(l_ref.shape, F32)
            acc_ref[...] = jnp.zeros(acc_ref.shape, F32)

    q = q_ref[...] * (dh ** -0.5)
    q_parts = [q[:, j * dh:(j + 1) * dh].astype(BF16) for j in order]

    def attend(score, value_rows, bias):
        s = jnp.concatenate([score(qj, j) for qj, j in zip(q_parts, order)], axis=0)
        if bias is not None:
            s = s + bias
        m_prev = m_ref[...]
        m_new = jnp.maximum(m_prev, jnp.max(s, axis=1, keepdims=True))
        p = jnp.exp(s - jnp.tile(m_new, (1, s.shape[1] // LANES)))
        alpha = jnp.exp(m_prev - m_new)
        l_ref[...] = alpha * l_ref[...] + jnp.sum(p, axis=1, keepdims=True)
        m_ref[...] = m_new
        pv = jnp.concatenate(
            [jnp.dot(p[2 * lq * h:2 * lq * (h + 1)].astype(BF16), value_rows(h).astype(BF16),
                     preferred_element_type=F32) for h in range(heads)], axis=0)
        acc_ref[...] = acc_ref[...] * alpha + pv

    is_last = g == ng - 1

    def page_score(qj, j):
        kt = jnp.concatenate([r[j * dh:(j + 1) * dh, :] for r in k_refs], axis=1)
        return jnp.dot(qj, kt.astype(BF16), preferred_element_type=F32)

    if "main" in phases:
        bias = None
        if near_slots:
            zero = jnp.zeros((rows, page), F32)
            bias = jnp.concatenate(
                [jnp.where(is_last, bias_ref[near_slots.index(s)], 0.0) if s in near_slots
                 else zero for s in range(pages)], axis=1)
        attend(page_score,
               lambda h: jnp.concatenate([r[pl.ds(h, page, stride=heads), :] for r in v_refs],
                                         axis=0),
               bias)

    if "post" not in phases:
        return

    @pl.when(is_last)
    def _():
        kn = kn_ref[...]
        vn = vn_ref[...]
        zk = jnp.zeros((page - lq, dh), F32)
        zv = jnp.zeros((page - lq, dv), F32)

        def new_score(qj, j):
            kj = jnp.concatenate([kn[:, j * dh:(j + 1) * dh], zk], axis=0).astype(BF16)
            return lax.dot_general(qj, kj, _NT, preferred_element_type=F32)

        attend(new_score,
               lambda h: jnp.concatenate([vn[:, h * dv:(h + 1) * dv], zv], axis=0),
               bias_ref[n_near])
        full = acc_ref[...] / l_ref[...]
        lam = lam_ref[0, 0]
        nw = nw_ref[...]
        for h in range(heads):
            o0 = full[2 * lq * h:2 * lq * h + lq]
            o1 = full[2 * lq * h + lq:2 * lq * (h + 1)]
            o_ref[:, h * dv:(h + 1) * dv] = (_rms(o0 - lam * o1, nw) * out_scale).astype(o_ref.dtype)


def _attn_decode(u, qcol, kcol, vcol, cache_k, cache_v, page_table, tbl, lam, nw,
                 batch, lq, heads, dh, dv, out_scale):
    n_pool, page = cache_k.shape[0], cache_k.shape[1]
    n_pages = page_table.shape[1]
    past_len = n_pages * page
    mh = 2 * heads
    kw, vw = mh * dh, heads * dv
    assert page % LANES == 0 and kw % LANES == 0 and vw % LANES == 0 and lq % 8 == 0
    n_buckets = tbl.shape[0]
    pages = next(p for p in (16, 8, 4, 2, 1)
                 if _decode_geometry(n_pages, page, n_buckets, p) is not None)
    near_slots = _decode_geometry(n_pages, page, n_buckets, pages)
    ck = jnp.transpose(cache_k, (0, 2, 3, 1)).reshape(n_pool * mh * dh, page)
    cv = cache_v.reshape(n_pool * page * heads, dv)
    trow = jnp.repeat(tbl.T.reshape(2, heads, -1).transpose(1, 0, 2).reshape(mh, -1), lq, axis=0)
    rows = mh * lq

    def page_spec(n_rows, width, slot):
        return pl.BlockSpec((n_rows, width),
                            lambda b, g, pt: (pt[b * n_pages + g * pages + slot], 0))

    smem = pl.BlockSpec(memory_space=pltpu.SMEM)
    in_specs = ([smem,
                 pl.BlockSpec((rows, n_buckets), lambda b, g, pt: (0, 0)),
                 pl.BlockSpec((1, dv), lambda b, g, pt: (0, 0)),
                 pl.BlockSpec((lq, kw), lambda b, g, pt: (b, qcol // kw)),
                 pl.BlockSpec((lq, kw), lambda b, g, pt: (b, kcol // kw)),
                 pl.BlockSpec((lq, vw), lambda b, g, pt: (b, vcol // vw))]
                + [page_spec(mh * dh, page, s) for s in range(pages)]
                + [page_spec(page * heads, dv, s) for s in range(pages)])
    assert qcol % kw == 0 and kcol % kw == 0 and vcol % vw == 0
    kern = functools.partial(_decode_kernel, pages=pages, page=page, lq=lq, heads=heads, dh=dh,
                             dv=dv, n_buckets=n_buckets, past_len=past_len,
                             near_slots=near_slots, out_scale=out_scale)
    grid_spec = pltpu.PrefetchScalarGridSpec(
        num_scalar_prefetch=1,
        grid=(batch, n_pages // pages),
        in_specs=in_specs,
        out_specs=pl.BlockSpec((lq, vw), lambda b, g, pt: (b, 0)),
        scratch_shapes=[pltpu.VMEM((rows, LANES), F32), pltpu.VMEM((rows, LANES), F32),
                        pltpu.VMEM((rows, dv), F32),
                        pltpu.VMEM((len(near_slots) + 1, rows, page), F32)])
    return pl.pallas_call(
        kern,
        grid_spec=grid_spec,
        out_shape=jax.ShapeDtypeStruct((batch * lq, vw), BF16),
        compiler_params=_params("arbitrary", "arbitrary"),
        name="attn_decode",
    )(page_table.reshape(-1), lam, trow, nw, u, u, u, *([ck] * pages), *([cv] * pages))


def _decode_geometry(n_pages, page, n_buckets, pages):
    past_len = n_pages * page
    far = _far_start(n_buckets)
    first_near = max(0, -(-(past_len - page + 1 - far + 1) // page))
    if pages < 1 or n_pages % pages or n_pages - pages > first_near:
        return None
    return tuple(s for s in range(pages) if n_pages - pages + s >= first_near)


def _fused_pages(batch_p, seqlen, heads, batch_s, n_pages):
    steps = (heads // 2) * batch_p * (seqlen // _pick(seqlen, (256, 128)))
    total = batch_s * n_pages
    return total // steps if total % steps == 0 else 0


def _attn_fused_kernel(pt_ref, *refs, n_dec_in, nb, nq, ng, attn_kw, dec_kw):
    hp, b, qb = pl.program_id(0), pl.program_id(1), pl.program_id(2)
    step = (hp * nb + b) * nq + qb
    a_in, rest = refs[:8], refs[8:]
    d_in, rest = rest[:n_dec_in], rest[n_dec_in:]
    o_attn, o_dec = rest[:2]
    a_scr, d_scr = rest[2:5], rest[5:]
    decode = functools.partial(_decode_body, step == 0, step % ng, ng, *d_in, o_dec, *d_scr,
                               **dec_kw)
    decode(phases=("pre",))
    _attn_body(hp, (b == 0) & (qb == 0), qb, *a_in, o_attn, *a_scr, **attn_kw,
               alongside=functools.partial(decode, phases=("main",)))
    decode(phases=("post",))


def _attn_fused(up, kt, us, qcol, kcol, vcol, cache_k, cache_v, page_table, tbl, lam, nw,
                batch_p, seqlen, batch_s, lq, heads, dh, dv, out_scale, pages):
    assert dv == LANES and 2 * dh == LANES and heads % 2 == 0
    n_buckets = tbl.shape[0]
    t = _pick(seqlen, (256, 128))
    assert seqlen % t == 0 and t % LANES == 0 and t + 1 >= _far_start(n_buckets)
    nq, nhp = seqlen // t, heads // 2
    qc, vc = qcol // LANES, vcol // (2 * dv)
    n_pool, page = cache_k.shape[0], cache_k.shape[1]
    n_pages = page_table.shape[1]
    ng = n_pages // pages
    mh = 2 * heads
    kw, vw = mh * dh, heads * dv
    near_slots = _decode_geometry(n_pages, page, n_buckets, pages)
    assert near_slots is not None and nhp * batch_p * nq == batch_s * ng
    assert page % LANES == 0 and lq % 8 == 0
    assert qcol % kw == 0 and kcol % kw == 0 and vcol % vw == 0
    ck = jnp.transpose(cache_k, (0, 2, 3, 1)).reshape(n_pool * mh * dh, page)
    cv = cache_v.reshape(n_pool * page * heads, dv)
    trow = jnp.repeat(tbl.T.reshape(2, heads, -1).transpose(1, 0, 2).reshape(mh, -1), lq, axis=0)
    rows = mh * lq

    def seq_grp(h, b, i):
        step = (h * batch_p + b) * nq + i
        return step // ng, step % ng

    def page_spec(n_rows, width, slot):
        def index(h, b, i, pt):
            seq, grp = seq_grp(h, b, i)
            return pt[seq * n_pages + grp * pages + slot], 0
        return pl.BlockSpec((n_rows, width), index)

    def sample_rows(width, col):
        return pl.BlockSpec((lq, width), lambda h, b, i, pt: (seq_grp(h, b, i)[0], col))

    smem = pl.BlockSpec(memory_space=pltpu.SMEM)
    const = lambda h, b, i, pt: (0, 0)
    attn_specs = [smem, smem, pl.BlockSpec((1, dv), const),
                  pl.BlockSpec((t, LANES), lambda h, b, i, pt: (b * nq + i, qc + h)),
                  pl.BlockSpec((t, LANES), lambda h, b, i, pt: (b * nq + i, qc + nhp + h)),
                  pl.BlockSpec((2 * dh, seqlen), lambda h, b, i, pt: (b * 2 * nhp + h, 0)),
                  pl.BlockSpec((2 * dh, seqlen), lambda h, b, i, pt: (b * 2 * nhp + nhp + h, 0)),
                  pl.BlockSpec((seqlen, 2 * dv), lambda h, b, i, pt: (b, vc + h))]
    dec_specs = ([smem, pl.BlockSpec((rows, n_buckets), const), pl.BlockSpec((1, dv), const),
                  sample_rows(kw, qcol // kw), sample_rows(kw, kcol // kw),
                  sample_rows(vw, vcol // vw)]
                 + [page_spec(mh * dh, page, s) for s in range(pages)]
                 + [page_spec(page * heads, dv, s) for s in range(pages)])
    attn_kw = dict(t=t, dh=dh, dv=dv, heads=heads, n_buckets=n_buckets, out_scale=out_scale)
    dec_kw = dict(pages=pages, page=page, lq=lq, heads=heads, dh=dh, dv=dv, n_buckets=n_buckets,
                  past_len=n_pages * page, near_slots=near_slots, out_scale=out_scale)
    kern = functools.partial(_attn_fused_kernel, n_dec_in=len(dec_specs), nb=batch_p, nq=nq,
                             ng=ng, attn_kw=attn_kw, dec_kw=dec_kw)
    grid_spec = pltpu.PrefetchScalarGridSpec(
        num_scalar_prefetch=1,
        grid=(nhp, batch_p, nq),
        in_specs=attn_specs + dec_specs,
        out_specs=[pl.BlockSpec((t, 2 * dv), lambda h, b, i, pt: (b * nq + i, h)),
                   pl.BlockSpec((lq, vw), lambda h, b, i, pt: (seq_grp(h, b, i)[0], 0))],
        scratch_shapes=[pltpu.VMEM((4 * t, 2 * t), F32), pltpu.VMEM((4 * t, LANES), F32),
                        pltpu.VMEM((4 * t, 2 * dv), F32),
                        pltpu.VMEM((rows, LANES), F32), pltpu.VMEM((rows, LANES), F32),
                        pltpu.VMEM((rows, dv), F32),
                        pltpu.VMEM((len(near_slots) + 1, rows, page), F32)])
    return pl.pallas_call(
        kern,
        grid_spec=grid_spec,
        out_shape=[jax.ShapeDtypeStruct((batch_p * seqlen, heads * dv), BF16),
                   jax.ShapeDtypeStruct((batch_s * lq, vw), BF16)],
        compiler_params=_params("arbitrary", "arbitrary", "arbitrary"),
        name="attn_fused",
    )(page_table.reshape(-1), tbl, lam, nw, up, up, kt, kt, up,
      lam, trow, nw, us, us, us, *([ck] * pages), *([cv] * pages))


def _merge_resident_kernel(*refs, n_g):
    ohg_ref, oda_ref = refs[:2]
    g_refs = refs[2:2 + 2 * n_g]
    x_ref, wbh_ref, wbd_ref, wo_ref, nw_ref, o_ref = refs[2 + 2 * n_g:]
    g_hg = jnp.concatenate([r[...] for r in g_refs[:n_g]], axis=1)
    g_da = jnp.concatenate([r[...] for r in g_refs[n_g:]], axis=1)
    y_hg = jnp.dot(ohg_ref[...], wbh_ref[...], preferred_element_type=F32)
    y_da = jnp.dot(oda_ref[...], wbd_ref[...], preferred_element_type=F32)
    mixed = (_sigmoid(g_hg) * y_hg + _sigmoid(g_da) * y_da).astype(BF16)
    z = jnp.dot(mixed, wo_ref[...], preferred_element_type=F32)
    o_ref[...] = x_ref[...] + _rms(z, nw_ref[...])


def _merge_resident(o_hg, o_da, u, gcol, x, wbh, wbd, wo, nw):
    m, d = x.shape
    w = o_hg.shape[1]
    gw = math.gcd(gcol, d)
    n_g = d // gw
    assert gw % LANES == 0
    tm = _pick(m, (256, 128, 64, 8))
    const = lambda i: (0, 0)
    g_specs = [pl.BlockSpec((tm, gw), functools.partial(lambda i, c: (i, c), c=gcol // gw + k))
               for k in range(2 * n_g)]
    return pl.pallas_call(
        functools.partial(_merge_resident_kernel, n_g=n_g),
        grid=(m // tm,),
        in_specs=[pl.BlockSpec((tm, w), lambda i: (i, 0)),
                  pl.BlockSpec((tm, w), lambda i: (i, 0))]
                 + g_specs
                 + [pl.BlockSpec((tm, d), lambda i: (i, 0)),
                    pl.BlockSpec((w, d), const, pipeline_mode=pl.Buffered(1)),
                    pl.BlockSpec((w, d), const, pipeline_mode=pl.Buffered(1)),
                    pl.BlockSpec((d, d), const, pipeline_mode=pl.Buffered(1)),
                    pl.BlockSpec((1, d), const)],
        out_specs=pl.BlockSpec((tm, d), lambda i: (i, 0)),
        out_shape=jax.ShapeDtypeStruct((m, d), F32),
        compiler_params=_params("parallel"),
        name="merge",
    )(o_hg, o_da, *([u] * (2 * n_g)), x, wbh, wbd, wo, nw)


def _merge_kernel(ohg_ref, oda_ref, ghg_ref, gda_ref, x_ref, wbh_ref, wbd_ref, wo_ref, nw_ref,
                  o_ref, *copies, emit_w):
    copies = copies if emit_w else (None, None, None)
    n = pl.program_id(1)

    @pl.when(n == 0)
    def _():
        o_ref[...] = jnp.zeros(o_ref.shape, F32)

    y_hg = jnp.dot(ohg_ref[...], _bf16_weight(wbh_ref, copies[0]), preferred_element_type=F32)
    y_da = jnp.dot(oda_ref[...], _bf16_weight(wbd_ref, copies[1]), preferred_element_type=F32)
    mixed = (_sigmoid(ghg_ref[...]) * y_hg + _sigmoid(gda_ref[...]) * y_da).astype(BF16)
    o_ref[...] += jnp.dot(mixed, _bf16_weight(wo_ref, copies[2]), preferred_element_type=F32)

    @pl.when(n == pl.num_programs(1) - 1)
    def _():
        o_ref[...] = x_ref[...] + _rms(o_ref[...], nw_ref[...])


def _merge(o_hg, o_da, u, gcol, x, wbh, wbd, wo, nw):
    m, d = x.shape
    w = o_hg.shape[1]
    emit_w = wo.dtype != BF16
    tm = _pick(m, (1024, 512, 256, 128, 64, 32, 16, 8))
    tn = _pick(math.gcd(gcol, d), (512, 256, 128))
    nn = d // tn
    gc = gcol // tn
    once = dict(pipeline_mode=pl.Buffered(1))
    out_specs = [pl.BlockSpec((tm, d), lambda i, n: (i, 0))]
    out_shape = [jax.ShapeDtypeStruct((m, d), F32)]
    if emit_w:
        assert m == tm, "the bf16 weight copies are written once per column tile"
        out_specs += [pl.BlockSpec((w, tn), lambda i, n: (0, n)),
                      pl.BlockSpec((w, tn), lambda i, n: (0, n)),
                      pl.BlockSpec((tn, d), lambda i, n: (n, 0))]
        out_shape += [jax.ShapeDtypeStruct((w, d), BF16), jax.ShapeDtypeStruct((w, d), BF16),
                      jax.ShapeDtypeStruct((d, d), BF16)]
    return pl.pallas_call(
        functools.partial(_merge_kernel, emit_w=emit_w),
        grid=(m // tm, nn),
        in_specs=[pl.BlockSpec((tm, w), lambda i, n: (i, 0), **once),
                  pl.BlockSpec((tm, w), lambda i, n: (i, 0), **once),
                  pl.BlockSpec((tm, tn), lambda i, n: (i, gc + n)),
                  pl.BlockSpec((tm, tn), lambda i, n: (i, gc + nn + n)),
                  pl.BlockSpec((tm, d), lambda i, n: (i, 0), **once),
                  pl.BlockSpec((w, tn), lambda i, n: (0, n)),
                  pl.BlockSpec((w, tn), lambda i, n: (0, n)),
                  pl.BlockSpec((tn, d), lambda i, n: (n, 0)),
                  pl.BlockSpec((1, d), lambda i, n: (0, 0))],
        out_specs=out_specs,
        out_shape=out_shape,
        compiler_params=_params("parallel", "arbitrary"),
        name="merge",
    )(o_hg, o_da, u, u, x, wbh, wbd, wo, nw)


def _ffn_kernel(x_ref, npre_ref, wg_ref, wu_ref, wd_ref, npost_ref, o_ref, *rest, emit_w):
    h_ref = rest[-1]
    copies = rest[:3] if emit_w else (None, None, None)
    j = pl.program_id(1)

    @pl.when(j == 0)
    def _():
        h_ref[...] = _rms(x_ref[...], npre_ref[...]).astype(BF16)
        o_ref[...] = jnp.zeros(o_ref.shape, F32)

    h = h_ref[...]
    gate = jnp.dot(h, _bf16_weight(wg_ref, copies[0]), preferred_element_type=F32)
    up = jnp.dot(h, _bf16_weight(wu_ref, copies[1]), preferred_element_type=F32)
    act = (gate * _sigmoid(gate) * up).astype(BF16)
    o_ref[...] += jnp.dot(act, _bf16_weight(wd_ref, copies[2]), preferred_element_type=F32)

    @pl.when(j == pl.num_programs(1) - 1)
    def _():
        o_ref[...] = x_ref[...] + _rms(o_ref[...], npost_ref[...])


def _ffn(x, npre, w_gate, w_up, w_down, npost):
    m, d = x.shape
    ff = w_down.shape[0]
    emit_w = w_down.dtype != BF16
    tm = _pick(m, (1024, 512, 256, 128, 64, 32, 16, 8))
    tf = _pick(ff, (512, 256, 128))
    nf = ff // tf
    (wg, gcol), (wu, ucol) = w_gate, w_up
    assert gcol % tf == 0 and ucol % tf == 0
    gc, uc = gcol // tf, ucol // tf
    out_specs = [pl.BlockSpec((tm, d), lambda i, j: (i, 0))]
    out_shape = [jax.ShapeDtypeStruct((m, d), F32)]
    if emit_w:
        assert m == tm, "the bf16 weight copies are written once per hidden-dim tile"
        out_specs += [pl.BlockSpec((d, tf), lambda i, j: (0, j)),
                      pl.BlockSpec((d, tf), lambda i, j: (0, j)),
                      pl.BlockSpec((tf, d), lambda i, j: (j, 0))]
        out_shape += [jax.ShapeDtypeStruct((d, ff), BF16), jax.ShapeDtypeStruct((d, ff), BF16),
                      jax.ShapeDtypeStruct((ff, d), BF16)]
    return pl.pallas_call(
        functools.partial(_ffn_kernel, emit_w=emit_w),
        grid=(m // tm, nf),
        in_specs=[pl.BlockSpec((tm, d), lambda i, j: (i, 0)),
                  pl.BlockSpec((1, d), lambda i, j: (0, 0)),
                  pl.BlockSpec((d, tf), lambda i, j: (0, gc + j)),
                  pl.BlockSpec((d, tf), lambda i, j: (0, uc + j)),
                  pl.BlockSpec((tf, d), lambda i, j: (j, 0)),
                  pl.BlockSpec((1, d), lambda i, j: (0, 0))],
        out_specs=out_specs,
        out_shape=out_shape,
        scratch_shapes=[pltpu.VMEM((tm, d), BF16)],
        compiler_params=_params("parallel", "arbitrary"),
        name="ffn",
    )(x, npre, wg, wu, w_down, npost)


def _mixer_tail(x2, u, o_da, s0, lw, wts, dims, batch, seqlen, gcol):
    hg_heads, hg_dk, hg_dv = dims[:3]
    o_hg, s_new = _hgrn(u, 0, lw["lb"], lw["hg_norm_w"], s0, batch, seqlen, hg_heads, hg_dk, hg_dv)
    merge_args = (o_hg, o_da, u, gcol, x2, wts["w_branch_hg"], wts["w_branch_da"], wts["w_out"],
                  lw["norm_mix_post"])
    if wts["w_out"].dtype == BF16:
        x1 = _merge_resident(*merge_args)
    else:
        x1, wts["w_branch_hg"], wts["w_branch_da"], wts["w_out"] = _merge(*merge_args)
    y, *copy = _ffn(x1, lw["norm_ffn_pre"], wts["w_gate"], wts["w_up"], wts["w_down"],
                    lw["norm_ffn_post"])
    if copy:
        wts["w_gate"], wts["w_up"], wts["w_down"] = (copy[0], 0), (copy[1], 0), copy[2]
    return y, s_new, wts


def _layer(x_p, x_s, s0_s, cache_k, cache_v, page_table, lw, wts, dims):
    (bp, lp, d), (bs, ls, _) = x_p.shape, x_s.shape
    wts = dict(wts)
    hg_heads, hg_dk, hg_dv, da_heads, da_dh, da_dv = dims
    hg_w, da_qk, da_w = hg_heads * hg_dk, 2 * da_heads * da_dh, da_heads * da_dv
    assert da_qk == da_w
    cols = np.cumsum([0, hg_w, hg_w, hg_heads * hg_dv, hg_heads * hg_dv, da_qk, da_qk, da_w, d, d])
    qcol, kcol, vcol, gcol = int(cols[4]), int(cols[5]), int(cols[6]), int(cols[7])
    xp2, xs2 = x_p.reshape(bp * lp, d), x_s.reshape(bs * ls, d)

    us, wts["w_in"] = _norm_proj(xs2, lw["norm_mix_pre"], wts["w_in"])
    up, kt, vp2 = _norm_proj(xp2, lw["norm_mix_pre"], wts["w_in"],
                             kv=(kcol, vcol, da_qk, lp, da_heads))

    attn = (lw["tbl"], lw["lam"], lw["da_subln_w"])
    shape = (da_heads, da_dh, da_dv, lw["out_scale"])
    n_pages, page = page_table.shape[1], cache_k.shape[1]
    pages = _fused_pages(bp, lp, da_heads, bs, n_pages)
    if _decode_geometry(n_pages, page, lw["tbl"].shape[0], pages) is not None:
        o_da_p, o_da_s = _attn_fused(up, kt, us, qcol, kcol, vcol, cache_k, cache_v, page_table,
                                     *attn, bp, lp, bs, ls, *shape, pages)
    else:
        o_da_p = _attn_prompt(up, qcol, vcol, kt, *attn, bp, lp, *shape)
        o_da_s = _attn_decode(us, qcol, kcol, vcol, cache_k, cache_v, page_table, *attn,
                              bs, ls, *shape)

    y_s, s_s, wts = _mixer_tail(xs2, us, o_da_s, s0_s, lw, wts, dims, bs, ls, gcol)
    y_p, s_p, _ = _mixer_tail(xp2, up, o_da_p, None, lw, wts, dims, bp, lp, gcol)

    k_p = kt.reshape(bp, 2 * da_heads, da_dh, lp).transpose(0, 3, 1, 2)
    v_p = vp2.reshape(bp, lp, da_heads, da_dv)
    k_s = us[:, kcol:kcol + da_qk].reshape(bs, ls, 2 * da_heads, da_dh)
    v_s = us[:, vcol:vcol + da_w].reshape(bs, ls, da_heads, da_dv)
    return (y_p.reshape(bp, lp, d), y_s.reshape(bs, ls, d)), (k_p, v_p, s_p, k_s, v_s, s_s)


def kernel(x_prompt, x_sample, cache_k, cache_v, state_hgrn, page_table, norm_mix_pre, norm_mix_post, norm_ffn_pre, norm_ffn_post, w_in, hg_lb_logits, hg_norm_w, da_lambda_q1, da_lambda_k1, da_lambda_q2, da_lambda_k2, da_subln_w, rel_bias_table, w_branch_hg, w_branch_da, w_out, w_ffn_up, w_ffn_down):
    depth = w_in.shape[0]
    _, _, hg_heads, hg_dk, hg_dv = state_hgrn.shape
    da_heads, da_dv = cache_v.shape[3], cache_v.shape[4]
    da_dh = cache_k.shape[4]
    dims = (hg_heads, hg_dk, hg_dv, da_heads, da_dh, da_dv)
    lb_all = jnp.cumsum(jax.nn.softmax(hg_lb_logits.astype(F32), axis=0), axis=0)

    y_p, y_s = x_prompt, x_sample
    outs = [[] for _ in range(6)]
    for l in range(depth):
        lam_init = 0.8 - 0.6 * math.exp(-0.3 * l)
        lam = (jnp.exp(jnp.sum(da_lambda_q1[l] * da_lambda_k1[l]))
               - jnp.exp(jnp.sum(da_lambda_q2[l] * da_lambda_k2[l])) + lam_init)
        lw = {
            "norm_mix_pre": norm_mix_pre[l][None], "norm_mix_post": norm_mix_post[l][None],
            "norm_ffn_pre": norm_ffn_pre[l][None], "norm_ffn_post": norm_ffn_post[l][None],
            "lb": lb_all[l][None], "hg_norm_w": hg_norm_w[l][None],
            "da_subln_w": da_subln_w[l][None], "tbl": rel_bias_table.astype(F32),
            "lam": lam.reshape(1, 1).astype(F32), "out_scale": 1.0 - lam_init,
        }
        ff = w_ffn_down.shape[1]
        wts = {"w_in": w_in[l], "w_branch_hg": w_branch_hg[l], "w_branch_da": w_branch_da[l],
               "w_out": w_out[l], "w_gate": (w_ffn_up[l], 0), "w_up": (w_ffn_up[l], ff),
               "w_down": w_ffn_down[l]}
        (y_p, y_s), caches = _layer(y_p, y_s, state_hgrn[l], cache_k[l], cache_v[l], page_table,
                                    lw, wts, dims)
        for acc, val in zip(outs, caches):
            acc.append(val)
    return (y_p, y_s) + tuple(jnp.stack(o) for o in outs)
```

```python
import functools
import math

import jax
import jax.numpy as jnp
import numpy as np
from jax import lax
from jax.experimental import pallas as pl
from jax.experimental.pallas import tpu as pltpu

F32 = jnp.float32
BF16 = jnp.bfloat16

RMS_EPS = 1e-6
NEG_INF = -1e30
HG_CHUNK = 64
REL_MAX_DISTANCE = 128
LANES = 128
VMEM_LIMIT = 62 * 1024 * 1024

_NT = (((1,), (1,)), ((), ()))
_TN = (((0,), (0,)), ((), ()))


def _params(*sem):
    return pltpu.CompilerParams(dimension_semantics=sem, vmem_limit_bytes=VMEM_LIMIT)


def _rms(x, w):
    return x * lax.rsqrt(jnp.mean(x * x, axis=-1, keepdims=True) + RMS_EPS) * w


def _sigmoid(x):
    return 1.0 / (1.0 + jnp.exp(-x))


def _pick(n, prefs):
    for p in prefs:
        if n % p == 0:
            return p
    return n


def _bf16_weight(w_ref, copy_ref):
    w = w_ref[...]
    if copy_ref is not None:
        w = w.astype(BF16)
        copy_ref[...] = w
    return w


def _norm_proj_kernel(x_ref, nw_ref, w_ref, o_ref, *rest, jk, jv, v_heads, emit_w):
    h_ref = rest[-1]
    j = pl.program_id(1)

    @pl.when(j == 0)
    def _():
        h_ref[...] = _rms(x_ref[...], nw_ref[...]).astype(BF16)

    w = _bf16_weight(w_ref, rest[-2] if emit_w else None)
    res = jnp.dot(h_ref[...], w, preferred_element_type=F32)
    o_ref[...] = res
    if jk is not None:
        kt_ref, v_ref = rest[:2]
        tm, tn = res.shape

        @pl.when(j == jk)
        def _():
            kt_ref[...] = res.T

        @pl.when(j == jv)
        def _():
            dv = tn // v_heads
            for h in range(v_heads):
                v_ref[pl.ds(h, tm, stride=v_heads), :] = res[:, h * dv:(h + 1) * dv]


def _norm_proj(x, nw, w, kv=None):
    m, d = x.shape
    n = w.shape[1]
    emit_w = w.dtype != BF16
    tm = _pick(m, (1024, 512, 256, 128, 64, 32, 16, 8))
    tn = _pick(n, (1024, 512, 256, 128))
    out_specs = [pl.BlockSpec((tm, tn), lambda i, j: (i, j))]
    out_shape = [jax.ShapeDtypeStruct((m, n), F32)]
    jk = jv = v_heads = None
    if kv is not None:
        kcol, vcol, width, seqlen, v_heads = kv
        tn = width
        assert n % tn == 0 and kcol % tn == 0 and vcol % tn == 0 and seqlen % tm == 0
        jk, jv = kcol // tn, vcol // tn
        nt = seqlen // tm
        out_specs = [pl.BlockSpec((tm, tn), lambda i, j: (i, j)),
                     pl.BlockSpec((tn, tm), lambda i, j: (i // nt, i % nt)),
                     pl.BlockSpec((tm * v_heads, tn // v_heads), lambda i, j: (i, 0))]
        out_shape += [jax.ShapeDtypeStruct((m // seqlen * tn, seqlen), F32),
                      jax.ShapeDtypeStruct((m * v_heads, tn // v_heads), F32)]
    if emit_w:
        out_specs.append(pl.BlockSpec((d, tn), lambda i, j: (0, j)))
        out_shape.append(jax.ShapeDtypeStruct((d, n), BF16))
        assert m == tm, "the bf16 weight copy is written once per column tile"
    return pl.pallas_call(
        functools.partial(_norm_proj_kernel, jk=jk, jv=jv, v_heads=v_heads, emit_w=emit_w),
        grid=(m // tm, n // tn),
        in_specs=[pl.BlockSpec((tm, d), lambda i, j: (i, 0)),
                  pl.BlockSpec((1, d), lambda i, j: (0, 0)),
                  pl.BlockSpec((d, tn), lambda i, j: (0, j))],
        out_specs=out_specs,
        out_shape=out_shape,
        scratch_shapes=[pltpu.VMEM((tm, d), BF16)],
        compiler_params=_params("parallel", "arbitrary"),
        name="norm_proj",
    )(x, nw, w)


def _hgrn_kernel(*refs, heads, dk, dv, chunk, rows, has_s0):
    if has_s0:
        q_ref, f_ref, i_ref, gate_ref, lb_ref, nw_ref, s0_ref, o_ref, sout_ref, st_ref = refs
    else:
        q_ref, f_ref, i_ref, gate_ref, lb_ref, nw_ref, o_ref, sout_ref, st_ref = refs
    t = pl.program_id(1)
    real = q_ref.shape[0]

    @pl.when(t == 0)
    def _():
        for h in range(heads):
            if has_s0:
                st_ref[h] = s0_ref[h].T
            else:
                st_ref[h] = jnp.zeros((dv, dk), F32)

    lb = lb_ref[...]
    f = lb + (1.0 - lb) * _sigmoid(f_ref[...])
    g = jnp.log(f)
    kin = 1.0 - f
    q = q_ref[...]
    v = i_ref[...]
    if rows > real:
        def pad(a):
            return jnp.concatenate([a, jnp.zeros((rows - real, a.shape[1]), F32)], axis=0)
        g, kin, q, v = pad(g), pad(kin), pad(q), pad(v)

    cs = math.gcd(rows, 2 * LANES)
    r = lax.broadcasted_iota(jnp.int32, (cs, cs), 0)
    c = lax.broadcasted_iota(jnp.int32, (cs, cs), 1)
    within = (r // chunk == c // chunk) & (c <= r)
    tri = jnp.where(within, 1.0, 0.0).astype(BF16)
    g_hi = g.astype(BF16)
    g_r1 = g - g_hi.astype(F32)
    g_mid = g_r1.astype(BF16)
    g_lo = (g_r1 - g_mid.astype(F32)).astype(BF16)
    G = jnp.concatenate(
        [jnp.dot(tri, g_hi[i:i + cs], preferred_element_type=F32)
         + jnp.dot(tri, g_mid[i:i + cs], preferred_element_type=F32)
         + jnp.dot(tri, g_lo[i:i + cs], preferred_element_type=F32)
         for i in range(0, rows, cs)], axis=0)

    qg = (q * jnp.exp(G)).astype(BF16)
    kg = (kin * jnp.exp(-G)).astype(BF16)
    vb = v.astype(BF16)
    nw = nw_ref[...]

    states = [st_ref[h] for h in range(heads)]
    for g0 in range(0, rows, cs):
        intra = []
        for h in range(heads):
            sk = slice(h * dk, (h + 1) * dk)
            a = lax.dot_general(qg[g0:g0 + cs, sk], kg[g0:g0 + cs, sk], _NT,
                                preferred_element_type=F32)
            a = jnp.where(within, a, 0.0).astype(BF16)
            intra.append(jnp.dot(a, vb[g0:g0 + cs, h * dv:(h + 1) * dv],
                                 preferred_element_type=F32))
        for ci in range(cs // chunk):
            lo = g0 + ci * chunk
            Gc = G[lo:lo + chunk]
            Gl = Gc[chunk - 1:chunk]
            kdec = (kin[lo:lo + chunk] * jnp.exp(Gl - Gc)).astype(BF16)
            decay = jnp.exp(Gl)
            n_out = min(chunk, real - lo)
            for h in range(heads):
                sk = slice(h * dk, (h + 1) * dk)
                sv = slice(h * dv, (h + 1) * dv)
                st = states[h]
                o = intra[h][ci * chunk:(ci + 1) * chunk] + lax.dot_general(
                    qg[lo:lo + chunk, sk], st.astype(BF16), _NT, preferred_element_type=F32)
                states[h] = st * decay[:, sk] + lax.dot_general(
                    vb[lo:lo + chunk, sv], kdec[:, sk], _TN, preferred_element_type=F32)
                if n_out > 0:
                    gt = gate_ref[lo:lo + n_out, sv]
                    on = _rms(o[:n_out], nw) * (gt * _sigmoid(gt))
                    o_ref[lo:lo + n_out, sv] = on.astype(o_ref.dtype)
    for h in range(heads):
        st_ref[h] = states[h]

    @pl.when(t == pl.num_programs(1) - 1)
    def _():
        for h in range(heads):
            sout_ref[h] = st_ref[h].T


def _hgrn(u, col0, lb, nw, s0, batch, seqlen, heads, dk, dv):
    width = heads * dk
    assert dk == dv and col0 % width == 0
    cb = col0 // width
    chunk = min(HG_CHUNK, seqlen)
    if seqlen >= LANES:
        tb = _pick(seqlen, (512, 256, 128))
        rows = tb
        assert tb % chunk == 0
    else:
        tb = seqlen
        rows = LANES
        chunk = LANES
    nt = seqlen // tb
    has_s0 = s0 is not None

    def col(k):
        return pl.BlockSpec((tb, width), lambda b, t: (b * nt + t, cb + k))

    in_specs = [col(0), col(1), col(2), col(3),
                pl.BlockSpec((1, width), lambda b, t: (0, 0)),
                pl.BlockSpec((1, dv), lambda b, t: (0, 0))]
    args = [u, u, u, u, lb, nw]
    if has_s0:
        in_specs.append(pl.BlockSpec((None, heads, dk, dv), lambda b, t: (b, 0, 0, 0)))
        args.append(s0)
    kern = functools.partial(_hgrn_kernel, heads=heads, dk=dk, dv=dv, chunk=chunk, rows=rows,
                             has_s0=has_s0)
    return pl.pallas_call(
        kern,
        grid=(batch, nt),
        in_specs=in_specs,
        out_specs=[pl.BlockSpec((tb, heads * dv), lambda b, t: (b * nt + t, 0)),
                   pl.BlockSpec((None, heads, dk, dv), lambda b, t: (b, 0, 0, 0))],
        out_shape=[jax.ShapeDtypeStruct((batch * seqlen, heads * dv), BF16),
                   jax.ShapeDtypeStruct((batch, heads, dk, dv), F32)],
        scratch_shapes=[pltpu.VMEM((heads, dv, dk), F32)],
        compiler_params=_params("parallel", "arbitrary"),
        name="hgrn",
    )(*args)


def _rel_bias(dist, value, n_buckets):
    n = jnp.maximum(dist, 0)
    max_exact = n_buckets // 2
    nf = jnp.maximum(n, 1).astype(F32)
    x = (jnp.log(nf / max_exact) / math.log(REL_MAX_DISTANCE / max_exact)
         * (n_buckets - max_exact))
    bias = jnp.zeros(dist.shape, F32)
    for k in range(n_buckets - 2, max_exact - 1, -1):
        bias = jnp.where(x < k - max_exact + 1, value(k), bias)
    for k in range(max_exact - 1, -1, -1):
        bias = jnp.where(n <= k, value(k), bias)
    return bias


def _far_start(n_buckets):
    n = np.arange(1, 8 * REL_MAX_DISTANCE, dtype=np.int64)
    max_exact = n_buckets // 2
    nf = n.astype(np.float32)
    large = max_exact + (np.log(nf / np.float32(max_exact)) / np.float32(math.log(REL_MAX_DISTANCE / max_exact))
                         * np.float32(n_buckets - max_exact)).astype(np.int32)
    b = np.where(n < max_exact, n, np.minimum(large, n_buckets - 1))
    below = n[b < n_buckets - 1]
    return int(below.max()) + 2


def _attn_kernel(*refs, **kw):
    _attn_body(pl.program_id(0), pl.program_id(1) == 0, None, *refs, **kw)


def _attn_body(hp, init, qb_only, tbl_ref, lam_ref, nw_ref, q1_ref, q2_ref, k1_ref, k2_ref, v_ref,
               o_ref, bias_ref, m_ref, acc_ref, *, t, dh, dv, heads, n_buckets, out_scale):
    @pl.when(init)
    def _():
        row = lax.broadcasted_iota(jnp.int32, (t, t), 0)
        col = lax.broadcasted_iota(jnp.int32, (t, t), 1)
        for e in range(2):
            for mp in range(2):
                hcol = mp * heads + 2 * hp + e
                far = tbl_ref[n_buckets - 1, hcol]
                value = lambda k: tbl_ref[k, hcol] - far
                bd = _rel_bias(row - col, value, n_buckets)
                bn = _rel_bias(row - col + t, value, n_buckets)
                rs = slice((2 * e + mp) * t, (2 * e + mp + 1) * t)
                bias_ref[rs, :t] = bn
                bias_ref[rs, t:] = jnp.where(col <= row, bd, NEG_INF)

    lane = lax.broadcasted_iota(jnp.int32, (t, 2 * dh), 1)
    scale = dh ** -0.5
    zero = jnp.zeros((t, 2 * dh), F32)
    lam = lam_ref[0, 0]
    nw = nw_ref[...]

    def q_block(qb, carry):
        rows = slice(None) if qb_only is not None else pl.ds(pl.multiple_of(qb * t, t), t)
        q_maps = [q1_ref[rows, :] * scale, q2_ref[rows, :] * scale]
        blocks = []
        for e in range(2):
            in_e = (lane >= e * dh) & (lane < (e + 1) * dh)
            for mp in range(2):
                qe = jnp.where(in_e, q_maps[mp], 0.0)
                blocks.append(jnp.concatenate([qe, zero] if mp == 0 else [zero, qe], axis=1))
        q_all = jnp.concatenate(blocks, axis=0).astype(BF16)

        m_ref[...] = jnp.full(m_ref.shape, NEG_INF, F32)
        acc_ref[...] = jnp.zeros(acc_ref.shape, F32)

        def step(j, n, bias=None):
            start = pl.multiple_of(j * t, t)
            kb = jnp.concatenate([k1_ref[:, pl.ds(start, n * t)], k2_ref[:, pl.ds(start, n * t)]],
                                 axis=0).astype(BF16)
            s = jnp.dot(q_all, kb, preferred_element_type=F32)
            if bias is not None:
                s = s + bias
            m_prev = m_ref[...]
            m_new = jnp.maximum(m_prev, jnp.max(s, axis=1, keepdims=True))
            pb = jnp.exp(s - jnp.tile(m_new, (1, n * t // LANES))).astype(BF16)
            alpha = jnp.tile(jnp.exp(m_prev - m_new), (1, 2))
            m_ref[...] = m_new
            ones = jnp.ones((n * t, dv), BF16)
            for e in range(2):
                ve = v_ref[pl.ds(start, n * t), e * dv:(e + 1) * dv].astype(BF16)
                rs = slice(2 * e * t, 2 * (e + 1) * t)
                acc_ref[rs, :] = acc_ref[rs, :] * alpha[rs] + jnp.dot(
                    pb[rs], jnp.concatenate([ve, ones], axis=1), preferred_element_type=F32)

        n_far = jnp.maximum(qb - 1, 0)

        def far_body(i, c):
            step(2 * i, 2)
            return c

        lax.fori_loop(0, n_far // 2, far_body, 0)

        @pl.when(n_far % 2 == 1)
        def _():
            step(n_far - 1, 1)

        @pl.when(qb > 0)
        def _():
            step(qb - 1, 2, bias_ref[...])

        @pl.when(qb == 0)
        def _():
            step(0, 1, bias_ref[:, t:])

        o = acc_ref[:, :dv] / acc_ref[:, dv:]
        for e in range(2):
            oe = o[2 * e * t:(2 * e + 1) * t] - lam * o[(2 * e + 1) * t:(2 * e + 2) * t]
            o_ref[rows, e * dv:(e + 1) * dv] = (_rms(oe, nw) * out_scale).astype(o_ref.dtype)
        return carry

    if qb_only is None:
        lax.fori_loop(0, q1_ref.shape[0] // t, q_block, 0)
    else:
        q_block(qb_only, 0)


def _attn_prompt(u, qcol, vcol, kt, tbl, lam, nw, batch, seqlen, heads, dh, dv, out_scale):
    assert dv == LANES and 2 * dh == LANES and heads % 2 == 0
    n_buckets = tbl.shape[0]
    t = _pick(seqlen, (256, 128))
    assert seqlen % t == 0 and t % LANES == 0 and t + 1 >= _far_start(n_buckets)
    nhp = heads // 2
    qc, vc = qcol // LANES, vcol // (2 * dv)
    smem = pl.BlockSpec(memory_space=pltpu.SMEM)
    kern = functools.partial(_attn_kernel, t=t, dh=dh, dv=dv, heads=heads, n_buckets=n_buckets,
                             out_scale=out_scale)
    return pl.pallas_call(
        kern,
        grid=(nhp, batch),
        in_specs=[smem, smem,
                  pl.BlockSpec((1, dv), lambda h, b: (0, 0)),
                  pl.BlockSpec((seqlen, LANES), lambda h, b: (b, qc + h)),
                  pl.BlockSpec((seqlen, LANES), lambda h, b: (b, qc + nhp + h)),
                  pl.BlockSpec((2 * dh, seqlen), lambda h, b: (b * 2 * nhp + h, 0)),
                  pl.BlockSpec((2 * dh, seqlen), lambda h, b: (b * 2 * nhp + nhp + h, 0)),
                  pl.BlockSpec((seqlen, 2 * dv), lambda h, b: (b, vc + h))],
        out_specs=pl.BlockSpec((seqlen, 2 * dv), lambda h, b: (b, h)),
        out_shape=jax.ShapeDtypeStruct((batch * seqlen, heads * dv), BF16),
        scratch_shapes=[pltpu.VMEM((4 * t, 2 * t), F32),
                        pltpu.VMEM((4 * t, LANES), F32), pltpu.VMEM((4 * t, 2 * dv), F32)],
        compiler_params=_params("arbitrary", "arbitrary"),
        name="attn_prompt",
    )(tbl, lam, nw, u, u, kt, kt, u)


def _decode_kernel(pt_ref, *refs, **kw):
    first = (pl.program_id(0) == 0) & (pl.program_id(1) == 0)
    _decode_body(first, pl.program_id(1), pl.num_programs(1), *refs, **kw)


def _decode_body(first, g, ng, lam_ref, trow_ref, nw_ref, q_ref, kn_ref, vn_ref, *rest,
                 pages, page, lq, heads, dh, dv, n_buckets, past_len, near_slots, out_scale):
    k_refs = rest[:pages]
    v_refs = rest[pages:2 * pages]
    o_ref, m_ref, l_ref, acc_ref, bias_ref = rest[2 * pages:]
    mh = 2 * heads
    rows = mh * lq
    n_near = len(near_slots)
    order = [m * heads + h for h in range(heads) for m in range(2)]

    def near_bias(key0, n_valid):
        r = lax.broadcasted_iota(jnp.int32, (rows, page), 0)
        col = lax.broadcasted_iota(jnp.int32, (rows, page), 1)
        dist = past_len + r % lq - (key0 + col)
        far = trow_ref[:, n_buckets - 1:n_buckets]
        bias = _rel_bias(dist, lambda k: trow_ref[:, k:k + 1] - far, n_buckets)
        return jnp.where((dist >= 0) & (col < n_valid), bias, NEG_INF)

    @pl.when(first)
    def _():
        for i, slot in enumerate(near_slots):
            bias_ref[i] = near_bias(past_len - (pages - slot) * page, page)
        bias_ref[n_near] = near_bias(past_len, lq)

    @pl.when(g == 0)
    def _():
        m_ref[...] = jnp.full(m_ref.shape, NEG_INF, F32)
        l_ref[...] = jnp.zeros(l_ref.shape, F32)
        acc_ref[...] = jnp.zeros(acc_ref.shape, F32)

    q = q_ref[...] * (dh ** -0.5)
    q_parts = [q[:, j * dh:(j + 1) * dh].astype(BF16) for j in order]

    def attend(score, value_rows, bias):
        s = jnp.concatenate([score(qj, j) for qj, j in zip(q_parts, order)], axis=0)
        if bias is not None:
            s = s + bias
        m_prev = m_ref[...]
        m_new = jnp.maximum(m_prev, jnp.max(s, axis=1, keepdims=True))
        p = jnp.exp(s - jnp.tile(m_new, (1, s.shape[1] // LANES)))
        alpha = jnp.exp(m_prev - m_new)
        l_ref[...] = alpha * l_ref[...] + jnp.sum(p, axis=1, keepdims=True)
        m_ref[...] = m_new
        pv = jnp.concatenate(
            [jnp.dot(p[2 * lq * h:2 * lq * (h + 1)].astype(BF16), value_rows(h).astype(BF16),
                     preferred_element_type=F32) for h in range(heads)], axis=0)
        acc_ref[...] = acc_ref[...] * alpha + pv

    is_last = g == ng - 1
    bias = None
    if near_slots:
        zero = jnp.zeros((rows, page), F32)
        bias = jnp.concatenate(
            [jnp.where(is_last, bias_ref[near_slots.index(s)], 0.0) if s in near_slots else zero
             for s in range(pages)], axis=1)
    def page_score(qj, j):
        kt = jnp.concatenate([r[j * dh:(j + 1) * dh, :] for r in k_refs], axis=1)
        return jnp.dot(qj, kt.astype(BF16), preferred_element_type=F32)

    attend(page_score,
           lambda h: jnp.concatenate([r[pl.ds(h, page, stride=heads), :] for r in v_refs], axis=0),
           bias)

    @pl.when(is_last)
    def _():
        kn = kn_ref[...]
        vn = vn_ref[...]
        zk = jnp.zeros((page - lq, dh), F32)
        zv = jnp.zeros((page - lq, dv), F32)

        def new_score(qj, j):
            kj = jnp.concatenate([kn[:, j * dh:(j + 1) * dh], zk], axis=0).astype(BF16)
            return lax.dot_general(qj, kj, _NT, preferred_element_type=F32)

        attend(new_score,
               lambda h: jnp.concatenate([vn[:, h * dv:(h + 1) * dv], zv], axis=0),
               bias_ref[n_near])
        full = acc_ref[...] / l_ref[...]
        lam = lam_ref[0, 0]
        nw = nw_ref[...]
        for h in range(heads):
            o0 = full[2 * lq * h:2 * lq * h + lq]
            o1 = full[2 * lq * h + lq:2 * lq * (h + 1)]
            o_ref[:, h * dv:(h + 1) * dv] = (_rms(o0 - lam * o1, nw) * out_scale).astype(o_ref.dtype)


def _attn_decode(u, qcol, kcol, vcol, cache_k, cache_v, page_table, tbl, lam, nw,
                 batch, lq, heads, dh, dv, out_scale):
    n_pool, page = cache_k.shape[0], cache_k.shape[1]
    n_pages = page_table.shape[1]
    past_len = n_pages * page
    mh = 2 * heads
    kw, vw = mh * dh, heads * dv
    assert page % LANES == 0 and kw % LANES == 0 and vw % LANES == 0 and lq % 8 == 0
    n_buckets = tbl.shape[0]
    pages = next(p for p in (16, 8, 4, 2, 1)
                 if _decode_geometry(n_pages, page, n_buckets, p) is not None)
    near_slots = _decode_geometry(n_pages, page, n_buckets, pages)
    ck = jnp.transpose(cache_k, (0, 2, 3, 1)).reshape(n_pool * mh * dh, page)
    cv = cache_v.reshape(n_pool * page * heads, dv)
    trow = jnp.repeat(tbl.T.reshape(2, heads, -1).transpose(1, 0, 2).reshape(mh, -1), lq, axis=0)
    rows = mh * lq

    def page_spec(n_rows, width, slot):
        return pl.BlockSpec((n_rows, width),
                            lambda b, g, pt: (pt[b * n_pages + g * pages + slot], 0))

    smem = pl.BlockSpec(memory_space=pltpu.SMEM)
    in_specs = ([smem,
                 pl.BlockSpec((rows, n_buckets), lambda b, g, pt: (0, 0)),
                 pl.BlockSpec((1, dv), lambda b, g, pt: (0, 0)),
                 pl.BlockSpec((lq, kw), lambda b, g, pt: (b, qcol // kw)),
                 pl.BlockSpec((lq, kw), lambda b, g, pt: (b, kcol // kw)),
                 pl.BlockSpec((lq, vw), lambda b, g, pt: (b, vcol // vw))]
                + [page_spec(mh * dh, page, s) for s in range(pages)]
                + [page_spec(page * heads, dv, s) for s in range(pages)])
    assert qcol % kw == 0 and kcol % kw == 0 and vcol % vw == 0
    kern = functools.partial(_decode_kernel, pages=pages, page=page, lq=lq, heads=heads, dh=dh,
                             dv=dv, n_buckets=n_buckets, past_len=past_len,
                             near_slots=near_slots, out_scale=out_scale)
    grid_spec = pltpu.PrefetchScalarGridSpec(
        num_scalar_prefetch=1,
        grid=(batch, n_pages // pages),
        in_specs=in_specs,
        out_specs=pl.BlockSpec((lq, vw), lambda b, g, pt: (b, 0)),
        scratch_shapes=[pltpu.VMEM((rows, LANES), F32), pltpu.VMEM((rows, LANES), F32),
                        pltpu.VMEM((rows, dv), F32),
                        pltpu.VMEM((len(near_slots) + 1, rows, page), F32)])
    return pl.pallas_call(
        kern,
        grid_spec=grid_spec,
        out_shape=jax.ShapeDtypeStruct((batch * lq, vw), BF16),
        compiler_params=_params("arbitrary", "arbitrary"),
        name="attn_decode",
    )(page_table.reshape(-1), lam, trow, nw, u, u, u, *([ck] * pages), *([cv] * pages))


def _decode_geometry(n_pages, page, n_buckets, pages):
    past_len = n_pages * page
    far = _far_start(n_buckets)
    first_near = max(0, -(-(past_len - page + 1 - far + 1) // page))
    if pages < 1 or n_pages % pages or n_pages - pages > first_near:
        return None
    return tuple(s for s in range(pages) if n_pages - pages + s >= first_near)


def _fused_pages(batch_p, seqlen, heads, batch_s, n_pages):
    steps = (heads // 2) * batch_p * (seqlen // _pick(seqlen, (256, 128)))
    total = batch_s * n_pages
    return total // steps if total % steps == 0 else 0


def _attn_fused_kernel(pt_ref, *refs, n_dec_in, nb, nq, ng, attn_kw, dec_kw):
    hp, b, qb = pl.program_id(0), pl.program_id(1), pl.program_id(2)
    step = (hp * nb + b) * nq + qb
    a_in, rest = refs[:8], refs[8:]
    d_in, rest = rest[:n_dec_in], rest[n_dec_in:]
    o_attn, o_dec = rest[:2]
    a_scr, d_scr = rest[2:5], rest[5:]
    _attn_body(hp, (b == 0) & (qb == 0), qb, *a_in, o_attn, *a_scr, **attn_kw)
    _decode_body(step == 0, step % ng, ng, *d_in, o_dec, *d_scr, **dec_kw)


def _attn_fused(up, kt, us, qcol, kcol, vcol, cache_k, cache_v, page_table, tbl, lam, nw,
                batch_p, seqlen, batch_s, lq, heads, dh, dv, out_scale, pages):
    assert dv == LANES and 2 * dh == LANES and heads % 2 == 0
    n_buckets = tbl.shape[0]
    t = _pick(seqlen, (256, 128))
    assert seqlen % t == 0 and t % LANES == 0 and t + 1 >= _far_start(n_buckets)
    nq, nhp = seqlen // t, heads // 2
    qc, vc = qcol // LANES, vcol // (2 * dv)
    n_pool, page = cache_k.shape[0], cache_k.shape[1]
    n_pages = page_table.shape[1]
    ng = n_pages // pages
    mh = 2 * heads
    kw, vw = mh * dh, heads * dv
    near_slots = _decode_geometry(n_pages, page, n_buckets, pages)
    assert near_slots is not None and nhp * batch_p * nq == batch_s * ng
    assert page % LANES == 0 and lq % 8 == 0
    assert qcol % kw == 0 and kcol % kw == 0 and vcol % vw == 0
    ck = jnp.transpose(cache_k, (0, 2, 3, 1)).reshape(n_pool * mh * dh, page)
    cv = cache_v.reshape(n_pool * page * heads, dv)
    trow = jnp.repeat(tbl.T.reshape(2, heads, -1).transpose(1, 0, 2).reshape(mh, -1), lq, axis=0)
    rows = mh * lq

    def seq_grp(h, b, i):
        step = (h * batch_p + b) * nq + i
        return step // ng, step % ng

    def page_spec(n_rows, width, slot):
        def index(h, b, i, pt):
            seq, grp = seq_grp(h, b, i)
            return pt[seq * n_pages + grp * pages + slot], 0
        return pl.BlockSpec((n_rows, width), index)

    def sample_rows(width, col):
        return pl.BlockSpec((lq, width), lambda h, b, i, pt: (seq_grp(h, b, i)[0], col))

    smem = pl.BlockSpec(memory_space=pltpu.SMEM)
    const = lambda h, b, i, pt: (0, 0)
    attn_specs = [smem, smem, pl.BlockSpec((1, dv), const),
                  pl.BlockSpec((t, LANES), lambda h, b, i, pt: (b * nq + i, qc + h)),
                  pl.BlockSpec((t, LANES), lambda h, b, i, pt: (b * nq + i, qc + nhp + h)),
                  pl.BlockSpec((2 * dh, seqlen), lambda h, b, i, pt: (b * 2 * nhp + h, 0)),
                  pl.BlockSpec((2 * dh, seqlen), lambda h, b, i, pt: (b * 2 * nhp + nhp + h, 0)),
                  pl.BlockSpec((seqlen, 2 * dv), lambda h, b, i, pt: (b, vc + h))]
    dec_specs = ([smem, pl.BlockSpec((rows, n_buckets), const), pl.BlockSpec((1, dv), const),
                  sample_rows(kw, qcol // kw), sample_rows(kw, kcol // kw),
                  sample_rows(vw, vcol // vw)]
                 + [page_spec(mh * dh, page, s) for s in range(pages)]
                 + [page_spec(page * heads, dv, s) for s in range(pages)])
    attn_kw = dict(t=t, dh=dh, dv=dv, heads=heads, n_buckets=n_buckets, out_scale=out_scale)
    dec_kw = dict(pages=pages, page=page, lq=lq, heads=heads, dh=dh, dv=dv, n_buckets=n_buckets,
                  past_len=n_pages * page, near_slots=near_slots, out_scale=out_scale)
    kern = functools.partial(_attn_fused_kernel, n_dec_in=len(dec_specs), nb=batch_p, nq=nq,
                             ng=ng, attn_kw=attn_kw, dec_kw=dec_kw)
    grid_spec = pltpu.PrefetchScalarGridSpec(
        num_scalar_prefetch=1,
        grid=(nhp, batch_p, nq),
        in_specs=attn_specs + dec_specs,
        out_specs=[pl.BlockSpec((t, 2 * dv), lambda h, b, i, pt: (b * nq + i, h)),
                   pl.BlockSpec((lq, vw), lambda h, b, i, pt: (seq_grp(h, b, i)[0], 0))],
        scratch_shapes=[pltpu.VMEM((4 * t, 2 * t), F32), pltpu.VMEM((4 * t, LANES), F32),
                        pltpu.VMEM((4 * t, 2 * dv), F32),
                        pltpu.VMEM((rows, LANES), F32), pltpu.VMEM((rows, LANES), F32),
                        pltpu.VMEM((rows, dv), F32),
                        pltpu.VMEM((len(near_slots) + 1, rows, page), F32)])
    return pl.pallas_call(
        kern,
        grid_spec=grid_spec,
        out_shape=[jax.ShapeDtypeStruct((batch_p * seqlen, heads * dv), BF16),
                   jax.ShapeDtypeStruct((batch_s * lq, vw), BF16)],
        compiler_params=_params("arbitrary", "arbitrary", "arbitrary"),
        name="attn_fused",
    )(page_table.reshape(-1), tbl, lam, nw, up, up, kt, kt, up,
      lam, trow, nw, us, us, us, *([ck] * pages), *([cv] * pages))


def _merge_resident_kernel(*refs, n_g):
    ohg_ref, oda_ref = refs[:2]
    g_refs = refs[2:2 + 2 * n_g]
    x_ref, wbh_ref, wbd_ref, wo_ref, nw_ref, o_ref = refs[2 + 2 * n_g:]
    g_hg = jnp.concatenate([r[...] for r in g_refs[:n_g]], axis=1)
    g_da = jnp.concatenate([r[...] for r in g_refs[n_g:]], axis=1)
    y_hg = jnp.dot(ohg_ref[...], wbh_ref[...], preferred_element_type=F32)
    y_da = jnp.dot(oda_ref[...], wbd_ref[...], preferred_element_type=F32)
    mixed = (_sigmoid(g_hg) * y_hg + _sigmoid(g_da) * y_da).astype(BF16)
    z = jnp.dot(mixed, wo_ref[...], preferred_element_type=F32)
    o_ref[...] = x_ref[...] + _rms(z, nw_ref[...])


def _merge_resident(o_hg, o_da, u, gcol, x, wbh, wbd, wo, nw):
    m, d = x.shape
    w = o_hg.shape[1]
    gw = math.gcd(gcol, d)
    n_g = d // gw
    assert gw % LANES == 0
    tm = _pick(m, (256, 128, 64, 8))
    const = lambda i: (0, 0)
    g_specs = [pl.BlockSpec((tm, gw), functools.partial(lambda i, c: (i, c), c=gcol // gw + k))
               for k in range(2 * n_g)]
    return pl.pallas_call(
        functools.partial(_merge_resident_kernel, n_g=n_g),
        grid=(m // tm,),
        in_specs=[pl.BlockSpec((tm, w), lambda i: (i, 0)),
                  pl.BlockSpec((tm, w), lambda i: (i, 0))]
                 + g_specs
                 + [pl.BlockSpec((tm, d), lambda i: (i, 0)),
                    pl.BlockSpec((w, d), const, pipeline_mode=pl.Buffered(1)),
                    pl.BlockSpec((w, d), const, pipeline_mode=pl.Buffered(1)),
                    pl.BlockSpec((d, d), const, pipeline_mode=pl.Buffered(1)),
                    pl.BlockSpec((1, d), const)],
        out_specs=pl.BlockSpec((tm, d), lambda i: (i, 0)),
        out_shape=jax.ShapeDtypeStruct((m, d), F32),
        compiler_params=_params("parallel"),
        name="merge",
    )(o_hg, o_da, *([u] * (2 * n_g)), x, wbh, wbd, wo, nw)


def _merge_kernel(ohg_ref, oda_ref, ghg_ref, gda_ref, x_ref, wbh_ref, wbd_ref, wo_ref, nw_ref,
                  o_ref, *copies, emit_w):
    copies = copies if emit_w else (None, None, None)
    n = pl.program_id(1)

    @pl.when(n == 0)
    def _():
        o_ref[...] = jnp.zeros(o_ref.shape, F32)

    y_hg = jnp.dot(ohg_ref[...], _bf16_weight(wbh_ref, copies[0]), preferred_element_type=F32)
    y_da = jnp.dot(oda_ref[...], _bf16_weight(wbd_ref, copies[1]), preferred_element_type=F32)
    mixed = (_sigmoid(ghg_ref[...]) * y_hg + _sigmoid(gda_ref[...]) * y_da).astype(BF16)
    o_ref[...] += jnp.dot(mixed, _bf16_weight(wo_ref, copies[2]), preferred_element_type=F32)

    @pl.when(n == pl.num_programs(1) - 1)
    def _():
        o_ref[...] = x_ref[...] + _rms(o_ref[...], nw_ref[...])


def _merge(o_hg, o_da, u, gcol, x, wbh, wbd, wo, nw):
    m, d = x.shape
    w = o_hg.shape[1]
    emit_w = wo.dtype != BF16
    tm = _pick(m, (1024, 512, 256, 128, 64, 32, 16, 8))
    tn = _pick(math.gcd(gcol, d), (512, 256, 128))
    nn = d // tn
    gc = gcol // tn
    once = dict(pipeline_mode=pl.Buffered(1))
    out_specs = [pl.BlockSpec((tm, d), lambda i, n: (i, 0))]
    out_shape = [jax.ShapeDtypeStruct((m, d), F32)]
    if emit_w:
        assert m == tm, "the bf16 weight copies are written once per column tile"
        out_specs += [pl.BlockSpec((w, tn), lambda i, n: (0, n)),
                      pl.BlockSpec((w, tn), lambda i, n: (0, n)),
                      pl.BlockSpec((tn, d), lambda i, n: (n, 0))]
        out_shape += [jax.ShapeDtypeStruct((w, d), BF16), jax.ShapeDtypeStruct((w, d), BF16),
                      jax.ShapeDtypeStruct((d, d), BF16)]
    return pl.pallas_call(
        functools.partial(_merge_kernel, emit_w=emit_w),
        grid=(m // tm, nn),
        in_specs=[pl.BlockSpec((tm, w), lambda i, n: (i, 0), **once),
                  pl.BlockSpec((tm, w), lambda i, n: (i, 0), **once),
                  pl.BlockSpec((tm, tn), lambda i, n: (i, gc + n)),
                  pl.BlockSpec((tm, tn), lambda i, n: (i, gc + nn + n)),
                  pl.BlockSpec((tm, d), lambda i, n: (i, 0), **once),
                  pl.BlockSpec((w, tn), lambda i, n: (0, n)),
                  pl.BlockSpec((w, tn), lambda i, n: (0, n)),
                  pl.BlockSpec((tn, d), lambda i, n: (n, 0)),
                  pl.BlockSpec((1, d), lambda i, n: (0, 0))],
        out_specs=out_specs,
        out_shape=out_shape,
        compiler_params=_params("parallel", "arbitrary"),
        name="merge",
    )(o_hg, o_da, u, u, x, wbh, wbd, wo, nw)


def _ffn_kernel(x_ref, npre_ref, wg_ref, wu_ref, wd_ref, npost_ref, o_ref, *rest, emit_w):
    h_ref = rest[-1]
    copies = rest[:3] if emit_w else (None, None, None)
    j = pl.program_id(1)

    @pl.when(j == 0)
    def _():
        h_ref[...] = _rms(x_ref[...], npre_ref[...]).astype(BF16)
        o_ref[...] = jnp.zeros(o_ref.shape, F32)

    h = h_ref[...]
    gate = jnp.dot(h, _bf16_weight(wg_ref, copies[0]), preferred_element_type=F32)
    up = jnp.dot(h, _bf16_weight(wu_ref, copies[1]), preferred_element_type=F32)
    act = (gate * _sigmoid(gate) * up).astype(BF16)
    o_ref[...] += jnp.dot(act, _bf16_weight(wd_ref, copies[2]), preferred_element_type=F32)

    @pl.when(j == pl.num_programs(1) - 1)
    def _():
        o_ref[...] = x_ref[...] + _rms(o_ref[...], npost_ref[...])


def _ffn(x, npre, w_gate, w_up, w_down, npost):
    m, d = x.shape
    ff = w_down.shape[0]
    emit_w = w_down.dtype != BF16
    tm = _pick(m, (1024, 512, 256, 128, 64, 32, 16, 8))
    tf = _pick(ff, (512, 256, 128))
    nf = ff // tf
    (wg, gcol), (wu, ucol) = w_gate, w_up
    assert gcol % tf == 0 and ucol % tf == 0
    gc, uc = gcol // tf, ucol // tf
    out_specs = [pl.BlockSpec((tm, d), lambda i, j: (i, 0))]
    out_shape = [jax.ShapeDtypeStruct((m, d), F32)]
    if emit_w:
        assert m == tm, "the bf16 weight copies are written once per hidden-dim tile"
        out_specs += [pl.BlockSpec((d, tf), lambda i, j: (0, j)),
                      pl.BlockSpec((d, tf), lambda i, j: (0, j)),
                      pl.BlockSpec((tf, d), lambda i, j: (j, 0))]
        out_shape += [jax.ShapeDtypeStruct((d, ff), BF16), jax.ShapeDtypeStruct((d, ff), BF16),
                      jax.ShapeDtypeStruct((ff, d), BF16)]
    return pl.pallas_call(
        functools.partial(_ffn_kernel, emit_w=emit_w),
        grid=(m // tm, nf),
        in_specs=[pl.BlockSpec((tm, d), lambda i, j: (i, 0)),
                  pl.BlockSpec((1, d), lambda i, j: (0, 0)),
                  pl.BlockSpec((d, tf), lambda i, j: (0, gc + j)),
                  pl.BlockSpec((d, tf), lambda i, j: (0, uc + j)),
                  pl.BlockSpec((tf, d), lambda i, j: (j, 0)),
                  pl.BlockSpec((1, d), lambda i, j: (0, 0))],
        out_specs=out_specs,
        out_shape=out_shape,
        scratch_shapes=[pltpu.VMEM((tm, d), BF16)],
        compiler_params=_params("parallel", "arbitrary"),
        name="ffn",
    )(x, npre, wg, wu, w_down, npost)


def _mixer_tail(x2, u, o_da, s0, lw, wts, dims, batch, seqlen, gcol):
    hg_heads, hg_dk, hg_dv = dims[:3]
    o_hg, s_new = _hgrn(u, 0, lw["lb"], lw["hg_norm_w"], s0, batch, seqlen, hg_heads, hg_dk, hg_dv)
    merge_args = (o_hg, o_da, u, gcol, x2, wts["w_branch_hg"], wts["w_branch_da"], wts["w_out"],
                  lw["norm_mix_post"])
    if wts["w_out"].dtype == BF16:
        x1 = _merge_resident(*merge_args)
    else:
        x1, wts["w_branch_hg"], wts["w_branch_da"], wts["w_out"] = _merge(*merge_args)
    y, *copy = _ffn(x1, lw["norm_ffn_pre"], wts["w_gate"], wts["w_up"], wts["w_down"],
                    lw["norm_ffn_post"])
    if copy:
        wts["w_gate"], wts["w_up"], wts["w_down"] = (copy[0], 0), (copy[1], 0), copy[2]
    return y, s_new, wts


def _layer(x_p, x_s, s0_s, cache_k, cache_v, page_table, lw, wts, dims):
    (bp, lp, d), (bs, ls, _) = x_p.shape, x_s.shape
    wts = dict(wts)
    hg_heads, hg_dk, hg_dv, da_heads, da_dh, da_dv = dims
    hg_w, da_qk, da_w = hg_heads * hg_dk, 2 * da_heads * da_dh, da_heads * da_dv
    assert da_qk == da_w
    cols = np.cumsum([0, hg_w, hg_w, hg_heads * hg_dv, hg_heads * hg_dv, da_qk, da_qk, da_w, d, d])
    qcol, kcol, vcol, gcol = int(cols[4]), int(cols[5]), int(cols[6]), int(cols[7])
    xp2, xs2 = x_p.reshape(bp * lp, d), x_s.reshape(bs * ls, d)

    us, wts["w_in"] = _norm_proj(xs2, lw["norm_mix_pre"], wts["w_in"])
    up, kt, vp2 = _norm_proj(xp2, lw["norm_mix_pre"], wts["w_in"],
                             kv=(kcol, vcol, da_qk, lp, da_heads))

    attn = (lw["tbl"], lw["lam"], lw["da_subln_w"])
    shape = (da_heads, da_dh, da_dv, lw["out_scale"])
    n_pages, page = page_table.shape[1], cache_k.shape[1]
    pages = _fused_pages(bp, lp, da_heads, bs, n_pages)
    if _decode_geometry(n_pages, page, lw["tbl"].shape[0], pages) is not None:
        o_da_p, o_da_s = _attn_fused(up, kt, us, qcol, kcol, vcol, cache_k, cache_v, page_table,
                                     *attn, bp, lp, bs, ls, *shape, pages)
    else:
        o_da_p = _attn_prompt(up, qcol, vcol, kt, *attn, bp, lp, *shape)
        o_da_s = _attn_decode(us, qcol, kcol, vcol, cache_k, cache_v, page_table, *attn,
                              bs, ls, *shape)

    y_s, s_s, wts = _mixer_tail(xs2, us, o_da_s, s0_s, lw, wts, dims, bs, ls, gcol)
    y_p, s_p, _ = _mixer_tail(xp2, up, o_da_p, None, lw, wts, dims, bp, lp, gcol)

    k_p = kt.reshape(bp, 2 * da_heads, da_dh, lp).transpose(0, 3, 1, 2)
    v_p = vp2.reshape(bp, lp, da_heads, da_dv)
    k_s = us[:, kcol:kcol + da_qk].reshape(bs, ls, 2 * da_heads, da_dh)
    v_s = us[:, vcol:vcol + da_w].reshape(bs, ls, da_heads, da_dv)
    return (y_p.reshape(bp, lp, d), y_s.reshape(bs, ls, d)), (k_p, v_p, s_p, k_s, v_s, s_s)


def kernel(x_prompt, x_sample, cache_k, cache_v, state_hgrn, page_table, norm_mix_pre, norm_mix_post, norm_ffn_pre, norm_ffn_post, w_in, hg_lb_logits, hg_norm_w, da_lambda_q1, da_lambda_k1, da_lambda_q2, da_lambda_k2, da_subln_w, rel_bias_table, w_branch_hg, w_branch_da, w_out, w_ffn_up, w_ffn_down):
    depth = w_in.shape[0]
    _, _, hg_heads, hg_dk, hg_dv = state_hgrn.shape
    da_heads, da_dv = cache_v.shape[3], cache_v.shape[4]
    da_dh = cache_k.shape[4]
    dims = (hg_heads, hg_dk, hg_dv, da_heads, da_dh, da_dv)
    lb_all = jnp.cumsum(jax.nn.softmax(hg_lb_logits.astype(F32), axis=0), axis=0)

    y_p, y_s = x_prompt, x_sample
    outs = [[] for _ in range(6)]
    for l in range(depth):
        lam_init = 0.8 - 0.6 * math.exp(-0.3 * l)
        lam = (jnp.exp(jnp.sum(da_lambda_q1[l] * da_lambda_k1[l]))
               - jnp.exp(jnp.sum(da_lambda_q2[l] * da_lambda_k2[l])) + lam_init)
        lw = {
            "norm_mix_pre": norm_mix_pre[l][None], "norm_mix_post": norm_mix_post[l][None],
            "norm_ffn_pre": norm_ffn_pre[l][None], "norm_ffn_post": norm_ffn_post[l][None],
            "lb": lb_all[l][None], "hg_norm_w": hg_norm_w[l][None],
            "da_subln_w": da_subln_w[l][None], "tbl": rel_bias_table.astype(F32),
            "lam": lam.reshape(1, 1).astype(F32), "out_scale": 1.0 - lam_init,
        }
        ff = w_ffn_down.shape[1]
        wts = {"w_in": w_in[l], "w_branch_hg": w_branch_hg[l], "w_branch_da": w_branch_da[l],
               "w_out": w_out[l], "w_gate": (w_ffn_up[l], 0), "w_up": (w_ffn_up[l], ff),
               "w_down": w_ffn_down[l]}
        (y_p, y_s), caches = _layer(y_p, y_s, state_hgrn[l], cache_k[l], cache_v[l], page_table,
                                    lw, wts, dims)
        for acc, val in zip(outs, caches):
            acc.append(val)
    return (y_p, y_s) + tuple(jnp.stack(o) for o in outs)
```

```python
import functools
import math

import jax
import jax.numpy as jnp
import numpy as np
from jax import lax
from jax.experimental import pallas as pl
from jax.experimental.pallas import tpu as pltpu

F32 = jnp.float32
BF16 = jnp.bfloat16

RMS_EPS = 1e-6
NEG_INF = -1e30
HG_CHUNK = 64
REL_MAX_DISTANCE = 128
LANES = 128
VMEM_LIMIT = 62 * 1024 * 1024

_NT = (((1,), (1,)), ((), ()))
_TN = (((0,), (0,)), ((), ()))


def _params(*sem):
    return pltpu.CompilerParams(dimension_semantics=sem, vmem_limit_bytes=VMEM_LIMIT)


def _rms(x, w):
    return x * lax.rsqrt(jnp.mean(x * x, axis=-1, keepdims=True) + RMS_EPS) * w


def _sigmoid(x):
    return 1.0 / (1.0 + jnp.exp(-x))


def _pick(n, prefs):
    for p in prefs:
        if n % p == 0:
            return p
    return n


def _bf16_weight(w_ref, copy_ref):
    w = w_ref[...]
    if copy_ref is not None:
        w = w.astype(BF16)
        copy_ref[...] = w
    return w


def _norm_proj_kernel(x_ref, nw_ref, w_ref, o_ref, *rest, jk, jv, v_heads, emit_w):
    h_ref = rest[-1]
    j = pl.program_id(1)

    @pl.when(j == 0)
    def _():
        h_ref[...] = _rms(x_ref[...], nw_ref[...]).astype(BF16)

    w = _bf16_weight(w_ref, rest[-2] if emit_w else None)
    res = jnp.dot(h_ref[...], w, preferred_element_type=F32)
    o_ref[...] = res
    if jk is not None:
        kt_ref, v_ref = rest[:2]
        tm, tn = res.shape

        @pl.when(j == jk)
        def _():
            kt_ref[...] = res.T

        @pl.when(j == jv)
        def _():
            dv = tn // v_heads
            for h in range(v_heads):
                v_ref[pl.ds(h, tm, stride=v_heads), :] = res[:, h * dv:(h + 1) * dv]


def _norm_proj(x, nw, w, kv=None):
    m, d = x.shape
    n = w.shape[1]
    emit_w = w.dtype != BF16
    tm = _pick(m, (1024, 512, 256, 128, 64, 32, 16, 8))
    tn = _pick(n, (1024, 512, 256, 128))
    out_specs = [pl.BlockSpec((tm, tn), lambda i, j: (i, j))]
    out_shape = [jax.ShapeDtypeStruct((m, n), F32)]
    jk = jv = v_heads = None
    if kv is not None:
        kcol, vcol, width, seqlen, v_heads = kv
        tn = width
        assert n % tn == 0 and kcol % tn == 0 and vcol % tn == 0 and seqlen % tm == 0
        jk, jv = kcol // tn, vcol // tn
        nt = seqlen // tm
        out_specs = [pl.BlockSpec((tm, tn), lambda i, j: (i, j)),
                     pl.BlockSpec((tn, tm), lambda i, j: (i // nt, i % nt)),
                     pl.BlockSpec((tm * v_heads, tn // v_heads), lambda i, j: (i, 0))]
        out_shape += [jax.ShapeDtypeStruct((m // seqlen * tn, seqlen), F32),
                      jax.ShapeDtypeStruct((m * v_heads, tn // v_heads), F32)]
    if emit_w:
        out_specs.append(pl.BlockSpec((d, tn), lambda i, j: (0, j)))
        out_shape.append(jax.ShapeDtypeStruct((d, n), BF16))
        assert m == tm, "the bf16 weight copy is written once per column tile"
    return pl.pallas_call(
        functools.partial(_norm_proj_kernel, jk=jk, jv=jv, v_heads=v_heads, emit_w=emit_w),
        grid=(m // tm, n // tn),
        in_specs=[pl.BlockSpec((tm, d), lambda i, j: (i, 0)),
                  pl.BlockSpec((1, d), lambda i, j: (0, 0)),
                  pl.BlockSpec((d, tn), lambda i, j: (0, j))],
        out_specs=out_specs,
        out_shape=out_shape,
        scratch_shapes=[pltpu.VMEM((tm, d), BF16)],
        compiler_params=_params("parallel", "arbitrary"),
        name="norm_proj",
    )(x, nw, w)


def _hgrn_kernel(*refs, heads, dk, dv, chunk, rows, has_s0):
    if has_s0:
        q_ref, f_ref, i_ref, gate_ref, lb_ref, nw_ref, s0_ref, o_ref, sout_ref, st_ref = refs
    else:
        q_ref, f_ref, i_ref, gate_ref, lb_ref, nw_ref, o_ref, sout_ref, st_ref = refs
    t = pl.program_id(1)
    real = q_ref.shape[0]

    @pl.when(t == 0)
    def _():
        for h in range(heads):
            if has_s0:
                st_ref[h] = s0_ref[h].T
            else:
                st_ref[h] = jnp.zeros((dv, dk), F32)

    lb = lb_ref[...]
    f = lb + (1.0 - lb) * _sigmoid(f_ref[...])
    g = jnp.log(f)
    kin = 1.0 - f
    q = q_ref[...]
    v = i_ref[...]
    if rows > real:
        def pad(a):
            return jnp.concatenate([a, jnp.zeros((rows - real, a.shape[1]), F32)], axis=0)
        g, kin, q, v = pad(g), pad(kin), pad(q), pad(v)

    cs = math.gcd(rows, 2 * LANES)
    r = lax.broadcasted_iota(jnp.int32, (cs, cs), 0)
    c = lax.broadcasted_iota(jnp.int32, (cs, cs), 1)
    within = (r // chunk == c // chunk) & (c <= r)
    tri = jnp.where(within, 1.0, 0.0).astype(BF16)
    g_hi = g.astype(BF16)
    g_r1 = g - g_hi.astype(F32)
    g_mid = g_r1.astype(BF16)
    g_lo = (g_r1 - g_mid.astype(F32)).astype(BF16)
    G = jnp.concatenate(
        [jnp.dot(tri, g_hi[i:i + cs], preferred_element_type=F32)
         + jnp.dot(tri, g_mid[i:i + cs], preferred_element_type=F32)
         + jnp.dot(tri, g_lo[i:i + cs], preferred_element_type=F32)
         for i in range(0, rows, cs)], axis=0)

    qg = (q * jnp.exp(G)).astype(BF16)
    kg = (kin * jnp.exp(-G)).astype(BF16)
    vb = v.astype(BF16)
    nw = nw_ref[...]

    states = [st_ref[h] for h in range(heads)]
    for g0 in range(0, rows, cs):
        intra = []
        for h in range(heads):
            sk = slice(h * dk, (h + 1) * dk)
            a = lax.dot_general(qg[g0:g0 + cs, sk], kg[g0:g0 + cs, sk], _NT,
                                preferred_element_type=F32)
            a = jnp.where(within, a, 0.0).astype(BF16)
            intra.append(jnp.dot(a, vb[g0:g0 + cs, h * dv:(h + 1) * dv],
                                 preferred_element_type=F32))
        for ci in range(cs // chunk):
            lo = g0 + ci * chunk
            Gc = G[lo:lo + chunk]
            Gl = Gc[chunk - 1:chunk]
            kdec = (kin[lo:lo + chunk] * jnp.exp(Gl - Gc)).astype(BF16)
            decay = jnp.exp(Gl)
            n_out = min(chunk, real - lo)
            for h in range(heads):
                sk = slice(h * dk, (h + 1) * dk)
                sv = slice(h * dv, (h + 1) * dv)
                st = states[h]
                o = intra[h][ci * chunk:(ci + 1) * chunk] + lax.dot_general(
                    qg[lo:lo + chunk, sk], st.astype(BF16), _NT, preferred_element_type=F32)
                states[h] = st * decay[:, sk] + lax.dot_general(
                    vb[lo:lo + chunk, sv], kdec[:, sk], _TN, preferred_element_type=F32)
                if n_out > 0:
                    gt = gate_ref[lo:lo + n_out, sv]
                    on = _rms(o[:n_out], nw) * (gt * _sigmoid(gt))
                    o_ref[lo:lo + n_out, sv] = on.astype(o_ref.dtype)
    for h in range(heads):
        st_ref[h] = states[h]

    @pl.when(t == pl.num_programs(1) - 1)
    def _():
        for h in range(heads):
            sout_ref[h] = st_ref[h].T


def _hgrn(u, col0, lb, nw, s0, batch, seqlen, heads, dk, dv):
    width = heads * dk
    assert dk == dv and col0 % width == 0
    cb = col0 // width
    chunk = min(HG_CHUNK, seqlen)
    if seqlen >= LANES:
        tb = _pick(seqlen, (512, 256, 128))
        rows = tb
        assert tb % chunk == 0
    else:
        tb = seqlen
        rows = LANES
        chunk = LANES
    nt = seqlen // tb
    has_s0 = s0 is not None

    def col(k):
        return pl.BlockSpec((tb, width), lambda b, t: (b * nt + t, cb + k))

    in_specs = [col(0), col(1), col(2), col(3),
                pl.BlockSpec((1, width), lambda b, t: (0, 0)),
                pl.BlockSpec((1, dv), lambda b, t: (0, 0))]
    args = [u, u, u, u, lb, nw]
    if has_s0:
        in_specs.append(pl.BlockSpec((None, heads, dk, dv), lambda b, t: (b, 0, 0, 0)))
        args.append(s0)
    kern = functools.partial(_hgrn_kernel, heads=heads, dk=dk, dv=dv, chunk=chunk, rows=rows,
                             has_s0=has_s0)
    return pl.pallas_call(
        kern,
        grid=(batch, nt),
        in_specs=in_specs,
        out_specs=[pl.BlockSpec((tb, heads * dv), lambda b, t: (b * nt + t, 0)),
                   pl.BlockSpec((None, heads, dk, dv), lambda b, t: (b, 0, 0, 0))],
        out_shape=[jax.ShapeDtypeStruct((batch * seqlen, heads * dv), BF16),
                   jax.ShapeDtypeStruct((batch, heads, dk, dv), F32)],
        scratch_shapes=[pltpu.VMEM((heads, dv, dk), F32)],
        compiler_params=_params("parallel", "arbitrary"),
        name="hgrn",
    )(*args)


def _rel_bias(dist, value, n_buckets):
    n = jnp.maximum(dist, 0)
    max_exact = n_buckets // 2
    nf = jnp.maximum(n, 1).astype(F32)
    x = (jnp.log(nf / max_exact) / math.log(REL_MAX_DISTANCE / max_exact)
         * (n_buckets - max_exact))
    bias = jnp.zeros(dist.shape, F32)
    for k in range(n_buckets - 2, max_exact - 1, -1):
        bias = jnp.where(x < k - max_exact + 1, value(k), bias)
    for k in range(max_exact - 1, -1, -1):
        bias = jnp.where(n <= k, value(k), bias)
    return bias


def _far_start(n_buckets):
    n = np.arange(1, 8 * REL_MAX_DISTANCE, dtype=np.int64)
    max_exact = n_buckets // 2
    nf = n.astype(np.float32)
    large = max_exact + (np.log(nf / np.float32(max_exact)) / np.float32(math.log(REL_MAX_DISTANCE / max_exact))
                         * np.float32(n_buckets - max_exact)).astype(np.int32)
    b = np.where(n < max_exact, n, np.minimum(large, n_buckets - 1))
    below = n[b < n_buckets - 1]
    return int(below.max()) + 2


def _attn_kernel(*refs, **kw):
    _attn_body(pl.program_id(0), pl.program_id(1) == 0, 0, *refs, **kw)


def _attn_body(hp, init, qb_first, tbl_ref, lam_ref, nw_ref, q1_ref, q2_ref, k1_ref, k2_ref, v_ref,
               o_ref, bias_ref, m_ref, acc_ref, *, t, dh, dv, heads, n_buckets, out_scale):
    @pl.when(init)
    def _():
        row = lax.broadcasted_iota(jnp.int32, (t, t), 0)
        col = lax.broadcasted_iota(jnp.int32, (t, t), 1)
        for e in range(2):
            for mp in range(2):
                hcol = mp * heads + 2 * hp + e
                far = tbl_ref[n_buckets - 1, hcol]
                value = lambda k: tbl_ref[k, hcol] - far
                bd = _rel_bias(row - col, value, n_buckets)
                bn = _rel_bias(row - col + t, value, n_buckets)
                rs = slice((2 * e + mp) * t, (2 * e + mp + 1) * t)
                bias_ref[rs, :t] = bn
                bias_ref[rs, t:] = jnp.where(col <= row, bd, NEG_INF)

    lane = lax.broadcasted_iota(jnp.int32, (t, 2 * dh), 1)
    scale = dh ** -0.5
    zero = jnp.zeros((t, 2 * dh), F32)
    lam = lam_ref[0, 0]
    nw = nw_ref[...]

    def q_block(held, carry):
        qb = qb_first + held
        rows = pl.ds(pl.multiple_of(held * t, t), t)
        q_maps = [q1_ref[rows, :] * scale, q2_ref[rows, :] * scale]
        blocks = []
        for e in range(2):
            in_e = (lane >= e * dh) & (lane < (e + 1) * dh)
            for mp in range(2):
                qe = jnp.where(in_e, q_maps[mp], 0.0)
                blocks.append(jnp.concatenate([qe, zero] if mp == 0 else [zero, qe], axis=1))
        q_all = jnp.concatenate(blocks, axis=0).astype(BF16)

        m_ref[...] = jnp.full(m_ref.shape, NEG_INF, F32)
        acc_ref[...] = jnp.zeros(acc_ref.shape, F32)

        def step(j, n, bias=None):
            start = pl.multiple_of(j * t, t)
            kb = jnp.concatenate([k1_ref[:, pl.ds(start, n * t)], k2_ref[:, pl.ds(start, n * t)]],
                                 axis=0).astype(BF16)
            s = jnp.dot(q_all, kb, preferred_element_type=F32)
            if bias is not None:
                s = s + bias
            m_prev = m_ref[...]
            m_new = jnp.maximum(m_prev, jnp.max(s, axis=1, keepdims=True))
            pb = jnp.exp(s - jnp.tile(m_new, (1, n * t // LANES))).astype(BF16)
            alpha = jnp.tile(jnp.exp(m_prev - m_new), (1, 2))
            m_ref[...] = m_new
            ones = jnp.ones((n * t, dv), BF16)
            for e in range(2):
                ve = v_ref[pl.ds(start, n * t), e * dv:(e + 1) * dv].astype(BF16)
                rs = slice(2 * e * t, 2 * (e + 1) * t)
                acc_ref[rs, :] = acc_ref[rs, :] * alpha[rs] + jnp.dot(
                    pb[rs], jnp.concatenate([ve, ones], axis=1), preferred_element_type=F32)

        n_far = jnp.maximum(qb - 1, 0)

        def far_body(i, c):
            step(2 * i, 2)
            return c

        lax.fori_loop(0, n_far // 2, far_body, 0)

        @pl.when(n_far % 2 == 1)
        def _():
            step(n_far - 1, 1)

        @pl.when(qb > 0)
        def _():
            step(qb - 1, 2, bias_ref[...])

        @pl.when(qb == 0)
        def _():
            step(0, 1, bias_ref[:, t:])

        o = acc_ref[:, :dv] / acc_ref[:, dv:]
        for e in range(2):
            oe = o[2 * e * t:(2 * e + 1) * t] - lam * o[(2 * e + 1) * t:(2 * e + 2) * t]
            o_ref[rows, e * dv:(e + 1) * dv] = (_rms(oe, nw) * out_scale).astype(o_ref.dtype)
        return carry

    lax.fori_loop(0, q1_ref.shape[0] // t, q_block, 0)


def _attn_prompt(u, qcol, vcol, kt, tbl, lam, nw, batch, seqlen, heads, dh, dv, out_scale):
    assert dv == LANES and 2 * dh == LANES and heads % 2 == 0
    n_buckets = tbl.shape[0]
    t = _pick(seqlen, (256, 128))
    assert seqlen % t == 0 and t % LANES == 0 and t + 1 >= _far_start(n_buckets)
    nhp = heads // 2
    qc, vc = qcol // LANES, vcol // (2 * dv)
    smem = pl.BlockSpec(memory_space=pltpu.SMEM)
    kern = functools.partial(_attn_kernel, t=t, dh=dh, dv=dv, heads=heads, n_buckets=n_buckets,
                             out_scale=out_scale)
    return pl.pallas_call(
        kern,
        grid=(nhp, batch),
        in_specs=[smem, smem,
                  pl.BlockSpec((1, dv), lambda h, b: (0, 0)),
                  pl.BlockSpec((seqlen, LANES), lambda h, b: (b, qc + h)),
                  pl.BlockSpec((seqlen, LANES), lambda h, b: (b, qc + nhp + h)),
                  pl.BlockSpec((2 * dh, seqlen), lambda h, b: (b * 2 * nhp + h, 0)),
                  pl.BlockSpec((2 * dh, seqlen), lambda h, b: (b * 2 * nhp + nhp + h, 0)),
                  pl.BlockSpec((seqlen, 2 * dv), lambda h, b: (b, vc + h))],
        out_specs=pl.BlockSpec((seqlen, 2 * dv), lambda h, b: (b, h)),
        out_shape=jax.ShapeDtypeStruct((batch * seqlen, heads * dv), BF16),
        scratch_shapes=[pltpu.VMEM((4 * t, 2 * t), F32),
                        pltpu.VMEM((4 * t, LANES), F32), pltpu.VMEM((4 * t, 2 * dv), F32)],
        compiler_params=_params("arbitrary", "arbitrary"),
        name="attn_prompt",
    )(tbl, lam, nw, u, u, kt, kt, u)


def _decode_kernel(pt_ref, *refs, **kw):
    first = (pl.program_id(0) == 0) & (pl.program_id(1) == 0)
    _decode_body(first, pl.program_id(1), pl.num_programs(1), *refs, **kw)


def _decode_body(first, g, ng, lam_ref, trow_ref, nw_ref, q_ref, kn_ref, vn_ref, *rest,
                 pages, page, lq, heads, dh, dv, n_buckets, past_len, near_slots, out_scale):
    k_refs = rest[:pages]
    v_refs = rest[pages:2 * pages]
    o_ref, m_ref, l_ref, acc_ref, bias_ref = rest[2 * pages:]
    mh = 2 * heads
    rows = mh * lq
    n_near = len(near_slots)
    order = [m * heads + h for h in range(heads) for m in range(2)]

    def near_bias(key0, n_valid):
        r = lax.broadcasted_iota(jnp.int32, (rows, page), 0)
        col = lax.broadcasted_iota(jnp.int32, (rows, page), 1)
        dist = past_len + r % lq - (key0 + col)
        far = trow_ref[:, n_buckets - 1:n_buckets]
        bias = _rel_bias(dist, lambda k: trow_ref[:, k:k + 1] - far, n_buckets)
        return jnp.where((dist >= 0) & (col < n_valid), bias, NEG_INF)

    @pl.when(first)
    def _():
        for i, slot in enumerate(near_slots):
            bias_ref[i] = near_bias(past_len - (pages - slot) * page, page)
        bias_ref[n_near] = near_bias(past_len, lq)

    @pl.when(g == 0)
    def _():
        m_ref[...] = jnp.full(m_ref.shape, NEG_INF, F32)
        l_ref[...] = jnp.zeros(l_ref.shape, F32)
        acc_ref[...] = jnp.zeros(acc_ref.shape, F32)

    q = q_ref[...] * (dh ** -0.5)
    q_parts = [q[:, j * dh:(j + 1) * dh].astype(BF16) for j in order]

    def attend(score, value_rows, bias):
        s = jnp.concatenate([score(qj, j) for qj, j in zip(q_parts, order)], axis=0)
        if bias is not None:
            s = s + bias
        m_prev = m_ref[...]
        m_new = jnp.maximum(m_prev, jnp.max(s, axis=1, keepdims=True))
        p = jnp.exp(s - jnp.tile(m_new, (1, s.shape[1] // LANES)))
        alpha = jnp.exp(m_prev - m_new)
        l_ref[...] = alpha * l_ref[...] + jnp.sum(p, axis=1, keepdims=True)
        m_ref[...] = m_new
        pv = jnp.concatenate(
            [jnp.dot(p[2 * lq * h:2 * lq * (h + 1)].astype(BF16), value_rows(h).astype(BF16),
                     preferred_element_type=F32) for h in range(heads)], axis=0)
        acc_ref[...] = acc_ref[...] * alpha + pv

    is_last = g == ng - 1
    bias = None
    if near_slots:
        zero = jnp.zeros((rows, page), F32)
        bias = jnp.concatenate(
            [jnp.where(is_last, bias_ref[near_slots.index(s)], 0.0) if s in near_slots else zero
             for s in range(pages)], axis=1)
    def page_score(qj, j):
        kt = jnp.concatenate([r[j * dh:(j + 1) * dh, :] for r in k_refs], axis=1)
        return jnp.dot(qj, kt.astype(BF16), preferred_element_type=F32)

    attend(page_score,
           lambda h: jnp.concatenate([r[pl.ds(h, page, stride=heads), :] for r in v_refs], axis=0),
           bias)

    @pl.when(is_last)
    def _():
        kn = kn_ref[...]
        vn = vn_ref[...]
        zk = jnp.zeros((page - lq, dh), F32)
        zv = jnp.zeros((page - lq, dv), F32)

        def new_score(qj, j):
            kj = jnp.concatenate([kn[:, j * dh:(j + 1) * dh], zk], axis=0).astype(BF16)
            return lax.dot_general(qj, kj, _NT, preferred_element_type=F32)

        attend(new_score,
               lambda h: jnp.concatenate([vn[:, h * dv:(h + 1) * dv], zv], axis=0),
               bias_ref[n_near])
        full = acc_ref[...] / l_ref[...]
        lam = lam_ref[0, 0]
        nw = nw_ref[...]
        for h in range(heads):
            o0 = full[2 * lq * h:2 * lq * h + lq]
            o1 = full[2 * lq * h + lq:2 * lq * (h + 1)]
            o_ref[:, h * dv:(h + 1) * dv] = (_rms(o0 - lam * o1, nw) * out_scale).astype(o_ref.dtype)


def _attn_decode(u, qcol, kcol, vcol, cache_k, cache_v, page_table, tbl, lam, nw,
                 batch, lq, heads, dh, dv, out_scale):
    n_pool, page = cache_k.shape[0], cache_k.shape[1]
    n_pages = page_table.shape[1]
    past_len = n_pages * page
    mh = 2 * heads
    kw, vw = mh * dh, heads * dv
    assert page % LANES == 0 and kw % LANES == 0 and vw % LANES == 0 and lq % 8 == 0
    n_buckets = tbl.shape[0]
    pages = next(p for p in (16, 8, 4, 2, 1)
                 if _decode_geometry(n_pages, page, n_buckets, p) is not None)
    near_slots = _decode_geometry(n_pages, page, n_buckets, pages)
    ck = jnp.transpose(cache_k, (0, 2, 3, 1)).reshape(n_pool * mh * dh, page)
    cv = cache_v.reshape(n_pool * page * heads, dv)
    trow = jnp.repeat(tbl.T.reshape(2, heads, -1).transpose(1, 0, 2).reshape(mh, -1), lq, axis=0)
    rows = mh * lq

    def page_spec(n_rows, width, slot):
        return pl.BlockSpec((n_rows, width),
                            lambda b, g, pt: (pt[b * n_pages + g * pages + slot], 0))

    smem = pl.BlockSpec(memory_space=pltpu.SMEM)
    in_specs = ([smem,
                 pl.BlockSpec((rows, n_buckets), lambda b, g, pt: (0, 0)),
                 pl.BlockSpec((1, dv), lambda b, g, pt: (0, 0)),
                 pl.BlockSpec((lq, kw), lambda b, g, pt: (b, qcol // kw)),
                 pl.BlockSpec((lq, kw), lambda b, g, pt: (b, kcol // kw)),
                 pl.BlockSpec((lq, vw), lambda b, g, pt: (b, vcol // vw))]
                + [page_spec(mh * dh, page, s) for s in range(pages)]
                + [page_spec(page * heads, dv, s) for s in range(pages)])
    assert qcol % kw == 0 and kcol % kw == 0 and vcol % vw == 0
    kern = functools.partial(_decode_kernel, pages=pages, page=page, lq=lq, heads=heads, dh=dh,
                             dv=dv, n_buckets=n_buckets, past_len=past_len,
                             near_slots=near_slots, out_scale=out_scale)
    grid_spec = pltpu.PrefetchScalarGridSpec(
        num_scalar_prefetch=1,
        grid=(batch, n_pages // pages),
        in_specs=in_specs,
        out_specs=pl.BlockSpec((lq, vw), lambda b, g, pt: (b, 0)),
        scratch_shapes=[pltpu.VMEM((rows, LANES), F32), pltpu.VMEM((rows, LANES), F32),
                        pltpu.VMEM((rows, dv), F32),
                        pltpu.VMEM((len(near_slots) + 1, rows, page), F32)])
    return pl.pallas_call(
        kern,
        grid_spec=grid_spec,
        out_shape=jax.ShapeDtypeStruct((batch * lq, vw), BF16),
        compiler_params=_params("arbitrary", "arbitrary"),
        name="attn_decode",
    )(page_table.reshape(-1), lam, trow, nw, u, u, u, *([ck] * pages), *([cv] * pages))


def _decode_geometry(n_pages, page, n_buckets, pages):
    past_len = n_pages * page
    far = _far_start(n_buckets)
    first_near = max(0, -(-(past_len - page + 1 - far + 1) // page))
    if pages < 1 or n_pages % pages or n_pages - pages > first_near:
        return None
    return tuple(s for s in range(pages) if n_pages - pages + s >= first_near)


def _fused_pages(batch_p, seqlen, heads, batch_s, n_pages, per):
    nq = seqlen // _pick(seqlen, (256, 128))
    if nq % per:
        return 0
    steps = (heads // 2) * batch_p * (nq // per)
    total = batch_s * n_pages
    return total // steps if total % steps == 0 else 0


def _attn_fused_kernel(pt_ref, *refs, n_dec_in, nb, nq, per, ng, attn_kw, dec_kw):
    hp, b, i = pl.program_id(0), pl.program_id(1), pl.program_id(2)
    step = (hp * nb + b) * nq + i
    a_in, rest = refs[:8], refs[8:]
    d_in, rest = rest[:n_dec_in], rest[n_dec_in:]
    o_attn, o_dec = rest[:2]
    a_scr, d_scr = rest[2:5], rest[5:]
    _attn_body(hp, (b == 0) & (i == 0), i * per, *a_in, o_attn, *a_scr, **attn_kw)
    _decode_body(step == 0, step % ng, ng, *d_in, o_dec, *d_scr, **dec_kw)


def _attn_fused(up, kt, us, qcol, kcol, vcol, cache_k, cache_v, page_table, tbl, lam, nw,
                batch_p, seqlen, batch_s, lq, heads, dh, dv, out_scale, pages, per):
    assert dv == LANES and 2 * dh == LANES and heads % 2 == 0
    n_buckets = tbl.shape[0]
    t = _pick(seqlen, (256, 128))
    assert seqlen % t == 0 and t % LANES == 0 and t + 1 >= _far_start(n_buckets)
    nq, nhp = seqlen // (per * t), heads // 2
    qc, vc = qcol // LANES, vcol // (2 * dv)
    n_pool, page = cache_k.shape[0], cache_k.shape[1]
    n_pages = page_table.shape[1]
    ng = n_pages // pages
    mh = 2 * heads
    kw, vw = mh * dh, heads * dv
    near_slots = _decode_geometry(n_pages, page, n_buckets, pages)
    assert near_slots is not None and nhp * batch_p * nq == batch_s * ng
    assert page % LANES == 0 and lq % 8 == 0
    assert qcol % kw == 0 and kcol % kw == 0 and vcol % vw == 0
    ck = jnp.transpose(cache_k, (0, 2, 3, 1)).reshape(n_pool * mh * dh, page)
    cv = cache_v.reshape(n_pool * page * heads, dv)
    trow = jnp.repeat(tbl.T.reshape(2, heads, -1).transpose(1, 0, 2).reshape(mh, -1), lq, axis=0)
    rows = mh * lq

    def seq_grp(h, b, i):
        step = (h * batch_p + b) * nq + i
        return step // ng, step % ng

    def page_spec(n_rows, width, slot):
        def index(h, b, i, pt):
            seq, grp = seq_grp(h, b, i)
            return pt[seq * n_pages + grp * pages + slot], 0
        return pl.BlockSpec((n_rows, width), index)

    def sample_rows(width, col):
        return pl.BlockSpec((lq, width), lambda h, b, i, pt: (seq_grp(h, b, i)[0], col))

    smem = pl.BlockSpec(memory_space=pltpu.SMEM)
    const = lambda h, b, i, pt: (0, 0)
    attn_specs = [smem, smem, pl.BlockSpec((1, dv), const),
                  pl.BlockSpec((per * t, LANES), lambda h, b, i, pt: (b * nq + i, qc + h)),
                  pl.BlockSpec((per * t, LANES), lambda h, b, i, pt: (b * nq + i, qc + nhp + h)),
                  pl.BlockSpec((2 * dh, seqlen), lambda h, b, i, pt: (b * 2 * nhp + h, 0)),
                  pl.BlockSpec((2 * dh, seqlen), lambda h, b, i, pt: (b * 2 * nhp + nhp + h, 0)),
                  pl.BlockSpec((seqlen, 2 * dv), lambda h, b, i, pt: (b, vc + h))]
    dec_specs = ([smem, pl.BlockSpec((rows, n_buckets), const), pl.BlockSpec((1, dv), const),
                  sample_rows(kw, qcol // kw), sample_rows(kw, kcol // kw),
                  sample_rows(vw, vcol // vw)]
                 + [page_spec(mh * dh, page, s) for s in range(pages)]
                 + [page_spec(page * heads, dv, s) for s in range(pages)])
    attn_kw = dict(t=t, dh=dh, dv=dv, heads=heads, n_buckets=n_buckets, out_scale=out_scale)
    dec_kw = dict(pages=pages, page=page, lq=lq, heads=heads, dh=dh, dv=dv, n_buckets=n_buckets,
                  past_len=n_pages * page, near_slots=near_slots, out_scale=out_scale)
    kern = functools.partial(_attn_fused_kernel, n_dec_in=len(dec_specs), nb=batch_p, nq=nq,
                             per=per, ng=ng, attn_kw=attn_kw, dec_kw=dec_kw)
    grid_spec = pltpu.PrefetchScalarGridSpec(
        num_scalar_prefetch=1,
        grid=(nhp, batch_p, nq),
        in_specs=attn_specs + dec_specs,
        out_specs=[pl.BlockSpec((per * t, 2 * dv), lambda h, b, i, pt: (b * nq + i, h)),
                   pl.BlockSpec((lq, vw), lambda h, b, i, pt: (seq_grp(h, b, i)[0], 0))],
        scratch_shapes=[pltpu.VMEM((4 * t, 2 * t), F32), pltpu.VMEM((4 * t, LANES), F32),
                        pltpu.VMEM((4 * t, 2 * dv), F32),
                        pltpu.VMEM((rows, LANES), F32), pltpu.VMEM((rows, LANES), F32),
                        pltpu.VMEM((rows, dv), F32),
                        pltpu.VMEM((len(near_slots) + 1, rows, page), F32)])
    return pl.pallas_call(
        kern,
        grid_spec=grid_spec,
        out_shape=[jax.ShapeDtypeStruct((batch_p * seqlen, heads * dv), BF16),
                   jax.ShapeDtypeStruct((batch_s * lq, vw), BF16)],
        compiler_params=_params("arbitrary", "arbitrary", "arbitrary"),
        name="attn_fused",
    )(page_table.reshape(-1), tbl, lam, nw, up, up, kt, kt, up,
      lam, trow, nw, us, us, us, *([ck] * pages), *([cv] * pages))


def _merge_resident_kernel(*refs, n_g):
    ohg_ref, oda_ref = refs[:2]
    g_refs = refs[2:2 + 2 * n_g]
    x_ref, wbh_ref, wbd_ref, wo_ref, nw_ref, o_ref = refs[2 + 2 * n_g:]
    g_hg = jnp.concatenate([r[...] for r in g_refs[:n_g]], axis=1)
    g_da = jnp.concatenate([r[...] for r in g_refs[n_g:]], axis=1)
    y_hg = jnp.dot(ohg_ref[...], wbh_ref[...], preferred_element_type=F32)
    y_da = jnp.dot(oda_ref[...], wbd_ref[...], preferred_element_type=F32)
    mixed = (_sigmoid(g_hg) * y_hg + _sigmoid(g_da) * y_da).astype(BF16)
    z = jnp.dot(mixed, wo_ref[...], preferred_element_type=F32)
    o_ref[...] = x_ref[...] + _rms(z, nw_ref[...])


def _merge_resident(o_hg, o_da, u, gcol, x, wbh, wbd, wo, nw):
    m, d = x.shape
    w = o_hg.shape[1]
    gw = math.gcd(gcol, d)
    n_g = d // gw
    assert gw % LANES == 0
    tm = _pick(m, (256, 128, 64, 8))
    const = lambda i: (0, 0)
    g_specs = [pl.BlockSpec((tm, gw), functools.partial(lambda i, c: (i, c), c=gcol // gw + k))
               for k in range(2 * n_g)]
    return pl.pallas_call(
        functools.partial(_merge_resident_kernel, n_g=n_g),
        grid=(m // tm,),
        in_specs=[pl.BlockSpec((tm, w), lambda i: (i, 0)),
                  pl.BlockSpec((tm, w), lambda i: (i, 0))]
                 + g_specs
                 + [pl.BlockSpec((tm, d), lambda i: (i, 0)),
                    pl.BlockSpec((w, d), const, pipeline_mode=pl.Buffered(1)),
                    pl.BlockSpec((w, d), const, pipeline_mode=pl.Buffered(1)),
                    pl.BlockSpec((d, d), const, pipeline_mode=pl.Buffered(1)),
                    pl.BlockSpec((1, d), const)],
        out_specs=pl.BlockSpec((tm, d), lambda i: (i, 0)),
        out_shape=jax.ShapeDtypeStruct((m, d), F32),
        compiler_params=_params("parallel"),
        name="merge",
    )(o_hg, o_da, *([u] * (2 * n_g)), x, wbh, wbd, wo, nw)


def _merge_kernel(ohg_ref, oda_ref, ghg_ref, gda_ref, x_ref, wbh_ref, wbd_ref, wo_ref, nw_ref,
                  o_ref, *copies, emit_w):
    copies = copies if emit_w else (None, None, None)
    n = pl.program_id(1)

    @pl.when(n == 0)
    def _():
        o_ref[...] = jnp.zeros(o_ref.shape, F32)

    y_hg = jnp.dot(ohg_ref[...], _bf16_weight(wbh_ref, copies[0]), preferred_element_type=F32)
    y_da = jnp.dot(oda_ref[...], _bf16_weight(wbd_ref, copies[1]), preferred_element_type=F32)
    mixed = (_sigmoid(ghg_ref[...]) * y_hg + _sigmoid(gda_ref[...]) * y_da).astype(BF16)
    o_ref[...] += jnp.dot(mixed, _bf16_weight(wo_ref, copies[2]), preferred_element_type=F32)

    @pl.when(n == pl.num_programs(1) - 1)
    def _():
        o_ref[...] = x_ref[...] + _rms(o_ref[...], nw_ref[...])


def _merge(o_hg, o_da, u, gcol, x, wbh, wbd, wo, nw):
    m, d = x.shape
    w = o_hg.shape[1]
    emit_w = wo.dtype != BF16
    tm = _pick(m, (1024, 512, 256, 128, 64, 32, 16, 8))
    tn = _pick(math.gcd(gcol, d), (512, 256, 128))
    nn = d // tn
    gc = gcol // tn
    once = dict(pipeline_mode=pl.Buffered(1))
    out_specs = [pl.BlockSpec((tm, d), lambda i, n: (i, 0))]
    out_shape = [jax.ShapeDtypeStruct((m, d), F32)]
    if emit_w:
        assert m == tm, "the bf16 weight copies are written once per column tile"
        out_specs += [pl.BlockSpec((w, tn), lambda i, n: (0, n)),
                      pl.BlockSpec((w, tn), lambda i, n: (0, n)),
                      pl.BlockSpec((tn, d), lambda i, n: (n, 0))]
        out_shape += [jax.ShapeDtypeStruct((w, d), BF16), jax.ShapeDtypeStruct((w, d), BF16),
                      jax.ShapeDtypeStruct((d, d), BF16)]
    return pl.pallas_call(
        functools.partial(_merge_kernel, emit_w=emit_w),
        grid=(m // tm, nn),
        in_specs=[pl.BlockSpec((tm, w), lambda i, n: (i, 0), **once),
                  pl.BlockSpec((tm, w), lambda i, n: (i, 0), **once),
                  pl.BlockSpec((tm, tn), lambda i, n: (i, gc + n)),
                  pl.BlockSpec((tm, tn), lambda i, n: (i, gc + nn + n)),
                  pl.BlockSpec((tm, d), lambda i, n: (i, 0), **once),
                  pl.BlockSpec((w, tn), lambda i, n: (0, n)),
                  pl.BlockSpec((w, tn), lambda i, n: (0, n)),
                  pl.BlockSpec((tn, d), lambda i, n: (n, 0)),
                  pl.BlockSpec((1, d), lambda i, n: (0, 0))],
        out_specs=out_specs,
        out_shape=out_shape,
        compiler_params=_params("parallel", "arbitrary"),
        name="merge",
    )(o_hg, o_da, u, u, x, wbh, wbd, wo, nw)


def _ffn_kernel(x_ref, npre_ref, wg_ref, wu_ref, wd_ref, npost_ref, o_ref, *rest, emit_w):
    h_ref = rest[-1]
    copies = rest[:3] if emit_w else (None, None, None)
    j = pl.program_id(1)

    @pl.when(j == 0)
    def _():
        h_ref[...] = _rms(x_ref[...], npre_ref[...]).astype(BF16)
        o_ref[...] = jnp.zeros(o_ref.shape, F32)

    h = h_ref[...]
    gate = jnp.dot(h, _bf16_weight(wg_ref, copies[0]), preferred_element_type=F32)
    up = jnp.dot(h, _bf16_weight(wu_ref, copies[1]), preferred_element_type=F32)
    act = (gate * _sigmoid(gate) * up).astype(BF16)
    o_ref[...] += jnp.dot(act, _bf16_weight(wd_ref, copies[2]), preferred_element_type=F32)

    @pl.when(j == pl.num_programs(1) - 1)
    def _():
        o_ref[...] = x_ref[...] + _rms(o_ref[...], npost_ref[...])


def _ffn(x, npre, w_gate, w_up, w_down, npost):
    m, d = x.shape
    ff = w_down.shape[0]
    emit_w = w_down.dtype != BF16
    tm = _pick(m, (1024, 512, 256, 128, 64, 32, 16, 8))
    tf = _pick(ff, (512, 256, 128))
    nf = ff // tf
    (wg, gcol), (wu, ucol) = w_gate, w_up
    assert gcol % tf == 0 and ucol % tf == 0
    gc, uc = gcol // tf, ucol // tf
    out_specs = [pl.BlockSpec((tm, d), lambda i, j: (i, 0))]
    out_shape = [jax.ShapeDtypeStruct((m, d), F32)]
    if emit_w:
        assert m == tm, "the bf16 weight copies are written once per hidden-dim tile"
        out_specs += [pl.BlockSpec((d, tf), lambda i, j: (0, j)),
                      pl.BlockSpec((d, tf), lambda i, j: (0, j)),
                      pl.BlockSpec((tf, d), lambda i, j: (j, 0))]
        out_shape += [jax.ShapeDtypeStruct((d, ff), BF16), jax.ShapeDtypeStruct((d, ff), BF16),
                      jax.ShapeDtypeStruct((ff, d), BF16)]
    return pl.pallas_call(
        functools.partial(_ffn_kernel, emit_w=emit_w),
        grid=(m // tm, nf),
        in_specs=[pl.BlockSpec((tm, d), lambda i, j: (i, 0)),
                  pl.BlockSpec((1, d), lambda i, j: (0, 0)),
                  pl.BlockSpec((d, tf), lambda i, j: (0, gc + j)),
                  pl.BlockSpec((d, tf), lambda i, j: (0, uc + j)),
                  pl.BlockSpec((tf, d), lambda i, j: (j, 0)),
                  pl.BlockSpec((1, d), lambda i, j: (0, 0))],
        out_specs=out_specs,
        out_shape=out_shape,
        scratch_shapes=[pltpu.VMEM((tm, d), BF16)],
        compiler_params=_params("parallel", "arbitrary"),
        name="ffn",
    )(x, npre, wg, wu, w_down, npost)


def _mixer_tail(x2, u, o_da, s0, lw, wts, dims, batch, seqlen, gcol):
    hg_heads, hg_dk, hg_dv = dims[:3]
    o_hg, s_new = _hgrn(u, 0, lw["lb"], lw["hg_norm_w"], s0, batch, seqlen, hg_heads, hg_dk, hg_dv)
    merge_args = (o_hg, o_da, u, gcol, x2, wts["w_branch_hg"], wts["w_branch_da"], wts["w_out"],
                  lw["norm_mix_post"])
    if wts["w_out"].dtype == BF16:
        x1 = _merge_resident(*merge_args)
    else:
        x1, wts["w_branch_hg"], wts["w_branch_da"], wts["w_out"] = _merge(*merge_args)
    y, *copy = _ffn(x1, lw["norm_ffn_pre"], wts["w_gate"], wts["w_up"], wts["w_down"],
                    lw["norm_ffn_post"])
    if copy:
        wts["w_gate"], wts["w_up"], wts["w_down"] = (copy[0], 0), (copy[1], 0), copy[2]
    return y, s_new, wts


def _layer(x_p, x_s, s0_s, cache_k, cache_v, page_table, lw, wts, dims):
    (bp, lp, d), (bs, ls, _) = x_p.shape, x_s.shape
    wts = dict(wts)
    hg_heads, hg_dk, hg_dv, da_heads, da_dh, da_dv = dims
    hg_w, da_qk, da_w = hg_heads * hg_dk, 2 * da_heads * da_dh, da_heads * da_dv
    assert da_qk == da_w
    cols = np.cumsum([0, hg_w, hg_w, hg_heads * hg_dv, hg_heads * hg_dv, da_qk, da_qk, da_w, d, d])
    qcol, kcol, vcol, gcol = int(cols[4]), int(cols[5]), int(cols[6]), int(cols[7])
    xp2, xs2 = x_p.reshape(bp * lp, d), x_s.reshape(bs * ls, d)

    us, wts["w_in"] = _norm_proj(xs2, lw["norm_mix_pre"], wts["w_in"])
    up, kt, vp2 = _norm_proj(xp2, lw["norm_mix_pre"], wts["w_in"],
                             kv=(kcol, vcol, da_qk, lp, da_heads))

    attn = (lw["tbl"], lw["lam"], lw["da_subln_w"])
    shape = (da_heads, da_dh, da_dv, lw["out_scale"])
    n_pages, page = page_table.shape[1], cache_k.shape[1]
    fits = [(per, _fused_pages(bp, lp, da_heads, bs, n_pages, per)) for per in (2, 1)]
    fits = [(per, pages) for per, pages in fits
            if _decode_geometry(n_pages, page, lw["tbl"].shape[0], pages) is not None]
    if fits:
        per, pages = fits[0]
        o_da_p, o_da_s = _attn_fused(up, kt, us, qcol, kcol, vcol, cache_k, cache_v, page_table,
                                     *attn, bp, lp, bs, ls, *shape, pages, per)
    else:
        o_da_p = _attn_prompt(up, qcol, vcol, kt, *attn, bp, lp, *shape)
        o_da_s = _attn_decode(us, qcol, kcol, vcol, cache_k, cache_v, page_table, *attn,
                              bs, ls, *shape)

    y_s, s_s, wts = _mixer_tail(xs2, us, o_da_s, s0_s, lw, wts, dims, bs, ls, gcol)
    y_p, s_p, _ = _mixer_tail(xp2, up, o_da_p, None, lw, wts, dims, bp, lp, gcol)

    k_p = kt.reshape(bp, 2 * da_heads, da_dh, lp).transpose(0, 3, 1, 2)
    v_p = vp2.reshape(bp, lp, da_heads, da_dv)
    k_s = us[:, kcol:kcol + da_qk].reshape(bs, ls, 2 * da_heads, da_dh)
    v_s = us[:, vcol:vcol + da_w].reshape(bs, ls, da_heads, da_dv)
    return (y_p.reshape(bp, lp, d), y_s.reshape(bs, ls, d)), (k_p, v_p, s_p, k_s, v_s, s_s)


def kernel(x_prompt, x_sample, cache_k, cache_v, state_hgrn, page_table, norm_mix_pre, norm_mix_post, norm_ffn_pre, norm_ffn_post, w_in, hg_lb_logits, hg_norm_w, da_lambda_q1, da_lambda_k1, da_lambda_q2, da_lambda_k2, da_subln_w, rel_bias_table, w_branch_hg, w_branch_da, w_out, w_ffn_up, w_ffn_down):
    depth = w_in.shape[0]
    _, _, hg_heads, hg_dk, hg_dv = state_hgrn.shape
    da_heads, da_dv = cache_v.shape[3], cache_v.shape[4]
    da_dh = cache_k.shape[4]
    dims = (hg_heads, hg_dk, hg_dv, da_heads, da_dh, da_dv)
    lb_all = jnp.cumsum(jax.nn.softmax(hg_lb_logits.astype(F32), axis=0), axis=0)

    y_p, y_s = x_prompt, x_sample
    outs = [[] for _ in range(6)]
    for l in range(depth):
        lam_init = 0.8 - 0.6 * math.exp(-0.3 * l)
        lam = (jnp.exp(jnp.sum(da_lambda_q1[l] * da_lambda_k1[l]))
               - jnp.exp(jnp.sum(da_lambda_q2[l] * da_lambda_k2[l])) + lam_init)
        lw = {
            "norm_mix_pre": norm_mix_pre[l][None], "norm_mix_post": norm_mix_post[l][None],
            "norm_ffn_pre": norm_ffn_pre[l][None], "norm_ffn_post": norm_ffn_post[l][None],
            "lb": lb_all[l][None], "hg_norm_w": hg_norm_w[l][None],
            "da_subln_w": da_subln_w[l][None], "tbl": rel_bias_table.astype(F32),
            "lam": lam.reshape(1, 1).astype(F32), "out_scale": 1.0 - lam_init,
        }
        ff = w_ffn_down.shape[1]
        wts = {"w_in": w_in[l], "w_branch_hg": w_branch_hg[l], "w_branch_da": w_branch_da[l],
               "w_out": w_out[l], "w_gate": (w_ffn_up[l], 0), "w_up": (w_ffn_up[l], ff),
               "w_down": w_ffn_down[l]}
        (y_p, y_s), caches = _layer(y_p, y_s, state_hgrn[l], cache_k[l], cache_v[l], page_table,
                                    lw, wts, dims)
        for acc, val in zip(outs, caches):
            acc.append(val)
    return (y_p, y_s) + tuple(jnp.stack(o) for o in outs)
```
